```python
import math
import jax, jax.numpy as jnp
from jax import lax
import numpy as np

D_MODEL = 1024
BATCH = 8
SEQ = 8192
DEPTH = 2

DIFF_QK_DIM = 64
DIFF_V_DIM = 2 * DIFF_QK_DIM
N_DIFF_HEADS = (D_MODEL // 2) // DIFF_V_DIM
DIFF_WIDTH = N_DIFF_HEADS * DIFF_V_DIM
SG_GROUP_DIM = 128
N_SG_GROUPS = (D_MODEL // 2) // SG_GROUP_DIM
SG_WIDTH = N_SG_GROUPS * SG_GROUP_DIM
MIX_WIDTH = DIFF_WIDTH + SG_WIDTH
CHUNK = 128
QK_COLS = N_DIFF_HEADS * 2 * DIFF_QK_DIM
IN_COLS = 2 * QK_COLS + DIFF_WIDTH + 2 * SG_WIDTH
ROPE_THETA = 10000.0
Q_BLOCK = 128
N_EXPERT_GROUPS = 4
EXPERTS_PER_GROUP = 8
N_EXPERTS = N_EXPERT_GROUPS * EXPERTS_PER_GROUP
TOP_K_IN_GROUP = 2
D_EXPERT = 512
DEEPNORM_ALPHA = (2.0 * DEPTH) ** 0.25
DEEPNORM_BETA = (8.0 * DEPTH) ** -0.25
LN_EPS = 1e-5
ADA_SCALE = 0.1

kernel_name = "hymba_diffattn_sgmlp_hmoe_deepnorm"


def layer_norm(x, g, b):
    xf = x.astype(jnp.float32)
    mu = jnp.mean(xf, axis=-1, keepdims=True)
    var = jnp.mean(jnp.square(xf - mu), axis=-1, keepdims=True)
    return ((xf - mu) * lax.rsqrt(var + LN_EPS) * g + b).astype(x.dtype)


def rms_norm(x, g):
    xf = x.astype(jnp.float32)
    return (xf * lax.rsqrt(jnp.mean(jnp.square(xf), axis=-1, keepdims=True) + LN_EPS) * g).astype(x.dtype)


def rope_tables(seq, dim):
    inv = 1.0 / (ROPE_THETA ** (jnp.arange(0, dim, 2, dtype=jnp.float32) / dim))
    ang = jnp.arange(seq, dtype=jnp.float32)[:, None] * inv[None, :]
    return jnp.cos(ang), jnp.sin(ang)


def apply_rope(t, cos, sin):
    t1, t2 = jnp.split(t, 2, axis=-1)
    c = cos[:, None, None, :].astype(t.dtype)
    s = sin[:, None, None, :].astype(t.dtype)
    return jnp.concatenate([t1 * c - t2 * s, t2 * c + t1 * s], axis=-1)


def diff_attention(q, k, v, lam):
    B, S, H, _, dk = q.shape
    nb = S // Q_BLOCK
    qb = (q * (dk ** -0.5)).reshape(B, nb, Q_BLOCK, H, 2, dk).transpose(1, 0, 2, 3, 4, 5)
    kpos = jnp.arange(S)

    def one_block(args):
        qblk, i = args
        s = jnp.einsum('bqhmd,bkhmd->bhmqk', qblk, k).astype(jnp.float32)
        qpos = i * Q_BLOCK + jnp.arange(Q_BLOCK)
        s = jnp.where(kpos[None, :] <= qpos[:, None], s, -jnp.inf)
        p = jax.nn.softmax(s, axis=-1)
        a = p[:, :, 0] - lam * p[:, :, 1]
        return jnp.einsum('bhqk,bkhd->bqhd', a.astype(v.dtype), v)

    out = lax.map(one_block, (qb, jnp.arange(nb)))
    return out.transpose(1, 0, 2, 3, 4).reshape(B, S, H, v.shape[-1])


def chunk_spatial_gate(zu, zv, ln_g, ln_b, w_s, b_s):
    B, S, G, C = zv.shape
    zv = layer_norm(zv, ln_g, ln_b)
    causal = jnp.tril(jnp.ones((CHUNK, CHUNK), dtype=bool))
    w = jnp.where(causal[None], w_s, jnp.zeros_like(w_s))
    zc = zv.reshape(B, S // CHUNK, CHUNK, G, C)
    mixed = jnp.einsum('gts,bnsgc->bntgc', w, zc) + b_s.T[None, None, :, :, None]
    return zu * mixed.reshape(B, S, G, C)


def hier_moe(h, w_grp, b_grp, w_rt, b_rt, w_gate, w_up, w_down):
    B, S, D = h.shape
    t = h.reshape(-1, D)
    T = t.shape[0]
    grp_logits = (t @ w_grp + b_grp).astype(jnp.float32)
    grp_prob = jax.nn.softmax(grp_logits, axis=-1)
    g_idx = jnp.argmax(grp_logits, axis=-1)
    g_p = jnp.take_along_axis(grp_prob, g_idx[:, None], axis=-1)[:, 0]
    exp_logits = (jnp.einsum('td,gde->tge', t, w_rt) + b_rt).astype(jnp.float32)
    sel = jnp.take_along_axis(exp_logits, g_idx[:, None, None], axis=1)[:, 0]
    top_v, top_i = lax.top_k(sel, TOP_K_IN_GROUP)
    top_w = jax.nn.softmax(top_v, axis=-1)
    within = jnp.sum(jax.nn.one_hot(top_i, EXPERTS_PER_GROUP, dtype=jnp.float32) * top_w[..., None], axis=1)
    grp_w = jax.nn.one_hot(g_idx, N_EXPERT_GROUPS, dtype=jnp.float32) * g_p[:, None]
    combine = (grp_w[:, :, None] * within[:, None, :]).reshape(T, N_EXPERTS).astype(t.dtype)
    out = jnp.zeros_like(t)
    for e in range(N_EXPERTS):
        a = jax.nn.silu(t @ w_gate[e]) * (t @ w_up[e])
        out = out + combine[:, e:e + 1] * (a @ w_down[e])
    return out.reshape(B, S, D)


def setup_inputs(seed: int = 0) -> dict:
    key = jax.random.key(seed)
    ks = jax.random.split(key, 32)
    n = jax.random.normal
    f32 = jnp.float32
    D, L = D_MODEL, DEPTH
    w_in = n(ks[2], (L, D, IN_COLS), f32) * D ** -0.5
    v0, v1 = 2 * QK_COLS, 2 * QK_COLS + DIFF_WIDTH + SG_WIDTH
    col_scale = jnp.ones((IN_COLS,), f32).at[v0:v1].set(DEEPNORM_BETA)
    w_in = w_in * col_scale
    return {
        "x": n(ks[0], (BATCH, SEQ, D), f32),
        "c": n(ks[1], (BATCH, D), f32),
        "w_ada": n(ks[3], (L, D, 6 * D), f32) * (D ** -0.5) * ADA_SCALE,
        "b_ada": n(ks[4], (L, 6 * D), f32) * 0.02,
        "w_in": w_in,
        "lambda_q1": n(ks[5], (L, DIFF_QK_DIM), f32) * 0.1,
        "lambda_k1": n(ks[6], (L, DIFF_QK_DIM), f32) * 0.1,
        "lambda_q2": n(ks[7], (L, DIFF_QK_DIM), f32) * 0.1,
        "lambda_k2": n(ks[8], (L, DIFF_QK_DIM), f32) * 0.1,
        "subln_g": 1.0 + 0.02 * n(ks[9], (L, DIFF_V_DIM), f32),
        "sg_ln_g": 1.0 + 0.02 * n(ks[10], (L, N_SG_GROUPS, SG_GROUP_DIM), f32),
        "sg_ln_b": 0.02 * n(ks[11], (L, N_SG_GROUPS, SG_GROUP_DIM), f32),
        "w_spatial": n(ks[12], (L, N_SG_GROUPS, CHUNK, CHUNK), f32) * CHUNK ** -0.5,
        "b_spatial": 1.0 + 0.02 * n(ks[13], (L, N_SG_GROUPS, CHUNK), f32),
        "w_out": n(ks[14], (L, MIX_WIDTH, D), f32) * (MIX_WIDTH ** -0.5) * DEEPNORM_BETA,
        "ln1_g": 1.0 + 0.02 * n(ks[15], (L, D), f32),
        "ln1_b": 0.02 * n(ks[16], (L, D), f32),
        "w_group": n(ks[17], (L, D, N_EXPERT_GROUPS), f32) * D ** -0.5,
        "b_group": 0.01 * n(ks[18], (L, N_EXPERT_GROUPS), f32),
        "w_router": n(ks[19], (L, N_EXPERT_GROUPS, D, EXPERTS_PER_GROUP), f32) * D ** -0.5,
        "b_router": 0.01 * n(ks[20], (L, N_EXPERT_GROUPS, EXPERTS_PER_GROUP), f32),
        "w_gate": n(ks[21], (L, N_EXPERTS, D, D_EXPERT), f32) * D ** -0.5,
        "w_up": n(ks[22], (L, N_EXPERTS, D, D_EXPERT), f32) * (D ** -0.5) * DEEPNORM_BETA,
        "w_down": n(ks[23], (L, N_EXPERTS, D_EXPERT, D), f32) * (D_EXPERT ** -0.5) * DEEPNORM_BETA,
        "ln2_g": 1.0 + 0.02 * n(ks[24], (L, D), f32),
        "ln2_b": 0.02 * n(ks[25], (L, D), f32),
    }


def reference(x, c, w_ada, b_ada, w_in, lambda_q1, lambda_k1, lambda_q2, lambda_k2, subln_g,
              sg_ln_g, sg_ln_b, w_spatial, b_spatial, w_out, ln1_g, ln1_b, w_group, b_group,
              w_router, b_router, w_gate, w_up, w_down, ln2_g, ln2_b):
    B, S, D = x.shape
    cos, sin = rope_tables(S, DIFF_QK_DIM)
    split_at = [QK_COLS, 2 * QK_COLS, 2 * QK_COLS + DIFF_WIDTH, 2 * QK_COLS + DIFF_WIDTH + SG_WIDTH]
    for l in range(DEPTH):
        mod = (jax.nn.silu(c) @ w_ada[l] + b_ada[l])[:, None, :]
        sh1, sc1, gt1, sh2, sc2, gt2 = jnp.split(mod, 6, axis=-1)

        h = x * (1 + sc1) + sh1
        proj = h @ w_in[l]
        q, k, v, u, vs = jnp.split(proj, split_at, axis=-1)
        q = apply_rope(q.reshape(B, S, N_DIFF_HEADS, 2, DIFF_QK_DIM), cos, sin)
        k = apply_rope(k.reshape(B, S, N_DIFF_HEADS, 2, DIFF_QK_DIM), cos, sin)
        v = v.reshape(B, S, N_DIFF_HEADS, DIFF_V_DIM)
        lam_init = 0.8 - 0.6 * math.exp(-0.3 * l)
        lam = (jnp.exp(jnp.sum((lambda_q1[l] * lambda_k1[l]).astype(jnp.float32)))
               - jnp.exp(jnp.sum((lambda_q2[l] * lambda_k2[l]).astype(jnp.float32))) + lam_init)
        att = diff_attention(q, k, v, lam)
        att = rms_norm(att, subln_g[l]) * (1.0 - lam_init)
        zu = jax.nn.gelu(u, approximate=False).reshape(B, S, N_SG_GROUPS, SG_GROUP_DIM)
        zv = jax.nn.gelu(vs, approximate=False).reshape(B, S, N_SG_GROUPS, SG_GROUP_DIM)
        sg = chunk_spatial_gate(zu, zv, sg_ln_g[l], sg_ln_b[l], w_spatial[l], b_spatial[l])
        mix = jnp.concatenate([att.reshape(B, S, DIFF_WIDTH), sg.reshape(B, S, SG_WIDTH)], axis=-1) @ w_out[l]
        x = layer_norm(DEEPNORM_ALPHA * x + (1 + gt1) * mix, ln1_g[l], ln1_b[l])

        h = x * (1 + sc2) + sh2
        ffn = hier_moe(h, w_group[l], b_group[l], w_router[l], b_router[l], w_gate[l], w_up[l], w_down[l])
        x = layer_norm(DEEPNORM_ALPHA * x + (1 + gt2) * ffn, ln2_g[l], ln2_b[l])
    return x
```

```python
import functools
import math

import jax
import jax.numpy as jnp
from jax import lax
from jax.experimental import pallas as pl
from jax.experimental.pallas import tpu as pltpu

F32 = jnp.float32
BF16 = jnp.bfloat16
U32 = jnp.uint32
I32 = jnp.int32

D_MODEL = 1024
N_HEADS = 4
QK_DIM = 64
V_DIM = 128
HEAD_COLS = 2 * QK_DIM
QK_COLS = N_HEADS * HEAD_COLS
DIFF_WIDTH = N_HEADS * V_DIM
N_SG = 4
SG_DIM = 128
SG_WIDTH = N_SG * SG_DIM
CHUNK = 128
N_GROUPS = 4
EXP_PER_GROUP = 8
N_EXPERTS = N_GROUPS * EXP_PER_GROUP
D_EXPERT = 512
ROPE_THETA = 10000.0
LN_EPS = 1e-5
LANES = 128
HALF = D_MODEL // 2

ROW_TILE = 512
ATT_TILE = 512
EXPERT_TILE = 512
DISPATCH_BLOCK = 2048
COMBINE_TILE = 256
ROUTE_LANE0 = 8
VMEM_LIMIT = 48 * 1024 * 1024

LOG2E = 1.4426950408889634
Q_SCALE = (QK_DIM ** -0.5) * LOG2E


def _cparams(sem):
    return pltpu.CompilerParams(dimension_semantics=sem, vmem_limit_bytes=VMEM_LIMIT)


def _layer_norm_rows(y, g, b):
    mu = jnp.mean(y, axis=-1, keepdims=True)
    yc = y - mu
    var = jnp.mean(yc * yc, axis=-1, keepdims=True)
    return yc * lax.rsqrt(var + LN_EPS) * g + b


def _gelu(x):
    return 0.5 * x * (1.0 + lax.erf(x * (2.0 ** -0.5)))


def _pack_halves(y):
    lo = pltpu.bitcast(y[:, :HALF].astype(BF16).astype(F32), U32) >> 16
    hi = pltpu.bitcast(y[:, HALF:].astype(BF16).astype(F32), U32) & jnp.uint32(0xFFFF0000)
    return lo | hi


def _unpack_halves(p):
    lo = pltpu.bitcast(p << 16, F32)
    hi = pltpu.bitcast(p & jnp.uint32(0xFFFF0000), F32)
    return lo, hi


def _ada_kernel(c_ref, w_ref, b_ref, o_ref):
    c = c_ref[...]
    sc = c * jax.nn.sigmoid(c)
    o_ref[...] = jnp.dot(sc, w_ref[...], precision=lax.Precision.HIGHEST,
                         preferred_element_type=F32) + b_ref[...]


def _ada(c, w_ada, b_ada):
    L, D, N = w_ada.shape
    B = c.shape[0]
    tn = 1536
    return pl.pallas_call(
        _ada_kernel,
        grid=(L, N // tn),
        in_specs=[
            pl.BlockSpec((B, D), lambda l, j: (0, 0)),
            pl.BlockSpec((None, D, tn), lambda l, j: (l, 0, j)),
            pl.BlockSpec((None, 1, tn), lambda l, j: (l, 0, j)),
        ],
        out_specs=pl.BlockSpec((None, B, tn), lambda l, j: (l, 0, j)),
        out_shape=jax.ShapeDtypeStruct((L, B, N), F32),
        compiler_params=_cparams(("arbitrary", "arbitrary")),
        name="ada",
    )(c, w_ada, b_ada.reshape(L, 1, N))


def _inproj_kernel(x_ref, mod_ref, w_ref, cs_ref, sa_ref, sb_ref, lng_ref, lnb_ref,
                   qT_ref, k_ref, vT_ref, zu_ref, zvn_ref):
    D = D_MODEL
    x = x_ref[...]
    sh = mod_ref[:, 0:D]
    sc = mod_ref[:, D:2 * D]
    h = (x * (1.0 + sc) + sh).astype(BF16)
    cs = cs_ref[...]
    sa = sa_ref[...]
    sb = sb_ref[...]

    def rope(t):
        return t * cs + pltpu.roll(t, 96, 1) * sa + pltpu.roll(t, 32, 1) * sb

    q = jnp.dot(h, w_ref[:, 0:QK_COLS], preferred_element_type=F32)
    for j in range(N_HEADS):
        sl = slice(j * LANES, (j + 1) * LANES)
        qT_ref[sl, :] = (rope(q[:, sl]) * Q_SCALE).T.astype(BF16)
    k = jnp.dot(h, w_ref[:, QK_COLS:2 * QK_COLS], preferred_element_type=F32)
    for j in range(N_HEADS):
        sl = slice(j * LANES, (j + 1) * LANES)
        k_ref[:, sl] = rope(k[:, sl]).astype(BF16)
    c0 = 2 * QK_COLS
    v = jnp.dot(h, w_ref[:, c0:c0 + DIFF_WIDTH], preferred_element_type=F32)
    vT_ref[...] = v.T.astype(BF16)
    c0 += DIFF_WIDTH
    u = jnp.dot(h, w_ref[:, c0:c0 + SG_WIDTH], preferred_element_type=F32)
    zu_ref[...] = _gelu(u).astype(BF16)
    c0 += SG_WIDTH
    z = _gelu(jnp.dot(h, w_ref[:, c0:c0 + SG_WIDTH], preferred_element_type=F32))
    for g in range(N_SG):
        sl = slice(g * SG_DIM, (g + 1) * SG_DIM)
        zvn_ref[:, sl] = _layer_norm_rows(z[:, sl], lng_ref[:, sl], lnb_ref[:, sl]).astype(BF16)


def _inproj(x, mod_l, w_in_b, rope_tabs, ln_g, ln_b):
    B, S, D = x.shape
    tm = min(ROW_TILE, S)
    nt = S // tm
    cs, sa, sb = rope_tabs
    row = lambda b, i: (b, i, 0)
    tab = pl.BlockSpec((tm, LANES), lambda b, i: (i, 0))
    return pl.pallas_call(
        _inproj_kernel,
        grid=(B, nt),
        in_specs=[
            pl.BlockSpec((None, tm, D), row),
            pl.BlockSpec((None, 1, 6 * D), lambda b, i: (b, 0, 0)),
            pl.BlockSpec(w_in_b.shape, lambda b, i: (0, 0)),
            tab, tab, tab,
            pl.BlockSpec((1, SG_WIDTH), lambda b, i: (0, 0)),
            pl.BlockSpec((1, SG_WIDTH), lambda b, i: (0, 0)),
        ],
        out_specs=[
            pl.BlockSpec((None, QK_COLS, tm), lambda b, i: (b, 0, i)),
            pl.BlockSpec((None, tm, QK_COLS), row),
            pl.BlockSpec((None, None, DIFF_WIDTH, tm), lambda b, i: (b, i, 0, 0)),
            pl.BlockSpec((None, tm, SG_WIDTH), row),
            pl.BlockSpec((None, tm, SG_WIDTH), row),
        ],
        out_shape=[
            jax.ShapeDtypeStruct((B, QK_COLS, S), BF16),
            jax.ShapeDtypeStruct((B, S, QK_COLS), BF16),
            jax.ShapeDtypeStruct((B, nt, DIFF_WIDTH, tm), BF16),
            jax.ShapeDtypeStruct((B, S, SG_WIDTH), BF16),
            jax.ShapeDtypeStruct((B, S, SG_WIDTH), BF16),
        ],
        compiler_params=_cparams(("arbitrary", "arbitrary")),
        name="inproj",
    )(x, mod_l, w_in_b, cs, sa, sb, ln_g, ln_b)


def _attn_kernel(lq1_ref, lk1_ref, lq2_ref, lk2_ref, g_ref, qT_ref, k_ref, vT_ref, o_ref,
                 m_scr, l_scr, acc_scr, *, lam_init):
    tq = qT_ref.shape[1]
    tk = k_ref.shape[0] // vT_ref.shape[0]
    i = pl.program_id(2)
    qT = qT_ref[...]
    rows = lax.broadcasted_iota(I32, qT.shape, 0)
    zero = jnp.zeros_like(qT)
    q_maps = (jnp.where(rows < QK_DIM, qT, zero), jnp.where(rows >= QK_DIM, qT, zero))
    m_scr[...] = jnp.full(m_scr.shape, -jnp.inf, F32)
    l_scr[...] = jnp.zeros(l_scr.shape, F32)
    acc_scr[...] = jnp.zeros(acc_scr.shape, F32)

    def step(j, masked):
        kb = k_ref[pl.ds(pl.multiple_of(j * tk, tk), tk), :]
        vb = vT_ref[j]
        for mp in range(2):
            s = jnp.dot(kb, q_maps[mp], preferred_element_type=F32)
            if masked:
                kpos = lax.broadcasted_iota(I32, s.shape, 0)
                qpos = lax.broadcasted_iota(I32, s.shape, 1)
                s = jnp.where(kpos <= qpos, s, -jnp.inf)
            m_old = m_scr[mp]
            m_new = jnp.maximum(m_old, jnp.max(s, axis=0, keepdims=True))
            alpha = jnp.exp2(m_old - m_new)
            p = jnp.exp2(s - m_new)
            l_scr[mp] = alpha * l_scr[mp] + jnp.sum(p, axis=0, keepdims=True)
            acc_scr[mp] = alpha * acc_scr[mp] + jnp.dot(vb, p.astype(BF16), preferred_element_type=F32)
            m_scr[mp] = m_new

    def full_step(j, carry):
        step(j, False)
        return carry

    lax.fori_loop(0, i, full_step, 0)
    step(i, True)

    lam = (jnp.exp(jnp.sum(lq1_ref[...] * lk1_ref[...], axis=1, keepdims=True))
           - jnp.exp(jnp.sum(lq2_ref[...] * lk2_ref[...], axis=1, keepdims=True)) + lam_init)
    out = acc_scr[0] / l_scr[0] - lam * (acc_scr[1] / l_scr[1])
    ms = jnp.mean(out * out, axis=0, keepdims=True)
    y = out * lax.rsqrt(ms + LN_EPS) * g_ref[...] * (1.0 - lam_init)
    o_ref[...] = y.T.astype(BF16)


def _attention(qT, k, vTb, lams, g_col, lam_init):
    B, _, S = qT.shape
    nkv, tk = vTb.shape[1], vTb.shape[3]
    tq = tk
    small = pl.BlockSpec((1, QK_DIM), lambda b, h, i: (0, 0))
    return pl.pallas_call(
        functools.partial(_attn_kernel, lam_init=lam_init),
        grid=(B, N_HEADS, S // tq),
        in_specs=[
            small, small, small, small,
            pl.BlockSpec((V_DIM, 1), lambda b, h, i: (0, 0)),
            pl.BlockSpec((None, HEAD_COLS, tq), lambda b, h, i: (b, h, i)),
            pl.BlockSpec((None, S, HEAD_COLS), lambda b, h, i: (b, 0, h)),
            pl.BlockSpec((None, nkv, V_DIM, tk), lambda b, h, i: (b, 0, h, 0)),
        ],
        out_specs=pl.BlockSpec((None, tq, V_DIM), lambda b, h, i: (b, i, h)),
        out_shape=jax.ShapeDtypeStruct((B, S, DIFF_WIDTH), BF16),
        scratch_shapes=[
            pltpu.VMEM((2, 1, tq), F32),
            pltpu.VMEM((2, 1, tq), F32),
            pltpu.VMEM((2, V_DIM, tq), F32),
        ],
        compiler_params=_cparams(("arbitrary", "arbitrary", "arbitrary")),
        name="diff_attn",
    )(*lams, g_col, qT, k, vTb)


def _mix_kernel(x_ref, mod_ref, att_ref, zu_ref, zvn_ref, wsp_ref, bsp_ref, wo_ref, g_ref, b_ref,
                o_ref, sg_scr, *, alpha):
    D = D_MODEL
    nc = zu_ref.shape[0]
    r = lax.broadcasted_iota(I32, (CHUNK, CHUNK), 0)
    c = lax.broadcasted_iota(I32, (CHUNK, CHUNK), 1)
    causal = r >= c
    for g in range(N_SG):
        sl = slice(g * SG_DIM, (g + 1) * SG_DIM)
        w = jnp.where(causal, wsp_ref[g], 0.0).astype(BF16)
        z = jnp.concatenate([zvn_ref[n, :, sl] for n in range(nc)], axis=1)
        mixed = jnp.dot(w, z, preferred_element_type=F32)
        for n in range(nc):
            gate = mixed[:, n * SG_DIM:(n + 1) * SG_DIM] + bsp_ref[g]
            sg_scr[n * CHUNK:(n + 1) * CHUNK, sl] = (zu_ref[n, :, sl].astype(F32) * gate).astype(BF16)
    mix = (jnp.dot(att_ref[...], wo_ref[0:DIFF_WIDTH, :], preferred_element_type=F32)
           + jnp.dot(sg_scr[...], wo_ref[DIFF_WIDTH:, :], preferred_element_type=F32))
    gt = mod_ref[:, 2 * D:3 * D]
    y = alpha * x_ref[...] + (1.0 + gt) * mix
    o_ref[...] = _layer_norm_rows(y, g_ref[...], b_ref[...])


def _mix(x, mod_l, att, zu, zvn, w_sp, b_sp_full, w_out_b, ln_g, ln_b, alpha):
    B, S, D = x.shape
    tm = min(ROW_TILE, S)
    nc = tm // CHUNK
    row = lambda b, i: (b, i, 0)
    zu4 = zu.reshape(B, S // CHUNK, CHUNK, SG_WIDTH)
    zvn4 = zvn.reshape(B, S // CHUNK, CHUNK, SG_WIDTH)
    chunked = pl.BlockSpec((None, nc, CHUNK, SG_WIDTH), lambda b, i: (b, i, 0, 0))
    full2 = lambda a: pl.BlockSpec(a.shape, lambda b, i: (0,) * a.ndim)
    return pl.pallas_call(
        functools.partial(_mix_kernel, alpha=alpha),
        grid=(B, S // tm),
        in_specs=[
            pl.BlockSpec((None, tm, D), row),
            pl.BlockSpec((None, 1, 6 * D), lambda b, i: (b, 0, 0)),
            pl.BlockSpec((None, tm, DIFF_WIDTH), row),
            chunked, chunked,
            full2(w_sp), full2(b_sp_full), full2(w_out_b), full2(ln_g), full2(ln_b),
        ],
        out_specs=pl.BlockSpec((None, tm, D), row),
        out_shape=jax.ShapeDtypeStruct((B, S, D), F32),
        scratch_shapes=[pltpu.VMEM((tm, SG_WIDTH), BF16)],
        compiler_params=_cparams(("arbitrary", "arbitrary")),
        name="sgate_outproj_ln",
    )(x, mod_l, att, zu4, zvn4, w_sp, b_sp_full, w_out_b, ln_g, ln_b)


def _route_kernel(x_ref, mod_ref, wr_hi_ref, wr_lo_ref, br_ref, hp_ref, rt_ref, cnt_ref, run_scr):
    D = D_MODEL
    tm = x_ref.shape[0]

    @pl.when((pl.program_id(0) == 0) & (pl.program_id(1) == 0))
    def _():
        run_scr[...] = jnp.zeros(run_scr.shape, F32)

    sh = mod_ref[:, 3 * D:4 * D]
    sc = mod_ref[:, 4 * D:5 * D]
    h = x_ref[...] * (1.0 + sc) + sh
    hp_ref[...] = _pack_halves(h)
    h_hi = h.astype(BF16)
    h_lo = (h - h_hi.astype(F32)).astype(BF16)
    logit = (jnp.dot(h_hi, wr_hi_ref[...], preferred_element_type=F32)
             + jnp.dot(h_lo, wr_hi_ref[...], preferred_element_type=F32)
             + jnp.dot(h_hi, wr_lo_ref[...], preferred_element_type=F32)) + br_ref[...]
    lane = lax.broadcasted_iota(I32, logit.shape, 1).astype(F32)
    neg = -jnp.inf
    big = float(LANES)

    def first_argmax(v):
        mx = jnp.max(v, axis=1, keepdims=True)
        idx = jnp.min(jnp.where(v == mx, lane, big), axis=1, keepdims=True)
        return mx, idx

    in_grp = lane < N_GROUPS
    gmax, gidx = first_argmax(jnp.where(in_grp, logit, neg))
    g_p = 1.0 / jnp.sum(jnp.where(in_grp, jnp.exp(logit - gmax), 0.0), axis=1, keepdims=True)
    lo_lane = ROUTE_LANE0 + EXP_PER_GROUP * gidx
    sel = jnp.where((lane >= lo_lane) & (lane < lo_lane + EXP_PER_GROUP), logit, neg)
    v1, i1 = first_argmax(sel)
    v2, i2 = first_argmax(jnp.where(lane == i1, neg, sel))
    t = jnp.exp(v2 - v1)
    w1 = g_p / (1.0 + t)
    w2 = g_p * t / (1.0 + t)

    hot1 = lane == i1
    hot2 = lane == i2
    twohot = jnp.where(hot1 | hot2, 1.0, 0.0)
    r = lax.broadcasted_iota(I32, (tm, tm), 0)
    c = lax.broadcasted_iota(I32, (tm, tm), 1)
    strict = jnp.where(r > c, 1.0, 0.0).astype(BF16)
    before = jnp.dot(strict, twohot.astype(BF16), preferred_element_type=F32) + run_scr[...]
    r1 = jnp.sum(jnp.where(hot1, before, 0.0), axis=1, keepdims=True)
    r2 = jnp.sum(jnp.where(hot2, before, 0.0), axis=1, keepdims=True)
    run_scr[...] = run_scr[...] + jnp.sum(twohot, axis=0, keepdims=True)
    cnt_ref[...] = run_scr[...]

    fields = (i1 - ROUTE_LANE0, i2 - ROUTE_LANE0, w1, w2, r1, r2)
    slab = jnp.zeros(logit.shape, F32)
    for n, f in enumerate(fields):
        slab = jnp.where(lane == float(n), f, slab)
    rt_ref[...] = slab.T[0:8, :]


def _route(x1, mod_l, wr_hi, wr_lo, br):
    B, S, D = x1.shape
    tm = min(ROW_TILE, S)
    row = lambda b, i: (b, i, 0)
    full2 = lambda a: pl.BlockSpec(a.shape, lambda b, i: (0,) * a.ndim)
    return pl.pallas_call(
        _route_kernel,
        grid=(B, S // tm),
        in_specs=[
            pl.BlockSpec((None, tm, D), row),
            pl.BlockSpec((None, 1, 6 * D), lambda b, i: (b, 0, 0)),
            full2(wr_hi), full2(wr_lo), full2(br),
        ],
        out_specs=[
            pl.BlockSpec((None, tm, HALF), row),
            pl.BlockSpec((None, 8, tm), lambda b, i: (b, 0, i)),
            pl.BlockSpec((1, LANES), lambda b, i: (0, 0)),
        ],
        out_shape=[
            jax.ShapeDtypeStruct((B, S, HALF), U32),
            jax.ShapeDtypeStruct((B, 8, S), F32),
            jax.ShapeDtypeStruct((1, LANES), F32),
        ],
        scratch_shapes=[pltpu.VMEM((1, LANES), F32)],
        compiler_params=_cparams(("arbitrary", "arbitrary")),
        name="moe_route",
    )(x1, mod_l, wr_hi, wr_lo, br)


def _dispatch_kernel(p0_ref, p1_ref, hp_ref, xs_in_ref, xs_ref, sem):
    del xs_in_ref
    nb = p0_ref.shape[2]
    base = pl.program_id(0) * nb

    def row_copy(t, p):
        return pltpu.make_async_copy(hp_ref.at[pl.ds(t, 1)], xs_ref.at[pl.ds(p, 1)], sem)

    def issue(t, carry):
        row_copy(base + t, p0_ref[0, 0, t]).start()
        row_copy(base + t, p1_ref[0, 0, t]).start()
        return carry

    lax.fori_loop(0, nb, issue, 0, unroll=8)

    def drain(t, carry):
        row_copy(0, 0).wait()
        row_copy(0, 0).wait()
        return carry

    lax.fori_loop(0, nb, drain, 0, unroll=8)


def _dispatch(pos0, pos1, hp, n_rows):
    T = hp.shape[0]
    nb = min(DISPATCH_BLOCK, T)
    blk = pl.BlockSpec((1, 1, nb), lambda i: (i, 0, 0), memory_space=pltpu.SMEM)
    xs0 = jnp.zeros((n_rows, HALF), U32)
    return pl.pallas_call(
        _dispatch_kernel,
        grid=(T // nb,),
        in_specs=[blk, blk, pl.BlockSpec(memory_space=pl.ANY), pl.BlockSpec(memory_space=pl.ANY)],
        out_specs=pl.BlockSpec(memory_space=pl.ANY),
        out_shape=jax.ShapeDtypeStruct((n_rows, HALF), U32),
        scratch_shapes=[pltpu.SemaphoreType.DMA(())],
        input_output_aliases={3: 0},
        compiler_params=_cparams(("arbitrary",)),
        name="moe_dispatch",
    )(pos0.reshape(T // nb, 1, nb), pos1.reshape(T // nb, 1, nb), hp, xs0)


def _expert_kernel(te_ref, nv_ref, xs_ref, wg_ref, wu_ref, wd_ref, ys_ref):
    del te_ref

    @pl.when(pl.program_id(0) < nv_ref[0])
    def _():
        lo, hi = _unpack_halves(xs_ref[...])
        lo = lo.astype(BF16)
        hi = hi.astype(BF16)
        g = (jnp.dot(lo, wg_ref[0:HALF, :], preferred_element_type=F32)
             + jnp.dot(hi, wg_ref[HALF:, :], preferred_element_type=F32))
        u = (jnp.dot(lo, wu_ref[0:HALF, :], preferred_element_type=F32)
             + jnp.dot(hi, wu_ref[HALF:, :], preferred_element_type=F32))
        a = (g * jax.nn.sigmoid(g) * u).astype(BF16)
        ys_ref[...] = _pack_halves(jnp.dot(a, wd_ref[...], preferred_element_type=F32))


def _experts(tile_expert, n_valid, xs, wg_b, wu_b, wd_b):
    n_rows = xs.shape[0]
    tm = EXPERT_TILE
    nt = n_rows // tm
    rows = lambda n, te, nv: (jnp.minimum(n, nv[0] - 1), 0)
    wsel = lambda n, te, nv: (te[n], 0, 0)
    return pl.pallas_call(
        _expert_kernel,
        grid_spec=pltpu.PrefetchScalarGridSpec(
            num_scalar_prefetch=2,
            grid=(nt,),
            in_specs=[
                pl.BlockSpec((tm, HALF), rows),
                pl.BlockSpec((None, D_MODEL, D_EXPERT), wsel),
                pl.BlockSpec((None, D_MODEL, D_EXPERT), wsel),
                pl.BlockSpec((None, D_EXPERT, D_MODEL), wsel),
            ],
            out_specs=pl.BlockSpec((tm, HALF), rows),
        ),
        out_shape=jax.ShapeDtypeStruct((n_rows, HALF), U32),
        compiler_params=_cparams(("arbitrary",)),
        name="moe_experts",
    )(tile_expert, n_valid, xs, wg_b, wu_b, wd_b)


def _combine_kernel(p0_ref, p1_ref, ys_ref, x_ref, mod_ref, rt_ref, g_ref, b_ref, o_ref,
                    buf, sem, *, alpha):
    D = D_MODEL
    tc = x_ref.shape[0]

    def row_copy(p, slot, t):
        return pltpu.make_async_copy(ys_ref.at[pl.ds(p, 1)], buf.at[slot, pl.ds(t, 1)], sem)

    def issue(t, carry):
        row_copy(p0_ref[0, 0, t], 0, t).start()
        row_copy(p1_ref[0, 0, t], 1, t).start()
        return carry

    lax.fori_loop(0, tc, issue, 0, unroll=8)

    def drain(t, carry):
        row_copy(0, 0, 0).wait()
        row_copy(0, 0, 0).wait()
        return carry

    lax.fori_loop(0, tc, drain, 0, unroll=8)

    wt = rt_ref[...].T
    w1 = wt[:, 2:3]
    w2 = wt[:, 3:4]
    lo1, hi1 = _unpack_halves(buf[0])
    lo2, hi2 = _unpack_halves(buf[1])
    ffn = jnp.concatenate([w1 * lo1 + w2 * lo2, w1 * hi1 + w2 * hi2], axis=1)
    gt = mod_ref[:, 5 * D:6 * D]
    y = alpha * x_ref[...] + (1.0 + gt) * ffn
    o_ref[...] = _layer_norm_rows(y, g_ref[...], b_ref[...])


def _combine(pos0, pos1, ys, x1, mod_l, rt, ln_g, ln_b, alpha):
    B, S, D = x1.shape
    tc = min(COMBINE_TILE, S)
    nt = S // tc
    blk = pl.BlockSpec((1, 1, tc), lambda b, i: (b * nt + i, 0, 0), memory_space=pltpu.SMEM)
    row = lambda b, i: (b, i, 0)
    full2 = lambda a: pl.BlockSpec(a.shape, lambda b, i: (0,) * a.ndim)
    return pl.pallas_call(
        functools.partial(_combine_kernel, alpha=alpha),
        grid=(B, nt),
        in_specs=[
            blk, blk,
            pl.BlockSpec(memory_space=pl.ANY),
            pl.BlockSpec((None, tc, D), row),
            pl.BlockSpec((None, 1, 6 * D), lambda b, i: (b, 0, 0)),
            pl.BlockSpec((None, 8, tc), lambda b, i: (b, 0, i)),
            full2(ln_g), full2(ln_b),
        ],
        out_specs=pl.BlockSpec((None, tc, D), row),
        out_shape=jax.ShapeDtypeStruct((B, S, D), F32),
        scratch_shapes=[pltpu.VMEM((2, tc, HALF), U32), pltpu.SemaphoreType.DMA(())],
        compiler_params=_cparams(("arbitrary", "arbitrary")),
        name="moe_combine_ln",
    )(pos0.reshape(B * nt, 1, tc), pos1.reshape(B * nt, 1, tc), ys, x1, mod_l, rt, ln_g, ln_b)


def _rope_tables(S):
    inv = 1.0 / (ROPE_THETA ** (jnp.arange(0, QK_DIM, 2, dtype=F32) / QK_DIM))
    ang = jnp.arange(S, dtype=F32)[:, None] * inv[None, :]
    cos, sin = jnp.cos(ang), jnp.sin(ang)
    half = QK_DIM // 2
    first = (jnp.arange(LANES) % QK_DIM) < half
    cs = jnp.tile(cos, (1, LANES // half))
    sn = jnp.tile(sin, (1, LANES // half))
    sa = jnp.where(first[None, :], -sn, 0.0)
    sb = jnp.where(first[None, :], 0.0, sn)
    return cs, sa, sb


def _router_matrix(w_group, b_group, w_router, b_router):
    D = w_group.shape[0]
    w = jnp.zeros((D, LANES), F32)
    w = w.at[:, 0:N_GROUPS].set(w_group)
    wr = jnp.transpose(w_router, (1, 0, 2)).reshape(D, N_EXPERTS)
    w = w.at[:, ROUTE_LANE0:ROUTE_LANE0 + N_EXPERTS].set(wr)
    b = jnp.zeros((1, LANES), F32)
    b = b.at[0, 0:N_GROUPS].set(b_group)
    b = b.at[0, ROUTE_LANE0:ROUTE_LANE0 + N_EXPERTS].set(b_router.reshape(N_EXPERTS))
    hi = w.astype(BF16)
    lo = (w - hi.astype(F32)).astype(BF16)
    return hi, lo, b


def _routing_tables(rt, cnt, n_tiles):
    B, _, S = rt.shape
    tm = EXPERT_TILE
    e = rt[:, 0:2, :].astype(I32)
    rank = rt[:, 4:6, :].astype(I32)
    counts = cnt[0, ROUTE_LANE0:ROUTE_LANE0 + N_EXPERTS].astype(I32)
    tiles = (counts + tm - 1) // tm
    tile_end = jnp.cumsum(tiles)
    tile_start = tile_end - tiles
    onehot = e[..., None] == jnp.arange(N_EXPERTS, dtype=I32)
    start = jnp.sum(jnp.where(onehot, tile_start * tm, 0), axis=-1)
    pos = start + rank
    pos0 = pos[:, 0, :].reshape(B * S)
    pos1 = pos[:, 1, :].reshape(B * S)
    tile_ids = jnp.arange(n_tiles, dtype=I32)
    tile_expert = jnp.minimum(jnp.sum(tile_ids[:, None] >= tile_end[None, :], axis=1), N_EXPERTS - 1).astype(I32)
    n_valid = tile_end[-1:].astype(I32)
    return pos0, pos1, tile_expert, n_valid


def kernel(x, c, w_ada, b_ada, w_in, lambda_q1, lambda_k1, lambda_q2, lambda_k2, subln_g, sg_ln_g, sg_ln_b, w_spatial, b_spatial, w_out, ln1_g, ln1_b, w_group, b_group, w_router, b_router, w_gate, w_up, w_down, ln2_g, ln2_b):
    B, S, D = x.shape
    depth = w_in.shape[0]
    T = B * S
    alpha = (2.0 * depth) ** 0.25
    n_tiles = (2 * T) // EXPERT_TILE + N_EXPERTS
    n_rows = n_tiles * EXPERT_TILE

    mod = _ada(c, w_ada, b_ada)
    tabs = _rope_tables(S)
    for l in range(depth):
        mod_l = mod[l].reshape(B, 1, 6 * D)
        lam_init = 0.8 - 0.6 * math.exp(-0.3 * l)

        qT, k, vTb, zu, zvn = _inproj(x, mod_l, w_in[l].astype(BF16), tabs,
                                      sg_ln_g[l].reshape(1, SG_WIDTH), sg_ln_b[l].reshape(1, SG_WIDTH))
        lams = (lambda_q1[l].reshape(1, QK_DIM), lambda_k1[l].reshape(1, QK_DIM),
                lambda_q2[l].reshape(1, QK_DIM), lambda_k2[l].reshape(1, QK_DIM))
        att = _attention(qT, k, vTb, lams, subln_g[l].reshape(V_DIM, 1), lam_init)
        b_sp_full = jnp.broadcast_to(b_spatial[l][:, :, None], (N_SG, CHUNK, SG_DIM))
        x1 = _mix(x, mod_l, att, zu, zvn, w_spatial[l], b_sp_full, w_out[l].astype(BF16),
                  ln1_g[l].reshape(1, D), ln1_b[l].reshape(1, D), alpha)

        wr_hi, wr_lo, br = _router_matrix(w_group[l], b_group[l], w_router[l], b_router[l])
        hp, rt, cnt = _route(x1, mod_l, wr_hi, wr_lo, br)
        pos0, pos1, tile_expert, n_valid = _routing_tables(rt, cnt, n_tiles)
        xs = _dispatch(pos0, pos1, hp.reshape(T, HALF), n_rows)
        ys = _experts(tile_expert, n_valid, xs, w_gate[l].astype(BF16), w_up[l].astype(BF16),
                      w_down[l].astype(BF16))
        x = _combine(pos0, pos1, ys, x1, mod_l, rt, ln2_g[l].reshape(1, D), ln2_b[l].reshape(1, D), alpha)
    return x
```

```python
import functools
import math

import jax
import jax.numpy as jnp
from jax import lax
from jax.experimental import pallas as pl
from jax.experimental.pallas import tpu as pltpu

F32 = jnp.float32
BF16 = jnp.bfloat16
U32 = jnp.uint32
I32 = jnp.int32

D_MODEL = 1024
N_HEADS = 4
QK_DIM = 64
V_DIM = 128
HEAD_COLS = 2 * QK_DIM
QK_COLS = N_HEADS * HEAD_COLS
DIFF_WIDTH = N_HEADS * V_DIM
N_SG = 4
SG_DIM = 128
SG_WIDTH = N_SG * SG_DIM
CHUNK = 128
N_GROUPS = 4
EXP_PER_GROUP = 8
N_EXPERTS = N_GROUPS * EXP_PER_GROUP
D_EXPERT = 512
ROPE_THETA = 10000.0
LN_EPS = 1e-5
LANES = 128
HALF = D_MODEL // 2

ROW_TILE = 512
ATT_TILE = 512
EXPERT_TILE = 512
DISPATCH_BLOCK = 2048
COMBINE_TILE = 256
ROUTE_LANE0 = 8
VMEM_LIMIT = 48 * 1024 * 1024

LOG2E = 1.4426950408889634
Q_SCALE = (QK_DIM ** -0.5) * LOG2E


def _cparams(sem):
    return pltpu.CompilerParams(dimension_semantics=sem, vmem_limit_bytes=VMEM_LIMIT)


def _layer_norm_rows(y, g, b):
    mu = jnp.mean(y, axis=-1, keepdims=True)
    yc = y - mu
    var = jnp.mean(yc * yc, axis=-1, keepdims=True)
    return yc * lax.rsqrt(var + LN_EPS) * g + b


def _gelu(x):
    return 0.5 * x * (1.0 + lax.erf(x * (2.0 ** -0.5)))


def _pack_halves(y):
    lo = pltpu.bitcast(y[:, :HALF].astype(BF16).astype(F32), U32) >> 16
    hi = pltpu.bitcast(y[:, HALF:].astype(BF16).astype(F32), U32) & jnp.uint32(0xFFFF0000)
    return lo | hi


def _unpack_halves(p):
    lo = pltpu.bitcast(p << 16, F32)
    hi = pltpu.bitcast(p & jnp.uint32(0xFFFF0000), F32)
    return lo, hi


def _ada_kernel(c_ref, w_ref, b_ref, o_ref):
    c = c_ref[...]
    sc = c * jax.nn.sigmoid(c)
    o_ref[...] = jnp.dot(sc, w_ref[...], precision=lax.Precision.HIGHEST,
                         preferred_element_type=F32) + b_ref[...]


def _ada(c, w_ada, b_ada):
    L, D, N = w_ada.shape
    B = c.shape[0]
    tn = 1536
    return pl.pallas_call(
        _ada_kernel,
        grid=(L, N // tn),
        in_specs=[
            pl.BlockSpec((B, D), lambda l, j: (0, 0)),
            pl.BlockSpec((None, D, tn), lambda l, j: (l, 0, j)),
            pl.BlockSpec((None, 1, tn), lambda l, j: (l, 0, j)),
        ],
        out_specs=pl.BlockSpec((None, B, tn), lambda l, j: (l, 0, j)),
        out_shape=jax.ShapeDtypeStruct((L, B, N), F32),
        compiler_params=_cparams(("arbitrary", "arbitrary")),
        name="ada",
    )(c, w_ada, b_ada.reshape(L, 1, N))


def _inproj_kernel(x_ref, mod_ref, w_ref, cs_ref, sa_ref, sb_ref, lng_ref, lnb_ref,
                   qT_ref, k_ref, vT_ref, zu_ref, zvn_ref):
    D = D_MODEL
    x = x_ref[...]
    sh = mod_ref[:, 0:D]
    sc = mod_ref[:, D:2 * D]
    h = (x * (1.0 + sc) + sh).astype(BF16)
    cs = cs_ref[...]
    sa = sa_ref[...]
    sb = sb_ref[...]

    def rope(t):
        return t * cs + pltpu.roll(t, 96, 1) * sa + pltpu.roll(t, 32, 1) * sb

    q = jnp.dot(h, w_ref[:, 0:QK_COLS], preferred_element_type=F32)
    for j in range(N_HEADS):
        sl = slice(j * LANES, (j + 1) * LANES)
        qT_ref[sl, :] = (rope(q[:, sl]) * Q_SCALE).T.astype(BF16)
    k = jnp.dot(h, w_ref[:, QK_COLS:2 * QK_COLS], preferred_element_type=F32)
    for j in range(N_HEADS):
        sl = slice(j * LANES, (j + 1) * LANES)
        k_ref[:, sl] = rope(k[:, sl]).astype(BF16)
    c0 = 2 * QK_COLS
    v = jnp.dot(h, w_ref[:, c0:c0 + DIFF_WIDTH], preferred_element_type=F32)
    vT_ref[...] = v.T.astype(BF16)
    c0 += DIFF_WIDTH
    u = jnp.dot(h, w_ref[:, c0:c0 + SG_WIDTH], preferred_element_type=F32)
    zu_ref[...] = _gelu(u).astype(BF16)
    c0 += SG_WIDTH
    z = _gelu(jnp.dot(h, w_ref[:, c0:c0 + SG_WIDTH], preferred_element_type=F32))
    for g in range(N_SG):
        sl = slice(g * SG_DIM, (g + 1) * SG_DIM)
        zvn_ref[:, sl] = _layer_norm_rows(z[:, sl], lng_ref[:, sl], lnb_ref[:, sl]).astype(BF16)


def _inproj(x, mod_l, w_in_b, rope_tabs, ln_g, ln_b):
    B, S, D = x.shape
    tm = min(ROW_TILE, S)
    nt = S // tm
    cs, sa, sb = rope_tabs
    row = lambda b, i: (b, i, 0)
    tab = pl.BlockSpec((tm, LANES), lambda b, i: (i, 0))
    return pl.pallas_call(
        _inproj_kernel,
        grid=(B, nt),
        in_specs=[
            pl.BlockSpec((None, tm, D), row),
            pl.BlockSpec((None, 1, 6 * D), lambda b, i: (b, 0, 0)),
            pl.BlockSpec(w_in_b.shape, lambda b, i: (0, 0)),
            tab, tab, tab,
            pl.BlockSpec((1, SG_WIDTH), lambda b, i: (0, 0)),
            pl.BlockSpec((1, SG_WIDTH), lambda b, i: (0, 0)),
        ],
        out_specs=[
            pl.BlockSpec((None, QK_COLS, tm), lambda b, i: (b, 0, i)),
            pl.BlockSpec((None, tm, QK_COLS), row),
            pl.BlockSpec((None, None, DIFF_WIDTH, tm), lambda b, i: (b, i, 0, 0)),
            pl.BlockSpec((None, tm, SG_WIDTH), row),
            pl.BlockSpec((None, tm, SG_WIDTH), row),
        ],
        out_shape=[
            jax.ShapeDtypeStruct((B, QK_COLS, S), BF16),
            jax.ShapeDtypeStruct((B, S, QK_COLS), BF16),
            jax.ShapeDtypeStruct((B, nt, DIFF_WIDTH, tm), BF16),
            jax.ShapeDtypeStruct((B, S, SG_WIDTH), BF16),
            jax.ShapeDtypeStruct((B, S, SG_WIDTH), BF16),
        ],
        compiler_params=_cparams(("arbitrary", "arbitrary")),
        name="inproj",
    )(x, mod_l, w_in_b, cs, sa, sb, ln_g, ln_b)


def _attn_kernel(lq1_ref, lk1_ref, lq2_ref, lk2_ref, g_ref, qT_ref, k_ref, vT_ref, o_ref,
                 m_scr, l_scr, acc_scr, *, lam_init):
    tq = qT_ref.shape[1]
    tk = k_ref.shape[0] // vT_ref.shape[0]
    i = pl.program_id(2)
    qT = qT_ref[...]
    rows = lax.broadcasted_iota(I32, qT.shape, 0)
    zero = jnp.zeros_like(qT)
    q_maps = (jnp.where(rows < QK_DIM, qT, zero), jnp.where(rows >= QK_DIM, qT, zero))
    m_scr[...] = jnp.full(m_scr.shape, -jnp.inf, F32)
    l_scr[...] = jnp.zeros(l_scr.shape, F32)
    acc_scr[...] = jnp.zeros(acc_scr.shape, F32)

    def step(j, masked):
        kb = k_ref[pl.ds(pl.multiple_of(j * tk, tk), tk), :]
        vb = vT_ref[j]
        for mp in range(2):
            s = jnp.dot(kb, q_maps[mp], preferred_element_type=F32)
            if masked:
                kpos = lax.broadcasted_iota(I32, s.shape, 0)
                qpos = lax.broadcasted_iota(I32, s.shape, 1)
                s = jnp.where(kpos <= qpos, s, -jnp.inf)
            m_old = m_scr[mp]
            m_new = jnp.maximum(m_old, jnp.max(s, axis=0, keepdims=True))
            alpha = jnp.exp2(m_old - m_new)
            p = jnp.exp2(s - m_new)
            l_scr[mp] = alpha * l_scr[mp] + jnp.sum(p, axis=0, keepdims=True)
            acc_scr[mp] = alpha * acc_scr[mp] + jnp.dot(vb, p.astype(BF16), preferred_element_type=F32)
            m_scr[mp] = m_new

    def full_step(j, carry):
        step(j, False)
        return carry

    lax.fori_loop(0, i, full_step, 0)
    step(i, True)

    lam = (jnp.exp(jnp.sum(lq1_ref[...] * lk1_ref[...], axis=1, keepdims=True))
           - jnp.exp(jnp.sum(lq2_ref[...] * lk2_ref[...], axis=1, keepdims=True)) + lam_init)
    out = acc_scr[0] / l_scr[0] - lam * (acc_scr[1] / l_scr[1])
    ms = jnp.mean(out * out, axis=0, keepdims=True)
    y = out * lax.rsqrt(ms + LN_EPS) * g_ref[...] * (1.0 - lam_init)
    o_ref[...] = y.T.astype(BF16)


def _attention(qT, k, vTb, lams, g_col, lam_init):
    B, _, S = qT.shape
    nkv, tk = vTb.shape[1], vTb.shape[3]
    tq = tk
    small = pl.BlockSpec((1, QK_DIM), lambda b, h, i: (0, 0))
    return pl.pallas_call(
        functools.partial(_attn_kernel, lam_init=lam_init),
        grid=(B, N_HEADS, S // tq),
        in_specs=[
            small, small, small, small,
            pl.BlockSpec((V_DIM, 1), lambda b, h, i: (0, 0)),
            pl.BlockSpec((None, HEAD_COLS, tq), lambda b, h, i: (b, h, i)),
            pl.BlockSpec((None, S, HEAD_COLS), lambda b, h, i: (b, 0, h)),
            pl.BlockSpec((None, nkv, V_DIM, tk), lambda b, h, i: (b, 0, h, 0)),
        ],
        out_specs=pl.BlockSpec((None, tq, V_DIM), lambda b, h, i: (b, i, h)),
        out_shape=jax.ShapeDtypeStruct((B, S, DIFF_WIDTH), BF16),
        scratch_shapes=[
            pltpu.VMEM((2, 1, tq), F32),
            pltpu.VMEM((2, 1, tq), F32),
            pltpu.VMEM((2, V_DIM, tq), F32),
        ],
        compiler_params=_cparams(("arbitrary", "arbitrary", "arbitrary")),
        name="diff_attn",
    )(*lams, g_col, qT, k, vTb)


def _mix_kernel(x_ref, mod_ref, att_ref, zu_ref, zvn_ref, wsp_ref, bsp_ref, wo_ref, g_ref, b_ref,
                o_ref, sg_scr, *, alpha):
    D = D_MODEL
    nc = zu_ref.shape[0]
    r = lax.broadcasted_iota(I32, (CHUNK, CHUNK), 0)
    c = lax.broadcasted_iota(I32, (CHUNK, CHUNK), 1)
    causal = r >= c
    for g in range(N_SG):
        sl = slice(g * SG_DIM, (g + 1) * SG_DIM)
        w = jnp.where(causal, wsp_ref[g], 0.0).astype(BF16)
        z = jnp.concatenate([zvn_ref[n, :, sl] for n in range(nc)], axis=1)
        mixed = jnp.dot(w, z, preferred_element_type=F32)
        for n in range(nc):
            gate = mixed[:, n * SG_DIM:(n + 1) * SG_DIM] + bsp_ref[g]
            sg_scr[n * CHUNK:(n + 1) * CHUNK, sl] = (zu_ref[n, :, sl].astype(F32) * gate).astype(BF16)
    mix = (jnp.dot(att_ref[...], wo_ref[0:DIFF_WIDTH, :], preferred_element_type=F32)
           + jnp.dot(sg_scr[...], wo_ref[DIFF_WIDTH:, :], preferred_element_type=F32))
    gt = mod_ref[:, 2 * D:3 * D]
    y = alpha * x_ref[...] + (1.0 + gt) * mix
    o_ref[...] = _layer_norm_rows(y, g_ref[...], b_ref[...])


def _mix(x, mod_l, att, zu, zvn, w_sp, b_sp_full, w_out_b, ln_g, ln_b, alpha):
    B, S, D = x.shape
    tm = min(ROW_TILE, S)
    nc = tm // CHUNK
    row = lambda b, i: (b, i, 0)
    zu4 = zu.reshape(B, S // CHUNK, CHUNK, SG_WIDTH)
    zvn4 = zvn.reshape(B, S // CHUNK, CHUNK, SG_WIDTH)
    chunked = pl.BlockSpec((None, nc, CHUNK, SG_WIDTH), lambda b, i: (b, i, 0, 0))
    full2 = lambda a: pl.BlockSpec(a.shape, lambda b, i: (0,) * a.ndim)
    return pl.pallas_call(
        functools.partial(_mix_kernel, alpha=alpha),
        grid=(B, S // tm),
        in_specs=[
            pl.BlockSpec((None, tm, D), row),
            pl.BlockSpec((None, 1, 6 * D), lambda b, i: (b, 0, 0)),
            pl.BlockSpec((None, tm, DIFF_WIDTH), row),
            chunked, chunked,
            full2(w_sp), full2(b_sp_full), full2(w_out_b), full2(ln_g), full2(ln_b),
        ],
        out_specs=pl.BlockSpec((None, tm, D), row),
        out_shape=jax.ShapeDtypeStruct((B, S, D), F32),
        scratch_shapes=[pltpu.VMEM((tm, SG_WIDTH), BF16)],
        compiler_params=_cparams(("arbitrary", "arbitrary")),
        name="sgate_outproj_ln",
    )(x, mod_l, att, zu4, zvn4, w_sp, b_sp_full, w_out_b, ln_g, ln_b)


def _route_kernel(x_ref, mod_ref, wr_hi_ref, wr_lo_ref, br_ref, hp_ref, rt_ref, cnt_ref, run_scr):
    D = D_MODEL
    tm = x_ref.shape[0]

    @pl.when((pl.program_id(0) == 0) & (pl.program_id(1) == 0))
    def _():
        run_scr[...] = jnp.zeros(run_scr.shape, F32)

    sh = mod_ref[:, 3 * D:4 * D]
    sc = mod_ref[:, 4 * D:5 * D]
    h = x_ref[...] * (1.0 + sc) + sh
    hp_ref[...] = _pack_halves(h)
    h_hi = h.astype(BF16)
    h_lo = (h - h_hi.astype(F32)).astype(BF16)
    logit = (jnp.dot(h_hi, wr_hi_ref[...], preferred_element_type=F32)
             + jnp.dot(h_lo, wr_hi_ref[...], preferred_element_type=F32)
             + jnp.dot(h_hi, wr_lo_ref[...], preferred_element_type=F32)) + br_ref[...]
    lane = lax.broadcasted_iota(I32, logit.shape, 1).astype(F32)
    neg = -jnp.inf
    big = float(LANES)

    def first_argmax(v):
        mx = jnp.max(v, axis=1, keepdims=True)
        idx = jnp.min(jnp.where(v == mx, lane, big), axis=1, keepdims=True)
        return mx, idx

    in_grp = lane < N_GROUPS
    gmax, gidx = first_argmax(jnp.where(in_grp, logit, neg))
    g_p = 1.0 / jnp.sum(jnp.where(in_grp, jnp.exp(logit - gmax), 0.0), axis=1, keepdims=True)
    lo_lane = ROUTE_LANE0 + EXP_PER_GROUP * gidx
    sel = jnp.where((lane >= lo_lane) & (lane < lo_lane + EXP_PER_GROUP), logit, neg)
    v1, i1 = first_argmax(sel)
    v2, i2 = first_argmax(jnp.where(lane == i1, neg, sel))
    t = jnp.exp(v2 - v1)
    w1 = g_p / (1.0 + t)
    w2 = g_p * t / (1.0 + t)

    hot1 = lane == i1
    hot2 = lane == i2
    twohot = jnp.where(hot1 | hot2, 1.0, 0.0)
    r = lax.broadcasted_iota(I32, (tm, tm), 0)
    c = lax.broadcasted_iota(I32, (tm, tm), 1)
    strict = jnp.where(r > c, 1.0, 0.0).astype(BF16)
    before = jnp.dot(strict, twohot.astype(BF16), preferred_element_type=F32) + run_scr[...]
    r1 = jnp.sum(jnp.where(hot1, before, 0.0), axis=1, keepdims=True)
    r2 = jnp.sum(jnp.where(hot2, before, 0.0), axis=1, keepdims=True)
    run_scr[...] = run_scr[...] + jnp.sum(twohot, axis=0, keepdims=True)
    cnt_ref[...] = run_scr[...]

    fields = (i1 - ROUTE_LANE0, i2 - ROUTE_LANE0, w1, w2, r1, r2)
    slab = jnp.zeros(logit.shape, F32)
    for n, f in enumerate(fields):
        slab = jnp.where(lane == float(n), f, slab)
    rt_ref[...] = slab.T[0:8, :]


def _route(x1, mod_l, wr_hi, wr_lo, br):
    B, S, D = x1.shape
    tm = min(ROW_TILE, S)
    row = lambda b, i: (b, i, 0)
    full2 = lambda a: pl.BlockSpec(a.shape, lambda b, i: (0,) * a.ndim)
    return pl.pallas_call(
        _route_kernel,
        grid=(B, S // tm),
        in_specs=[
            pl.BlockSpec((None, tm, D), row),
            pl.BlockSpec((None, 1, 6 * D), lambda b, i: (b, 0, 0)),
            full2(wr_hi), full2(wr_lo), full2(br),
        ],
        out_specs=[
            pl.BlockSpec((None, tm, HALF), row),
            pl.BlockSpec((None, 8, tm), lambda b, i: (b, 0, i)),
            pl.BlockSpec((1, LANES), lambda b, i: (0, 0)),
        ],
        out_shape=[
            jax.ShapeDtypeStruct((B, S, HALF), U32),
            jax.ShapeDtypeStruct((B, 8, S), F32),
            jax.ShapeDtypeStruct((1, LANES), F32),
        ],
        scratch_shapes=[pltpu.VMEM((1, LANES), F32)],
        compiler_params=_cparams(("arbitrary", "arbitrary")),
        name="moe_route",
    )(x1, mod_l, wr_hi, wr_lo, br)


def _dispatch_kernel(p0_ref, p1_ref, hp_ref, xs_in_ref, xs_ref, sem):
    del xs_in_ref
    nb = p0_ref.shape[2]

    def row_copy(t, p):
        return pltpu.make_async_copy(hp_ref.at[pl.ds(t, 1)], xs_ref.at[pl.ds(p, 1)], sem)

    def issue(t, carry):
        row_copy(t, p0_ref[0, 0, t]).start()
        row_copy(t, p1_ref[0, 0, t]).start()
        return carry

    lax.fori_loop(0, nb, issue, 0, unroll=8)

    def drain(t, carry):
        row_copy(0, 0).wait()
        row_copy(0, 0).wait()
        return carry

    lax.fori_loop(0, nb, drain, 0, unroll=8)


def _dispatch(pos0, pos1, hp, n_rows):
    T = hp.shape[0]
    nb = min(DISPATCH_BLOCK, T)
    blk = pl.BlockSpec((1, 1, nb), lambda i: (i, 0, 0), memory_space=pltpu.SMEM)
    xs0 = jnp.zeros((n_rows, HALF), U32)
    return pl.pallas_call(
        _dispatch_kernel,
        grid=(T // nb,),
        in_specs=[blk, blk, pl.BlockSpec((nb, HALF), lambda i: (i, 0)), pl.BlockSpec(memory_space=pl.ANY)],
        out_specs=pl.BlockSpec(memory_space=pl.ANY),
        out_shape=jax.ShapeDtypeStruct((n_rows, HALF), U32),
        scratch_shapes=[pltpu.SemaphoreType.DMA(())],
        input_output_aliases={3: 0},
        compiler_params=_cparams(("arbitrary",)),
        name="moe_dispatch",
    )(pos0.reshape(T // nb, 1, nb), pos1.reshape(T // nb, 1, nb), hp, xs0)


def _expert_kernel(te_ref, nv_ref, xs_ref, wg_ref, wu_ref, wd_ref, ys_ref):
    del te_ref

    @pl.when(pl.program_id(0) < nv_ref[0])
    def _():
        lo, hi = _unpack_halves(xs_ref[...])
        lo = lo.astype(BF16)
        hi = hi.astype(BF16)
        g = (jnp.dot(lo, wg_ref[0:HALF, :], preferred_element_type=F32)
             + jnp.dot(hi, wg_ref[HALF:, :], preferred_element_type=F32))
        u = (jnp.dot(lo, wu_ref[0:HALF, :], preferred_element_type=F32)
             + jnp.dot(hi, wu_ref[HALF:, :], preferred_element_type=F32))
        a = (g * jax.nn.sigmoid(g) * u).astype(BF16)
        ys_ref[...] = _pack_halves(jnp.dot(a, wd_ref[...], preferred_element_type=F32))


def _experts(tile_expert, n_valid, xs, wg_b, wu_b, wd_b):
    n_rows = xs.shape[0]
    tm = EXPERT_TILE
    nt = n_rows // tm
    rows = lambda n, te, nv: (jnp.minimum(n, nv[0] - 1), 0)
    wsel = lambda n, te, nv: (te[n], 0, 0)
    return pl.pallas_call(
        _expert_kernel,
        grid_spec=pltpu.PrefetchScalarGridSpec(
            num_scalar_prefetch=2,
            grid=(nt,),
            in_specs=[
                pl.BlockSpec((tm, HALF), rows),
                pl.BlockSpec((None, D_MODEL, D_EXPERT), wsel),
                pl.BlockSpec((None, D_MODEL, D_EXPERT), wsel),
                pl.BlockSpec((None, D_EXPERT, D_MODEL), wsel),
            ],
            out_specs=pl.BlockSpec((tm, HALF), rows),
        ),
        out_shape=jax.ShapeDtypeStruct((n_rows, HALF), U32),
        compiler_params=_cparams(("arbitrary",)),
        name="moe_experts",
    )(tile_expert, n_valid, xs, wg_b, wu_b, wd_b)


def _combine_kernel(p0_ref, p1_ref, ys_ref, x_ref, mod_ref, rt_ref, g_ref, b_ref, o_ref,
                    buf, sem, *, alpha):
    D = D_MODEL
    tc = x_ref.shape[0]

    def row_copy(p, slot, t):
        return pltpu.make_async_copy(ys_ref.at[pl.ds(p, 1)], buf.at[slot, pl.ds(t, 1)], sem)

    def issue(t, carry):
        row_copy(p0_ref[0, 0, t], 0, t).start()
        row_copy(p1_ref[0, 0, t], 1, t).start()
        return carry

    lax.fori_loop(0, tc, issue, 0, unroll=8)

    def drain(t, carry):
        row_copy(0, 0, 0).wait()
        row_copy(0, 0, 0).wait()
        return carry

    lax.fori_loop(0, tc, drain, 0, unroll=8)

    wt = rt_ref[...].T
    w1 = wt[:, 2:3]
    w2 = wt[:, 3:4]
    lo1, hi1 = _unpack_halves(buf[0])
    lo2, hi2 = _unpack_halves(buf[1])
    ffn = jnp.concatenate([w1 * lo1 + w2 * lo2, w1 * hi1 + w2 * hi2], axis=1)
    gt = mod_ref[:, 5 * D:6 * D]
    y = alpha * x_ref[...] + (1.0 + gt) * ffn
    o_ref[...] = _layer_norm_rows(y, g_ref[...], b_ref[...])


def _combine(pos0, pos1, ys, x1, mod_l, rt, ln_g, ln_b, alpha):
    B, S, D = x1.shape
    tc = min(COMBINE_TILE, S)
    nt = S // tc
    blk = pl.BlockSpec((1, 1, tc), lambda b, i: (b * nt + i, 0, 0), memory_space=pltpu.SMEM)
    row = lambda b, i: (b, i, 0)
    full2 = lambda a: pl.BlockSpec(a.shape, lambda b, i: (0,) * a.ndim)
    return pl.pallas_call(
        functools.partial(_combine_kernel, alpha=alpha),
        grid=(B, nt),
        in_specs=[
            blk, blk,
            pl.BlockSpec(memory_space=pl.ANY),
            pl.BlockSpec((None, tc, D), row),
            pl.BlockSpec((None, 1, 6 * D), lambda b, i: (b, 0, 0)),
            pl.BlockSpec((None, 8, tc), lambda b, i: (b, 0, i)),
            full2(ln_g), full2(ln_b),
        ],
        out_specs=pl.BlockSpec((None, tc, D), row),
        out_shape=jax.ShapeDtypeStruct((B, S, D), F32),
        scratch_shapes=[pltpu.VMEM((2, tc, HALF), U32), pltpu.SemaphoreType.DMA(())],
        compiler_params=_cparams(("arbitrary", "arbitrary")),
        name="moe_combine_ln",
    )(pos0.reshape(B * nt, 1, tc), pos1.reshape(B * nt, 1, tc), ys, x1, mod_l, rt, ln_g, ln_b)


def _rope_tables(S):
    inv = 1.0 / (ROPE_THETA ** (jnp.arange(0, QK_DIM, 2, dtype=F32) / QK_DIM))
    ang = jnp.arange(S, dtype=F32)[:, None] * inv[None, :]
    cos, sin = jnp.cos(ang), jnp.sin(ang)
    half = QK_DIM // 2
    first = (jnp.arange(LANES) % QK_DIM) < half
    cs = jnp.tile(cos, (1, LANES // half))
    sn = jnp.tile(sin, (1, LANES // half))
    sa = jnp.where(first[None, :], -sn, 0.0)
    sb = jnp.where(first[None, :], 0.0, sn)
    return cs, sa, sb


def _router_matrix(w_group, b_group, w_router, b_router):
    D = w_group.shape[0]
    w = jnp.zeros((D, LANES), F32)
    w = w.at[:, 0:N_GROUPS].set(w_group)
    wr = jnp.transpose(w_router, (1, 0, 2)).reshape(D, N_EXPERTS)
    w = w.at[:, ROUTE_LANE0:ROUTE_LANE0 + N_EXPERTS].set(wr)
    b = jnp.zeros((1, LANES), F32)
    b = b.at[0, 0:N_GROUPS].set(b_group)
    b = b.at[0, ROUTE_LANE0:ROUTE_LANE0 + N_EXPERTS].set(b_router.reshape(N_EXPERTS))
    hi = w.astype(BF16)
    lo = (w - hi.astype(F32)).astype(BF16)
    return hi, lo, b


def _routing_tables(rt, cnt, n_tiles):
    B, _, S = rt.shape
    tm = EXPERT_TILE
    e = rt[:, 0:2, :].astype(I32)
    rank = rt[:, 4:6, :].astype(I32)
    counts = cnt[0, ROUTE_LANE0:ROUTE_LANE0 + N_EXPERTS].astype(I32)
    tiles = (counts + tm - 1) // tm
    tile_end = jnp.cumsum(tiles)
    tile_start = tile_end - tiles
    onehot = e[..., None] == jnp.arange(N_EXPERTS, dtype=I32)
    start = jnp.sum(jnp.where(onehot, tile_start * tm, 0), axis=-1)
    pos = start + rank
    pos0 = pos[:, 0, :].reshape(B * S)
    pos1 = pos[:, 1, :].reshape(B * S)
    tile_ids = jnp.arange(n_tiles, dtype=I32)
    tile_expert = jnp.minimum(jnp.sum(tile_ids[:, None] >= tile_end[None, :], axis=1), N_EXPERTS - 1).astype(I32)
    n_valid = tile_end[-1:].astype(I32)
    return pos0, pos1, tile_expert, n_valid


def kernel(x, c, w_ada, b_ada, w_in, lambda_q1, lambda_k1, lambda_q2, lambda_k2, subln_g, sg_ln_g, sg_ln_b, w_spatial, b_spatial, w_out, ln1_g, ln1_b, w_group, b_group, w_router, b_router, w_gate, w_up, w_down, ln2_g, ln2_b):
    B, S, D = x.shape
    depth = w_in.shape[0]
    T = B * S
    alpha = (2.0 * depth) ** 0.25
    n_tiles = (2 * T) // EXPERT_TILE + N_EXPERTS
    n_rows = n_tiles * EXPERT_TILE

    mod = _ada(c, w_ada, b_ada)
    tabs = _rope_tables(S)
    for l in range(depth):
        mod_l = mod[l].reshape(B, 1, 6 * D)
        lam_init = 0.8 - 0.6 * math.exp(-0.3 * l)

        qT, k, vTb, zu, zvn = _inproj(x, mod_l, w_in[l].astype(BF16), tabs,
                                      sg_ln_g[l].reshape(1, SG_WIDTH), sg_ln_b[l].reshape(1, SG_WIDTH))
        lams = (lambda_q1[l].reshape(1, QK_DIM), lambda_k1[l].reshape(1, QK_DIM),
                lambda_q2[l].reshape(1, QK_DIM), lambda_k2[l].reshape(1, QK_DIM))
        att = _attention(qT, k, vTb, lams, subln_g[l].reshape(V_DIM, 1), lam_init)
        b_sp_full = jnp.broadcast_to(b_spatial[l][:, :, None], (N_SG, CHUNK, SG_DIM))
        x1 = _mix(x, mod_l, att, zu, zvn, w_spatial[l], b_sp_full, w_out[l].astype(BF16),
                  ln1_g[l].reshape(1, D), ln1_b[l].reshape(1, D), alpha)

        wr_hi, wr_lo, br = _router_matrix(w_group[l], b_group[l], w_router[l], b_router[l])
        hp, rt, cnt = _route(x1, mod_l, wr_hi, wr_lo, br)
        pos0, pos1, tile_expert, n_valid = _routing_tables(rt, cnt, n_tiles)
        xs = _dispatch(pos0, pos1, hp.reshape(T, HALF), n_rows)
        ys = _experts(tile_expert, n_valid, xs, w_gate[l].astype(BF16), w_up[l].astype(BF16),
                      w_down[l].astype(BF16))
        x = _combine(pos0, pos1, ys, x1, mod_l, rt, ln2_g[l].reshape(1, D), ln2_b[l].reshape(1, D), alpha)
    return x
```

```python
import functools
import math

import jax
import jax.numpy as jnp
from jax import lax
from jax.experimental import pallas as pl
from jax.experimental.pallas import tpu as pltpu

F32 = jnp.float32
BF16 = jnp.bfloat16
U32 = jnp.uint32
I32 = jnp.int32

D_MODEL = 1024
N_HEADS = 4
QK_DIM = 64
V_DIM = 128
HEAD_COLS = 2 * QK_DIM
QK_COLS = N_HEADS * HEAD_COLS
DIFF_WIDTH = N_HEADS * V_DIM
N_SG = 4
SG_DIM = 128
SG_WIDTH = N_SG * SG_DIM
CHUNK = 128
N_GROUPS = 4
EXP_PER_GROUP = 8
N_EXPERTS = N_GROUPS * EXP_PER_GROUP
D_EXPERT = 512
ROPE_THETA = 10000.0
LN_EPS = 1e-5
LANES = 128
HALF = D_MODEL // 2

ROW_TILE = 512
ATT_TILE = 512
EXPERT_TILE = 512
DISPATCH_BLOCK = 2048
COMBINE_TILE = 256
ROUTE_LANE0 = 8
VMEM_LIMIT = 48 * 1024 * 1024

LOG2E = 1.4426950408889634
Q_SCALE = (QK_DIM ** -0.5) * LOG2E


def _cparams(sem):
    return pltpu.CompilerParams(dimension_semantics=sem, vmem_limit_bytes=VMEM_LIMIT)


def _layer_norm_rows(y, g, b):
    mu = jnp.mean(y, axis=-1, keepdims=True)
    yc = y - mu
    var = jnp.mean(yc * yc, axis=-1, keepdims=True)
    return yc * lax.rsqrt(var + LN_EPS) * g + b


def _gelu(x):
    return 0.5 * x * (1.0 + lax.erf(x * (2.0 ** -0.5)))


def _pack_halves(y):
    lo = pltpu.bitcast(y[:, :HALF].astype(BF16).astype(F32), U32) >> 16
    hi = pltpu.bitcast(y[:, HALF:].astype(BF16).astype(F32), U32) & jnp.uint32(0xFFFF0000)
    return lo | hi


def _unpack_halves(p):
    lo = pltpu.bitcast(p << 16, F32)
    hi = pltpu.bitcast(p & jnp.uint32(0xFFFF0000), F32)
    return lo, hi


def _ada_kernel(c_ref, w_ref, b_ref, o_ref):
    c = c_ref[...]
    sc = c * jax.nn.sigmoid(c)
    o_ref[...] = jnp.dot(sc, w_ref[...], precision=lax.Precision.HIGHEST,
                         preferred_element_type=F32) + b_ref[...]


def _ada(c, w_ada, b_ada):
    L, D, N = w_ada.shape
    B = c.shape[0]
    tn = 1536
    return pl.pallas_call(
        _ada_kernel,
        grid=(L, N // tn),
        in_specs=[
            pl.BlockSpec((B, D), lambda l, j: (0, 0)),
            pl.BlockSpec((None, D, tn), lambda l, j: (l, 0, j)),
            pl.BlockSpec((None, 1, tn), lambda l, j: (l, 0, j)),
        ],
        out_specs=pl.BlockSpec((None, B, tn), lambda l, j: (l, 0, j)),
        out_shape=jax.ShapeDtypeStruct((L, B, N), F32),
        compiler_params=_cparams(("arbitrary", "arbitrary")),
        name="ada",
    )(c, w_ada, b_ada.reshape(L, 1, N))


def _inproj_kernel(x_ref, mod_ref, w_ref, cs_ref, sa_ref, sb_ref, lng_ref, lnb_ref,
                   qT_ref, k_ref, vT_ref, zu_ref, zvn_ref):
    D = D_MODEL
    x = x_ref[...]
    sh = mod_ref[:, 0:D]
    sc = mod_ref[:, D:2 * D]
    h = (x * (1.0 + sc) + sh).astype(BF16)
    cs = cs_ref[...]
    sa = sa_ref[...]
    sb = sb_ref[...]

    def rope(t):
        return t * cs + pltpu.roll(t, 96, 1) * sa + pltpu.roll(t, 32, 1) * sb

    q = jnp.dot(h, w_ref[:, 0:QK_COLS], preferred_element_type=F32)
    for j in range(N_HEADS):
        sl = slice(j * LANES, (j + 1) * LANES)
        qT_ref[sl, :] = (rope(q[:, sl]) * Q_SCALE).T.astype(BF16)
    k = jnp.dot(h, w_ref[:, QK_COLS:2 * QK_COLS], preferred_element_type=F32)
    for j in range(N_HEADS):
        sl = slice(j * LANES, (j + 1) * LANES)
        k_ref[:, sl] = rope(k[:, sl]).astype(BF16)
    c0 = 2 * QK_COLS
    v = jnp.dot(h, w_ref[:, c0:c0 + DIFF_WIDTH], preferred_element_type=F32)
    vT_ref[...] = v.T.astype(BF16)
    c0 += DIFF_WIDTH
    u = jnp.dot(h, w_ref[:, c0:c0 + SG_WIDTH], preferred_element_type=F32)
    zu_ref[...] = _gelu(u).astype(BF16)
    c0 += SG_WIDTH
    z = _gelu(jnp.dot(h, w_ref[:, c0:c0 + SG_WIDTH], preferred_element_type=F32))
    for g in range(N_SG):
        sl = slice(g * SG_DIM, (g + 1) * SG_DIM)
        zvn_ref[:, sl] = _layer_norm_rows(z[:, sl], lng_ref[:, sl], lnb_ref[:, sl]).astype(BF16)


def _inproj(x, mod_l, w_in_b, rope_tabs, ln_g, ln_b):
    B, S, D = x.shape
    tm = min(ROW_TILE, S)
    nt = S // tm
    cs, sa, sb = rope_tabs
    row = lambda b, i: (b, i, 0)
    tab = pl.BlockSpec((tm, LANES), lambda b, i: (i, 0))
    return pl.pallas_call(
        _inproj_kernel,
        grid=(B, nt),
        in_specs=[
            pl.BlockSpec((None, tm, D), row),
            pl.BlockSpec((None, 1, 6 * D), lambda b, i: (b, 0, 0)),
            pl.BlockSpec(w_in_b.shape, lambda b, i: (0, 0)),
            tab, tab, tab,
            pl.BlockSpec((1, SG_WIDTH), lambda b, i: (0, 0)),
            pl.BlockSpec((1, SG_WIDTH), lambda b, i: (0, 0)),
        ],
        out_specs=[
            pl.BlockSpec((None, QK_COLS, tm), lambda b, i: (b, 0, i)),
            pl.BlockSpec((None, tm, QK_COLS), row),
            pl.BlockSpec((None, None, DIFF_WIDTH, tm), lambda b, i: (b, i, 0, 0)),
            pl.BlockSpec((None, tm, SG_WIDTH), row),
            pl.BlockSpec((None, tm, SG_WIDTH), row),
        ],
        out_shape=[
            jax.ShapeDtypeStruct((B, QK_COLS, S), BF16),
            jax.ShapeDtypeStruct((B, S, QK_COLS), BF16),
            jax.ShapeDtypeStruct((B, nt, DIFF_WIDTH, tm), BF16),
            jax.ShapeDtypeStruct((B, S, SG_WIDTH), BF16),
            jax.ShapeDtypeStruct((B, S, SG_WIDTH), BF16),
        ],
        compiler_params=_cparams(("arbitrary", "arbitrary")),
        name="inproj",
    )(x, mod_l, w_in_b, cs, sa, sb, ln_g, ln_b)


def _attn_kernel(lq1_ref, lk1_ref, lq2_ref, lk2_ref, g_ref, qT_ref, k_ref, vT_ref, o_ref,
                 q_scr, s_scr, m_scr, l_scr, acc_scr, *, lam_init):
    tq = qT_ref.shape[1]
    tk = k_ref.shape[0] // vT_ref.shape[0]
    i = pl.program_id(2)
    qT = qT_ref[...]
    rows = lax.broadcasted_iota(I32, qT.shape, 0)
    zero = jnp.zeros_like(qT)
    q_scr[0] = jnp.where(rows < QK_DIM, qT, zero)
    q_scr[1] = jnp.where(rows >= QK_DIM, qT, zero)
    m_scr[...] = jnp.full(m_scr.shape, -jnp.inf, F32)
    l_scr[...] = jnp.zeros(l_scr.shape, F32)
    acc_scr[...] = jnp.zeros(acc_scr.shape, F32)

    def scores(j, slot):
        kb = k_ref[pl.ds(pl.multiple_of(j * tk, tk), tk), :]
        for mp in range(2):
            s_scr[slot, mp] = jnp.dot(kb, q_scr[mp], preferred_element_type=F32)

    def softmax_pv(j, slot, masked):
        vb = vT_ref[j]
        for mp in range(2):
            s = s_scr[slot, mp]
            if masked:
                kpos = lax.broadcasted_iota(I32, s.shape, 0)
                qpos = lax.broadcasted_iota(I32, s.shape, 1)
                s = jnp.where(kpos <= qpos, s, -jnp.inf)
            m_old = m_scr[mp]
            m_new = jnp.maximum(m_old, jnp.max(s, axis=0, keepdims=True))
            alpha = jnp.exp2(m_old - m_new)
            p = jnp.exp2(s - m_new)
            l_scr[mp] = alpha * l_scr[mp] + jnp.sum(p, axis=0, keepdims=True)
            acc_scr[mp] = alpha * acc_scr[mp] + jnp.dot(vb, p.astype(BF16), preferred_element_type=F32)
            m_scr[mp] = m_new

    scores(0, 0)

    def body(t, carry):
        for par in range(2):
            @pl.when(t % 2 == par)
            def _():
                scores(t, par)
                softmax_pv(t - 1, 1 - par, False)
        return carry

    lax.fori_loop(1, i + 1, body, 0)
    for par in range(2):
        @pl.when(i % 2 == par)
        def _():
            softmax_pv(i, par, True)

    lam = (jnp.exp(jnp.sum(lq1_ref[...] * lk1_ref[...], axis=1, keepdims=True))
           - jnp.exp(jnp.sum(lq2_ref[...] * lk2_ref[...], axis=1, keepdims=True)) + lam_init)
    out = acc_scr[0] / l_scr[0] - lam * (acc_scr[1] / l_scr[1])
    ms = jnp.mean(out * out, axis=0, keepdims=True)
    y = out * lax.rsqrt(ms + LN_EPS) * g_ref[...] * (1.0 - lam_init)
    o_ref[...] = y.T.astype(BF16)


def _attention(qT, k, vTb, lams, g_col, lam_init):
    B, _, S = qT.shape
    nkv, tk = vTb.shape[1], vTb.shape[3]
    tq = tk
    small = pl.BlockSpec((1, QK_DIM), lambda b, h, i: (0, 0))
    return pl.pallas_call(
        functools.partial(_attn_kernel, lam_init=lam_init),
        grid=(B, N_HEADS, S // tq),
        in_specs=[
            small, small, small, small,
            pl.BlockSpec((V_DIM, 1), lambda b, h, i: (0, 0)),
            pl.BlockSpec((None, HEAD_COLS, tq), lambda b, h, i: (b, h, i)),
            pl.BlockSpec((None, S, HEAD_COLS), lambda b, h, i: (b, 0, h)),
            pl.BlockSpec((None, nkv, V_DIM, tk), lambda b, h, i: (b, 0, h, 0)),
        ],
        out_specs=pl.BlockSpec((None, tq, V_DIM), lambda b, h, i: (b, i, h)),
        out_shape=jax.ShapeDtypeStruct((B, S, DIFF_WIDTH), BF16),
        scratch_shapes=[
            pltpu.VMEM((2, HEAD_COLS, tq), BF16),
            pltpu.VMEM((2, 2, tk, tq), F32),
            pltpu.VMEM((2, 1, tq), F32),
            pltpu.VMEM((2, 1, tq), F32),
            pltpu.VMEM((2, V_DIM, tq), F32),
        ],
        compiler_params=_cparams(("arbitrary", "arbitrary", "arbitrary")),
        name="diff_attn",
    )(*lams, g_col, qT, k, vTb)


def _mix_kernel(x_ref, mod_ref, att_ref, zu_ref, zvn_ref, wsp_ref, bsp_ref, wo_ref, g_ref, b_ref,
                o_ref, sg_scr, *, alpha):
    D = D_MODEL
    nc = zu_ref.shape[0]
    r = lax.broadcasted_iota(I32, (CHUNK, CHUNK), 0)
    c = lax.broadcasted_iota(I32, (CHUNK, CHUNK), 1)
    causal = r >= c
    for g in range(N_SG):
        sl = slice(g * SG_DIM, (g + 1) * SG_DIM)
        w = jnp.where(causal, wsp_ref[g], 0.0).astype(BF16)
        z = jnp.concatenate([zvn_ref[n, :, sl] for n in range(nc)], axis=1)
        mixed = jnp.dot(w, z, preferred_element_type=F32)
        for n in range(nc):
            gate = mixed[:, n * SG_DIM:(n + 1) * SG_DIM] + bsp_ref[g]
            sg_scr[n * CHUNK:(n + 1) * CHUNK, sl] = (zu_ref[n, :, sl].astype(F32) * gate).astype(BF16)
    mix = (jnp.dot(att_ref[...], wo_ref[0:DIFF_WIDTH, :], preferred_element_type=F32)
           + jnp.dot(sg_scr[...], wo_ref[DIFF_WIDTH:, :], preferred_element_type=F32))
    gt = mod_ref[:, 2 * D:3 * D]
    y = alpha * x_ref[...] + (1.0 + gt) * mix
    o_ref[...] = _layer_norm_rows(y, g_ref[...], b_ref[...])


def _mix(x, mod_l, att, zu, zvn, w_sp, b_sp_full, w_out_b, ln_g, ln_b, alpha):
    B, S, D = x.shape
    tm = min(ROW_TILE, S)
    nc = tm // CHUNK
    row = lambda b, i: (b, i, 0)
    zu4 = zu.reshape(B, S // CHUNK, CHUNK, SG_WIDTH)
    zvn4 = zvn.reshape(B, S // CHUNK, CHUNK, SG_WIDTH)
    chunked = pl.BlockSpec((None, nc, CHUNK, SG_WIDTH), lambda b, i: (b, i, 0, 0))
    full2 = lambda a: pl.BlockSpec(a.shape, lambda b, i: (0,) * a.ndim)
    return pl.pallas_call(
        functools.partial(_mix_kernel, alpha=alpha),
        grid=(B, S // tm),
        in_specs=[
            pl.BlockSpec((None, tm, D), row),
            pl.BlockSpec((None, 1, 6 * D), lambda b, i: (b, 0, 0)),
            pl.BlockSpec((None, tm, DIFF_WIDTH), row),
            chunked, chunked,
            full2(w_sp), full2(b_sp_full), full2(w_out_b), full2(ln_g), full2(ln_b),
        ],
        out_specs=pl.BlockSpec((None, tm, D), row),
        out_shape=jax.ShapeDtypeStruct((B, S, D), F32),
        scratch_shapes=[pltpu.VMEM((tm, SG_WIDTH), BF16)],
        compiler_params=_cparams(("arbitrary", "arbitrary")),
        name="sgate_outproj_ln",
    )(x, mod_l, att, zu4, zvn4, w_sp, b_sp_full, w_out_b, ln_g, ln_b)


def _route_kernel(x_ref, mod_ref, wr_hi_ref, wr_lo_ref, br_ref, hp_ref, rt_ref, cnt_ref, run_scr):
    D = D_MODEL
    tm = x_ref.shape[0]

    @pl.when((pl.program_id(0) == 0) & (pl.program_id(1) == 0))
    def _():
        run_scr[...] = jnp.zeros(run_scr.shape, F32)

    sh = mod_ref[:, 3 * D:4 * D]
    sc = mod_ref[:, 4 * D:5 * D]
    h = x_ref[...] * (1.0 + sc) + sh
    hp_ref[...] = _pack_halves(h)
    h_hi = h.astype(BF16)
    h_lo = (h - h_hi.astype(F32)).astype(BF16)
    logit = (jnp.dot(h_hi, wr_hi_ref[...], preferred_element_type=F32)
             + jnp.dot(h_lo, wr_hi_ref[...], preferred_element_type=F32)
             + jnp.dot(h_hi, wr_lo_ref[...], preferred_element_type=F32)) + br_ref[...]
    lane = lax.broadcasted_iota(I32, logit.shape, 1).astype(F32)
    neg = -jnp.inf
    big = float(LANES)

    def first_argmax(v):
        mx = jnp.max(v, axis=1, keepdims=True)
        idx = jnp.min(jnp.where(v == mx, lane, big), axis=1, keepdims=True)
        return mx, idx

    in_grp = lane < N_GROUPS
    gmax, gidx = first_argmax(jnp.where(in_grp, logit, neg))
    g_p = 1.0 / jnp.sum(jnp.where(in_grp, jnp.exp(logit - gmax), 0.0), axis=1, keepdims=True)
    lo_lane = ROUTE_LANE0 + EXP_PER_GROUP * gidx
    sel = jnp.where((lane >= lo_lane) & (lane < lo_lane + EXP_PER_GROUP), logit, neg)
    v1, i1 = first_argmax(sel)
    v2, i2 = first_argmax(jnp.where(lane == i1, neg, sel))
    t = jnp.exp(v2 - v1)
    w1 = g_p / (1.0 + t)
    w2 = g_p * t / (1.0 + t)

    hot1 = lane == i1
    hot2 = lane == i2
    twohot = jnp.where(hot1 | hot2, 1.0, 0.0)
    r = lax.broadcasted_iota(I32, (tm, tm), 0)
    c = lax.broadcasted_iota(I32, (tm, tm), 1)
    strict = jnp.where(r > c, 1.0, 0.0).astype(BF16)
    before = jnp.dot(strict, twohot.astype(BF16), preferred_element_type=F32) + run_scr[...]
    r1 = jnp.sum(jnp.where(hot1, before, 0.0), axis=1, keepdims=True)
    r2 = jnp.sum(jnp.where(hot2, before, 0.0), axis=1, keepdims=True)
    run_scr[...] = run_scr[...] + jnp.sum(twohot, axis=0, keepdims=True)
    cnt_ref[...] = run_scr[...]

    fields = (i1 - ROUTE_LANE0, i2 - ROUTE_LANE0, w1, w2, r1, r2)
    slab = jnp.zeros(logit.shape, F32)
    for n, f in enumerate(fields):
        slab = jnp.where(lane == float(n), f, slab)
    rt_ref[...] = slab.T[0:8, :]


def _route(x1, mod_l, wr_hi, wr_lo, br):
    B, S, D = x1.shape
    tm = min(ROW_TILE, S)
    row = lambda b, i: (b, i, 0)
    full2 = lambda a: pl.BlockSpec(a.shape, lambda b, i: (0,) * a.ndim)
    return pl.pallas_call(
        _route_kernel,
        grid=(B, S // tm),
        in_specs=[
            pl.BlockSpec((None, tm, D), row),
            pl.BlockSpec((None, 1, 6 * D), lambda b, i: (b, 0, 0)),
            full2(wr_hi), full2(wr_lo), full2(br),
        ],
        out_specs=[
            pl.BlockSpec((None, tm, HALF), row),
            pl.BlockSpec((None, 8, tm), lambda b, i: (b, 0, i)),
            pl.BlockSpec((1, LANES), lambda b, i: (0, 0)),
        ],
        out_shape=[
            jax.ShapeDtypeStruct((B, S, HALF), U32),
            jax.ShapeDtypeStruct((B, 8, S), F32),
            jax.ShapeDtypeStruct((1, LANES), F32),
        ],
        scratch_shapes=[pltpu.VMEM((1, LANES), F32)],
        compiler_params=_cparams(("arbitrary", "arbitrary")),
        name="moe_route",
    )(x1, mod_l, wr_hi, wr_lo, br)


def _dispatch_kernel(p0_ref, p1_ref, hp_ref, xs_in_ref, xs_ref, sem):
    del xs_in_ref
    nb = p0_ref.shape[2]

    def row_copy(t, p):
        return pltpu.make_async_copy(hp_ref.at[pl.ds(t, 1)], xs_ref.at[pl.ds(p, 1)], sem)

    def issue(t, carry):
        row_copy(t, p0_ref[0, 0, t]).start()
        row_copy(t, p1_ref[0, 0, t]).start()
        return carry

    lax.fori_loop(0, nb, issue, 0, unroll=8)

    def drain(t, carry):
        row_copy(0, 0).wait()
        row_copy(0, 0).wait()
        return carry

    lax.fori_loop(0, nb, drain, 0, unroll=8)


def _dispatch(pos0, pos1, hp, n_rows):
    T = hp.shape[0]
    nb = min(DISPATCH_BLOCK, T)
    blk = pl.BlockSpec((1, 1, nb), lambda i: (i, 0, 0), memory_space=pltpu.SMEM)
    xs0 = jnp.zeros((n_rows, HALF), U32)
    return pl.pallas_call(
        _dispatch_kernel,
        grid=(T // nb,),
        in_specs=[blk, blk, pl.BlockSpec((nb, HALF), lambda i: (i, 0)), pl.BlockSpec(memory_space=pl.ANY)],
        out_specs=pl.BlockSpec(memory_space=pl.ANY),
        out_shape=jax.ShapeDtypeStruct((n_rows, HALF), U32),
        scratch_shapes=[pltpu.SemaphoreType.DMA(())],
        input_output_aliases={3: 0},
        compiler_params=_cparams(("arbitrary",)),
        name="moe_dispatch",
    )(pos0.reshape(T // nb, 1, nb), pos1.reshape(T // nb, 1, nb), hp, xs0)


def _expert_kernel(te_ref, nv_ref, xs_ref, wg_ref, wu_ref, wd_ref, ys_ref):
    del te_ref

    @pl.when(pl.program_id(0) < nv_ref[0])
    def _():
        lo, hi = _unpack_halves(xs_ref[...])
        lo = lo.astype(BF16)
        hi = hi.astype(BF16)
        g = (jnp.dot(lo, wg_ref[0:HALF, :], preferred_element_type=F32)
             + jnp.dot(hi, wg_ref[HALF:, :], preferred_element_type=F32))
        u = (jnp.dot(lo, wu_ref[0:HALF, :], preferred_element_type=F32)
             + jnp.dot(hi, wu_ref[HALF:, :], preferred_element_type=F32))
        a = (g * jax.nn.sigmoid(g) * u).astype(BF16)
        ys_ref[...] = _pack_halves(jnp.dot(a, wd_ref[...], preferred_element_type=F32))


def _experts(tile_expert, n_valid, xs, wg_b, wu_b, wd_b):
    n_rows = xs.shape[0]
    tm = EXPERT_TILE
    nt = n_rows // tm
    rows = lambda n, te, nv: (jnp.minimum(n, nv[0] - 1), 0)
    wsel = lambda n, te, nv: (te[n], 0, 0)
    return pl.pallas_call(
        _expert_kernel,
        grid_spec=pltpu.PrefetchScalarGridSpec(
            num_scalar_prefetch=2,
            grid=(nt,),
            in_specs=[
                pl.BlockSpec((tm, HALF), rows),
                pl.BlockSpec((None, D_MODEL, D_EXPERT), wsel),
                pl.BlockSpec((None, D_MODEL, D_EXPERT), wsel),
                pl.BlockSpec((None, D_EXPERT, D_MODEL), wsel),
            ],
            out_specs=pl.BlockSpec((tm, HALF), rows),
        ),
        out_shape=jax.ShapeDtypeStruct((n_rows, HALF), U32),
        compiler_params=_cparams(("arbitrary",)),
        name="moe_experts",
    )(tile_expert, n_valid, xs, wg_b, wu_b, wd_b)


def _combine_kernel(p0_ref, p1_ref, ys_ref, x_ref, mod_ref, rt_ref, g_ref, b_ref, o_ref,
                    buf, sem, *, alpha):
    D = D_MODEL
    tc = x_ref.shape[0]

    def row_copy(p, slot, t):
        return pltpu.make_async_copy(ys_ref.at[pl.ds(p, 1)], buf.at[slot, pl.ds(t, 1)], sem)

    def issue(t, carry):
        row_copy(p0_ref[0, 0, t], 0, t).start()
        row_copy(p1_ref[0, 0, t], 1, t).start()
        return carry

    lax.fori_loop(0, tc, issue, 0, unroll=8)

    def drain(t, carry):
        row_copy(0, 0, 0).wait()
        row_copy(0, 0, 0).wait()
        return carry

    lax.fori_loop(0, tc, drain, 0, unroll=8)

    wt = rt_ref[...].T
    w1 = wt[:, 2:3]
    w2 = wt[:, 3:4]
    lo1, hi1 = _unpack_halves(buf[0])
    lo2, hi2 = _unpack_halves(buf[1])
    ffn = jnp.concatenate([w1 * lo1 + w2 * lo2, w1 * hi1 + w2 * hi2], axis=1)
    gt = mod_ref[:, 5 * D:6 * D]
    y = alpha * x_ref[...] + (1.0 + gt) * ffn
    o_ref[...] = _layer_norm_rows(y, g_ref[...], b_ref[...])


def _combine(pos0, pos1, ys, x1, mod_l, rt, ln_g, ln_b, alpha):
    B, S, D = x1.shape
    tc = min(COMBINE_TILE, S)
    nt = S // tc
    blk = pl.BlockSpec((1, 1, tc), lambda b, i: (b * nt + i, 0, 0), memory_space=pltpu.SMEM)
    row = lambda b, i: (b, i, 0)
    full2 = lambda a: pl.BlockSpec(a.shape, lambda b, i: (0,) * a.ndim)
    return pl.pallas_call(
        functools.partial(_combine_kernel, alpha=alpha),
        grid=(B, nt),
        in_specs=[
            blk, blk,
            pl.BlockSpec(memory_space=pl.ANY),
            pl.BlockSpec((None, tc, D), row),
            pl.BlockSpec((None, 1, 6 * D), lambda b, i: (b, 0, 0)),
            pl.BlockSpec((None, 8, tc), lambda b, i: (b, 0, i)),
            full2(ln_g), full2(ln_b),
        ],
        out_specs=pl.BlockSpec((None, tc, D), row),
        out_shape=jax.ShapeDtypeStruct((B, S, D), F32),
        scratch_shapes=[pltpu.VMEM((2, tc, HALF), U32), pltpu.SemaphoreType.DMA(())],
        compiler_params=_cparams(("arbitrary", "arbitrary")),
        name="moe_combine_ln",
    )(pos0.reshape(B * nt, 1, tc), pos1.reshape(B * nt, 1, tc), ys, x1, mod_l, rt, ln_g, ln_b)


def _rope_tables(S):
    inv = 1.0 / (ROPE_THETA ** (jnp.arange(0, QK_DIM, 2, dtype=F32) / QK_DIM))
    ang = jnp.arange(S, dtype=F32)[:, None] * inv[None, :]
    cos, sin = jnp.cos(ang), jnp.sin(ang)
    half = QK_DIM // 2
    first = (jnp.arange(LANES) % QK_DIM) < half
    cs = jnp.tile(cos, (1, LANES // half))
    sn = jnp.tile(sin, (1, LANES // half))
    sa = jnp.where(first[None, :], -sn, 0.0)
    sb = jnp.where(first[None, :], 0.0, sn)
    return cs, sa, sb


def _router_matrix(w_group, b_group, w_router, b_router):
    D = w_group.shape[0]
    w = jnp.zeros((D, LANES), F32)
    w = w.at[:, 0:N_GROUPS].set(w_group)
    wr = jnp.transpose(w_router, (1, 0, 2)).reshape(D, N_EXPERTS)
    w = w.at[:, ROUTE_LANE0:ROUTE_LANE0 + N_EXPERTS].set(wr)
    b = jnp.zeros((1, LANES), F32)
    b = b.at[0, 0:N_GROUPS].set(b_group)
    b = b.at[0, ROUTE_LANE0:ROUTE_LANE0 + N_EXPERTS].set(b_router.reshape(N_EXPERTS))
    hi = w.astype(BF16)
    lo = (w - hi.astype(F32)).astype(BF16)
    return hi, lo, b


def _routing_tables(rt, cnt, n_tiles):
    B, _, S = rt.shape
    tm = EXPERT_TILE
    e = rt[:, 0:2, :].astype(I32)
    rank = rt[:, 4:6, :].astype(I32)
    counts = cnt[0, ROUTE_LANE0:ROUTE_LANE0 + N_EXPERTS].astype(I32)
    tiles = (counts + tm - 1) // tm
    tile_end = jnp.cumsum(tiles)
    tile_start = tile_end - tiles
    onehot = e[..., None] == jnp.arange(N_EXPERTS, dtype=I32)
    start = jnp.sum(jnp.where(onehot, tile_start * tm, 0), axis=-1)
    pos = start + rank
    pos0 = pos[:, 0, :].reshape(B * S)
    pos1 = pos[:, 1, :].reshape(B * S)
    tile_ids = jnp.arange(n_tiles, dtype=I32)
    tile_expert = jnp.minimum(jnp.sum(tile_ids[:, None] >= tile_end[None, :], axis=1), N_EXPERTS - 1).astype(I32)
    n_valid = tile_end[-1:].astype(I32)
    return pos0, pos1, tile_expert, n_valid


def kernel(x, c, w_ada, b_ada, w_in, lambda_q1, lambda_k1, lambda_q2, lambda_k2, subln_g, sg_ln_g, sg_ln_b, w_spatial, b_spatial, w_out, ln1_g, ln1_b, w_group, b_group, w_router, b_router, w_gate, w_up, w_down, ln2_g, ln2_b):
    B, S, D = x.shape
    depth = w_in.shape[0]
    T = B * S
    alpha = (2.0 * depth) ** 0.25
    n_tiles = (2 * T) // EXPERT_TILE + N_EXPERTS
    n_rows = n_tiles * EXPERT_TILE

    mod = _ada(c, w_ada, b_ada)
    tabs = _rope_tables(S)
    for l in range(depth):
        mod_l = mod[l].reshape(B, 1, 6 * D)
        lam_init = 0.8 - 0.6 * math.exp(-0.3 * l)

        qT, k, vTb, zu, zvn = _inproj(x, mod_l, w_in[l].astype(BF16), tabs,
                                      sg_ln_g[l].reshape(1, SG_WIDTH), sg_ln_b[l].reshape(1, SG_WIDTH))
        lams = (lambda_q1[l].reshape(1, QK_DIM), lambda_k1[l].reshape(1, QK_DIM),
                lambda_q2[l].reshape(1, QK_DIM), lambda_k2[l].reshape(1, QK_DIM))
        att = _attention(qT, k, vTb, lams, subln_g[l].reshape(V_DIM, 1), lam_init)
        b_sp_full = jnp.broadcast_to(b_spatial[l][:, :, None], (N_SG, CHUNK, SG_DIM))
        x1 = _mix(x, mod_l, att, zu, zvn, w_spatial[l], b_sp_full, w_out[l].astype(BF16),
                  ln1_g[l].reshape(1, D), ln1_b[l].reshape(1, D), alpha)

        wr_hi, wr_lo, br = _router_matrix(w_group[l], b_group[l], w_router[l], b_router[l])
        hp, rt, cnt = _route(x1, mod_l, wr_hi, wr_lo, br)
        pos0, pos1, tile_expert, n_valid = _routing_tables(rt, cnt, n_tiles)
        xs = _dispatch(pos0, pos1, hp.reshape(T, HALF), n_rows)
        ys = _experts(tile_expert, n_valid, xs, w_gate[l].astype(BF16), w_up[l].astype(BF16),
                      w_down[l].astype(BF16))
        x = _combine(pos0, pos1, ys, x1, mod_l, rt, ln2_g[l].reshape(1, D), ln2_b[l].reshape(1, D), alpha)
    return x
```

```python
import functools
import math

import jax
import jax.numpy as jnp
from jax import lax
from jax.experimental import pallas as pl
from jax.experimental.pallas import tpu as pltpu

F32 = jnp.float32
BF16 = jnp.bfloat16
U32 = jnp.uint32
I32 = jnp.int32

D_MODEL = 1024
N_HEADS = 4
QK_DIM = 64
V_DIM = 128
HEAD_COLS = 2 * QK_DIM
QK_COLS = N_HEADS * HEAD_COLS
DIFF_WIDTH = N_HEADS * V_DIM
N_SG = 4
SG_DIM = 128
SG_WIDTH = N_SG * SG_DIM
CHUNK = 128
N_GROUPS = 4
EXP_PER_GROUP = 8
N_EXPERTS = N_GROUPS * EXP_PER_GROUP
D_EXPERT = 512
ROPE_THETA = 10000.0
LN_EPS = 1e-5
LANES = 128
HALF = D_MODEL // 2

ROW_TILE = 512
V_ROWS = V_DIM + 16
EXPERT_TILE = 512
DISPATCH_BLOCK = 2048
COMBINE_TILE = 256
ROUTE_LANE0 = 8
VMEM_LIMIT = 48 * 1024 * 1024

LOG2E = 1.4426950408889634
Q_SCALE = (QK_DIM ** -0.5) * LOG2E


def _cparams(sem):
    return pltpu.CompilerParams(dimension_semantics=sem, vmem_limit_bytes=VMEM_LIMIT)


def _layer_norm_rows(y, g, b):
    mu = jnp.mean(y, axis=-1, keepdims=True)
    yc = y - mu
    var = jnp.mean(yc * yc, axis=-1, keepdims=True)
    return yc * lax.rsqrt(var + LN_EPS) * g + b


def _gelu(x):
    return 0.5 * x * (1.0 + lax.erf(x * (2.0 ** -0.5)))


def _pack_halves(y):
    lo = pltpu.bitcast(y[:, :HALF].astype(BF16).astype(F32), U32) >> 16
    hi = pltpu.bitcast(y[:, HALF:].astype(BF16).astype(F32), U32) & jnp.uint32(0xFFFF0000)
    return lo | hi


def _unpack_halves(p):
    lo = pltpu.bitcast(p << 16, F32)
    hi = pltpu.bitcast(p & jnp.uint32(0xFFFF0000), F32)
    return lo, hi


def _ada_kernel(c_ref, w_ref, b_ref, o_ref):
    c = c_ref[...]
    sc = c * jax.nn.sigmoid(c)
    o_ref[...] = jnp.dot(sc, w_ref[...], precision=lax.Precision.HIGHEST,
                         preferred_element_type=F32) + b_ref[...]


def _ada(c, w_ada, b_ada):
    L, D, N = w_ada.shape
    B = c.shape[0]
    tn = 1536
    return pl.pallas_call(
        _ada_kernel,
        grid=(L, N // tn),
        in_specs=[
            pl.BlockSpec((B, D), lambda l, j: (0, 0)),
            pl.BlockSpec((None, D, tn), lambda l, j: (l, 0, j)),
            pl.BlockSpec((None, 1, tn), lambda l, j: (l, 0, j)),
        ],
        out_specs=pl.BlockSpec((None, B, tn), lambda l, j: (l, 0, j)),
        out_shape=jax.ShapeDtypeStruct((L, B, N), F32),
        compiler_params=_cparams(("arbitrary", "arbitrary")),
        name="ada",
    )(c, w_ada, b_ada.reshape(L, 1, N))


def _inproj_kernel(x_ref, mod_ref, w_ref, cs_ref, sa_ref, sb_ref, lng_ref, lnb_ref,
                   qT_ref, k_ref, vT_ref, zu_ref, zvn_ref):
    D = D_MODEL
    x = x_ref[...]
    sh = mod_ref[:, 0:D]
    sc = mod_ref[:, D:2 * D]
    h = (x * (1.0 + sc) + sh).astype(BF16)
    cs = cs_ref[...]
    sa = sa_ref[...]
    sb = sb_ref[...]

    def rope(t):
        return t * cs + pltpu.roll(t, 96, 1) * sa + pltpu.roll(t, 32, 1) * sb

    q = jnp.dot(h, w_ref[:, 0:QK_COLS], preferred_element_type=F32)
    for j in range(N_HEADS):
        sl = slice(j * LANES, (j + 1) * LANES)
        qT_ref[sl, :] = (rope(q[:, sl]) * Q_SCALE).T.astype(BF16)
    k = jnp.dot(h, w_ref[:, QK_COLS:2 * QK_COLS], preferred_element_type=F32)
    for j in range(N_HEADS):
        sl = slice(j * LANES, (j + 1) * LANES)
        k_ref[:, sl] = rope(k[:, sl]).astype(BF16)
    c0 = 2 * QK_COLS
    v = jnp.dot(h, w_ref[:, c0:c0 + DIFF_WIDTH], preferred_element_type=F32)
    for j in range(N_HEADS):
        vT_ref[j * V_ROWS:j * V_ROWS + V_DIM, :] = v[:, j * V_DIM:(j + 1) * V_DIM].T.astype(BF16)
        vT_ref[j * V_ROWS + V_DIM:(j + 1) * V_ROWS, :] = jnp.ones((V_ROWS - V_DIM, v.shape[0]), BF16)
    c0 += DIFF_WIDTH
    u = jnp.dot(h, w_ref[:, c0:c0 + SG_WIDTH], preferred_element_type=F32)
    zu_ref[...] = _gelu(u).astype(BF16)
    c0 += SG_WIDTH
    z = _gelu(jnp.dot(h, w_ref[:, c0:c0 + SG_WIDTH], preferred_element_type=F32))
    for g in range(N_SG):
        sl = slice(g * SG_DIM, (g + 1) * SG_DIM)
        zvn_ref[:, sl] = _layer_norm_rows(z[:, sl], lng_ref[:, sl], lnb_ref[:, sl]).astype(BF16)


def _inproj(x, mod_l, w_in_b, rope_tabs, ln_g, ln_b):
    B, S, D = x.shape
    tm = min(ROW_TILE, S)
    nt = S // tm
    cs, sa, sb = rope_tabs
    row = lambda b, i: (b, i, 0)
    tab = pl.BlockSpec((tm, LANES), lambda b, i: (i, 0))
    return pl.pallas_call(
        _inproj_kernel,
        grid=(B, nt),
        in_specs=[
            pl.BlockSpec((None, tm, D), row),
            pl.BlockSpec((None, 1, 6 * D), lambda b, i: (b, 0, 0)),
            pl.BlockSpec(w_in_b.shape, lambda b, i: (0, 0)),
            tab, tab, tab,
            pl.BlockSpec((1, SG_WIDTH), lambda b, i: (0, 0)),
            pl.BlockSpec((1, SG_WIDTH), lambda b, i: (0, 0)),
        ],
        out_specs=[
            pl.BlockSpec((None, QK_COLS, tm), lambda b, i: (b, 0, i)),
            pl.BlockSpec((None, tm, QK_COLS), row),
            pl.BlockSpec((None, None, N_HEADS * V_ROWS, tm), lambda b, i: (b, i, 0, 0)),
            pl.BlockSpec((None, tm, SG_WIDTH), row),
            pl.BlockSpec((None, tm, SG_WIDTH), row),
        ],
        out_shape=[
            jax.ShapeDtypeStruct((B, QK_COLS, S), BF16),
            jax.ShapeDtypeStruct((B, S, QK_COLS), BF16),
            jax.ShapeDtypeStruct((B, nt, N_HEADS * V_ROWS, tm), BF16),
            jax.ShapeDtypeStruct((B, S, SG_WIDTH), BF16),
            jax.ShapeDtypeStruct((B, S, SG_WIDTH), BF16),
        ],
        compiler_params=_cparams(("arbitrary", "arbitrary")),
        name="inproj",
    )(x, mod_l, w_in_b, cs, sa, sb, ln_g, ln_b)


def _attn_kernel(lq1_ref, lk1_ref, lq2_ref, lk2_ref, g_ref, qT_ref, k_ref, vT_ref, o_ref,
                 q_scr, s_scr, m_scr, acc_scr, *, lam_init):
    tq = qT_ref.shape[1]
    tk = vT_ref.shape[2]
    i = pl.program_id(2)
    qT = qT_ref[...]
    rows = lax.broadcasted_iota(I32, qT.shape, 0)
    zero = jnp.zeros_like(qT)
    q_scr[0] = jnp.where(rows < QK_DIM, qT, zero)
    q_scr[1] = jnp.where(rows >= QK_DIM, qT, zero)
    m_scr[...] = jnp.full(m_scr.shape, -jnp.inf, F32)
    acc_scr[...] = jnp.zeros(acc_scr.shape, F32)

    def scores(j, slot, c0):
        kb = k_ref[pl.ds(pl.multiple_of(j * tk, tk), tk), :]
        for mp in range(2):
            s_scr[slot, mp, :, c0:] = jnp.dot(kb, q_scr[mp, :, c0:], preferred_element_type=F32)

    def softmax_pv(j, slot, c0, masked):
        vb = vT_ref[j]
        for mp in range(2):
            s = s_scr[slot, mp, :, c0:]
            if masked:
                kpos = lax.broadcasted_iota(I32, s.shape, 0)
                qpos = lax.broadcasted_iota(I32, s.shape, 1)
                s = jnp.where(kpos <= qpos, s, -jnp.inf)
            m_old = m_scr[mp, :, c0:]
            m_new = jnp.maximum(m_old, jnp.max(s, axis=0, keepdims=True))
            alpha = jnp.exp2(m_old - m_new)
            p = jnp.exp2(s - m_new).astype(BF16)
            acc_scr[mp, :, c0:] = alpha * acc_scr[mp, :, c0:] + jnp.dot(vb, p, preferred_element_type=F32)
            m_scr[mp, :, c0:] = m_new

    scores(0, 0, 0)

    def body(u, carry):
        t = 2 * u + 1
        scores(t, 1, 0)
        softmax_pv(t - 1, 0, 0, False)
        scores(t + 1, 0, 0)
        softmax_pv(t, 1, 0, False)
        return carry

    lax.fori_loop(0, i, body, 0)
    scores(2 * i + 1, 1, tk)
    softmax_pv(2 * i, 0, 0, True)
    softmax_pv(2 * i + 1, 1, tk, True)

    lam = (jnp.exp(jnp.sum(lq1_ref[...] * lk1_ref[...], axis=1, keepdims=True))
           - jnp.exp(jnp.sum(lq2_ref[...] * lk2_ref[...], axis=1, keepdims=True)) + lam_init)
    a1 = acc_scr[0]
    a2 = acc_scr[1]
    out = a1[0:V_DIM] / a1[V_DIM:V_DIM + 1] - lam * (a2[0:V_DIM] / a2[V_DIM:V_DIM + 1])
    ms = jnp.mean(out * out, axis=0, keepdims=True)
    y = out * lax.rsqrt(ms + LN_EPS) * g_ref[...] * (1.0 - lam_init)
    o_ref[...] = y.T.astype(BF16)


def _attention(qT, k, vTb, lams, g_col, lam_init):
    B, _, S = qT.shape
    nkv, tk = vTb.shape[1], vTb.shape[3]
    tq = 2 * tk
    small = pl.BlockSpec((1, QK_DIM), lambda b, h, i: (0, 0))
    return pl.pallas_call(
        functools.partial(_attn_kernel, lam_init=lam_init),
        grid=(B, N_HEADS, S // tq),
        in_specs=[
            small, small, small, small,
            pl.BlockSpec((V_DIM, 1), lambda b, h, i: (0, 0)),
            pl.BlockSpec((None, HEAD_COLS, tq), lambda b, h, i: (b, h, i)),
            pl.BlockSpec((None, S, HEAD_COLS), lambda b, h, i: (b, 0, h)),
            pl.BlockSpec((None, nkv, V_ROWS, tk), lambda b, h, i: (b, 0, h, 0)),
        ],
        out_specs=pl.BlockSpec((None, tq, V_DIM), lambda b, h, i: (b, i, h)),
        out_shape=jax.ShapeDtypeStruct((B, S, DIFF_WIDTH), BF16),
        scratch_shapes=[
            pltpu.VMEM((2, HEAD_COLS, tq), BF16),
            pltpu.VMEM((2, 2, tk, tq), F32),
            pltpu.VMEM((2, 1, tq), F32),
            pltpu.VMEM((2, V_ROWS, tq), F32),
        ],
        compiler_params=_cparams(("arbitrary", "arbitrary", "arbitrary")),
        name="diff_attn",
    )(*lams, g_col, qT, k, vTb)


def _mix_kernel(x_ref, mod_ref, att_ref, zu_ref, zvn_ref, wsp_ref, bsp_ref, wo_ref, g_ref, b_ref,
                o_ref, sg_scr, *, alpha):
    D = D_MODEL
    nc = zu_ref.shape[0]
    r = lax.broadcasted_iota(I32, (CHUNK, CHUNK), 0)
    c = lax.broadcasted_iota(I32, (CHUNK, CHUNK), 1)
    causal = r >= c
    for g in range(N_SG):
        sl = slice(g * SG_DIM, (g + 1) * SG_DIM)
        w = jnp.where(causal, wsp_ref[g], 0.0).astype(BF16)
        z = jnp.concatenate([zvn_ref[n, :, sl] for n in range(nc)], axis=1)
        mixed = jnp.dot(w, z, preferred_element_type=F32)
        for n in range(nc):
            gate = mixed[:, n * SG_DIM:(n + 1) * SG_DIM] + bsp_ref[g]
            sg_scr[n * CHUNK:(n + 1) * CHUNK, sl] = (zu_ref[n, :, sl].astype(F32) * gate).astype(BF16)
    mix = (jnp.dot(att_ref[...], wo_ref[0:DIFF_WIDTH, :], preferred_element_type=F32)
           + jnp.dot(sg_scr[...], wo_ref[DIFF_WIDTH:, :], preferred_element_type=F32))
    gt = mod_ref[:, 2 * D:3 * D]
    y = alpha * x_ref[...] + (1.0 + gt) * mix
    o_ref[...] = _layer_norm_rows(y, g_ref[...], b_ref[...])


def _mix(x, mod_l, att, zu, zvn, w_sp, b_sp_full, w_out_b, ln_g, ln_b, alpha):
    B, S, D = x.shape
    tm = min(ROW_TILE, S)
    nc = tm // CHUNK
    row = lambda b, i: (b, i, 0)
    zu4 = zu.reshape(B, S // CHUNK, CHUNK, SG_WIDTH)
    zvn4 = zvn.reshape(B, S // CHUNK, CHUNK, SG_WIDTH)
    chunked = pl.BlockSpec((None, nc, CHUNK, SG_WIDTH), lambda b, i: (b, i, 0, 0))
    full2 = lambda a: pl.BlockSpec(a.shape, lambda b, i: (0,) * a.ndim)
    return pl.pallas_call(
        functools.partial(_mix_kernel, alpha=alpha),
        grid=(B, S // tm),
        in_specs=[
            pl.BlockSpec((None, tm, D), row),
            pl.BlockSpec((None, 1, 6 * D), lambda b, i: (b, 0, 0)),
            pl.BlockSpec((None, tm, DIFF_WIDTH), row),
            chunked, chunked,
            full2(w_sp), full2(b_sp_full), full2(w_out_b), full2(ln_g), full2(ln_b),
        ],
        out_specs=pl.BlockSpec((None, tm, D), row),
        out_shape=jax.ShapeDtypeStruct((B, S, D), F32),
        scratch_shapes=[pltpu.VMEM((tm, SG_WIDTH), BF16)],
        compiler_params=_cparams(("arbitrary", "arbitrary")),
        name="sgate_outproj_ln",
    )(x, mod_l, att, zu4, zvn4, w_sp, b_sp_full, w_out_b, ln_g, ln_b)


def _route_kernel(x_ref, mod_ref, wr_hi_ref, wr_lo_ref, br_ref, hp_ref, rt_ref, cnt_ref, run_scr):
    D = D_MODEL
    tm = x_ref.shape[0]

    @pl.when((pl.program_id(0) == 0) & (pl.program_id(1) == 0))
    def _():
        run_scr[...] = jnp.zeros(run_scr.shape, F32)

    sh = mod_ref[:, 3 * D:4 * D]
    sc = mod_ref[:, 4 * D:5 * D]
    h = x_ref[...] * (1.0 + sc) + sh
    hp_ref[...] = _pack_halves(h)
    h_hi = h.astype(BF16)
    h_lo = (h - h_hi.astype(F32)).astype(BF16)
    logit = (jnp.dot(h_hi, wr_hi_ref[...], preferred_element_type=F32)
             + jnp.dot(h_lo, wr_hi_ref[...], preferred_element_type=F32)
             + jnp.dot(h_hi, wr_lo_ref[...], preferred_element_type=F32)) + br_ref[...]
    lane = lax.broadcasted_iota(I32, logit.shape, 1).astype(F32)
    neg = -jnp.inf
    big = float(LANES)

    def first_argmax(v):
        mx = jnp.max(v, axis=1, keepdims=True)
        idx = jnp.min(jnp.where(v == mx, lane, big), axis=1, keepdims=True)
        return mx, idx

    in_grp = lane < N_GROUPS
    gmax, gidx = first_argmax(jnp.where(in_grp, logit, neg))
    g_p = 1.0 / jnp.sum(jnp.where(in_grp, jnp.exp(logit - gmax), 0.0), axis=1, keepdims=True)
    lo_lane = ROUTE_LANE0 + EXP_PER_GROUP * gidx
    sel = jnp.where((lane >= lo_lane) & (lane < lo_lane + EXP_PER_GROUP), logit, neg)
    v1, i1 = first_argmax(sel)
    v2, i2 = first_argmax(jnp.where(lane == i1, neg, sel))
    t = jnp.exp(v2 - v1)
    w1 = g_p / (1.0 + t)
    w2 = g_p * t / (1.0 + t)

    hot1 = lane == i1
    hot2 = lane == i2
    twohot = jnp.where(hot1 | hot2, 1.0, 0.0)
    r = lax.broadcasted_iota(I32, (tm, tm), 0)
    c = lax.broadcasted_iota(I32, (tm, tm), 1)
    strict = jnp.where(r > c, 1.0, 0.0).astype(BF16)
    before = jnp.dot(strict, twohot.astype(BF16), preferred_element_type=F32) + run_scr[...]
    r1 = jnp.sum(jnp.where(hot1, before, 0.0), axis=1, keepdims=True)
    r2 = jnp.sum(jnp.where(hot2, before, 0.0), axis=1, keepdims=True)
    run_scr[...] = run_scr[...] + jnp.sum(twohot, axis=0, keepdims=True)
    cnt_ref[...] = run_scr[...]

    fields = (i1 - ROUTE_LANE0, i2 - ROUTE_LANE0, w1, w2, r1, r2)
    slab = jnp.zeros(logit.shape, F32)
    for n, f in enumerate(fields):
        slab = jnp.where(lane == float(n), f, slab)
    rt_ref[...] = slab.T[0:8, :]


def _route(x1, mod_l, wr_hi, wr_lo, br):
    B, S, D = x1.shape
    tm = min(ROW_TILE, S)
    row = lambda b, i: (b, i, 0)
    full2 = lambda a: pl.BlockSpec(a.shape, lambda b, i: (0,) * a.ndim)
    return pl.pallas_call(
        _route_kernel,
        grid=(B, S // tm),
        in_specs=[
            pl.BlockSpec((None, tm, D), row),
            pl.BlockSpec((None, 1, 6 * D), lambda b, i: (b, 0, 0)),
            full2(wr_hi), full2(wr_lo), full2(br),
        ],
        out_specs=[
            pl.BlockSpec((None, tm, HALF), row),
            pl.BlockSpec((None, 8, tm), lambda b, i: (b, 0, i)),
            pl.BlockSpec((1, LANES), lambda b, i: (0, 0)),
        ],
        out_shape=[
            jax.ShapeDtypeStruct((B, S, HALF), U32),
            jax.ShapeDtypeStruct((B, 8, S), F32),
            jax.ShapeDtypeStruct((1, LANES), F32),
        ],
        scratch_shapes=[pltpu.VMEM((1, LANES), F32)],
        compiler_params=_cparams(("arbitrary", "arbitrary")),
        name="moe_route",
    )(x1, mod_l, wr_hi, wr_lo, br)


def _dispatch_kernel(p0_ref, p1_ref, hp_ref, xs_in_ref, xs_ref, sem):
    del xs_in_ref
    nb = p0_ref.shape[2]

    def row_copy(t, p):
        return pltpu.make_async_copy(hp_ref.at[pl.ds(t, 1)], xs_ref.at[pl.ds(p, 1)], sem)

    def issue(t, carry):
        row_copy(t, p0_ref[0, 0, t]).start()
        row_copy(t, p1_ref[0, 0, t]).start()
        return carry

    lax.fori_loop(0, nb, issue, 0, unroll=8)

    def drain(t, carry):
        row_copy(0, 0).wait()
        row_copy(0, 0).wait()
        return carry

    lax.fori_loop(0, nb, drain, 0, unroll=8)


def _dispatch(pos0, pos1, hp, n_rows):
    T = hp.shape[0]
    nb = min(DISPATCH_BLOCK, T)
    blk = pl.BlockSpec((1, 1, nb), lambda i: (i, 0, 0), memory_space=pltpu.SMEM)
    xs0 = jnp.zeros((n_rows, HALF), U32)
    return pl.pallas_call(
        _dispatch_kernel,
        grid=(T // nb,),
        in_specs=[blk, blk, pl.BlockSpec((nb, HALF), lambda i: (i, 0)), pl.BlockSpec(memory_space=pl.ANY)],
        out_specs=pl.BlockSpec(memory_space=pl.ANY),
        out_shape=jax.ShapeDtypeStruct((n_rows, HALF), U32),
        scratch_shapes=[pltpu.SemaphoreType.DMA(())],
        input_output_aliases={3: 0},
        compiler_params=_cparams(("arbitrary",)),
        name="moe_dispatch",
    )(pos0.reshape(T // nb, 1, nb), pos1.reshape(T // nb, 1, nb), hp, xs0)


def _expert_kernel(te_ref, nv_ref, xs_ref, wg_ref, wu_ref, wd_ref, ys_ref):
    del te_ref

    @pl.when(pl.program_id(0) < nv_ref[0])
    def _():
        lo, hi = _unpack_halves(xs_ref[...])
        lo = lo.astype(BF16)
        hi = hi.astype(BF16)
        g = (jnp.dot(lo, wg_ref[0:HALF, :], preferred_element_type=F32)
             + jnp.dot(hi, wg_ref[HALF:, :], preferred_element_type=F32))
        u = (jnp.dot(lo, wu_ref[0:HALF, :], preferred_element_type=F32)
             + jnp.dot(hi, wu_ref[HALF:, :], preferred_element_type=F32))
        a = (g * jax.nn.sigmoid(g) * u).astype(BF16)
        ys_ref[...] = _pack_halves(jnp.dot(a, wd_ref[...], preferred_element_type=F32))


def _experts(tile_expert, n_valid, xs, wg_b, wu_b, wd_b):
    n_rows = xs.shape[0]
    tm = EXPERT_TILE
    nt = n_rows // tm
    rows = lambda n, te, nv: (jnp.minimum(n, nv[0] - 1), 0)
    wsel = lambda n, te, nv: (te[n], 0, 0)
    return pl.pallas_call(
        _expert_kernel,
        grid_spec=pltpu.PrefetchScalarGridSpec(
            num_scalar_prefetch=2,
            grid=(nt,),
            in_specs=[
                pl.BlockSpec((tm, HALF), rows),
                pl.BlockSpec((None, D_MODEL, D_EXPERT), wsel),
                pl.BlockSpec((None, D_MODEL, D_EXPERT), wsel),
                pl.BlockSpec((None, D_EXPERT, D_MODEL), wsel),
            ],
            out_specs=pl.BlockSpec((tm, HALF), rows),
        ),
        out_shape=jax.ShapeDtypeStruct((n_rows, HALF), U32),
        compiler_params=_cparams(("arbitrary",)),
        name="moe_experts",
    )(tile_expert, n_valid, xs, wg_b, wu_b, wd_b)


def _combine_kernel(p0_ref, p1_ref, ys_ref, x_ref, mod_ref, rt_ref, g_ref, b_ref, o_ref,
                    buf, sem, *, alpha):
    D = D_MODEL
    tc = x_ref.shape[0]

    def row_copy(p, slot, t):
        return pltpu.make_async_copy(ys_ref.at[pl.ds(p, 1)], buf.at[slot, pl.ds(t, 1)], sem)

    def issue(t, carry):
        row_copy(p0_ref[0, 0, t], 0, t).start()
        row_copy(p1_ref[0, 0, t], 1, t).start()
        return carry

    lax.fori_loop(0, tc, issue, 0, unroll=8)

    def drain(t, carry):
        row_copy(0, 0, 0).wait()
        row_copy(0, 0, 0).wait()
        return carry

    lax.fori_loop(0, tc, drain, 0, unroll=8)

    wt = rt_ref[...].T
    w1 = wt[:, 2:3]
    w2 = wt[:, 3:4]
    lo1, hi1 = _unpack_halves(buf[0])
    lo2, hi2 = _unpack_halves(buf[1])
    ffn = jnp.concatenate([w1 * lo1 + w2 * lo2, w1 * hi1 + w2 * hi2], axis=1)
    gt = mod_ref[:, 5 * D:6 * D]
    y = alpha * x_ref[...] + (1.0 + gt) * ffn
    o_ref[...] = _layer_norm_rows(y, g_ref[...], b_ref[...])


def _combine(pos0, pos1, ys, x1, mod_l, rt, ln_g, ln_b, alpha):
    B, S, D = x1.shape
    tc = min(COMBINE_TILE, S)
    nt = S // tc
    blk = pl.BlockSpec((1, 1, tc), lambda b, i: (b * nt + i, 0, 0), memory_space=pltpu.SMEM)
    row = lambda b, i: (b, i, 0)
    full2 = lambda a: pl.BlockSpec(a.shape, lambda b, i: (0,) * a.ndim)
    return pl.pallas_call(
        functools.partial(_combine_kernel, alpha=alpha),
        grid=(B, nt),
        in_specs=[
            blk, blk,
            pl.BlockSpec(memory_space=pl.ANY),
            pl.BlockSpec((None, tc, D), row),
            pl.BlockSpec((None, 1, 6 * D), lambda b, i: (b, 0, 0)),
            pl.BlockSpec((None, 8, tc), lambda b, i: (b, 0, i)),
            full2(ln_g), full2(ln_b),
        ],
        out_specs=pl.BlockSpec((None, tc, D), row),
        out_shape=jax.ShapeDtypeStruct((B, S, D), F32),
        scratch_shapes=[pltpu.VMEM((2, tc, HALF), U32), pltpu.SemaphoreType.DMA(())],
        compiler_params=_cparams(("arbitrary", "arbitrary")),
        name="moe_combine_ln",
    )(pos0.reshape(B * nt, 1, tc), pos1.reshape(B * nt, 1, tc), ys, x1, mod_l, rt, ln_g, ln_b)


def _rope_tables(S):
    inv = 1.0 / (ROPE_THETA ** (jnp.arange(0, QK_DIM, 2, dtype=F32) / QK_DIM))
    ang = jnp.arange(S, dtype=F32)[:, None] * inv[None, :]
    cos, sin = jnp.cos(ang), jnp.sin(ang)
    half = QK_DIM // 2
    first = (jnp.arange(LANES) % QK_DIM) < half
    cs = jnp.tile(cos, (1, LANES // half))
    sn = jnp.tile(sin, (1, LANES // half))
    sa = jnp.where(first[None, :], -sn, 0.0)
    sb = jnp.where(first[None, :], 0.0, sn)
    return cs, sa, sb


def _router_matrix(w_group, b_group, w_router, b_router):
    D = w_group.shape[0]
    w = jnp.zeros((D, LANES), F32)
    w = w.at[:, 0:N_GROUPS].set(w_group)
    wr = jnp.transpose(w_router, (1, 0, 2)).reshape(D, N_EXPERTS)
    w = w.at[:, ROUTE_LANE0:ROUTE_LANE0 + N_EXPERTS].set(wr)
    b = jnp.zeros((1, LANES), F32)
    b = b.at[0, 0:N_GROUPS].set(b_group)
    b = b.at[0, ROUTE_LANE0:ROUTE_LANE0 + N_EXPERTS].set(b_router.reshape(N_EXPERTS))
    hi = w.astype(BF16)
    lo = (w - hi.astype(F32)).astype(BF16)
    return hi, lo, b


def _routing_tables(rt, cnt, n_tiles):
    B, _, S = rt.shape
    tm = EXPERT_TILE
    e = rt[:, 0:2, :].astype(I32)
    rank = rt[:, 4:6, :].astype(I32)
    counts = cnt[0, ROUTE_LANE0:ROUTE_LANE0 + N_EXPERTS].astype(I32)
    tiles = (counts + tm - 1) // tm
    tile_end = jnp.cumsum(tiles)
    tile_start = tile_end - tiles
    onehot = e[..., None] == jnp.arange(N_EXPERTS, dtype=I32)
    start = jnp.sum(jnp.where(onehot, tile_start * tm, 0), axis=-1)
    pos = start + rank
    pos0 = pos[:, 0, :].reshape(B * S)
    pos1 = pos[:, 1, :].reshape(B * S)
    tile_ids = jnp.arange(n_tiles, dtype=I32)
    tile_expert = jnp.minimum(jnp.sum(tile_ids[:, None] >= tile_end[None, :], axis=1), N_EXPERTS - 1).astype(I32)
    n_valid = tile_end[-1:].astype(I32)
    return pos0, pos1, tile_expert, n_valid


def kernel(x, c, w_ada, b_ada, w_in, lambda_q1, lambda_k1, lambda_q2, lambda_k2, subln_g, sg_ln_g, sg_ln_b, w_spatial, b_spatial, w_out, ln1_g, ln1_b, w_group, b_group, w_router, b_router, w_gate, w_up, w_down, ln2_g, ln2_b):
    B, S, D = x.shape
    depth = w_in.shape[0]
    T = B * S
    alpha = (2.0 * depth) ** 0.25
    n_tiles = (2 * T) // EXPERT_TILE + N_EXPERTS
    n_rows = n_tiles * EXPERT_TILE

    mod = _ada(c, w_ada, b_ada)
    tabs = _rope_tables(S)
    for l in range(depth):
        mod_l = mod[l].reshape(B, 1, 6 * D)
        lam_init = 0.8 - 0.6 * math.exp(-0.3 * l)

        qT, k, vTb, zu, zvn = _inproj(x, mod_l, w_in[l].astype(BF16), tabs,
                                      sg_ln_g[l].reshape(1, SG_WIDTH), sg_ln_b[l].reshape(1, SG_WIDTH))
        lams = (lambda_q1[l].reshape(1, QK_DIM), lambda_k1[l].reshape(1, QK_DIM),
                lambda_q2[l].reshape(1, QK_DIM), lambda_k2[l].reshape(1, QK_DIM))
        att = _attention(qT, k, vTb, lams, subln_g[l].reshape(V_DIM, 1), lam_init)
        b_sp_full = jnp.broadcast_to(b_spatial[l][:, :, None], (N_SG, CHUNK, SG_DIM))
        x1 = _mix(x, mod_l, att, zu, zvn, w_spatial[l], b_sp_full, w_out[l].astype(BF16),
                  ln1_g[l].reshape(1, D), ln1_b[l].reshape(1, D), alpha)

        wr_hi, wr_lo, br = _router_matrix(w_group[l], b_group[l], w_router[l], b_router[l])
        hp, rt, cnt = _route(x1, mod_l, wr_hi, wr_lo, br)
        pos0, pos1, tile_expert, n_valid = _routing_tables(rt, cnt, n_tiles)
        xs = _dispatch(pos0, pos1, hp.reshape(T, HALF), n_rows)
        ys = _experts(tile_expert, n_valid, xs, w_gate[l].astype(BF16), w_up[l].astype(BF16),
                      w_down[l].astype(BF16))
        x = _combine(pos0, pos1, ys, x1, mod_l, rt, ln2_g[l].reshape(1, D), ln2_b[l].reshape(1, D), alpha)
    return x
```

```python
import functools
import math

import jax
import jax.numpy as jnp
from jax import lax
from jax.experimental import pallas as pl
from jax.experimental.pallas import tpu as pltpu

F32 = jnp.float32
BF16 = jnp.bfloat16
U32 = jnp.uint32
I32 = jnp.int32

D_MODEL = 1024
N_HEADS = 4
QK_DIM = 64
V_DIM = 128
HEAD_COLS = 2 * QK_DIM
QK_COLS = N_HEADS * HEAD_COLS
DIFF_WIDTH = N_HEADS * V_DIM
N_SG = 4
SG_DIM = 128
SG_WIDTH = N_SG * SG_DIM
CHUNK = 128
N_GROUPS = 4
EXP_PER_GROUP = 8
N_EXPERTS = N_GROUPS * EXP_PER_GROUP
D_EXPERT = 512
ROPE_THETA = 10000.0
LN_EPS = 1e-5
LANES = 128
HALF = D_MODEL // 2

ROW_TILE = 512
V_ROWS = V_DIM + 16
EXPERT_TILE = 512
DISPATCH_BLOCK = 2048
COMBINE_TILE = 256
ROW_DMA_UNROLL = 8
ROUTE_LANE0 = 8
VMEM_LIMIT = 48 * 1024 * 1024

LOG2E = 1.4426950408889634
Q_SCALE = (QK_DIM ** -0.5) * LOG2E


def _cparams(sem):
    return pltpu.CompilerParams(dimension_semantics=sem, vmem_limit_bytes=VMEM_LIMIT)


def _layer_norm_rows(y, g, b):
    mu = jnp.mean(y, axis=-1, keepdims=True)
    yc = y - mu
    var = jnp.mean(yc * yc, axis=-1, keepdims=True)
    return yc * lax.rsqrt(var + LN_EPS) * g + b


def _gelu(x):
    return 0.5 * x * (1.0 + lax.erf(x * (2.0 ** -0.5)))


def _pack_halves(y):
    lo = pltpu.bitcast(y[:, :HALF].astype(BF16).astype(F32), U32) >> 16
    hi = pltpu.bitcast(y[:, HALF:].astype(BF16).astype(F32), U32) & jnp.uint32(0xFFFF0000)
    return lo | hi


def _unpack_halves(p):
    lo = pltpu.bitcast(p << 16, F32)
    hi = pltpu.bitcast(p & jnp.uint32(0xFFFF0000), F32)
    return lo, hi


def _ada_kernel(c_ref, w_ref, b_ref, o_ref):
    c = c_ref[...]
    sc = c * jax.nn.sigmoid(c)
    o_ref[...] = jnp.dot(sc, w_ref[...], precision=lax.Precision.HIGHEST,
                         preferred_element_type=F32) + b_ref[...]


def _ada(c, w_ada, b_ada):
    L, D, N = w_ada.shape
    B = c.shape[0]
    tn = 1536
    return pl.pallas_call(
        _ada_kernel,
        grid=(L, N // tn),
        in_specs=[
            pl.BlockSpec((B, D), lambda l, j: (0, 0)),
            pl.BlockSpec((None, D, tn), lambda l, j: (l, 0, j)),
            pl.BlockSpec((None, 1, tn), lambda l, j: (l, 0, j)),
        ],
        out_specs=pl.BlockSpec((None, B, tn), lambda l, j: (l, 0, j)),
        out_shape=jax.ShapeDtypeStruct((L, B, N), F32),
        compiler_params=_cparams(("arbitrary", "arbitrary")),
        name="ada",
    )(c, w_ada, b_ada.reshape(L, 1, N))


def _inproj_kernel(x_ref, mod_ref, w_ref, cs_ref, sa_ref, sb_ref, lng_ref, lnb_ref,
                   qT_ref, k_ref, vT_ref, zu_ref, zvn_ref):
    D = D_MODEL
    x = x_ref[...]
    sh = mod_ref[:, 0:D]
    sc = mod_ref[:, D:2 * D]
    h = (x * (1.0 + sc) + sh).astype(BF16)
    cs = cs_ref[...]
    sa = sa_ref[...]
    sb = sb_ref[...]

    def rope(t):
        return t * cs + pltpu.roll(t, 96, 1) * sa + pltpu.roll(t, 32, 1) * sb

    q = jnp.dot(h, w_ref[:, 0:QK_COLS], preferred_element_type=F32)
    for j in range(N_HEADS):
        sl = slice(j * LANES, (j + 1) * LANES)
        qT_ref[sl, :] = (rope(q[:, sl]) * Q_SCALE).T.astype(BF16)
    k = jnp.dot(h, w_ref[:, QK_COLS:2 * QK_COLS], preferred_element_type=F32)
    for j in range(N_HEADS):
        sl = slice(j * LANES, (j + 1) * LANES)
        k_ref[:, sl] = rope(k[:, sl]).astype(BF16)
    c0 = 2 * QK_COLS
    v = jnp.dot(h, w_ref[:, c0:c0 + DIFF_WIDTH], preferred_element_type=F32)
    for j in range(N_HEADS):
        vT_ref[j * V_ROWS:j * V_ROWS + V_DIM, :] = v[:, j * V_DIM:(j + 1) * V_DIM].T.astype(BF16)
        vT_ref[j * V_ROWS + V_DIM:(j + 1) * V_ROWS, :] = jnp.ones((V_ROWS - V_DIM, v.shape[0]), BF16)
    c0 += DIFF_WIDTH
    u = jnp.dot(h, w_ref[:, c0:c0 + SG_WIDTH], preferred_element_type=F32)
    zu_ref[...] = _gelu(u).astype(BF16)
    c0 += SG_WIDTH
    z = _gelu(jnp.dot(h, w_ref[:, c0:c0 + SG_WIDTH], preferred_element_type=F32))
    for g in range(N_SG):
        sl = slice(g * SG_DIM, (g + 1) * SG_DIM)
        zvn_ref[:, sl] = _layer_norm_rows(z[:, sl], lng_ref[:, sl], lnb_ref[:, sl]).astype(BF16)


def _inproj(x, mod_l, w_in_b, rope_tabs, ln_g, ln_b):
    B, S, D = x.shape
    tm = min(ROW_TILE, S)
    nt = S // tm
    cs, sa, sb = rope_tabs
    row = lambda b, i: (b, i, 0)
    tab = pl.BlockSpec((tm, LANES), lambda b, i: (i, 0))
    return pl.pallas_call(
        _inproj_kernel,
        grid=(B, nt),
        in_specs=[
            pl.BlockSpec((None, tm, D), row),
            pl.BlockSpec((None, 1, 6 * D), lambda b, i: (b, 0, 0)),
            pl.BlockSpec(w_in_b.shape, lambda b, i: (0, 0)),
            tab, tab, tab,
            pl.BlockSpec((1, SG_WIDTH), lambda b, i: (0, 0)),
            pl.BlockSpec((1, SG_WIDTH), lambda b, i: (0, 0)),
        ],
        out_specs=[
            pl.BlockSpec((None, QK_COLS, tm), lambda b, i: (b, 0, i)),
            pl.BlockSpec((None, tm, QK_COLS), row),
            pl.BlockSpec((None, None, N_HEADS * V_ROWS, tm), lambda b, i: (b, i, 0, 0)),
            pl.BlockSpec((None, tm, SG_WIDTH), row),
            pl.BlockSpec((None, tm, SG_WIDTH), row),
        ],
        out_shape=[
            jax.ShapeDtypeStruct((B, QK_COLS, S), BF16),
            jax.ShapeDtypeStruct((B, S, QK_COLS), BF16),
            jax.ShapeDtypeStruct((B, nt, N_HEADS * V_ROWS, tm), BF16),
            jax.ShapeDtypeStruct((B, S, SG_WIDTH), BF16),
            jax.ShapeDtypeStruct((B, S, SG_WIDTH), BF16),
        ],
        compiler_params=_cparams(("arbitrary", "arbitrary")),
        name="inproj",
    )(x, mod_l, w_in_b, cs, sa, sb, ln_g, ln_b)


def _attn_kernel(lq1_ref, lk1_ref, lq2_ref, lk2_ref, g_ref, qT_ref, k_ref, vT_ref, o_ref,
                 q_scr, s_scr, m_scr, acc_scr, *, lam_init):
    tq = qT_ref.shape[1]
    tk = vT_ref.shape[2]
    i = pl.program_id(2)
    qT = qT_ref[...]
    rows = lax.broadcasted_iota(I32, qT.shape, 0)
    zero = jnp.zeros_like(qT)
    q_scr[0] = jnp.where(rows < QK_DIM, qT, zero)
    q_scr[1] = jnp.where(rows >= QK_DIM, qT, zero)
    m_scr[...] = jnp.full(m_scr.shape, -jnp.inf, F32)
    acc_scr[...] = jnp.zeros(acc_scr.shape, F32)

    def scores(j, slot, c0):
        kb = k_ref[pl.ds(pl.multiple_of(j * tk, tk), tk), :]
        for mp in range(2):
            s_scr[slot, mp, :, c0:] = jnp.dot(kb, q_scr[mp, :, c0:], preferred_element_type=F32)

    def softmax_pv(j, slot, c0, masked):
        vb = vT_ref[j]
        for mp in range(2):
            s = s_scr[slot, mp, :, c0:]
            if masked:
                kpos = lax.broadcasted_iota(I32, s.shape, 0)
                qpos = lax.broadcasted_iota(I32, s.shape, 1)
                s = jnp.where(kpos <= qpos, s, -jnp.inf)
            m_old = m_scr[mp, :, c0:]
            m_new = jnp.maximum(m_old, jnp.max(s, axis=0, keepdims=True))
            alpha = jnp.exp2(m_old - m_new)
            p = jnp.exp2(s - m_new).astype(BF16)
            acc_scr[mp, :, c0:] = alpha * acc_scr[mp, :, c0:] + jnp.dot(vb, p, preferred_element_type=F32)
            m_scr[mp, :, c0:] = m_new

    scores(0, 0, 0)

    def body(u, carry):
        t = 2 * u + 1
        scores(t, 1, 0)
        softmax_pv(t - 1, 0, 0, False)
        scores(t + 1, 0, 0)
        softmax_pv(t, 1, 0, False)
        return carry

    lax.fori_loop(0, i, body, 0)
    scores(2 * i + 1, 1, tk)
    softmax_pv(2 * i, 0, 0, True)
    softmax_pv(2 * i + 1, 1, tk, True)

    lam = (jnp.exp(jnp.sum(lq1_ref[...] * lk1_ref[...], axis=1, keepdims=True))
           - jnp.exp(jnp.sum(lq2_ref[...] * lk2_ref[...], axis=1, keepdims=True)) + lam_init)
    a1 = acc_scr[0]
    a2 = acc_scr[1]
    out = a1[0:V_DIM] / a1[V_DIM:V_DIM + 1] - lam * (a2[0:V_DIM] / a2[V_DIM:V_DIM + 1])
    ms = jnp.mean(out * out, axis=0, keepdims=True)
    y = out * lax.rsqrt(ms + LN_EPS) * g_ref[...] * (1.0 - lam_init)
    o_ref[...] = y.T.astype(BF16)


def _attention(qT, k, vTb, lams, g_col, lam_init):
    B, _, S = qT.shape
    nkv, tk = vTb.shape[1], vTb.shape[3]
    tq = 2 * tk
    small = pl.BlockSpec((1, QK_DIM), lambda b, h, i: (0, 0))
    return pl.pallas_call(
        functools.partial(_attn_kernel, lam_init=lam_init),
        grid=(B, N_HEADS, S // tq),
        in_specs=[
            small, small, small, small,
            pl.BlockSpec((V_DIM, 1), lambda b, h, i: (0, 0)),
            pl.BlockSpec((None, HEAD_COLS, tq), lambda b, h, i: (b, h, i)),
            pl.BlockSpec((None, S, HEAD_COLS), lambda b, h, i: (b, 0, h)),
            pl.BlockSpec((None, nkv, V_ROWS, tk), lambda b, h, i: (b, 0, h, 0)),
        ],
        out_specs=pl.BlockSpec((None, tq, V_DIM), lambda b, h, i: (b, i, h)),
        out_shape=jax.ShapeDtypeStruct((B, S, DIFF_WIDTH), BF16),
        scratch_shapes=[
            pltpu.VMEM((2, HEAD_COLS, tq), BF16),
            pltpu.VMEM((2, 2, tk, tq), F32),
            pltpu.VMEM((2, 1, tq), F32),
            pltpu.VMEM((2, V_ROWS, tq), F32),
        ],
        compiler_params=_cparams(("arbitrary", "arbitrary", "arbitrary")),
        name="diff_attn",
    )(*lams, g_col, qT, k, vTb)


def _mix_kernel(x_ref, mod_ref, att_ref, zu_ref, zvn_ref, wsp_ref, bsp_ref, wo_ref, g_ref, b_ref,
                o_ref, sg_scr, *, alpha):
    D = D_MODEL
    nc = zu_ref.shape[0]
    r = lax.broadcasted_iota(I32, (CHUNK, CHUNK), 0)
    c = lax.broadcasted_iota(I32, (CHUNK, CHUNK), 1)
    causal = r >= c
    for g in range(N_SG):
        sl = slice(g * SG_DIM, (g + 1) * SG_DIM)
        w = jnp.where(causal, wsp_ref[g], 0.0).astype(BF16)
        z = jnp.concatenate([zvn_ref[n, :, sl] for n in range(nc)], axis=1)
        mixed = jnp.dot(w, z, preferred_element_type=F32)
        for n in range(nc):
            gate = mixed[:, n * SG_DIM:(n + 1) * SG_DIM] + bsp_ref[g]
            sg_scr[n * CHUNK:(n + 1) * CHUNK, sl] = (zu_ref[n, :, sl].astype(F32) * gate).astype(BF16)
    mix = (jnp.dot(att_ref[...], wo_ref[0:DIFF_WIDTH, :], preferred_element_type=F32)
           + jnp.dot(sg_scr[...], wo_ref[DIFF_WIDTH:, :], preferred_element_type=F32))
    gt = mod_ref[:, 2 * D:3 * D]
    y = alpha * x_ref[...] + (1.0 + gt) * mix
    o_ref[...] = _layer_norm_rows(y, g_ref[...], b_ref[...])


def _mix(x, mod_l, att, zu, zvn, w_sp, b_sp_full, w_out_b, ln_g, ln_b, alpha):
    B, S, D = x.shape
    tm = min(ROW_TILE, S)
    nc = tm // CHUNK
    row = lambda b, i: (b, i, 0)
    zu4 = zu.reshape(B, S // CHUNK, CHUNK, SG_WIDTH)
    zvn4 = zvn.reshape(B, S // CHUNK, CHUNK, SG_WIDTH)
    chunked = pl.BlockSpec((None, nc, CHUNK, SG_WIDTH), lambda b, i: (b, i, 0, 0))
    full2 = lambda a: pl.BlockSpec(a.shape, lambda b, i: (0,) * a.ndim)
    return pl.pallas_call(
        functools.partial(_mix_kernel, alpha=alpha),
        grid=(B, S // tm),
        in_specs=[
            pl.BlockSpec((None, tm, D), row),
            pl.BlockSpec((None, 1, 6 * D), lambda b, i: (b, 0, 0)),
            pl.BlockSpec((None, tm, DIFF_WIDTH), row),
            chunked, chunked,
            full2(w_sp), full2(b_sp_full), full2(w_out_b), full2(ln_g), full2(ln_b),
        ],
        out_specs=pl.BlockSpec((None, tm, D), row),
        out_shape=jax.ShapeDtypeStruct((B, S, D), F32),
        scratch_shapes=[pltpu.VMEM((tm, SG_WIDTH), BF16)],
        compiler_params=_cparams(("arbitrary", "arbitrary")),
        name="sgate_outproj_ln",
    )(x, mod_l, att, zu4, zvn4, w_sp, b_sp_full, w_out_b, ln_g, ln_b)


def _route_kernel(x_ref, mod_ref, wr_hi_ref, wr_lo_ref, br_ref, hp_ref, rt_ref, cnt_ref, run_scr):
    D = D_MODEL
    tm = x_ref.shape[0]

    @pl.when((pl.program_id(0) == 0) & (pl.program_id(1) == 0))
    def _():
        run_scr[...] = jnp.zeros(run_scr.shape, F32)

    sh = mod_ref[:, 3 * D:4 * D]
    sc = mod_ref[:, 4 * D:5 * D]
    h = x_ref[...] * (1.0 + sc) + sh
    hp_ref[...] = _pack_halves(h)
    h_hi = h.astype(BF16)
    h_lo = (h - h_hi.astype(F32)).astype(BF16)
    logit = (jnp.dot(h_hi, wr_hi_ref[...], preferred_element_type=F32)
             + jnp.dot(h_lo, wr_hi_ref[...], preferred_element_type=F32)
             + jnp.dot(h_hi, wr_lo_ref[...], preferred_element_type=F32)) + br_ref[...]
    lane = lax.broadcasted_iota(I32, logit.shape, 1).astype(F32)
    neg = -jnp.inf
    big = float(LANES)

    def first_argmax(v):
        mx = jnp.max(v, axis=1, keepdims=True)
        idx = jnp.min(jnp.where(v == mx, lane, big), axis=1, keepdims=True)
        return mx, idx

    in_grp = lane < N_GROUPS
    gmax, gidx = first_argmax(jnp.where(in_grp, logit, neg))
    g_p = 1.0 / jnp.sum(jnp.where(in_grp, jnp.exp(logit - gmax), 0.0), axis=1, keepdims=True)
    lo_lane = ROUTE_LANE0 + EXP_PER_GROUP * gidx
    sel = jnp.where((lane >= lo_lane) & (lane < lo_lane + EXP_PER_GROUP), logit, neg)
    v1, i1 = first_argmax(sel)
    v2, i2 = first_argmax(jnp.where(lane == i1, neg, sel))
    t = jnp.exp(v2 - v1)
    w1 = g_p / (1.0 + t)
    w2 = g_p * t / (1.0 + t)

    hot1 = lane == i1
    hot2 = lane == i2
    twohot = jnp.where(hot1 | hot2, 1.0, 0.0)
    r = lax.broadcasted_iota(I32, (tm, tm), 0)
    c = lax.broadcasted_iota(I32, (tm, tm), 1)
    strict = jnp.where(r > c, 1.0, 0.0).astype(BF16)
    before = jnp.dot(strict, twohot.astype(BF16), preferred_element_type=F32) + run_scr[...]
    r1 = jnp.sum(jnp.where(hot1, before, 0.0), axis=1, keepdims=True)
    r2 = jnp.sum(jnp.where(hot2, before, 0.0), axis=1, keepdims=True)
    run_scr[...] = run_scr[...] + jnp.sum(twohot, axis=0, keepdims=True)
    cnt_ref[...] = run_scr[...]

    fields = (i1 - ROUTE_LANE0, i2 - ROUTE_LANE0, w1, w2, r1, r2)
    slab = jnp.zeros(logit.shape, F32)
    for n, f in enumerate(fields):
        slab = jnp.where(lane == float(n), f, slab)
    rt_ref[...] = slab.T[0:8, :]


def _route(x1, mod_l, wr_hi, wr_lo, br):
    B, S, D = x1.shape
    tm = min(ROW_TILE, S)
    row = lambda b, i: (b, i, 0)
    full2 = lambda a: pl.BlockSpec(a.shape, lambda b, i: (0,) * a.ndim)
    return pl.pallas_call(
        _route_kernel,
        grid=(B, S // tm),
        in_specs=[
            pl.BlockSpec((None, tm, D), row),
            pl.BlockSpec((None, 1, 6 * D), lambda b, i: (b, 0, 0)),
            full2(wr_hi), full2(wr_lo), full2(br),
        ],
        out_specs=[
            pl.BlockSpec((None, tm, HALF), row),
            pl.BlockSpec((None, 8, tm), lambda b, i: (b, 0, i)),
            pl.BlockSpec((1, LANES), lambda b, i: (0, 0)),
        ],
        out_shape=[
            jax.ShapeDtypeStruct((B, S, HALF), U32),
            jax.ShapeDtypeStruct((B, 8, S), F32),
            jax.ShapeDtypeStruct((1, LANES), F32),
        ],
        scratch_shapes=[pltpu.VMEM((1, LANES), F32)],
        compiler_params=_cparams(("arbitrary", "arbitrary")),
        name="moe_route",
    )(x1, mod_l, wr_hi, wr_lo, br)


def _dispatch_kernel(p0_ref, p1_ref, hp_ref, xs_in_ref, xs_ref, sem):
    del xs_in_ref
    nb = p0_ref.shape[2]

    def row_copy(t, p):
        return pltpu.make_async_copy(hp_ref.at[pl.ds(t, 1)], xs_ref.at[pl.ds(p, 1)], sem)

    def issue(t8, carry):
        for u in range(ROW_DMA_UNROLL):
            t = t8 * ROW_DMA_UNROLL + u
            row_copy(t, p0_ref[0, 0, t]).start(priority=u % 2)
            row_copy(t, p1_ref[0, 0, t]).start(priority=(u + 1) % 2)
        return carry

    lax.fori_loop(0, nb // ROW_DMA_UNROLL, issue, 0)

    def drain(t, carry):
        row_copy(0, 0).wait()
        row_copy(0, 0).wait()
        return carry

    lax.fori_loop(0, nb, drain, 0, unroll=8)


def _dispatch(pos0, pos1, hp, n_rows):
    T = hp.shape[0]
    nb = min(DISPATCH_BLOCK, T)
    blk = pl.BlockSpec((1, 1, nb), lambda i: (i, 0, 0), memory_space=pltpu.SMEM)
    xs0 = jnp.zeros((n_rows, HALF), U32)
    return pl.pallas_call(
        _dispatch_kernel,
        grid=(T // nb,),
        in_specs=[blk, blk, pl.BlockSpec((nb, HALF), lambda i: (i, 0)), pl.BlockSpec(memory_space=pl.ANY)],
        out_specs=pl.BlockSpec(memory_space=pl.ANY),
        out_shape=jax.ShapeDtypeStruct((n_rows, HALF), U32),
        scratch_shapes=[pltpu.SemaphoreType.DMA(())],
        input_output_aliases={3: 0},
        compiler_params=_cparams(("arbitrary",)),
        name="moe_dispatch",
    )(pos0.reshape(T // nb, 1, nb), pos1.reshape(T // nb, 1, nb), hp, xs0)


def _expert_kernel(te_ref, nv_ref, xs_ref, wg_ref, wu_ref, wd_ref, ys_ref):
    del te_ref

    @pl.when(pl.program_id(0) < nv_ref[0])
    def _():
        lo, hi = _unpack_halves(xs_ref[...])
        lo = lo.astype(BF16)
        hi = hi.astype(BF16)
        g = (jnp.dot(lo, wg_ref[0:HALF, :], preferred_element_type=F32)
             + jnp.dot(hi, wg_ref[HALF:, :], preferred_element_type=F32))
        u = (jnp.dot(lo, wu_ref[0:HALF, :], preferred_element_type=F32)
             + jnp.dot(hi, wu_ref[HALF:, :], preferred_element_type=F32))
        a = (g * jax.nn.sigmoid(g) * u).astype(BF16)
        ys_ref[...] = _pack_halves(jnp.dot(a, wd_ref[...], preferred_element_type=F32))


def _experts(tile_expert, n_valid, xs, wg_b, wu_b, wd_b):
    n_rows = xs.shape[0]
    tm = EXPERT_TILE
    nt = n_rows // tm
    rows = lambda n, te, nv: (jnp.minimum(n, nv[0] - 1), 0)
    wsel = lambda n, te, nv: (te[n], 0, 0)
    return pl.pallas_call(
        _expert_kernel,
        grid_spec=pltpu.PrefetchScalarGridSpec(
            num_scalar_prefetch=2,
            grid=(nt,),
            in_specs=[
                pl.BlockSpec((tm, HALF), rows),
                pl.BlockSpec((None, D_MODEL, D_EXPERT), wsel),
                pl.BlockSpec((None, D_MODEL, D_EXPERT), wsel),
                pl.BlockSpec((None, D_EXPERT, D_MODEL), wsel),
            ],
            out_specs=pl.BlockSpec((tm, HALF), rows),
        ),
        out_shape=jax.ShapeDtypeStruct((n_rows, HALF), U32),
        compiler_params=_cparams(("arbitrary",)),
        name="moe_experts",
    )(tile_expert, n_valid, xs, wg_b, wu_b, wd_b)


def _combine_kernel(p0_ref, p1_ref, p0n_ref, p1n_ref, ys_ref, x_ref, mod_ref, rt_ref, g_ref, b_ref, o_ref,
                    buf, sems, *, alpha):
    D = D_MODEL
    tc = x_ref.shape[0]
    g = pl.program_id(0) * pl.num_programs(1) + pl.program_id(1)
    n_steps = pl.num_programs(0) * pl.num_programs(1)
    slot = g % 2

    def row_copy(p, s, choice, t):
        return pltpu.make_async_copy(ys_ref.at[pl.ds(p, 1)], buf.at[s, choice, pl.ds(t, 1)], sems.at[s])

    def gather(pa_ref, pb_ref, s):
        def issue(t8, carry):
            for u in range(ROW_DMA_UNROLL):
                t = t8 * ROW_DMA_UNROLL + u
                row_copy(pa_ref[0, 0, t], s, 0, t).start(priority=u % 2)
                row_copy(pb_ref[0, 0, t], s, 1, t).start(priority=(u + 1) % 2)
            return carry

        lax.fori_loop(0, tc // ROW_DMA_UNROLL, issue, 0)

    @pl.when(g == 0)
    def _():
        gather(p0_ref, p1_ref, 0)

    @pl.when(g + 1 < n_steps)
    def _():
        gather(p0n_ref, p1n_ref, 1 - slot)

    def drain(t, carry):
        row_copy(0, slot, 0, 0).wait()
        row_copy(0, slot, 0, 0).wait()
        return carry

    lax.fori_loop(0, tc, drain, 0, unroll=8)

    wt = rt_ref[...].T
    w1 = wt[:, 2:3]
    w2 = wt[:, 3:4]
    lo1, hi1 = _unpack_halves(buf[slot, 0])
    lo2, hi2 = _unpack_halves(buf[slot, 1])
    ffn = jnp.concatenate([w1 * lo1 + w2 * lo2, w1 * hi1 + w2 * hi2], axis=1)
    gt = mod_ref[:, 5 * D:6 * D]
    y = alpha * x_ref[...] + (1.0 + gt) * ffn
    o_ref[...] = _layer_norm_rows(y, g_ref[...], b_ref[...])


def _combine(pos0, pos1, ys, x1, mod_l, rt, ln_g, ln_b, alpha):
    B, S, D = x1.shape
    tc = min(COMBINE_TILE, S)
    nt = S // tc
    blk = pl.BlockSpec((1, 1, tc), lambda b, i: (b * nt + i, 0, 0), memory_space=pltpu.SMEM)
    nxt = pl.BlockSpec((1, 1, tc), lambda b, i: (jnp.minimum(b * nt + i + 1, B * nt - 1), 0, 0),
                       memory_space=pltpu.SMEM)
    row = lambda b, i: (b, i, 0)
    full2 = lambda a: pl.BlockSpec(a.shape, lambda b, i: (0,) * a.ndim)
    p0 = pos0.reshape(B * nt, 1, tc)
    p1 = pos1.reshape(B * nt, 1, tc)
    return pl.pallas_call(
        functools.partial(_combine_kernel, alpha=alpha),
        grid=(B, nt),
        in_specs=[
            blk, blk, nxt, nxt,
            pl.BlockSpec(memory_space=pl.ANY),
            pl.BlockSpec((None, tc, D), row),
            pl.BlockSpec((None, 1, 6 * D), lambda b, i: (b, 0, 0)),
            pl.BlockSpec((None, 8, tc), lambda b, i: (b, 0, i)),
            full2(ln_g), full2(ln_b),
        ],
        out_specs=pl.BlockSpec((None, tc, D), row),
        out_shape=jax.ShapeDtypeStruct((B, S, D), F32),
        scratch_shapes=[pltpu.VMEM((2, 2, tc, HALF), U32), pltpu.SemaphoreType.DMA((2,))],
        compiler_params=_cparams(("arbitrary", "arbitrary")),
        name="moe_combine_ln",
    )(p0, p1, p0, p1, ys, x1, mod_l, rt, ln_g, ln_b)


def _rope_tables(S):
    inv = 1.0 / (ROPE_THETA ** (jnp.arange(0, QK_DIM, 2, dtype=F32) / QK_DIM))
    ang = jnp.arange(S, dtype=F32)[:, None] * inv[None, :]
    cos, sin = jnp.cos(ang), jnp.sin(ang)
    half = QK_DIM // 2
    first = (jnp.arange(LANES) % QK_DIM) < half
    cs = jnp.tile(cos, (1, LANES // half))
    sn = jnp.tile(sin, (1, LANES // half))
    sa = jnp.where(first[None, :], -sn, 0.0)
    sb = jnp.where(first[None, :], 0.0, sn)
    return cs, sa, sb


def _router_matrix(w_group, b_group, w_router, b_router):
    D = w_group.shape[0]
    w = jnp.zeros((D, LANES), F32)
    w = w.at[:, 0:N_GROUPS].set(w_group)
    wr = jnp.transpose(w_router, (1, 0, 2)).reshape(D, N_EXPERTS)
    w = w.at[:, ROUTE_LANE0:ROUTE_LANE0 + N_EXPERTS].set(wr)
    b = jnp.zeros((1, LANES), F32)
    b = b.at[0, 0:N_GROUPS].set(b_group)
    b = b.at[0, ROUTE_LANE0:ROUTE_LANE0 + N_EXPERTS].set(b_router.reshape(N_EXPERTS))
    hi = w.astype(BF16)
    lo = (w - hi.astype(F32)).astype(BF16)
    return hi, lo, b


def _routing_tables(rt, cnt, n_tiles):
    B, _, S = rt.shape
    tm = EXPERT_TILE
    e = rt[:, 0:2, :].astype(I32)
    rank = rt[:, 4:6, :].astype(I32)
    counts = cnt[0, ROUTE_LANE0:ROUTE_LANE0 + N_EXPERTS].astype(I32)
    tiles = (counts + tm - 1) // tm
    tile_end = jnp.cumsum(tiles)
    tile_start = tile_end - tiles
    onehot = e[..., None] == jnp.arange(N_EXPERTS, dtype=I32)
    start = jnp.sum(jnp.where(onehot, tile_start * tm, 0), axis=-1)
    pos = start + rank
    pos0 = pos[:, 0, :].reshape(B * S)
    pos1 = pos[:, 1, :].reshape(B * S)
    tile_ids = jnp.arange(n_tiles, dtype=I32)
    tile_expert = jnp.minimum(jnp.sum(tile_ids[:, None] >= tile_end[None, :], axis=1), N_EXPERTS - 1).astype(I32)
    n_valid = tile_end[-1:].astype(I32)
    return pos0, pos1, tile_expert, n_valid


def kernel(x, c, w_ada, b_ada, w_in, lambda_q1, lambda_k1, lambda_q2, lambda_k2, subln_g, sg_ln_g, sg_ln_b, w_spatial, b_spatial, w_out, ln1_g, ln1_b, w_group, b_group, w_router, b_router, w_gate, w_up, w_down, ln2_g, ln2_b):
    B, S, D = x.shape
    depth = w_in.shape[0]
    T = B * S
    alpha = (2.0 * depth) ** 0.25
    n_tiles = (2 * T) // EXPERT_TILE + N_EXPERTS
    n_rows = n_tiles * EXPERT_TILE

    mod = _ada(c, w_ada, b_ada)
    tabs = _rope_tables(S)
    for l in range(depth):
        mod_l = mod[l].reshape(B, 1, 6 * D)
        lam_init = 0.8 - 0.6 * math.exp(-0.3 * l)

        qT, k, vTb, zu, zvn = _inproj(x, mod_l, w_in[l].astype(BF16), tabs,
                                      sg_ln_g[l].reshape(1, SG_WIDTH), sg_ln_b[l].reshape(1, SG_WIDTH))
        lams = (lambda_q1[l].reshape(1, QK_DIM), lambda_k1[l].reshape(1, QK_DIM),
                lambda_q2[l].reshape(1, QK_DIM), lambda_k2[l].reshape(1, QK_DIM))
        att = _attention(qT, k, vTb, lams, subln_g[l].reshape(V_DIM, 1), lam_init)
        b_sp_full = jnp.broadcast_to(b_spatial[l][:, :, None], (N_SG, CHUNK, SG_DIM))
        x1 = _mix(x, mod_l, att, zu, zvn, w_spatial[l], b_sp_full, w_out[l].astype(BF16),
                  ln1_g[l].reshape(1, D), ln1_b[l].reshape(1, D), alpha)

        wr_hi, wr_lo, br = _router_matrix(w_group[l], b_group[l], w_router[l], b_router[l])
        hp, rt, cnt = _route(x1, mod_l, wr_hi, wr_lo, br)
        pos0, pos1, tile_expert, n_valid = _routing_tables(rt, cnt, n_tiles)
        xs = _dispatch(pos0, pos1, hp.reshape(T, HALF), n_rows)
        ys = _experts(tile_expert, n_valid, xs, w_gate[l].astype(BF16), w_up[l].astype(BF16),
                      w_down[l].astype(BF16))
        x = _combine(pos0, pos1, ys, x1, mod_l, rt, ln2_g[l].reshape(1, D), ln2_b[l].reshape(1, D), alpha)
    return x
```

```python
import functools
import math

import jax
import jax.numpy as jnp
from jax import lax
from jax.experimental import pallas as pl
from jax.experimental.pallas import tpu as pltpu

F32 = jnp.float32
BF16 = jnp.bfloat16
U32 = jnp.uint32
I32 = jnp.int32

D_MODEL = 1024
N_HEADS = 4
QK_DIM = 64
V_DIM = 128
HEAD_COLS = 2 * QK_DIM
QK_COLS = N_HEADS * HEAD_COLS
DIFF_WIDTH = N_HEADS * V_DIM
N_SG = 4
SG_DIM = 128
SG_WIDTH = N_SG * SG_DIM
CHUNK = 128
N_GROUPS = 4
EXP_PER_GROUP = 8
N_EXPERTS = N_GROUPS * EXP_PER_GROUP
D_EXPERT = 512
ROPE_THETA = 10000.0
LN_EPS = 1e-5
LANES = 128
HALF = D_MODEL // 2

ROW_TILE = 512
V_ROWS = V_DIM + 16
EXPERT_TILE = 512
DISPATCH_BLOCK = 2048
COMBINE_TILE = 256
ROW_DMA_UNROLL = 8
ROUTE_LANE0 = 8
VMEM_LIMIT = 48 * 1024 * 1024

DEN_MIN = 2.0 ** -40
DEN_MAX = 2.0 ** 40
LOG2E = 1.4426950408889634
Q_SCALE = (QK_DIM ** -0.5) * LOG2E


def _cparams(sem):
    return pltpu.CompilerParams(dimension_semantics=sem, vmem_limit_bytes=VMEM_LIMIT)


def _layer_norm_rows(y, g, b):
    mu = jnp.mean(y, axis=-1, keepdims=True)
    yc = y - mu
    var = jnp.mean(yc * yc, axis=-1, keepdims=True)
    return yc * lax.rsqrt(var + LN_EPS) * g + b


def _gelu(x):
    return 0.5 * x * (1.0 + lax.erf(x * (2.0 ** -0.5)))


def _pack_halves(y):
    lo = pltpu.bitcast(y[:, :HALF].astype(BF16).astype(F32), U32) >> 16
    hi = pltpu.bitcast(y[:, HALF:].astype(BF16).astype(F32), U32) & jnp.uint32(0xFFFF0000)
    return lo | hi


def _unpack_halves(p):
    lo = pltpu.bitcast(p << 16, F32)
    hi = pltpu.bitcast(p & jnp.uint32(0xFFFF0000), F32)
    return lo, hi


def _ada_kernel(c_ref, w_ref, b_ref, o_ref):
    c = c_ref[...]
    sc = c * jax.nn.sigmoid(c)
    o_ref[...] = jnp.dot(sc, w_ref[...], precision=lax.Precision.HIGHEST,
                         preferred_element_type=F32) + b_ref[...]


def _ada(c, w_ada, b_ada):
    L, D, N = w_ada.shape
    B = c.shape[0]
    tn = 1536
    return pl.pallas_call(
        _ada_kernel,
        grid=(L, N // tn),
        in_specs=[
            pl.BlockSpec((B, D), lambda l, j: (0, 0)),
            pl.BlockSpec((None, D, tn), lambda l, j: (l, 0, j)),
            pl.BlockSpec((None, 1, tn), lambda l, j: (l, 0, j)),
        ],
        out_specs=pl.BlockSpec((None, B, tn), lambda l, j: (l, 0, j)),
        out_shape=jax.ShapeDtypeStruct((L, B, N), F32),
        compiler_params=_cparams(("arbitrary", "arbitrary")),
        name="ada",
    )(c, w_ada, b_ada.reshape(L, 1, N))


def _inproj_kernel(x_ref, mod_ref, w_ref, cs_ref, sa_ref, sb_ref, lng_ref, lnb_ref,
                   qT_ref, k_ref, vT_ref, zu_ref, zvn_ref):
    D = D_MODEL
    x = x_ref[...]
    sh = mod_ref[:, 0:D]
    sc = mod_ref[:, D:2 * D]
    h = (x * (1.0 + sc) + sh).astype(BF16)
    cs = cs_ref[...]
    sa = sa_ref[...]
    sb = sb_ref[...]

    def rope(t):
        return t * cs + pltpu.roll(t, 96, 1) * sa + pltpu.roll(t, 32, 1) * sb

    q = jnp.dot(h, w_ref[:, 0:QK_COLS], preferred_element_type=F32)
    for j in range(N_HEADS):
        sl = slice(j * LANES, (j + 1) * LANES)
        qT_ref[sl, :] = (rope(q[:, sl]) * Q_SCALE).T.astype(BF16)
    k = jnp.dot(h, w_ref[:, QK_COLS:2 * QK_COLS], preferred_element_type=F32)
    for j in range(N_HEADS):
        sl = slice(j * LANES, (j + 1) * LANES)
        k_ref[:, sl] = rope(k[:, sl]).astype(BF16)
    c0 = 2 * QK_COLS
    v = jnp.dot(h, w_ref[:, c0:c0 + DIFF_WIDTH], preferred_element_type=F32)
    for j in range(N_HEADS):
        vT_ref[j * V_ROWS:j * V_ROWS + V_DIM, :] = v[:, j * V_DIM:(j + 1) * V_DIM].T.astype(BF16)
        vT_ref[j * V_ROWS + V_DIM:(j + 1) * V_ROWS, :] = jnp.ones((V_ROWS - V_DIM, v.shape[0]), BF16)
    c0 += DIFF_WIDTH
    u = jnp.dot(h, w_ref[:, c0:c0 + SG_WIDTH], preferred_element_type=F32)
    zu_ref[...] = _gelu(u).astype(BF16)
    c0 += SG_WIDTH
    z = _gelu(jnp.dot(h, w_ref[:, c0:c0 + SG_WIDTH], preferred_element_type=F32))
    for g in range(N_SG):
        sl = slice(g * SG_DIM, (g + 1) * SG_DIM)
        zvn_ref[:, sl] = _layer_norm_rows(z[:, sl], lng_ref[:, sl], lnb_ref[:, sl]).astype(BF16)


def _inproj(x, mod_l, w_in_b, rope_tabs, ln_g, ln_b):
    B, S, D = x.shape
    tm = min(ROW_TILE, S)
    nt = S // tm
    cs, sa, sb = rope_tabs
    row = lambda b, i: (b, i, 0)
    tab = pl.BlockSpec((tm, LANES), lambda b, i: (i, 0))
    return pl.pallas_call(
        _inproj_kernel,
        grid=(B, nt),
        in_specs=[
            pl.BlockSpec((None, tm, D), row),
            pl.BlockSpec((None, 1, 6 * D), lambda b, i: (b, 0, 0)),
            pl.BlockSpec(w_in_b.shape, lambda b, i: (0, 0)),
            tab, tab, tab,
            pl.BlockSpec((1, SG_WIDTH), lambda b, i: (0, 0)),
            pl.BlockSpec((1, SG_WIDTH), lambda b, i: (0, 0)),
        ],
        out_specs=[
            pl.BlockSpec((None, QK_COLS, tm), lambda b, i: (b, 0, i)),
            pl.BlockSpec((None, tm, QK_COLS), row),
            pl.BlockSpec((None, None, N_HEADS * V_ROWS, tm), lambda b, i: (b, i, 0, 0)),
            pl.BlockSpec((None, tm, SG_WIDTH), row),
            pl.BlockSpec((None, tm, SG_WIDTH), row),
        ],
        out_shape=[
            jax.ShapeDtypeStruct((B, QK_COLS, S), BF16),
            jax.ShapeDtypeStruct((B, S, QK_COLS), BF16),
            jax.ShapeDtypeStruct((B, nt, N_HEADS * V_ROWS, tm), BF16),
            jax.ShapeDtypeStruct((B, S, SG_WIDTH), BF16),
            jax.ShapeDtypeStruct((B, S, SG_WIDTH), BF16),
        ],
        compiler_params=_cparams(("arbitrary", "arbitrary")),
        name="inproj",
    )(x, mod_l, w_in_b, cs, sa, sb, ln_g, ln_b)


def _attn_kernel(lq1_ref, lk1_ref, lq2_ref, lk2_ref, g_ref, qT_ref, k_ref, vT_ref, o_ref,
                 q_scr, s_scr, m_scr, acc_scr, *, lam_init):
    tq = qT_ref.shape[1]
    tk = vT_ref.shape[2]
    i = pl.program_id(2)
    qT = qT_ref[...]
    rows = lax.broadcasted_iota(I32, qT.shape, 0)
    zero = jnp.zeros_like(qT)
    q_scr[0] = jnp.where(rows < QK_DIM, qT, zero)
    q_scr[1] = jnp.where(rows >= QK_DIM, qT, zero)

    def scores(j, slot, c0):
        kb = k_ref[pl.ds(pl.multiple_of(j * tk, tk), tk), :]
        for mp in range(2):
            s_scr[slot, mp, :, c0:] = jnp.dot(kb, q_scr[mp, :, c0:], preferred_element_type=F32)

    def block_scores(slot, mp, c0, c1, masked):
        s = s_scr[slot, mp, :, c0:c1]
        if masked:
            kpos = lax.broadcasted_iota(I32, s.shape, 0)
            qpos = lax.broadcasted_iota(I32, s.shape, 1)
            s = jnp.where(kpos <= qpos, s, -jnp.inf)
        return s

    def pv_unshifted(j, slot, c0, c1, masked):
        vb = vT_ref[j]
        for mp in range(2):
            p = jnp.exp2(block_scores(slot, mp, c0, c1, masked)).astype(BF16)
            acc_scr[mp, :, c0:c1] += jnp.dot(vb, p, preferred_element_type=F32)

    def pv_online(j, slot, c0, c1, masked):
        vb = vT_ref[j]
        for mp in range(2):
            s = block_scores(slot, mp, c0, c1, masked)
            m_old = m_scr[mp, :, c0:c1]
            m_new = jnp.maximum(m_old, jnp.max(s, axis=0, keepdims=True))
            alpha = jnp.exp2(m_old - m_new)
            p = jnp.exp2(s - m_new).astype(BF16)
            acc_scr[mp, :, c0:c1] = alpha * acc_scr[mp, :, c0:c1] + jnp.dot(vb, p, preferred_element_type=F32)
            m_scr[mp, :, c0:c1] = m_new

    def sweep(block):
        acc_scr[...] = jnp.zeros(acc_scr.shape, F32)
        scores(0, 0, 0)

        def body(u, carry):
            t = 2 * u + 1
            scores(t, 1, 0)
            block(t - 1, 0, 0, tq, False)
            scores(t + 1, 0, 0)
            block(t, 1, 0, tq, False)
            return carry

        lax.fori_loop(0, i, body, 0)
        scores(2 * i + 1, 1, tk)
        block(2 * i, 0, 0, tk, True)
        block(2 * i, 0, tk, tq, False)
        block(2 * i + 1, 1, tk, tq, True)

    sweep(pv_unshifted)
    den = jnp.concatenate([acc_scr[0, V_DIM:V_DIM + 1, :], acc_scr[1, V_DIM:V_DIM + 1, :]], axis=0)
    in_range = (den >= DEN_MIN) & (den <= DEN_MAX)
    n_bad = jnp.sum(jnp.where(in_range, 0.0, 1.0))

    @pl.when(n_bad > 0.0)
    def _():
        m_scr[...] = jnp.full(m_scr.shape, -jnp.inf, F32)
        sweep(pv_online)

    lam = (jnp.exp(jnp.sum(lq1_ref[...] * lk1_ref[...], axis=1, keepdims=True))
           - jnp.exp(jnp.sum(lq2_ref[...] * lk2_ref[...], axis=1, keepdims=True)) + lam_init)
    a1 = acc_scr[0]
    a2 = acc_scr[1]
    out = a1[0:V_DIM] / a1[V_DIM:V_DIM + 1] - lam * (a2[0:V_DIM] / a2[V_DIM:V_DIM + 1])
    ms = jnp.mean(out * out, axis=0, keepdims=True)
    y = out * lax.rsqrt(ms + LN_EPS) * g_ref[...] * (1.0 - lam_init)
    o_ref[...] = y.T.astype(BF16)


def _attention(qT, k, vTb, lams, g_col, lam_init):
    B, _, S = qT.shape
    nkv, tk = vTb.shape[1], vTb.shape[3]
    tq = 2 * tk
    small = pl.BlockSpec((1, QK_DIM), lambda b, h, i: (0, 0))
    return pl.pallas_call(
        functools.partial(_attn_kernel, lam_init=lam_init),
        grid=(B, N_HEADS, S // tq),
        in_specs=[
            small, small, small, small,
            pl.BlockSpec((V_DIM, 1), lambda b, h, i: (0, 0)),
            pl.BlockSpec((None, HEAD_COLS, tq), lambda b, h, i: (b, h, i)),
            pl.BlockSpec((None, S, HEAD_COLS), lambda b, h, i: (b, 0, h)),
            pl.BlockSpec((None, nkv, V_ROWS, tk), lambda b, h, i: (b, 0, h, 0)),
        ],
        out_specs=pl.BlockSpec((None, tq, V_DIM), lambda b, h, i: (b, i, h)),
        out_shape=jax.ShapeDtypeStruct((B, S, DIFF_WIDTH), BF16),
        scratch_shapes=[
            pltpu.VMEM((2, HEAD_COLS, tq), BF16),
            pltpu.VMEM((2, 2, tk, tq), F32),
            pltpu.VMEM((2, 1, tq), F32),
            pltpu.VMEM((2, V_ROWS, tq), F32),
        ],
        compiler_params=_cparams(("arbitrary", "arbitrary", "arbitrary")),
        name="diff_attn",
    )(*lams, g_col, qT, k, vTb)


def _mix_kernel(x_ref, mod_ref, att_ref, zu_ref, zvn_ref, wsp_ref, bsp_ref, wo_ref, g_ref, b_ref,
                o_ref, sg_scr, *, alpha):
    D = D_MODEL
    nc = zu_ref.shape[0]
    r = lax.broadcasted_iota(I32, (CHUNK, CHUNK), 0)
    c = lax.broadcasted_iota(I32, (CHUNK, CHUNK), 1)
    causal = r >= c
    for g in range(N_SG):
        sl = slice(g * SG_DIM, (g + 1) * SG_DIM)
        w = jnp.where(causal, wsp_ref[g], 0.0).astype(BF16)
        z = jnp.concatenate([zvn_ref[n, :, sl] for n in range(nc)], axis=1)
        mixed = jnp.dot(w, z, preferred_element_type=F32)
        for n in range(nc):
            gate = mixed[:, n * SG_DIM:(n + 1) * SG_DIM] + bsp_ref[g]
            sg_scr[n * CHUNK:(n + 1) * CHUNK, sl] = (zu_ref[n, :, sl].astype(F32) * gate).astype(BF16)
    mix = (jnp.dot(att_ref[...], wo_ref[0:DIFF_WIDTH, :], preferred_element_type=F32)
           + jnp.dot(sg_scr[...], wo_ref[DIFF_WIDTH:, :], preferred_element_type=F32))
    gt = mod_ref[:, 2 * D:3 * D]
    y = alpha * x_ref[...] + (1.0 + gt) * mix
    o_ref[...] = _layer_norm_rows(y, g_ref[...], b_ref[...])


def _mix(x, mod_l, att, zu, zvn, w_sp, b_sp_full, w_out_b, ln_g, ln_b, alpha):
    B, S, D = x.shape
    tm = min(ROW_TILE, S)
    nc = tm // CHUNK
    row = lambda b, i: (b, i, 0)
    zu4 = zu.reshape(B, S // CHUNK, CHUNK, SG_WIDTH)
    zvn4 = zvn.reshape(B, S // CHUNK, CHUNK, SG_WIDTH)
    chunked = pl.BlockSpec((None, nc, CHUNK, SG_WIDTH), lambda b, i: (b, i, 0, 0))
    full2 = lambda a: pl.BlockSpec(a.shape, lambda b, i: (0,) * a.ndim)
    return pl.pallas_call(
        functools.partial(_mix_kernel, alpha=alpha),
        grid=(B, S // tm),
        in_specs=[
            pl.BlockSpec((None, tm, D), row),
            pl.BlockSpec((None, 1, 6 * D), lambda b, i: (b, 0, 0)),
            pl.BlockSpec((None, tm, DIFF_WIDTH), row),
            chunked, chunked,
            full2(w_sp), full2(b_sp_full), full2(w_out_b), full2(ln_g), full2(ln_b),
        ],
        out_specs=pl.BlockSpec((None, tm, D), row),
        out_shape=jax.ShapeDtypeStruct((B, S, D), F32),
        scratch_shapes=[pltpu.VMEM((tm, SG_WIDTH), BF16)],
        compiler_params=_cparams(("arbitrary", "arbitrary")),
        name="sgate_outproj_ln",
    )(x, mod_l, att, zu4, zvn4, w_sp, b_sp_full, w_out_b, ln_g, ln_b)


def _route_kernel(x_ref, mod_ref, wr_hi_ref, wr_lo_ref, br_ref, hp_ref, rt_ref, cnt_ref, run_scr):
    D = D_MODEL
    tm = x_ref.shape[0]

    @pl.when((pl.program_id(0) == 0) & (pl.program_id(1) == 0))
    def _():
        run_scr[...] = jnp.zeros(run_scr.shape, F32)

    sh = mod_ref[:, 3 * D:4 * D]
    sc = mod_ref[:, 4 * D:5 * D]
    h = x_ref[...] * (1.0 + sc) + sh
    hp_ref[...] = _pack_halves(h)
    h_hi = h.astype(BF16)
    h_lo = (h - h_hi.astype(F32)).astype(BF16)
    logit = (jnp.dot(h_hi, wr_hi_ref[...], preferred_element_type=F32)
             + jnp.dot(h_lo, wr_hi_ref[...], preferred_element_type=F32)
             + jnp.dot(h_hi, wr_lo_ref[...], preferred_element_type=F32)) + br_ref[...]
    lane = lax.broadcasted_iota(I32, logit.shape, 1).astype(F32)
    neg = -jnp.inf
    big = float(LANES)

    def first_argmax(v):
        mx = jnp.max(v, axis=1, keepdims=True)
        idx = jnp.min(jnp.where(v == mx, lane, big), axis=1, keepdims=True)
        return mx, idx

    in_grp = lane < N_GROUPS
    gmax, gidx = first_argmax(jnp.where(in_grp, logit, neg))
    g_p = 1.0 / jnp.sum(jnp.where(in_grp, jnp.exp(logit - gmax), 0.0), axis=1, keepdims=True)
    lo_lane = ROUTE_LANE0 + EXP_PER_GROUP * gidx
    sel = jnp.where((lane >= lo_lane) & (lane < lo_lane + EXP_PER_GROUP), logit, neg)
    v1, i1 = first_argmax(sel)
    v2, i2 = first_argmax(jnp.where(lane == i1, neg, sel))
    t = jnp.exp(v2 - v1)
    w1 = g_p / (1.0 + t)
    w2 = g_p * t / (1.0 + t)

    hot1 = lane == i1
    hot2 = lane == i2
    twohot = jnp.where(hot1 | hot2, 1.0, 0.0)
    r = lax.broadcasted_iota(I32, (tm, tm), 0)
    c = lax.broadcasted_iota(I32, (tm, tm), 1)
    strict = jnp.where(r > c, 1.0, 0.0).astype(BF16)
    before = jnp.dot(strict, twohot.astype(BF16), preferred_element_type=F32) + run_scr[...]
    r1 = jnp.sum(jnp.where(hot1, before, 0.0), axis=1, keepdims=True)
    r2 = jnp.sum(jnp.where(hot2, before, 0.0), axis=1, keepdims=True)
    run_scr[...] = run_scr[...] + jnp.sum(twohot, axis=0, keepdims=True)
    cnt_ref[...] = run_scr[...]

    fields = (i1 - ROUTE_LANE0, i2 - ROUTE_LANE0, w1, w2, r1, r2)
    slab = jnp.zeros(logit.shape, F32)
    for n, f in enumerate(fields):
        slab = jnp.where(lane == float(n), f, slab)
    rt_ref[...] = slab.T[0:8, :]


def _route(x1, mod_l, wr_hi, wr_lo, br):
    B, S, D = x1.shape
    tm = min(ROW_TILE, S)
    row = lambda b, i: (b, i, 0)
    full2 = lambda a: pl.BlockSpec(a.shape, lambda b, i: (0,) * a.ndim)
    return pl.pallas_call(
        _route_kernel,
        grid=(B, S // tm),
        in_specs=[
            pl.BlockSpec((None, tm, D), row),
            pl.BlockSpec((None, 1, 6 * D), lambda b, i: (b, 0, 0)),
            full2(wr_hi), full2(wr_lo), full2(br),
        ],
        out_specs=[
            pl.BlockSpec((None, tm, HALF), row),
            pl.BlockSpec((None, 8, tm), lambda b, i: (b, 0, i)),
            pl.BlockSpec((1, LANES), lambda b, i: (0, 0)),
        ],
        out_shape=[
            jax.ShapeDtypeStruct((B, S, HALF), U32),
            jax.ShapeDtypeStruct((B, 8, S), F32),
            jax.ShapeDtypeStruct((1, LANES), F32),
        ],
        scratch_shapes=[pltpu.VMEM((1, LANES), F32)],
        compiler_params=_cparams(("arbitrary", "arbitrary")),
        name="moe_route",
    )(x1, mod_l, wr_hi, wr_lo, br)


def _dispatch_kernel(p0_ref, p1_ref, hp_ref, xs_in_ref, xs_ref, sem):
    del xs_in_ref
    nb = p0_ref.shape[2]

    def row_copy(t, p):
        return pltpu.make_async_copy(hp_ref.at[pl.ds(t, 1)], xs_ref.at[pl.ds(p, 1)], sem)

    def issue(t8, carry):
        for u in range(ROW_DMA_UNROLL):
            t = t8 * ROW_DMA_UNROLL + u
            row_copy(t, p0_ref[0, 0, t]).start(priority=u % 2)
            row_copy(t, p1_ref[0, 0, t]).start(priority=(u + 1) % 2)
        return carry

    lax.fori_loop(0, nb // ROW_DMA_UNROLL, issue, 0)

    def drain(t, carry):
        row_copy(0, 0).wait()
        row_copy(0, 0).wait()
        return carry

    lax.fori_loop(0, nb, drain, 0, unroll=8)


def _dispatch(pos0, pos1, hp, n_rows):
    T = hp.shape[0]
    nb = min(DISPATCH_BLOCK, T)
    blk = pl.BlockSpec((1, 1, nb), lambda i: (i, 0, 0), memory_space=pltpu.SMEM)
    xs0 = jnp.zeros((n_rows, HALF), U32)
    return pl.pallas_call(
        _dispatch_kernel,
        grid=(T // nb,),
        in_specs=[blk, blk, pl.BlockSpec((nb, HALF), lambda i: (i, 0)), pl.BlockSpec(memory_space=pl.ANY)],
        out_specs=pl.BlockSpec(memory_space=pl.ANY),
        out_shape=jax.ShapeDtypeStruct((n_rows, HALF), U32),
        scratch_shapes=[pltpu.SemaphoreType.DMA(())],
        input_output_aliases={3: 0},
        compiler_params=_cparams(("arbitrary",)),
        name="moe_dispatch",
    )(pos0.reshape(T // nb, 1, nb), pos1.reshape(T // nb, 1, nb), hp, xs0)


def _expert_kernel(te_ref, nv_ref, xs_ref, wg_ref, wu_ref, wd_ref, ys_ref):
    del te_ref

    @pl.when(pl.program_id(0) < nv_ref[0])
    def _():
        lo, hi = _unpack_halves(xs_ref[...])
        lo = lo.astype(BF16)
        hi = hi.astype(BF16)
        g = (jnp.dot(lo, wg_ref[0:HALF, :], preferred_element_type=F32)
             + jnp.dot(hi, wg_ref[HALF:, :], preferred_element_type=F32))
        u = (jnp.dot(lo, wu_ref[0:HALF, :], preferred_element_type=F32)
             + jnp.dot(hi, wu_ref[HALF:, :], preferred_element_type=F32))
        a = (g * jax.nn.sigmoid(g) * u).astype(BF16)
        ys_ref[...] = _pack_halves(jnp.dot(a, wd_ref[...], preferred_element_type=F32))


def _experts(tile_expert, n_valid, xs, wg_b, wu_b, wd_b):
    n_rows = xs.shape[0]
    tm = EXPERT_TILE
    nt = n_rows // tm
    rows = lambda n, te, nv: (jnp.minimum(n, nv[0] - 1), 0)
    wsel = lambda n, te, nv: (te[n], 0, 0)
    return pl.pallas_call(
        _expert_kernel,
        grid_spec=pltpu.PrefetchScalarGridSpec(
            num_scalar_prefetch=2,
            grid=(nt,),
            in_specs=[
                pl.BlockSpec((tm, HALF), rows),
                pl.BlockSpec((None, D_MODEL, D_EXPERT), wsel),
                pl.BlockSpec((None, D_MODEL, D_EXPERT), wsel),
                pl.BlockSpec((None, D_EXPERT, D_MODEL), wsel),
            ],
            out_specs=pl.BlockSpec((tm, HALF), rows),
        ),
        out_shape=jax.ShapeDtypeStruct((n_rows, HALF), U32),
        compiler_params=_cparams(("arbitrary",)),
        name="moe_experts",
    )(tile_expert, n_valid, xs, wg_b, wu_b, wd_b)


def _combine_kernel(p0_ref, p1_ref, p0n_ref, p1n_ref, ys_ref, x_ref, mod_ref, rt_ref, g_ref, b_ref, o_ref,
                    buf, sems, *, alpha):
    D = D_MODEL
    tc = x_ref.shape[0]
    g = pl.program_id(0) * pl.num_programs(1) + pl.program_id(1)
    n_steps = pl.num_programs(0) * pl.num_programs(1)
    slot = g % 2

    def row_copy(p, s, choice, t):
        return pltpu.make_async_copy(ys_ref.at[pl.ds(p, 1)], buf.at[s, choice, pl.ds(t, 1)], sems.at[s])

    def gather(pa_ref, pb_ref, s):
        def issue(t8, carry):
            for u in range(ROW_DMA_UNROLL):
                t = t8 * ROW_DMA_UNROLL + u
                row_copy(pa_ref[0, 0, t], s, 0, t).start(priority=u % 2)
                row_copy(pb_ref[0, 0, t], s, 1, t).start(priority=(u + 1) % 2)
            return carry

        lax.fori_loop(0, tc // ROW_DMA_UNROLL, issue, 0)

    @pl.when(g == 0)
    def _():
        gather(p0_ref, p1_ref, 0)

    @pl.when(g + 1 < n_steps)
    def _():
        gather(p0n_ref, p1n_ref, 1 - slot)

    def drain(t, carry):
        row_copy(0, slot, 0, 0).wait()
        row_copy(0, slot, 0, 0).wait()
        return carry

    lax.fori_loop(0, tc, drain, 0, unroll=8)

    wt = rt_ref[...].T
    w1 = wt[:, 2:3]
    w2 = wt[:, 3:4]
    lo1, hi1 = _unpack_halves(buf[slot, 0])
    lo2, hi2 = _unpack_halves(buf[slot, 1])
    ffn = jnp.concatenate([w1 * lo1 + w2 * lo2, w1 * hi1 + w2 * hi2], axis=1)
    gt = mod_ref[:, 5 * D:6 * D]
    y = alpha * x_ref[...] + (1.0 + gt) * ffn
    o_ref[...] = _layer_norm_rows(y, g_ref[...], b_ref[...])


def _combine(pos0, pos1, ys, x1, mod_l, rt, ln_g, ln_b, alpha):
    B, S, D = x1.shape
    tc = min(COMBINE_TILE, S)
    nt = S // tc
    blk = pl.BlockSpec((1, 1, tc), lambda b, i: (b * nt + i, 0, 0), memory_space=pltpu.SMEM)
    nxt = pl.BlockSpec((1, 1, tc), lambda b, i: (jnp.minimum(b * nt + i + 1, B * nt - 1), 0, 0),
                       memory_space=pltpu.SMEM)
    row = lambda b, i: (b, i, 0)
    full2 = lambda a: pl.BlockSpec(a.shape, lambda b, i: (0,) * a.ndim)
    p0 = pos0.reshape(B * nt, 1, tc)
    p1 = pos1.reshape(B * nt, 1, tc)
    return pl.pallas_call(
        functools.partial(_combine_kernel, alpha=alpha),
        grid=(B, nt),
        in_specs=[
            blk, blk, nxt, nxt,
            pl.BlockSpec(memory_space=pl.ANY),
            pl.BlockSpec((None, tc, D), row),
            pl.BlockSpec((None, 1, 6 * D), lambda b, i: (b, 0, 0)),
            pl.BlockSpec((None, 8, tc), lambda b, i: (b, 0, i)),
            full2(ln_g), full2(ln_b),
        ],
        out_specs=pl.BlockSpec((None, tc, D), row),
        out_shape=jax.ShapeDtypeStruct((B, S, D), F32),
        scratch_shapes=[pltpu.VMEM((2, 2, tc, HALF), U32), pltpu.SemaphoreType.DMA((2,))],
        compiler_params=_cparams(("arbitrary", "arbitrary")),
        name="moe_combine_ln",
    )(p0, p1, p0, p1, ys, x1, mod_l, rt, ln_g, ln_b)


def _rope_tables(S):
    inv = 1.0 / (ROPE_THETA ** (jnp.arange(0, QK_DIM, 2, dtype=F32) / QK_DIM))
    ang = jnp.arange(S, dtype=F32)[:, None] * inv[None, :]
    cos, sin = jnp.cos(ang), jnp.sin(ang)
    half = QK_DIM // 2
    first = (jnp.arange(LANES) % QK_DIM) < half
    cs = jnp.tile(cos, (1, LANES // half))
    sn = jnp.tile(sin, (1, LANES // half))
    sa = jnp.where(first[None, :], -sn, 0.0)
    sb = jnp.where(first[None, :], 0.0, sn)
    return cs, sa, sb


def _router_matrix(w_group, b_group, w_router, b_router):
    D = w_group.shape[0]
    w = jnp.zeros((D, LANES), F32)
    w = w.at[:, 0:N_GROUPS].set(w_group)
    wr = jnp.transpose(w_router, (1, 0, 2)).reshape(D, N_EXPERTS)
    w = w.at[:, ROUTE_LANE0:ROUTE_LANE0 + N_EXPERTS].set(wr)
    b = jnp.zeros((1, LANES), F32)
    b = b.at[0, 0:N_GROUPS].set(b_group)
    b = b.at[0, ROUTE_LANE0:ROUTE_LANE0 + N_EXPERTS].set(b_router.reshape(N_EXPERTS))
    hi = w.astype(BF16)
    lo = (w - hi.astype(F32)).astype(BF16)
    return hi, lo, b


def _routing_tables(rt, cnt, n_tiles):
    B, _, S = rt.shape
    tm = EXPERT_TILE
    e = rt[:, 0:2, :].astype(I32)
    rank = rt[:, 4:6, :].astype(I32)
    counts = cnt[0, ROUTE_LANE0:ROUTE_LANE0 + N_EXPERTS].astype(I32)
    tiles = (counts + tm - 1) // tm
    tile_end = jnp.cumsum(tiles)
    tile_start = tile_end - tiles
    onehot = e[..., None] == jnp.arange(N_EXPERTS, dtype=I32)
    start = jnp.sum(jnp.where(onehot, tile_start * tm, 0), axis=-1)
    pos = start + rank
    pos0 = pos[:, 0, :].reshape(B * S)
    pos1 = pos[:, 1, :].reshape(B * S)
    tile_ids = jnp.arange(n_tiles, dtype=I32)
    tile_expert = jnp.minimum(jnp.sum(tile_ids[:, None] >= tile_end[None, :], axis=1), N_EXPERTS - 1).astype(I32)
    n_valid = tile_end[-1:].astype(I32)
    return pos0, pos1, tile_expert, n_valid


def kernel(x, c, w_ada, b_ada, w_in, lambda_q1, lambda_k1, lambda_q2, lambda_k2, subln_g, sg_ln_g, sg_ln_b, w_spatial, b_spatial, w_out, ln1_g, ln1_b, w_group, b_group, w_router, b_router, w_gate, w_up, w_down, ln2_g, ln2_b):
    B, S, D = x.shape
    depth = w_in.shape[0]
    T = B * S
    alpha = (2.0 * depth) ** 0.25
    n_tiles = (2 * T) // EXPERT_TILE + N_EXPERTS
    n_rows = n_tiles * EXPERT_TILE

    mod = _ada(c, w_ada, b_ada)
    tabs = _rope_tables(S)
    for l in range(depth):
        mod_l = mod[l].reshape(B, 1, 6 * D)
        lam_init = 0.8 - 0.6 * math.exp(-0.3 * l)

        qT, k, vTb, zu, zvn = _inproj(x, mod_l, w_in[l].astype(BF16), tabs,
                                      sg_ln_g[l].reshape(1, SG_WIDTH), sg_ln_b[l].reshape(1, SG_WIDTH))
        lams = (lambda_q1[l].reshape(1, QK_DIM), lambda_k1[l].reshape(1, QK_DIM),
                lambda_q2[l].reshape(1, QK_DIM), lambda_k2[l].reshape(1, QK_DIM))
        att = _attention(qT, k, vTb, lams, subln_g[l].reshape(V_DIM, 1), lam_init)
        b_sp_full = jnp.broadcast_to(b_spatial[l][:, :, None], (N_SG, CHUNK, SG_DIM))
        x1 = _mix(x, mod_l, att, zu, zvn, w_spatial[l], b_sp_full, w_out[l].astype(BF16),
                  ln1_g[l].reshape(1, D), ln1_b[l].reshape(1, D), alpha)

        wr_hi, wr_lo, br = _router_matrix(w_group[l], b_group[l], w_router[l], b_router[l])
        hp, rt, cnt = _route(x1, mod_l, wr_hi, wr_lo, br)
        pos0, pos1, tile_expert, n_valid = _routing_tables(rt, cnt, n_tiles)
        xs = _dispatch(pos0, pos1, hp.reshape(T, HALF), n_rows)
        ys = _experts(tile_expert, n_valid, xs, w_gate[l].astype(BF16), w_up[l].astype(BF16),
                      w_down[l].astype(BF16))
        x = _combine(pos0, pos1, ys, x1, mod_l, rt, ln2_g[l].reshape(1, D), ln2_b[l].reshape(1, D), alpha)
    return x
```

```python
import functools
import math

import jax
import jax.numpy as jnp
from jax import lax
from jax.experimental import pallas as pl
from jax.experimental.pallas import tpu as pltpu

F32 = jnp.float32
BF16 = jnp.bfloat16
U32 = jnp.uint32
I32 = jnp.int32

D_MODEL = 1024
N_HEADS = 4
QK_DIM = 64
V_DIM = 128
HEAD_COLS = 2 * QK_DIM
QK_COLS = N_HEADS * HEAD_COLS
DIFF_WIDTH = N_HEADS * V_DIM
N_SG = 4
SG_DIM = 128
SG_WIDTH = N_SG * SG_DIM
CHUNK = 128
N_GROUPS = 4
EXP_PER_GROUP = 8
N_EXPERTS = N_GROUPS * EXP_PER_GROUP
D_EXPERT = 512
ROPE_THETA = 10000.0
LN_EPS = 1e-5
LANES = 128
HALF = D_MODEL // 2

ROW_TILE = 512
V_ROWS = V_DIM + 16
EXPERT_TILE = 512
DISPATCH_BLOCK = 2048
COMBINE_TILE = 256
ROW_DMA_UNROLL = 8
ROUTE_LANE0 = 8
VMEM_LIMIT = 48 * 1024 * 1024

DEN_MIN = 2.0 ** -40
DEN_MAX = 2.0 ** 40
LOG2E = 1.4426950408889634
Q_SCALE = (QK_DIM ** -0.5) * LOG2E


def _cparams(sem):
    return pltpu.CompilerParams(dimension_semantics=sem, vmem_limit_bytes=VMEM_LIMIT)


def _layer_norm_rows(y, g, b):
    mu = jnp.mean(y, axis=-1, keepdims=True)
    yc = y - mu
    var = jnp.mean(yc * yc, axis=-1, keepdims=True)
    return yc * lax.rsqrt(var + LN_EPS) * g + b


def _gelu(x):
    return 0.5 * x * (1.0 + lax.erf(x * (2.0 ** -0.5)))


def _pack_halves(y):
    lo = pltpu.bitcast(y[:, :HALF].astype(BF16).astype(F32), U32) >> 16
    hi = pltpu.bitcast(y[:, HALF:].astype(BF16).astype(F32), U32) & jnp.uint32(0xFFFF0000)
    return lo | hi


def _unpack_halves(p):
    lo = pltpu.bitcast(p << 16, F32)
    hi = pltpu.bitcast(p & jnp.uint32(0xFFFF0000), F32)
    return lo, hi


def _ada_kernel(c_ref, w_ref, b_ref, o_ref):
    c = c_ref[...]
    sc = c * jax.nn.sigmoid(c)
    o_ref[...] = jnp.dot(sc, w_ref[...], precision=lax.Precision.HIGHEST,
                         preferred_element_type=F32) + b_ref[...]


def _ada(c, w_ada, b_ada):
    L, D, N = w_ada.shape
    B = c.shape[0]
    tn = 1536
    return pl.pallas_call(
        _ada_kernel,
        grid=(L, N // tn),
        in_specs=[
            pl.BlockSpec((B, D), lambda l, j: (0, 0)),
            pl.BlockSpec((None, D, tn), lambda l, j: (l, 0, j)),
            pl.BlockSpec((None, 1, tn), lambda l, j: (l, 0, j)),
        ],
        out_specs=pl.BlockSpec((None, B, tn), lambda l, j: (l, 0, j)),
        out_shape=jax.ShapeDtypeStruct((L, B, N), F32),
        compiler_params=_cparams(("arbitrary", "arbitrary")),
        name="ada",
    )(c, w_ada, b_ada.reshape(L, 1, N))


def _inproj_kernel(x_ref, mod_ref, w_ref, cs_ref, sa_ref, sb_ref, lng_ref, lnb_ref,
                   qT_ref, k_ref, vT_ref, zu_ref, zvn_ref):
    D = D_MODEL
    x = x_ref[...]
    sh = mod_ref[:, 0:D]
    sc = mod_ref[:, D:2 * D]
    h = (x * (1.0 + sc) + sh).astype(BF16)
    cs = cs_ref[...]
    sa = sa_ref[...]
    sb = sb_ref[...]

    def rope(t):
        return t * cs + pltpu.roll(t, 96, 1) * sa + pltpu.roll(t, 32, 1) * sb

    q = jnp.dot(h, w_ref[:, 0:QK_COLS], preferred_element_type=F32)
    for j in range(N_HEADS):
        sl = slice(j * LANES, (j + 1) * LANES)
        qT_ref[sl, :] = (rope(q[:, sl]) * Q_SCALE).T.astype(BF16)
    k = jnp.dot(h, w_ref[:, QK_COLS:2 * QK_COLS], preferred_element_type=F32)
    for j in range(N_HEADS):
        sl = slice(j * LANES, (j + 1) * LANES)
        k_ref[:, sl] = rope(k[:, sl]).astype(BF16)
    c0 = 2 * QK_COLS
    v = jnp.dot(h, w_ref[:, c0:c0 + DIFF_WIDTH], preferred_element_type=F32)
    for j in range(N_HEADS):
        vT_ref[j * V_ROWS:j * V_ROWS + V_DIM, :] = v[:, j * V_DIM:(j + 1) * V_DIM].T.astype(BF16)
        vT_ref[j * V_ROWS + V_DIM:(j + 1) * V_ROWS, :] = jnp.ones((V_ROWS - V_DIM, v.shape[0]), BF16)
    c0 += DIFF_WIDTH
    u = jnp.dot(h, w_ref[:, c0:c0 + SG_WIDTH], preferred_element_type=F32)
    zu_ref[...] = _gelu(u).astype(BF16)
    c0 += SG_WIDTH
    z = _gelu(jnp.dot(h, w_ref[:, c0:c0 + SG_WIDTH], preferred_element_type=F32))
    for g in range(N_SG):
        sl = slice(g * SG_DIM, (g + 1) * SG_DIM)
        zvn_ref[:, sl] = _layer_norm_rows(z[:, sl], lng_ref[:, sl], lnb_ref[:, sl]).astype(BF16)


def _inproj(x, mod_l, w_in_b, rope_tabs, ln_g, ln_b):
    B, S, D = x.shape
    tm = min(ROW_TILE, S)
    nt = S // tm
    cs, sa, sb = rope_tabs
    row = lambda b, i: (b, i, 0)
    tab = pl.BlockSpec((tm, LANES), lambda b, i: (i, 0))
    return pl.pallas_call(
        _inproj_kernel,
        grid=(B, nt),
        in_specs=[
            pl.BlockSpec((None, tm, D), row),
            pl.BlockSpec((None, 1, 6 * D), lambda b, i: (b, 0, 0)),
            pl.BlockSpec(w_in_b.shape, lambda b, i: (0, 0)),
            tab, tab, tab,
            pl.BlockSpec((1, SG_WIDTH), lambda b, i: (0, 0)),
            pl.BlockSpec((1, SG_WIDTH), lambda b, i: (0, 0)),
        ],
        out_specs=[
            pl.BlockSpec((None, QK_COLS, tm), lambda b, i: (b, 0, i)),
            pl.BlockSpec((None, tm, QK_COLS), row),
            pl.BlockSpec((None, None, N_HEADS * V_ROWS, tm), lambda b, i: (b, i, 0, 0)),
            pl.BlockSpec((None, tm, SG_WIDTH), row),
            pl.BlockSpec((None, tm, SG_WIDTH), row),
        ],
        out_shape=[
            jax.ShapeDtypeStruct((B, QK_COLS, S), BF16),
            jax.ShapeDtypeStruct((B, S, QK_COLS), BF16),
            jax.ShapeDtypeStruct((B, nt, N_HEADS * V_ROWS, tm), BF16),
            jax.ShapeDtypeStruct((B, S, SG_WIDTH), BF16),
            jax.ShapeDtypeStruct((B, S, SG_WIDTH), BF16),
        ],
        compiler_params=_cparams(("arbitrary", "arbitrary")),
        name="inproj",
    )(x, mod_l, w_in_b, cs, sa, sb, ln_g, ln_b)


def _attn_kernel(lq1_ref, lk1_ref, lq2_ref, lk2_ref, g_ref, qT_ref, k_ref, vT_ref, o_ref,
                 q_scr, s_scr, m_scr, acc_scr, *, lam_init):
    tq = qT_ref.shape[1]
    tk = vT_ref.shape[2]
    i = pl.program_id(2)
    qT = qT_ref[...]
    rows = lax.broadcasted_iota(I32, qT.shape, 0)
    zero = jnp.zeros_like(qT)
    q_scr[0] = jnp.where(rows < QK_DIM, qT, zero)
    q_scr[1] = jnp.where(rows >= QK_DIM, qT, zero)

    def scores(j, slot, c0):
        kb = k_ref[pl.ds(pl.multiple_of(j * tk, tk), tk), :]
        for mp in range(2):
            s_scr[slot, mp, :, c0:] = jnp.dot(kb, q_scr[mp, :, c0:], preferred_element_type=F32)

    def block_scores(slot, mp, c0, c1, masked):
        s = s_scr[slot, mp, :, c0:c1]
        if masked:
            kpos = lax.broadcasted_iota(I32, s.shape, 0)
            qpos = lax.broadcasted_iota(I32, s.shape, 1)
            s = jnp.where(kpos <= qpos, s, -jnp.inf)
        return s

    def pv_unshifted(j, slot, c0, c1, masked):
        vb = vT_ref[j]
        for mp in range(2):
            p = jnp.exp2(block_scores(slot, mp, c0, c1, masked)).astype(BF16)
            acc_scr[mp, :, c0:c1] += jnp.dot(vb, p, preferred_element_type=F32)

    def pv_online(j, slot, c0, c1, masked):
        vb = vT_ref[j]
        for mp in range(2):
            s = block_scores(slot, mp, c0, c1, masked)
            m_old = m_scr[mp, :, c0:c1]
            m_new = jnp.maximum(m_old, jnp.max(s, axis=0, keepdims=True))
            alpha = jnp.exp2(m_old - m_new)
            p = jnp.exp2(s - m_new).astype(BF16)
            acc_scr[mp, :, c0:c1] = alpha * acc_scr[mp, :, c0:c1] + jnp.dot(vb, p, preferred_element_type=F32)
            m_scr[mp, :, c0:c1] = m_new

    def sweep(block):
        acc_scr[...] = jnp.zeros(acc_scr.shape, F32)
        scores(0, 0, 0)

        def body(u, carry):
            t = 2 * u + 1
            scores(t, 1, 0)
            block(t - 1, 0, 0, tq, False)
            scores(t + 1, 0, 0)
            block(t, 1, 0, tq, False)
            return carry

        lax.fori_loop(0, i, body, 0)
        scores(2 * i + 1, 1, tk)
        block(2 * i, 0, 0, tk, True)
        block(2 * i, 0, tk, tq, False)
        block(2 * i + 1, 1, tk, tq, True)

    sweep(pv_unshifted)
    den = jnp.concatenate([acc_scr[0, V_DIM:V_DIM + 1, :], acc_scr[1, V_DIM:V_DIM + 1, :]], axis=0)
    in_range = (den >= DEN_MIN) & (den <= DEN_MAX)
    n_bad = jnp.sum(jnp.where(in_range, 0.0, 1.0))

    @pl.when(n_bad > 0.0)
    def _():
        m_scr[...] = jnp.full(m_scr.shape, -jnp.inf, F32)
        sweep(pv_online)

    lam = (jnp.exp(jnp.sum(lq1_ref[...] * lk1_ref[...], axis=1, keepdims=True))
           - jnp.exp(jnp.sum(lq2_ref[...] * lk2_ref[...], axis=1, keepdims=True)) + lam_init)
    a1 = acc_scr[0]
    a2 = acc_scr[1]
    out = a1[0:V_DIM] / a1[V_DIM:V_DIM + 1] - lam * (a2[0:V_DIM] / a2[V_DIM:V_DIM + 1])
    ms = jnp.mean(out * out, axis=0, keepdims=True)
    y = out * lax.rsqrt(ms + LN_EPS) * g_ref[...] * (1.0 - lam_init)
    o_ref[...] = y.T.astype(BF16)


def _attention(qT, k, vTb, lams, g_col, lam_init):
    B, _, S = qT.shape
    nkv, tk = vTb.shape[1], vTb.shape[3]
    tq = 2 * tk
    small = pl.BlockSpec((1, QK_DIM), lambda b, h, i: (0, 0))
    return pl.pallas_call(
        functools.partial(_attn_kernel, lam_init=lam_init),
        grid=(B, N_HEADS, S // tq),
        in_specs=[
            small, small, small, small,
            pl.BlockSpec((V_DIM, 1), lambda b, h, i: (0, 0)),
            pl.BlockSpec((None, HEAD_COLS, tq), lambda b, h, i: (b, h, i)),
            pl.BlockSpec((None, S, HEAD_COLS), lambda b, h, i: (b, 0, h)),
            pl.BlockSpec((None, nkv, V_ROWS, tk), lambda b, h, i: (b, 0, h, 0)),
        ],
        out_specs=pl.BlockSpec((None, tq, V_DIM), lambda b, h, i: (b, i, h)),
        out_shape=jax.ShapeDtypeStruct((B, S, DIFF_WIDTH), BF16),
        scratch_shapes=[
            pltpu.VMEM((2, HEAD_COLS, tq), BF16),
            pltpu.VMEM((2, 2, tk, tq), F32),
            pltpu.VMEM((2, 1, tq), F32),
            pltpu.VMEM((2, V_ROWS, tq), F32),
        ],
        compiler_params=_cparams(("arbitrary", "arbitrary", "arbitrary")),
        name="diff_attn",
    )(*lams, g_col, qT, k, vTb)


def _mix_kernel(x_ref, mod_ref, att_ref, zu_ref, zvn_ref, wsp_ref, bsp_ref, wo_ref, g_ref, b_ref,
                o_ref, sg_scr, *, alpha):
    D = D_MODEL
    nc = zu_ref.shape[0]
    r = lax.broadcasted_iota(I32, (CHUNK, CHUNK), 0)
    c = lax.broadcasted_iota(I32, (CHUNK, CHUNK), 1)
    causal = r >= c
    for g in range(N_SG):
        sl = slice(g * SG_DIM, (g + 1) * SG_DIM)
        w = jnp.where(causal, wsp_ref[g], 0.0).astype(BF16)
        z = jnp.concatenate([zvn_ref[n, :, sl] for n in range(nc)], axis=1)
        mixed = jnp.dot(w, z, preferred_element_type=F32)
        for n in range(nc):
            gate = mixed[:, n * SG_DIM:(n + 1) * SG_DIM] + bsp_ref[g]
            sg_scr[n * CHUNK:(n + 1) * CHUNK, sl] = (zu_ref[n, :, sl].astype(F32) * gate).astype(BF16)
    mix = (jnp.dot(att_ref[...], wo_ref[0:DIFF_WIDTH, :], preferred_element_type=F32)
           + jnp.dot(sg_scr[...], wo_ref[DIFF_WIDTH:, :], preferred_element_type=F32))
    gt = mod_ref[:, 2 * D:3 * D]
    y = alpha * x_ref[...] + (1.0 + gt) * mix
    o_ref[...] = _layer_norm_rows(y, g_ref[...], b_ref[...])


def _mix(x, mod_l, att, zu, zvn, w_sp, b_sp_full, w_out_b, ln_g, ln_b, alpha):
    B, S, D = x.shape
    tm = min(ROW_TILE, S)
    nc = tm // CHUNK
    row = lambda b, i: (b, i, 0)
    zu4 = zu.reshape(B, S // CHUNK, CHUNK, SG_WIDTH)
    zvn4 = zvn.reshape(B, S // CHUNK, CHUNK, SG_WIDTH)
    chunked = pl.BlockSpec((None, nc, CHUNK, SG_WIDTH), lambda b, i: (b, i, 0, 0))
    full2 = lambda a: pl.BlockSpec(a.shape, lambda b, i: (0,) * a.ndim)
    return pl.pallas_call(
        functools.partial(_mix_kernel, alpha=alpha),
        grid=(B, S // tm),
        in_specs=[
            pl.BlockSpec((None, tm, D), row),
            pl.BlockSpec((None, 1, 6 * D), lambda b, i: (b, 0, 0)),
            pl.BlockSpec((None, tm, DIFF_WIDTH), row),
            chunked, chunked,
            full2(w_sp), full2(b_sp_full), full2(w_out_b), full2(ln_g), full2(ln_b),
        ],
        out_specs=pl.BlockSpec((None, tm, D), row),
        out_shape=jax.ShapeDtypeStruct((B, S, D), F32),
        scratch_shapes=[pltpu.VMEM((tm, SG_WIDTH), BF16)],
        compiler_params=_cparams(("arbitrary", "arbitrary")),
        name="sgate_outproj_ln",
    )(x, mod_l, att, zu4, zvn4, w_sp, b_sp_full, w_out_b, ln_g, ln_b)


def _route_kernel(x_ref, mod_ref, wr_hi_ref, wr_lo_ref, br_ref, hp_ref, rt_ref, cnt_ref, run_scr):
    D = D_MODEL
    tm = x_ref.shape[0]

    @pl.when((pl.program_id(0) == 0) & (pl.program_id(1) == 0))
    def _():
        run_scr[...] = jnp.zeros(run_scr.shape, F32)

    sh = mod_ref[:, 3 * D:4 * D]
    sc = mod_ref[:, 4 * D:5 * D]
    h = x_ref[...] * (1.0 + sc) + sh
    hp_ref[...] = _pack_halves(h)
    h_hi = h.astype(BF16)
    h_lo = (h - h_hi.astype(F32)).astype(BF16)
    logit = (jnp.dot(h_hi, wr_hi_ref[...], preferred_element_type=F32)
             + jnp.dot(h_lo, wr_hi_ref[...], preferred_element_type=F32)
             + jnp.dot(h_hi, wr_lo_ref[...], preferred_element_type=F32)) + br_ref[...]
    lane = lax.broadcasted_iota(I32, logit.shape, 1).astype(F32)
    neg = -jnp.inf
    big = float(LANES)

    def first_argmax(v):
        mx = jnp.max(v, axis=1, keepdims=True)
        idx = jnp.min(jnp.where(v == mx, lane, big), axis=1, keepdims=True)
        return mx, idx

    in_grp = lane < N_GROUPS
    gmax, gidx = first_argmax(jnp.where(in_grp, logit, neg))
    g_p = 1.0 / jnp.sum(jnp.where(in_grp, jnp.exp(logit - gmax), 0.0), axis=1, keepdims=True)
    lo_lane = ROUTE_LANE0 + EXP_PER_GROUP * gidx
    sel = jnp.where((lane >= lo_lane) & (lane < lo_lane + EXP_PER_GROUP), logit, neg)
    v1, i1 = first_argmax(sel)
    v2, i2 = first_argmax(jnp.where(lane == i1, neg, sel))
    t = jnp.exp(v2 - v1)
    w1 = g_p / (1.0 + t)
    w2 = g_p * t / (1.0 + t)

    hot1 = lane == i1
    hot2 = lane == i2
    twohot = jnp.where(hot1 | hot2, 1.0, 0.0)
    r = lax.broadcasted_iota(I32, (tm, tm), 0)
    c = lax.broadcasted_iota(I32, (tm, tm), 1)
    strict = jnp.where(r > c, 1.0, 0.0).astype(BF16)
    before = jnp.dot(strict, twohot.astype(BF16), preferred_element_type=F32) + run_scr[...]
    r1 = jnp.sum(jnp.where(hot1, before, 0.0), axis=1, keepdims=True)
    r2 = jnp.sum(jnp.where(hot2, before, 0.0), axis=1, keepdims=True)
    run_scr[...] = run_scr[...] + jnp.sum(twohot, axis=0, keepdims=True)
    cnt_ref[...] = run_scr[...]

    fields = (i1 - ROUTE_LANE0, i2 - ROUTE_LANE0, w1, w2, r1, r2)
    slab = jnp.zeros(logit.shape, F32)
    for n, f in enumerate(fields):
        slab = jnp.where(lane == float(n), f, slab)
    rt_ref[...] = slab.T[0:8, :]


def _route(x1, mod_l, wr_hi, wr_lo, br):
    B, S, D = x1.shape
    tm = min(ROW_TILE, S)
    row = lambda b, i: (b, i, 0)
    full2 = lambda a: pl.BlockSpec(a.shape, lambda b, i: (0,) * a.ndim)
    return pl.pallas_call(
        _route_kernel,
        grid=(B, S // tm),
        in_specs=[
            pl.BlockSpec((None, tm, D), row),
            pl.BlockSpec((None, 1, 6 * D), lambda b, i: (b, 0, 0)),
            full2(wr_hi), full2(wr_lo), full2(br),
        ],
        out_specs=[
            pl.BlockSpec((None, tm, HALF), row),
            pl.BlockSpec((None, 8, tm), lambda b, i: (b, 0, i)),
            pl.BlockSpec((1, LANES), lambda b, i: (0, 0)),
        ],
        out_shape=[
            jax.ShapeDtypeStruct((B, S, HALF), U32),
            jax.ShapeDtypeStruct((B, 8, S), F32),
            jax.ShapeDtypeStruct((1, LANES), F32),
        ],
        scratch_shapes=[pltpu.VMEM((1, LANES), F32)],
        compiler_params=_cparams(("arbitrary", "arbitrary")),
        name="moe_route",
    )(x1, mod_l, wr_hi, wr_lo, br)


def _dispatch_kernel(p0_ref, p1_ref, hp_ref, xs_in_ref, xs_ref, sem):
    del xs_in_ref
    nb = p0_ref.shape[2]

    def row_copy(t, p):
        return pltpu.make_async_copy(hp_ref.at[pl.ds(t, 1)], xs_ref.at[pl.ds(p, 1)], sem)

    def issue(t8, carry):
        for u in range(ROW_DMA_UNROLL):
            t = t8 * ROW_DMA_UNROLL + u
            row_copy(t, p0_ref[0, 0, t]).start(priority=u % 2)
            row_copy(t, p1_ref[0, 0, t]).start(priority=(u + 1) % 2)
        return carry

    lax.fori_loop(0, nb // ROW_DMA_UNROLL, issue, 0)

    for _ in range(2):
        pltpu.make_async_copy(hp_ref, xs_ref.at[pl.ds(0, nb)], sem).wait()


def _dispatch(pos0, pos1, hp, n_rows):
    T = hp.shape[0]
    nb = min(DISPATCH_BLOCK, T)
    blk = pl.BlockSpec((1, 1, nb), lambda i: (i, 0, 0), memory_space=pltpu.SMEM)
    xs0 = jnp.zeros((n_rows, HALF), U32)
    return pl.pallas_call(
        _dispatch_kernel,
        grid=(T // nb,),
        in_specs=[blk, blk, pl.BlockSpec((nb, HALF), lambda i: (i, 0)), pl.BlockSpec(memory_space=pl.ANY)],
        out_specs=pl.BlockSpec(memory_space=pl.ANY),
        out_shape=jax.ShapeDtypeStruct((n_rows, HALF), U32),
        scratch_shapes=[pltpu.SemaphoreType.DMA(())],
        input_output_aliases={3: 0},
        compiler_params=_cparams(("arbitrary",)),
        name="moe_dispatch",
    )(pos0.reshape(T // nb, 1, nb), pos1.reshape(T // nb, 1, nb), hp, xs0)


def _expert_kernel(te_ref, nv_ref, xs_ref, wg_ref, wu_ref, wd_ref, ys_ref, wg_scr, wu_scr, wd_scr):
    n = pl.program_id(0)
    valid = n < nv_ref[0]
    new_expert = (n == 0) | (te_ref[n] != te_ref[jnp.maximum(n - 1, 0)])

    @pl.when(valid & new_expert)
    def _():
        wg_scr[...] = wg_ref[...].astype(BF16)
        wu_scr[...] = wu_ref[...].astype(BF16)
        wd_scr[...] = wd_ref[...].astype(BF16)

    @pl.when(valid)
    def _():
        lo, hi = _unpack_halves(xs_ref[...])
        lo = lo.astype(BF16)
        hi = hi.astype(BF16)
        g = (jnp.dot(lo, wg_scr[0:HALF, :], preferred_element_type=F32)
             + jnp.dot(hi, wg_scr[HALF:, :], preferred_element_type=F32))
        u = (jnp.dot(lo, wu_scr[0:HALF, :], preferred_element_type=F32)
             + jnp.dot(hi, wu_scr[HALF:, :], preferred_element_type=F32))
        a = (g * jax.nn.sigmoid(g) * u).astype(BF16)
        ys_ref[...] = _pack_halves(jnp.dot(a, wd_scr[...], preferred_element_type=F32))


def _experts(tile_expert, n_valid, xs, w_gate, w_up, w_down, layer):
    n_rows = xs.shape[0]
    tm = EXPERT_TILE
    nt = n_rows // tm
    rows = lambda n, te, nv: (jnp.minimum(n, nv[0] - 1), 0)
    wsel = lambda n, te, nv: (layer, te[n], 0, 0)
    return pl.pallas_call(
        _expert_kernel,
        grid_spec=pltpu.PrefetchScalarGridSpec(
            num_scalar_prefetch=2,
            grid=(nt,),
            in_specs=[
                pl.BlockSpec((tm, HALF), rows),
                pl.BlockSpec((None, None, D_MODEL, D_EXPERT), wsel),
                pl.BlockSpec((None, None, D_MODEL, D_EXPERT), wsel),
                pl.BlockSpec((None, None, D_EXPERT, D_MODEL), wsel),
            ],
            out_specs=pl.BlockSpec((tm, HALF), rows),
            scratch_shapes=[
                pltpu.VMEM((D_MODEL, D_EXPERT), BF16),
                pltpu.VMEM((D_MODEL, D_EXPERT), BF16),
                pltpu.VMEM((D_EXPERT, D_MODEL), BF16),
            ],
        ),
        out_shape=jax.ShapeDtypeStruct((n_rows, HALF), U32),
        compiler_params=_cparams(("arbitrary",)),
        name="moe_experts",
    )(tile_expert, n_valid, xs, w_gate, w_up, w_down)


def _combine_kernel(p0_ref, p1_ref, p0n_ref, p1n_ref, ys_ref, x_ref, mod_ref, rt_ref, g_ref, b_ref, o_ref,
                    buf, sems, *, alpha):
    D = D_MODEL
    tc = x_ref.shape[0]
    g = pl.program_id(0) * pl.num_programs(1) + pl.program_id(1)
    n_steps = pl.num_programs(0) * pl.num_programs(1)
    slot = g % 2

    def row_copy(p, s, choice, t):
        return pltpu.make_async_copy(ys_ref.at[pl.ds(p, 1)], buf.at[s, choice, pl.ds(t, 1)], sems.at[s])

    def gather(pa_ref, pb_ref, s):
        def issue(t8, carry):
            for u in range(ROW_DMA_UNROLL):
                t = t8 * ROW_DMA_UNROLL + u
                row_copy(pa_ref[0, 0, t], s, 0, t).start(priority=u % 2)
                row_copy(pb_ref[0, 0, t], s, 1, t).start(priority=(u + 1) % 2)
            return carry

        lax.fori_loop(0, tc // ROW_DMA_UNROLL, issue, 0)

    @pl.when(g == 0)
    def _():
        gather(p0_ref, p1_ref, 0)

    @pl.when(g + 1 < n_steps)
    def _():
        gather(p0n_ref, p1n_ref, 1 - slot)

    for choice in range(2):
        pltpu.make_async_copy(ys_ref.at[pl.ds(0, tc)], buf.at[slot, choice], sems.at[slot]).wait()

    wt = rt_ref[...].T
    w1 = wt[:, 2:3]
    w2 = wt[:, 3:4]
    lo1, hi1 = _unpack_halves(buf[slot, 0])
    lo2, hi2 = _unpack_halves(buf[slot, 1])
    ffn = jnp.concatenate([w1 * lo1 + w2 * lo2, w1 * hi1 + w2 * hi2], axis=1)
    gt = mod_ref[:, 5 * D:6 * D]
    y = alpha * x_ref[...] + (1.0 + gt) * ffn
    o_ref[...] = _layer_norm_rows(y, g_ref[...], b_ref[...])


def _combine(pos0, pos1, ys, x1, mod_l, rt, ln_g, ln_b, alpha):
    B, S, D = x1.shape
    tc = min(COMBINE_TILE, S)
    nt = S // tc
    blk = pl.BlockSpec((1, 1, tc), lambda b, i: (b * nt + i, 0, 0), memory_space=pltpu.SMEM)
    nxt = pl.BlockSpec((1, 1, tc), lambda b, i: (jnp.minimum(b * nt + i + 1, B * nt - 1), 0, 0),
                       memory_space=pltpu.SMEM)
    row = lambda b, i: (b, i, 0)
    full2 = lambda a: pl.BlockSpec(a.shape, lambda b, i: (0,) * a.ndim)
    p0 = pos0.reshape(B * nt, 1, tc)
    p1 = pos1.reshape(B * nt, 1, tc)
    return pl.pallas_call(
        functools.partial(_combine_kernel, alpha=alpha),
        grid=(B, nt),
        in_specs=[
            blk, blk, nxt, nxt,
            pl.BlockSpec(memory_space=pl.ANY),
            pl.BlockSpec((None, tc, D), row),
            pl.BlockSpec((None, 1, 6 * D), lambda b, i: (b, 0, 0)),
            pl.BlockSpec((None, 8, tc), lambda b, i: (b, 0, i)),
            full2(ln_g), full2(ln_b),
        ],
        out_specs=pl.BlockSpec((None, tc, D), row),
        out_shape=jax.ShapeDtypeStruct((B, S, D), F32),
        scratch_shapes=[pltpu.VMEM((2, 2, tc, HALF), U32), pltpu.SemaphoreType.DMA((2,))],
        compiler_params=_cparams(("arbitrary", "arbitrary")),
        name="moe_combine_ln",
    )(p0, p1, p0, p1, ys, x1, mod_l, rt, ln_g, ln_b)


def _rope_tables(S):
    inv = 1.0 / (ROPE_THETA ** (jnp.arange(0, QK_DIM, 2, dtype=F32) / QK_DIM))
    ang = jnp.arange(S, dtype=F32)[:, None] * inv[None, :]
    cos, sin = jnp.cos(ang), jnp.sin(ang)
    half = QK_DIM // 2
    first = (jnp.arange(LANES) % QK_DIM) < half
    cs = jnp.tile(cos, (1, LANES // half))
    sn = jnp.tile(sin, (1, LANES // half))
    sa = jnp.where(first[None, :], -sn, 0.0)
    sb = jnp.where(first[None, :], 0.0, sn)
    return cs, sa, sb


def _router_matrix(w_group, b_group, w_router, b_router):
    D = w_group.shape[0]
    w = jnp.zeros((D, LANES), F32)
    w = w.at[:, 0:N_GROUPS].set(w_group)
    wr = jnp.transpose(w_router, (1, 0, 2)).reshape(D, N_EXPERTS)
    w = w.at[:, ROUTE_LANE0:ROUTE_LANE0 + N_EXPERTS].set(wr)
    b = jnp.zeros((1, LANES), F32)
    b = b.at[0, 0:N_GROUPS].set(b_group)
    b = b.at[0, ROUTE_LANE0:ROUTE_LANE0 + N_EXPERTS].set(b_router.reshape(N_EXPERTS))
    hi = w.astype(BF16)
    lo = (w - hi.astype(F32)).astype(BF16)
    return hi, lo, b


def _routing_tables(rt, cnt, n_tiles):
    B, _, S = rt.shape
    tm = EXPERT_TILE
    e = rt[:, 0:2, :].astype(I32)
    rank = rt[:, 4:6, :].astype(I32)
    counts = cnt[0, ROUTE_LANE0:ROUTE_LANE0 + N_EXPERTS].astype(I32)
    tiles = (counts + tm - 1) // tm
    tile_end = jnp.cumsum(tiles)
    tile_start = tile_end - tiles
    onehot = e[..., None] == jnp.arange(N_EXPERTS, dtype=I32)
    start = jnp.sum(jnp.where(onehot, tile_start * tm, 0), axis=-1)
    pos = start + rank
    pos0 = pos[:, 0, :].reshape(B * S)
    pos1 = pos[:, 1, :].reshape(B * S)
    tile_ids = jnp.arange(n_tiles, dtype=I32)
    tile_expert = jnp.minimum(jnp.sum(tile_ids[:, None] >= tile_end[None, :], axis=1), N_EXPERTS - 1).astype(I32)
    n_valid = tile_end[-1:].astype(I32)
    return pos0, pos1, tile_expert, n_valid


def kernel(x, c, w_ada, b_ada, w_in, lambda_q1, lambda_k1, lambda_q2, lambda_k2, subln_g, sg_ln_g, sg_ln_b, w_spatial, b_spatial, w_out, ln1_g, ln1_b, w_group, b_group, w_router, b_router, w_gate, w_up, w_down, ln2_g, ln2_b):
    B, S, D = x.shape
    depth = w_in.shape[0]
    T = B * S
    alpha = (2.0 * depth) ** 0.25
    n_tiles = (2 * T) // EXPERT_TILE + N_EXPERTS
    n_rows = n_tiles * EXPERT_TILE

    mod = _ada(c, w_ada, b_ada)
    tabs = _rope_tables(S)
    for l in range(depth):
        mod_l = mod[l].reshape(B, 1, 6 * D)
        lam_init = 0.8 - 0.6 * math.exp(-0.3 * l)

        qT, k, vTb, zu, zvn = _inproj(x, mod_l, w_in[l].astype(BF16), tabs,
                                      sg_ln_g[l].reshape(1, SG_WIDTH), sg_ln_b[l].reshape(1, SG_WIDTH))
        lams = (lambda_q1[l].reshape(1, QK_DIM), lambda_k1[l].reshape(1, QK_DIM),
                lambda_q2[l].reshape(1, QK_DIM), lambda_k2[l].reshape(1, QK_DIM))
        att = _attention(qT, k, vTb, lams, subln_g[l].reshape(V_DIM, 1), lam_init)
        b_sp_full = jnp.broadcast_to(b_spatial[l][:, :, None], (N_SG, CHUNK, SG_DIM))
        x1 = _mix(x, mod_l, att, zu, zvn, w_spatial[l], b_sp_full, w_out[l].astype(BF16),
                  ln1_g[l].reshape(1, D), ln1_b[l].reshape(1, D), alpha)

        wr_hi, wr_lo, br = _router_matrix(w_group[l], b_group[l], w_router[l], b_router[l])
        hp, rt, cnt = _route(x1, mod_l, wr_hi, wr_lo, br)
        pos0, pos1, tile_expert, n_valid = _routing_tables(rt, cnt, n_tiles)
        xs = _dispatch(pos0, pos1, hp.reshape(T, HALF), n_rows)
        ys = _experts(tile_expert, n_valid, xs, w_gate, w_up, w_down, l)
        x = _combine(pos0, pos1, ys, x1, mod_l, rt, ln2_g[l].reshape(1, D), ln2_b[l].reshape(1, D), alpha)
    return x
```

```python
import functools
import math

import jax
import jax.numpy as jnp
from jax import lax
from jax.experimental import pallas as pl
from jax.experimental.pallas import tpu as pltpu

F32 = jnp.float32
BF16 = jnp.bfloat16
U32 = jnp.uint32
I32 = jnp.int32

D_MODEL = 1024
N_HEADS = 4
QK_DIM = 64
V_DIM = 128
HEAD_COLS = 2 * QK_DIM
QK_COLS = N_HEADS * HEAD_COLS
DIFF_WIDTH = N_HEADS * V_DIM
N_SG = 4
SG_DIM = 128
SG_WIDTH = N_SG * SG_DIM
CHUNK = 128
N_GROUPS = 4
EXP_PER_GROUP = 8
N_EXPERTS = N_GROUPS * EXP_PER_GROUP
D_EXPERT = 512
ROPE_THETA = 10000.0
LN_EPS = 1e-5
LANES = 128
HALF = D_MODEL // 2

ROW_TILE = 512
V_ROWS = V_DIM + 16
EXPERT_TILE = 512
DISPATCH_BLOCK = 2048
COMBINE_TILE = 256
COMBINE_ROWS = 64
SUBLANES = 8
ROUTE_LANE0 = 8
VMEM_LIMIT = 48 * 1024 * 1024

DEN_MIN = 2.0 ** -40
DEN_MAX = 2.0 ** 40
LOG2E = 1.4426950408889634
Q_SCALE = (QK_DIM ** -0.5) * LOG2E


def _cparams(sem):
    return pltpu.CompilerParams(dimension_semantics=sem, vmem_limit_bytes=VMEM_LIMIT)


def _layer_norm_rows(y, g, b):
    mu = jnp.mean(y, axis=-1, keepdims=True)
    yc = y - mu
    var = jnp.mean(yc * yc, axis=-1, keepdims=True)
    return yc * lax.rsqrt(var + LN_EPS) * g + b


def _gelu(x):
    return 0.5 * x * (1.0 + lax.erf(x * (2.0 ** -0.5)))


def _pack_halves(y):
    lo = pltpu.bitcast(y[:, :HALF].astype(BF16).astype(F32), U32) >> 16
    hi = pltpu.bitcast(y[:, HALF:].astype(BF16).astype(F32), U32) & jnp.uint32(0xFFFF0000)
    return lo | hi


def _unpack_halves(p):
    lo = pltpu.bitcast(p << 16, F32)
    hi = pltpu.bitcast(p & jnp.uint32(0xFFFF0000), F32)
    return lo, hi


def _ada_kernel(c_ref, w_ref, b_ref, o_ref):
    c = c_ref[...]
    sc = c * jax.nn.sigmoid(c)
    o_ref[...] = jnp.dot(sc, w_ref[...], precision=lax.Precision.HIGHEST,
                         preferred_element_type=F32) + b_ref[...]


def _ada(c, w_ada, b_ada):
    L, D, N = w_ada.shape
    B = c.shape[0]
    tn = 1536
    return pl.pallas_call(
        _ada_kernel,
        grid=(L, N // tn),
        in_specs=[
            pl.BlockSpec((B, D), lambda l, j: (0, 0)),
            pl.BlockSpec((None, D, tn), lambda l, j: (l, 0, j)),
            pl.BlockSpec((None, 1, tn), lambda l, j: (l, 0, j)),
        ],
        out_specs=pl.BlockSpec((None, B, tn), lambda l, j: (l, 0, j)),
        out_shape=jax.ShapeDtypeStruct((L, B, N), F32),
        compiler_params=_cparams(("arbitrary", "arbitrary")),
        name="ada",
    )(c, w_ada, b_ada.reshape(L, 1, N))


def _inproj_kernel(x_ref, mod_ref, w_ref, cs_ref, sa_ref, sb_ref, lng_ref, lnb_ref,
                   qT_ref, k_ref, vT_ref, zu_ref, zvn_ref):
    D = D_MODEL
    x = x_ref[...]
    sh = mod_ref[:, 0:D]
    sc = mod_ref[:, D:2 * D]
    h = (x * (1.0 + sc) + sh).astype(BF16)
    cs = cs_ref[...]
    sa = sa_ref[...]
    sb = sb_ref[...]

    def rope(t):
        return t * cs + pltpu.roll(t, 96, 1) * sa + pltpu.roll(t, 32, 1) * sb

    q = jnp.dot(h, w_ref[:, 0:QK_COLS], preferred_element_type=F32)
    for j in range(N_HEADS):
        sl = slice(j * LANES, (j + 1) * LANES)
        qT_ref[sl, :] = (rope(q[:, sl]) * Q_SCALE).T.astype(BF16)
    k = jnp.dot(h, w_ref[:, QK_COLS:2 * QK_COLS], preferred_element_type=F32)
    for j in range(N_HEADS):
        sl = slice(j * LANES, (j + 1) * LANES)
        k_ref[:, sl] = rope(k[:, sl]).astype(BF16)
    c0 = 2 * QK_COLS
    v = jnp.dot(h, w_ref[:, c0:c0 + DIFF_WIDTH], preferred_element_type=F32)
    for j in range(N_HEADS):
        vT_ref[j * V_ROWS:j * V_ROWS + V_DIM, :] = v[:, j * V_DIM:(j + 1) * V_DIM].T.astype(BF16)
        vT_ref[j * V_ROWS + V_DIM:(j + 1) * V_ROWS, :] = jnp.ones((V_ROWS - V_DIM, v.shape[0]), BF16)
    c0 += DIFF_WIDTH
    u = jnp.dot(h, w_ref[:, c0:c0 + SG_WIDTH], preferred_element_type=F32)
    zu_ref[...] = _gelu(u).astype(BF16)
    c0 += SG_WIDTH
    z = _gelu(jnp.dot(h, w_ref[:, c0:c0 + SG_WIDTH], preferred_element_type=F32))
    for g in range(N_SG):
        sl = slice(g * SG_DIM, (g + 1) * SG_DIM)
        zvn_ref[:, sl] = _layer_norm_rows(z[:, sl], lng_ref[:, sl], lnb_ref[:, sl]).astype(BF16)


def _inproj(x, mod_l, w_in_b, rope_tabs, ln_g, ln_b):
    B, S, D = x.shape
    tm = min(ROW_TILE, S)
    nt = S // tm
    cs, sa, sb = rope_tabs
    row = lambda b, i: (b, i, 0)
    tab = pl.BlockSpec((tm, LANES), lambda b, i: (i, 0))
    return pl.pallas_call(
        _inproj_kernel,
        grid=(B, nt),
        in_specs=[
            pl.BlockSpec((None, tm, D), row),
            pl.BlockSpec((None, 1, 6 * D), lambda b, i: (b, 0, 0)),
            pl.BlockSpec(w_in_b.shape, lambda b, i: (0, 0)),
            tab, tab, tab,
            pl.BlockSpec((1, SG_WIDTH), lambda b, i: (0, 0)),
            pl.BlockSpec((1, SG_WIDTH), lambda b, i: (0, 0)),
        ],
        out_specs=[
            pl.BlockSpec((None, QK_COLS, tm), lambda b, i: (b, 0, i)),
            pl.BlockSpec((None, tm, QK_COLS), row),
            pl.BlockSpec((None, None, N_HEADS * V_ROWS, tm), lambda b, i: (b, i, 0, 0)),
            pl.BlockSpec((None, tm, SG_WIDTH), row),
            pl.BlockSpec((None, tm, SG_WIDTH), row),
        ],
        out_shape=[
            jax.ShapeDtypeStruct((B, QK_COLS, S), BF16),
            jax.ShapeDtypeStruct((B, S, QK_COLS), BF16),
            jax.ShapeDtypeStruct((B, nt, N_HEADS * V_ROWS, tm), BF16),
            jax.ShapeDtypeStruct((B, S, SG_WIDTH), BF16),
            jax.ShapeDtypeStruct((B, S, SG_WIDTH), BF16),
        ],
        compiler_params=_cparams(("arbitrary", "arbitrary")),
        name="inproj",
    )(x, mod_l, w_in_b, cs, sa, sb, ln_g, ln_b)


def _attn_kernel(lq1_ref, lk1_ref, lq2_ref, lk2_ref, g_ref, qT_ref, k_ref, vT_ref, o_ref,
                 q_scr, s_scr, m_scr, acc_scr, *, lam_init):
    tq = qT_ref.shape[1]
    tk = vT_ref.shape[2]
    i = pl.program_id(2)
    qT = qT_ref[...]
    rows = lax.broadcasted_iota(I32, qT.shape, 0)
    zero = jnp.zeros_like(qT)
    q_scr[0] = jnp.where(rows < QK_DIM, qT, zero)
    q_scr[1] = jnp.where(rows >= QK_DIM, qT, zero)

    def scores(j, slot, c0):
        kb = k_ref[pl.ds(pl.multiple_of(j * tk, tk), tk), :]
        for mp in range(2):
            s_scr[slot, mp, :, c0:] = jnp.dot(kb, q_scr[mp, :, c0:], preferred_element_type=F32)

    def block_scores(slot, mp, c0, c1, masked):
        s = s_scr[slot, mp, :, c0:c1]
        if masked:
            kpos = lax.broadcasted_iota(I32, s.shape, 0)
            qpos = lax.broadcasted_iota(I32, s.shape, 1)
            s = jnp.where(kpos <= qpos, s, -jnp.inf)
        return s

    def pv_unshifted(j, slot, c0, c1, masked):
        vb = vT_ref[j]
        for mp in range(2):
            p = jnp.exp2(block_scores(slot, mp, c0, c1, masked)).astype(BF16)
            acc_scr[mp, :, c0:c1] += jnp.dot(vb, p, preferred_element_type=F32)

    def pv_online(j, slot, c0, c1, masked):
        vb = vT_ref[j]
        for mp in range(2):
            s = block_scores(slot, mp, c0, c1, masked)
            m_old = m_scr[mp, :, c0:c1]
            m_new = jnp.maximum(m_old, jnp.max(s, axis=0, keepdims=True))
            alpha = jnp.exp2(m_old - m_new)
            p = jnp.exp2(s - m_new).astype(BF16)
            acc_scr[mp, :, c0:c1] = alpha * acc_scr[mp, :, c0:c1] + jnp.dot(vb, p, preferred_element_type=F32)
            m_scr[mp, :, c0:c1] = m_new

    def sweep(block):
        acc_scr[...] = jnp.zeros(acc_scr.shape, F32)
        scores(0, 0, 0)

        def body(u, carry):
            t = 2 * u + 1
            scores(t, 1, 0)
            block(t - 1, 0, 0, tq, False)
            scores(t + 1, 0, 0)
            block(t, 1, 0, tq, False)
            return carry

        lax.fori_loop(0, i, body, 0)
        scores(2 * i + 1, 1, tk)
        block(2 * i, 0, 0, tk, True)
        block(2 * i, 0, tk, tq, False)
        block(2 * i + 1, 1, tk, tq, True)

    sweep(pv_unshifted)
    den = jnp.concatenate([acc_scr[0, V_DIM:V_DIM + 1, :], acc_scr[1, V_DIM:V_DIM + 1, :]], axis=0)
    in_range = (den >= DEN_MIN) & (den <= DEN_MAX)
    n_bad = jnp.sum(jnp.where(in_range, 0.0, 1.0))

    @pl.when(n_bad > 0.0)
    def _():
        m_scr[...] = jnp.full(m_scr.shape, -jnp.inf, F32)
        sweep(pv_online)

    lam = (jnp.exp(jnp.sum(lq1_ref[...] * lk1_ref[...], axis=1, keepdims=True))
           - jnp.exp(jnp.sum(lq2_ref[...] * lk2_ref[...], axis=1, keepdims=True)) + lam_init)
    a1 = acc_scr[0]
    a2 = acc_scr[1]
    out = a1[0:V_DIM] / a1[V_DIM:V_DIM + 1] - lam * (a2[0:V_DIM] / a2[V_DIM:V_DIM + 1])
    ms = jnp.mean(out * out, axis=0, keepdims=True)
    y = out * lax.rsqrt(ms + LN_EPS) * g_ref[...] * (1.0 - lam_init)
    o_ref[...] = y.T.astype(BF16)


def _attention(qT, k, vTb, lams, g_col, lam_init):
    B, _, S = qT.shape
    nkv, tk = vTb.shape[1], vTb.shape[3]
    tq = 2 * tk
    small = pl.BlockSpec((1, QK_DIM), lambda b, h, i: (0, 0))
    return pl.pallas_call(
        functools.partial(_attn_kernel, lam_init=lam_init),
        grid=(B, N_HEADS, S // tq),
        in_specs=[
            small, small, small, small,
            pl.BlockSpec((V_DIM, 1), lambda b, h, i: (0, 0)),
            pl.BlockSpec((None, HEAD_COLS, tq), lambda b, h, i: (b, h, i)),
            pl.BlockSpec((None, S, HEAD_COLS), lambda b, h, i: (b, 0, h)),
            pl.BlockSpec((None, nkv, V_ROWS, tk), lambda b, h, i: (b, 0, h, 0)),
        ],
        out_specs=pl.BlockSpec((None, tq, V_DIM), lambda b, h, i: (b, i, h)),
        out_shape=jax.ShapeDtypeStruct((B, S, DIFF_WIDTH), BF16),
        scratch_shapes=[
            pltpu.VMEM((2, HEAD_COLS, tq), BF16),
            pltpu.VMEM((2, 2, tk, tq), F32),
            pltpu.VMEM((2, 1, tq), F32),
            pltpu.VMEM((2, V_ROWS, tq), F32),
        ],
        compiler_params=_cparams(("arbitrary", "arbitrary", "arbitrary")),
        name="diff_attn",
    )(*lams, g_col, qT, k, vTb)


def _mix_kernel(x_ref, mod_ref, att_ref, zu_ref, zvn_ref, wsp_ref, bsp_ref, wo_ref, g_ref, b_ref,
                o_ref, sg_scr, *, alpha):
    D = D_MODEL
    nc = zu_ref.shape[0]
    r = lax.broadcasted_iota(I32, (CHUNK, CHUNK), 0)
    c = lax.broadcasted_iota(I32, (CHUNK, CHUNK), 1)
    causal = r >= c
    for g in range(N_SG):
        sl = slice(g * SG_DIM, (g + 1) * SG_DIM)
        w = jnp.where(causal, wsp_ref[g], 0.0).astype(BF16)
        z = jnp.concatenate([zvn_ref[n, :, sl] for n in range(nc)], axis=1)
        mixed = jnp.dot(w, z, preferred_element_type=F32)
        for n in range(nc):
            gate = mixed[:, n * SG_DIM:(n + 1) * SG_DIM] + bsp_ref[g]
            sg_scr[n * CHUNK:(n + 1) * CHUNK, sl] = (zu_ref[n, :, sl].astype(F32) * gate).astype(BF16)
    mix = (jnp.dot(att_ref[...], wo_ref[0:DIFF_WIDTH, :], preferred_element_type=F32)
           + jnp.dot(sg_scr[...], wo_ref[DIFF_WIDTH:, :], preferred_element_type=F32))
    gt = mod_ref[:, 2 * D:3 * D]
    y = alpha * x_ref[...] + (1.0 + gt) * mix
    o_ref[...] = _layer_norm_rows(y, g_ref[...], b_ref[...])


def _mix(x, mod_l, att, zu, zvn, w_sp, b_sp_full, w_out_b, ln_g, ln_b, alpha):
    B, S, D = x.shape
    tm = min(ROW_TILE, S)
    nc = tm // CHUNK
    row = lambda b, i: (b, i, 0)
    zu4 = zu.reshape(B, S // CHUNK, CHUNK, SG_WIDTH)
    zvn4 = zvn.reshape(B, S // CHUNK, CHUNK, SG_WIDTH)
    chunked = pl.BlockSpec((None, nc, CHUNK, SG_WIDTH), lambda b, i: (b, i, 0, 0))
    full2 = lambda a: pl.BlockSpec(a.shape, lambda b, i: (0,) * a.ndim)
    return pl.pallas_call(
        functools.partial(_mix_kernel, alpha=alpha),
        grid=(B, S // tm),
        in_specs=[
            pl.BlockSpec((None, tm, D), row),
            pl.BlockSpec((None, 1, 6 * D), lambda b, i: (b, 0, 0)),
            pl.BlockSpec((None, tm, DIFF_WIDTH), row),
            chunked, chunked,
            full2(w_sp), full2(b_sp_full), full2(w_out_b), full2(ln_g), full2(ln_b),
        ],
        out_specs=pl.BlockSpec((None, tm, D), row),
        out_shape=jax.ShapeDtypeStruct((B, S, D), F32),
        scratch_shapes=[pltpu.VMEM((tm, SG_WIDTH), BF16)],
        compiler_params=_cparams(("arbitrary", "arbitrary")),
        name="sgate_outproj_ln",
    )(x, mod_l, att, zu4, zvn4, w_sp, b_sp_full, w_out_b, ln_g, ln_b)


def _route_kernel(x_ref, mod_ref, wr_hi_ref, wr_lo_ref, br_ref, hp_ref, rt_ref, cnt_ref, run_scr):
    D = D_MODEL
    tm = x_ref.shape[0]

    @pl.when((pl.program_id(0) == 0) & (pl.program_id(1) == 0))
    def _():
        run_scr[...] = jnp.zeros(run_scr.shape, F32)

    sh = mod_ref[:, 3 * D:4 * D]
    sc = mod_ref[:, 4 * D:5 * D]
    h = x_ref[...] * (1.0 + sc) + sh
    hp_ref[...] = _pack_halves(h)
    h_hi = h.astype(BF16)
    h_lo = (h - h_hi.astype(F32)).astype(BF16)
    logit = (jnp.dot(h_hi, wr_hi_ref[...], preferred_element_type=F32)
             + jnp.dot(h_lo, wr_hi_ref[...], preferred_element_type=F32)
             + jnp.dot(h_hi, wr_lo_ref[...], preferred_element_type=F32)) + br_ref[...]
    lane = lax.broadcasted_iota(I32, logit.shape, 1).astype(F32)
    neg = -jnp.inf
    big = float(LANES)

    def first_argmax(v):
        mx = jnp.max(v, axis=1, keepdims=True)
        idx = jnp.min(jnp.where(v == mx, lane, big), axis=1, keepdims=True)
        return mx, idx

    in_grp = lane < N_GROUPS
    gmax, gidx = first_argmax(jnp.where(in_grp, logit, neg))
    g_p = 1.0 / jnp.sum(jnp.where(in_grp, jnp.exp(logit - gmax), 0.0), axis=1, keepdims=True)
    lo_lane = ROUTE_LANE0 + EXP_PER_GROUP * gidx
    sel = jnp.where((lane >= lo_lane) & (lane < lo_lane + EXP_PER_GROUP), logit, neg)
    v1, i1 = first_argmax(sel)
    v2, i2 = first_argmax(jnp.where(lane == i1, neg, sel))
    t = jnp.exp(v2 - v1)
    w1 = g_p / (1.0 + t)
    w2 = g_p * t / (1.0 + t)

    hot1 = lane == i1
    hot2 = lane == i2
    twohot = jnp.where(hot1 | hot2, 1.0, 0.0)
    r = lax.broadcasted_iota(I32, (tm, tm), 0)
    c = lax.broadcasted_iota(I32, (tm, tm), 1)
    strict = jnp.where(r > c, 1.0, 0.0).astype(BF16)
    before = jnp.dot(strict, twohot.astype(BF16), preferred_element_type=F32) + run_scr[...]
    r1 = jnp.sum(jnp.where(hot1, before, 0.0), axis=1, keepdims=True)
    r2 = jnp.sum(jnp.where(hot2, before, 0.0), axis=1, keepdims=True)
    run_scr[...] = run_scr[...] + jnp.sum(twohot, axis=0, keepdims=True)
    cnt_ref[...] = run_scr[...]

    fields = (i1 - ROUTE_LANE0, i2 - ROUTE_LANE0, w1, w2, r1, r2)
    slab = jnp.zeros(logit.shape, F32)
    for n, f in enumerate(fields):
        slab = jnp.where(lane == float(n), f, slab)
    rt_ref[...] = slab.T[0:8, :]


def _route(x1, mod_l, wr_hi, wr_lo, br):
    B, S, D = x1.shape
    tm = min(ROW_TILE, S)
    row = lambda b, i: (b, i, 0)
    full2 = lambda a: pl.BlockSpec(a.shape, lambda b, i: (0,) * a.ndim)
    return pl.pallas_call(
        _route_kernel,
        grid=(B, S // tm),
        in_specs=[
            pl.BlockSpec((None, tm, D), row),
            pl.BlockSpec((None, 1, 6 * D), lambda b, i: (b, 0, 0)),
            full2(wr_hi), full2(wr_lo), full2(br),
        ],
        out_specs=[
            pl.BlockSpec((None, tm, HALF), row),
            pl.BlockSpec((None, 8, tm), lambda b, i: (b, 0, i)),
            pl.BlockSpec((1, LANES), lambda b, i: (0, 0)),
        ],
        out_shape=[
            jax.ShapeDtypeStruct((B, S, HALF), U32),
            jax.ShapeDtypeStruct((B, 8, S), F32),
            jax.ShapeDtypeStruct((1, LANES), F32),
        ],
        scratch_shapes=[pltpu.VMEM((1, LANES), F32)],
        compiler_params=_cparams(("arbitrary", "arbitrary")),
        name="moe_route",
    )(x1, mod_l, wr_hi, wr_lo, br)


def _row_of(ref, p):
    return ref.at[p >> 3, pl.ds(p & (SUBLANES - 1), 1)]


def _dispatch_kernel(p0_ref, p1_ref, hp_ref, xs_in_ref, xs_ref, sem):
    del xs_in_ref
    ng = hp_ref.shape[0]

    def row_copy(g, u, p):
        return pltpu.make_async_copy(hp_ref.at[g, pl.ds(u, 1)], _row_of(xs_ref, p), sem)

    def issue(g, carry):
        for u in range(SUBLANES):
            t = g * SUBLANES + u
            row_copy(g, u, p0_ref[0, 0, t]).start(priority=u % 2)
            row_copy(g, u, p1_ref[0, 0, t]).start(priority=(u + 1) % 2)
        return carry

    lax.fori_loop(0, ng, issue, 0)

    for _ in range(2):
        pltpu.make_async_copy(hp_ref, xs_ref.at[pl.ds(0, ng)], sem).wait()


def _dispatch(pos0, pos1, hp, n_rows):
    T = hp.shape[0] * SUBLANES
    nb = min(DISPATCH_BLOCK, T)
    blk = pl.BlockSpec((1, 1, nb), lambda i: (i, 0, 0), memory_space=pltpu.SMEM)
    xs0 = jnp.zeros((n_rows // SUBLANES, SUBLANES, HALF), U32)
    return pl.pallas_call(
        _dispatch_kernel,
        grid=(T // nb,),
        in_specs=[blk, blk, pl.BlockSpec((nb // SUBLANES, SUBLANES, HALF), lambda i: (i, 0, 0)),
                  pl.BlockSpec(memory_space=pl.ANY)],
        out_specs=pl.BlockSpec(memory_space=pl.ANY),
        out_shape=jax.ShapeDtypeStruct(xs0.shape, U32),
        scratch_shapes=[pltpu.SemaphoreType.DMA(())],
        input_output_aliases={3: 0},
        compiler_params=_cparams(("arbitrary",)),
        name="moe_dispatch",
    )(pos0.reshape(T // nb, 1, nb), pos1.reshape(T // nb, 1, nb), hp, xs0)


def _expert_kernel(te_ref, nv_ref, xs_ref, wg_ref, wu_ref, wd_ref, ys_ref, wg_scr, wu_scr, wd_scr):
    n = pl.program_id(0)
    valid = n < nv_ref[0]
    new_expert = (n == 0) | (te_ref[n] != te_ref[jnp.maximum(n - 1, 0)])

    @pl.when(valid & new_expert)
    def _():
        wg_scr[...] = wg_ref[...].astype(BF16)
        wu_scr[...] = wu_ref[...].astype(BF16)
        wd_scr[...] = wd_ref[...].astype(BF16)

    @pl.when(valid)
    def _():
        lo, hi = _unpack_halves(xs_ref[...].reshape(EXPERT_TILE, HALF))
        lo = lo.astype(BF16)
        hi = hi.astype(BF16)
        g = (jnp.dot(lo, wg_scr[0:HALF, :], preferred_element_type=F32)
             + jnp.dot(hi, wg_scr[HALF:, :], preferred_element_type=F32))
        u = (jnp.dot(lo, wu_scr[0:HALF, :], preferred_element_type=F32)
             + jnp.dot(hi, wu_scr[HALF:, :], preferred_element_type=F32))
        a = (g * jax.nn.sigmoid(g) * u).astype(BF16)
        y = _pack_halves(jnp.dot(a, wd_scr[...], preferred_element_type=F32))
        ys_ref[...] = y.reshape(ys_ref.shape)


def _experts(tile_expert, n_valid, xs, w_gate, w_up, w_down, layer):
    n_rows = xs.shape[0] * SUBLANES
    tm = EXPERT_TILE
    nt = n_rows // tm
    rows = lambda n, te, nv: (jnp.minimum(n, nv[0] - 1), 0, 0)
    wsel = lambda n, te, nv: (layer, te[n], 0, 0)
    return pl.pallas_call(
        _expert_kernel,
        grid_spec=pltpu.PrefetchScalarGridSpec(
            num_scalar_prefetch=2,
            grid=(nt,),
            in_specs=[
                pl.BlockSpec((tm // SUBLANES, SUBLANES, HALF), rows),
                pl.BlockSpec((None, None, D_MODEL, D_EXPERT), wsel),
                pl.BlockSpec((None, None, D_MODEL, D_EXPERT), wsel),
                pl.BlockSpec((None, None, D_EXPERT, D_MODEL), wsel),
            ],
            out_specs=pl.BlockSpec((tm // SUBLANES, SUBLANES, HALF), rows),
            scratch_shapes=[
                pltpu.VMEM((D_MODEL, D_EXPERT), BF16),
                pltpu.VMEM((D_MODEL, D_EXPERT), BF16),
                pltpu.VMEM((D_EXPERT, D_MODEL), BF16),
            ],
        ),
        out_shape=jax.ShapeDtypeStruct(xs.shape, U32),
        compiler_params=_cparams(("arbitrary",)),
        name="moe_experts",
    )(tile_expert, n_valid, xs, w_gate, w_up, w_down)


def _combine_kernel(p0_ref, p1_ref, p0n_ref, p1n_ref, ys_ref, x_ref, mod_ref, rt_ref, g_ref, b_ref, o_ref,
                    buf, wt_scr, sems, *, alpha):
    D = D_MODEL
    tc = x_ref.shape[0]
    g = pl.program_id(0) * pl.num_programs(1) + pl.program_id(1)
    n_steps = pl.num_programs(0) * pl.num_programs(1)
    slot = g % 2

    gpt = COMBINE_ROWS // SUBLANES

    def row_copy(p, s, choice, g, u):
        return pltpu.make_async_copy(_row_of(ys_ref, p), buf.at[s, choice, g, pl.ds(u, 1)], sems.at[s])

    def issue(pa_ref, pb_ref, s, trip):
        for j in range(COMBINE_ROWS):
            t = trip * COMBINE_ROWS + j
            g, u = trip * gpt + j // SUBLANES, j % SUBLANES
            row_copy(pa_ref[0, 0, t], s, 0, g, u).start(priority=j % 2)
            row_copy(pb_ref[0, 0, t], s, 1, g, u).start(priority=(j + 1) % 2)

    def wait_slot(s):
        for choice in range(2):
            pltpu.make_async_copy(ys_ref.at[pl.ds(0, tc // SUBLANES)], buf.at[s, choice], sems.at[s]).wait()

    @pl.when(g == 0)
    def _():
        def first(trip, carry):
            issue(p0_ref, p1_ref, 0, trip)
            return carry

        lax.fori_loop(0, tc // COMBINE_ROWS, first, 0)

    wait_slot(slot)
    wt_scr[...] = rt_ref[...].T
    gt1 = 1.0 + mod_ref[:, 5 * D:6 * D]
    gamma = g_ref[...]
    beta = b_ref[...]

    def body(trip, carry):
        r = pl.ds(pl.multiple_of(trip * COMBINE_ROWS, COMBINE_ROWS), COMBINE_ROWS)
        w1 = wt_scr[r, 2:3]
        w2 = wt_scr[r, 3:4]
        gs = pl.ds(trip * gpt, gpt)
        lo1, hi1 = _unpack_halves(buf[slot, 0, gs].reshape(COMBINE_ROWS, HALF))
        lo2, hi2 = _unpack_halves(buf[slot, 1, gs].reshape(COMBINE_ROWS, HALF))
        ffn = jnp.concatenate([w1 * lo1 + w2 * lo2, w1 * hi1 + w2 * hi2], axis=1)
        y = alpha * x_ref[r, :] + gt1 * ffn
        o_ref[r, :] = _layer_norm_rows(y, gamma, beta)
        issue(p0n_ref, p1n_ref, 1 - slot, trip)
        return carry

    lax.fori_loop(0, tc // COMBINE_ROWS, body, 0)

    @pl.when(g + 1 == n_steps)
    def _():
        wait_slot(1 - slot)


def _combine(pos0, pos1, ys, x1, mod_l, rt, ln_g, ln_b, alpha):
    B, S, D = x1.shape
    tc = min(COMBINE_TILE, S)
    nt = S // tc
    blk = pl.BlockSpec((1, 1, tc), lambda b, i: (b * nt + i, 0, 0), memory_space=pltpu.SMEM)
    nxt = pl.BlockSpec((1, 1, tc), lambda b, i: (jnp.minimum(b * nt + i + 1, B * nt - 1), 0, 0),
                       memory_space=pltpu.SMEM)
    row = lambda b, i: (b, i, 0)
    full2 = lambda a: pl.BlockSpec(a.shape, lambda b, i: (0,) * a.ndim)
    p0 = pos0.reshape(B * nt, 1, tc)
    p1 = pos1.reshape(B * nt, 1, tc)
    return pl.pallas_call(
        functools.partial(_combine_kernel, alpha=alpha),
        grid=(B, nt),
        in_specs=[
            blk, blk, nxt, nxt,
            pl.BlockSpec(memory_space=pl.ANY),
            pl.BlockSpec((None, tc, D), row),
            pl.BlockSpec((None, 1, 6 * D), lambda b, i: (b, 0, 0)),
            pl.BlockSpec((None, 8, tc), lambda b, i: (b, 0, i)),
            full2(ln_g), full2(ln_b),
        ],
        out_specs=pl.BlockSpec((None, tc, D), row),
        out_shape=jax.ShapeDtypeStruct((B, S, D), F32),
        scratch_shapes=[pltpu.VMEM((2, 2, tc // SUBLANES, SUBLANES, HALF), U32), pltpu.VMEM((tc, 8), F32),
                        pltpu.SemaphoreType.DMA((2,))],
        compiler_params=_cparams(("arbitrary", "arbitrary")),
        name="moe_combine_ln",
    )(p0, p1, p0, p1, ys, x1, mod_l, rt, ln_g, ln_b)


def _rope_tables(S):
    inv = 1.0 / (ROPE_THETA ** (jnp.arange(0, QK_DIM, 2, dtype=F32) / QK_DIM))
    ang = jnp.arange(S, dtype=F32)[:, None] * inv[None, :]
    cos, sin = jnp.cos(ang), jnp.sin(ang)
    half = QK_DIM // 2
    first = (jnp.arange(LANES) % QK_DIM) < half
    cs = jnp.tile(cos, (1, LANES // half))
    sn = jnp.tile(sin, (1, LANES // half))
    sa = jnp.where(first[None, :], -sn, 0.0)
    sb = jnp.where(first[None, :], 0.0, sn)
    return cs, sa, sb


def _router_matrix(w_group, b_group, w_router, b_router):
    D = w_group.shape[0]
    w = jnp.zeros((D, LANES), F32)
    w = w.at[:, 0:N_GROUPS].set(w_group)
    wr = jnp.transpose(w_router, (1, 0, 2)).reshape(D, N_EXPERTS)
    w = w.at[:, ROUTE_LANE0:ROUTE_LANE0 + N_EXPERTS].set(wr)
    b = jnp.zeros((1, LANES), F32)
    b = b.at[0, 0:N_GROUPS].set(b_group)
    b = b.at[0, ROUTE_LANE0:ROUTE_LANE0 + N_EXPERTS].set(b_router.reshape(N_EXPERTS))
    hi = w.astype(BF16)
    lo = (w - hi.astype(F32)).astype(BF16)
    return hi, lo, b


def _routing_tables(rt, cnt, n_tiles):
    B, _, S = rt.shape
    tm = EXPERT_TILE
    e = rt[:, 0:2, :].astype(I32)
    rank = rt[:, 4:6, :].astype(I32)
    counts = cnt[0, ROUTE_LANE0:ROUTE_LANE0 + N_EXPERTS].astype(I32)
    tiles = (counts + tm - 1) // tm
    tile_end = jnp.cumsum(tiles)
    tile_start = tile_end - tiles
    onehot = e[..., None] == jnp.arange(N_EXPERTS, dtype=I32)
    start = jnp.sum(jnp.where(onehot, tile_start * tm, 0), axis=-1)
    pos = start + rank
    pos0 = pos[:, 0, :].reshape(B * S)
    pos1 = pos[:, 1, :].reshape(B * S)
    tile_ids = jnp.arange(n_tiles, dtype=I32)
    tile_expert = jnp.minimum(jnp.sum(tile_ids[:, None] >= tile_end[None, :], axis=1), N_EXPERTS - 1).astype(I32)
    n_valid = tile_end[-1:].astype(I32)
    return pos0, pos1, tile_expert, n_valid


def kernel(x, c, w_ada, b_ada, w_in, lambda_q1, lambda_k1, lambda_q2, lambda_k2, subln_g, sg_ln_g, sg_ln_b, w_spatial, b_spatial, w_out, ln1_g, ln1_b, w_group, b_group, w_router, b_router, w_gate, w_up, w_down, ln2_g, ln2_b):
    B, S, D = x.shape
    depth = w_in.shape[0]
    T = B * S
    alpha = (2.0 * depth) ** 0.25
    n_tiles = (2 * T) // EXPERT_TILE + N_EXPERTS
    n_rows = n_tiles * EXPERT_TILE

    mod = _ada(c, w_ada, b_ada)
    tabs = _rope_tables(S)
    for l in range(depth):
        mod_l = mod[l].reshape(B, 1, 6 * D)
        lam_init = 0.8 - 0.6 * math.exp(-0.3 * l)

        qT, k, vTb, zu, zvn = _inproj(x, mod_l, w_in[l].astype(BF16), tabs,
                                      sg_ln_g[l].reshape(1, SG_WIDTH), sg_ln_b[l].reshape(1, SG_WIDTH))
        lams = (lambda_q1[l].reshape(1, QK_DIM), lambda_k1[l].reshape(1, QK_DIM),
                lambda_q2[l].reshape(1, QK_DIM), lambda_k2[l].reshape(1, QK_DIM))
        att = _attention(qT, k, vTb, lams, subln_g[l].reshape(V_DIM, 1), lam_init)
        b_sp_full = jnp.broadcast_to(b_spatial[l][:, :, None], (N_SG, CHUNK, SG_DIM))
        x1 = _mix(x, mod_l, att, zu, zvn, w_spatial[l], b_sp_full, w_out[l].astype(BF16),
                  ln1_g[l].reshape(1, D), ln1_b[l].reshape(1, D), alpha)

        wr_hi, wr_lo, br = _router_matrix(w_group[l], b_group[l], w_router[l], b_router[l])
        hp, rt, cnt = _route(x1, mod_l, wr_hi, wr_lo, br)
        pos0, pos1, tile_expert, n_valid = _routing_tables(rt, cnt, n_tiles)
        xs = _dispatch(pos0, pos1, hp.reshape(T // SUBLANES, SUBLANES, HALF), n_rows)
        ys = _experts(tile_expert, n_valid, xs, w_gate, w_up, w_down, l)
        x = _combine(pos0, pos1, ys, x1, mod_l, rt, ln2_g[l].reshape(1, D), ln2_b[l].reshape(1, D), alpha)
    return x
```

```python
import functools
import math

import jax
import jax.numpy as jnp
from jax import lax
from jax.experimental import pallas as pl
from jax.experimental.pallas import tpu as pltpu

F32 = jnp.float32
BF16 = jnp.bfloat16
U32 = jnp.uint32
I32 = jnp.int32

D_MODEL = 1024
N_HEADS = 4
QK_DIM = 64
V_DIM = 128
HEAD_COLS = 2 * QK_DIM
QK_COLS = N_HEADS * HEAD_COLS
DIFF_WIDTH = N_HEADS * V_DIM
N_SG = 4
SG_DIM = 128
SG_WIDTH = N_SG * SG_DIM
CHUNK = 128
N_GROUPS = 4
EXP_PER_GROUP = 8
N_EXPERTS = N_GROUPS * EXP_PER_GROUP
N_PAIRS = EXP_PER_GROUP * (EXP_PER_GROUP - 1) // 2
N_BUCKETS = N_GROUPS * N_PAIRS
D_EXPERT = 512
ROPE_THETA = 10000.0
LN_EPS = 1e-5
LANES = 128
HALF = D_MODEL // 2
ROW_WORDS = HALF + LANES

ROW_TILE = 512
V_ROWS = V_DIM + 16
EXPERT_TILE = 256
DISPATCH_BLOCK = 2048
COMBINE_TILE = 256
ROW_DMA_UNROLL = 8
ROUTE_LANE0 = 8
VMEM_LIMIT = 48 * 1024 * 1024

DEN_MIN = 2.0 ** -40
DEN_MAX = 2.0 ** 40
LOG2E = 1.4426950408889634
Q_SCALE = (QK_DIM ** -0.5) * LOG2E


def _cparams(sem):
    return pltpu.CompilerParams(dimension_semantics=sem, vmem_limit_bytes=VMEM_LIMIT)


def _layer_norm_rows(y, g, b):
    mu = jnp.mean(y, axis=-1, keepdims=True)
    yc = y - mu
    var = jnp.mean(yc * yc, axis=-1, keepdims=True)
    return yc * lax.rsqrt(var + LN_EPS) * g + b


def _gelu(x):
    return 0.5 * x * (1.0 + lax.erf(x * (2.0 ** -0.5)))


def _pack_halves(y):
    lo = pltpu.bitcast(y[:, :HALF].astype(BF16).astype(F32), U32) >> 16
    hi = pltpu.bitcast(y[:, HALF:].astype(BF16).astype(F32), U32) & jnp.uint32(0xFFFF0000)
    return lo | hi


def _unpack_halves(p):
    lo = pltpu.bitcast(p << 16, F32)
    hi = pltpu.bitcast(p & jnp.uint32(0xFFFF0000), F32)
    return lo, hi


def _ada_kernel(c_ref, w_ref, b_ref, o_ref):
    c = c_ref[...]
    sc = c * jax.nn.sigmoid(c)
    o_ref[...] = jnp.dot(sc, w_ref[...], precision=lax.Precision.HIGHEST,
                         preferred_element_type=F32) + b_ref[...]


def _ada(c, w_ada, b_ada):
    L, D, N = w_ada.shape
    B = c.shape[0]
    tn = 1536
    return pl.pallas_call(
        _ada_kernel,
        grid=(L, N // tn),
        in_specs=[
            pl.BlockSpec((B, D), lambda l, j: (0, 0)),
            pl.BlockSpec((None, D, tn), lambda l, j: (l, 0, j)),
            pl.BlockSpec((None, 1, tn), lambda l, j: (l, 0, j)),
        ],
        out_specs=pl.BlockSpec((None, B, tn), lambda l, j: (l, 0, j)),
        out_shape=jax.ShapeDtypeStruct((L, B, N), F32),
        compiler_params=_cparams(("arbitrary", "arbitrary")),
        name="ada",
    )(c, w_ada, b_ada.reshape(L, 1, N))


def _inproj_kernel(x_ref, mod_ref, w_ref, cs_ref, sa_ref, sb_ref, lng_ref, lnb_ref,
                   qT_ref, k_ref, vT_ref, zu_ref, zvn_ref):
    D = D_MODEL
    x = x_ref[...]
    sh = mod_ref[:, 0:D]
    sc = mod_ref[:, D:2 * D]
    h = (x * (1.0 + sc) + sh).astype(BF16)
    cs = cs_ref[...]
    sa = sa_ref[...]
    sb = sb_ref[...]

    def rope(t):
        return t * cs + pltpu.roll(t, 96, 1) * sa + pltpu.roll(t, 32, 1) * sb

    q = jnp.dot(h, w_ref[:, 0:QK_COLS], preferred_element_type=F32)
    for j in range(N_HEADS):
        sl = slice(j * LANES, (j + 1) * LANES)
        qT_ref[sl, :] = (rope(q[:, sl]) * Q_SCALE).T.astype(BF16)
    k = jnp.dot(h, w_ref[:, QK_COLS:2 * QK_COLS], preferred_element_type=F32)
    for j in range(N_HEADS):
        sl = slice(j * LANES, (j + 1) * LANES)
        k_ref[:, sl] = rope(k[:, sl]).astype(BF16)
    c0 = 2 * QK_COLS
    v = jnp.dot(h, w_ref[:, c0:c0 + DIFF_WIDTH], preferred_element_type=F32)
    for j in range(N_HEADS):
        vT_ref[j * V_ROWS:j * V_ROWS + V_DIM, :] = v[:, j * V_DIM:(j + 1) * V_DIM].T.astype(BF16)
        vT_ref[j * V_ROWS + V_DIM:(j + 1) * V_ROWS, :] = jnp.ones((V_ROWS - V_DIM, v.shape[0]), BF16)
    c0 += DIFF_WIDTH
    u = jnp.dot(h, w_ref[:, c0:c0 + SG_WIDTH], preferred_element_type=F32)
    zu_ref[...] = _gelu(u).astype(BF16)
    c0 += SG_WIDTH
    z = _gelu(jnp.dot(h, w_ref[:, c0:c0 + SG_WIDTH], preferred_element_type=F32))
    for g in range(N_SG):
        sl = slice(g * SG_DIM, (g + 1) * SG_DIM)
        zvn_ref[:, sl] = _layer_norm_rows(z[:, sl], lng_ref[:, sl], lnb_ref[:, sl]).astype(BF16)


def _inproj(x, mod_l, w_in_b, rope_tabs, ln_g, ln_b):
    B, S, D = x.shape
    tm = min(ROW_TILE, S)
    nt = S // tm
    cs, sa, sb = rope_tabs
    row = lambda b, i: (b, i, 0)
    tab = pl.BlockSpec((tm, LANES), lambda b, i: (i, 0))
    return pl.pallas_call(
        _inproj_kernel,
        grid=(B, nt),
        in_specs=[
            pl.BlockSpec((None, tm, D), row),
            pl.BlockSpec((None, 1, 6 * D), lambda b, i: (b, 0, 0)),
            pl.BlockSpec(w_in_b.shape, lambda b, i: (0, 0)),
            tab, tab, tab,
            pl.BlockSpec((1, SG_WIDTH), lambda b, i: (0, 0)),
            pl.BlockSpec((1, SG_WIDTH), lambda b, i: (0, 0)),
        ],
        out_specs=[
            pl.BlockSpec((None, QK_COLS, tm), lambda b, i: (b, 0, i)),
            pl.BlockSpec((None, tm, QK_COLS), row),
            pl.BlockSpec((None, None, N_HEADS * V_ROWS, tm), lambda b, i: (b, i, 0, 0)),
            pl.BlockSpec((None, tm, SG_WIDTH), row),
            pl.BlockSpec((None, tm, SG_WIDTH), row),
        ],
        out_shape=[
            jax.ShapeDtypeStruct((B, QK_COLS, S), BF16),
            jax.ShapeDtypeStruct((B, S, QK_COLS), BF16),
            jax.ShapeDtypeStruct((B, nt, N_HEADS * V_ROWS, tm), BF16),
            jax.ShapeDtypeStruct((B, S, SG_WIDTH), BF16),
            jax.ShapeDtypeStruct((B, S, SG_WIDTH), BF16),
        ],
        compiler_params=_cparams(("arbitrary", "arbitrary")),
        name="inproj",
    )(x, mod_l, w_in_b, cs, sa, sb, ln_g, ln_b)


def _attn_kernel(lq1_ref, lk1_ref, lq2_ref, lk2_ref, g_ref, qT_ref, k_ref, vT_ref, o_ref,
                 q_scr, s_scr, m_scr, acc_scr, *, lam_init):
    tq = qT_ref.shape[1]
    tk = vT_ref.shape[2]
    i = pl.program_id(2)
    qT = qT_ref[...]
    rows = lax.broadcasted_iota(I32, qT.shape, 0)
    zero = jnp.zeros_like(qT)
    q_scr[0] = jnp.where(rows < QK_DIM, qT, zero)
    q_scr[1] = jnp.where(rows >= QK_DIM, qT, zero)

    def scores(j, slot, c0):
        kb = k_ref[pl.ds(pl.multiple_of(j * tk, tk), tk), :]
        for mp in range(2):
            s_scr[slot, mp, :, c0:] = jnp.dot(kb, q_scr[mp, :, c0:], preferred_element_type=F32)

    def block_scores(slot, mp, c0, c1, masked):
        s = s_scr[slot, mp, :, c0:c1]
        if masked:
            kpos = lax.broadcasted_iota(I32, s.shape, 0)
            qpos = lax.broadcasted_iota(I32, s.shape, 1)
            s = jnp.where(kpos <= qpos, s, -jnp.inf)
        return s

    def pv_unshifted(j, slot, c0, c1, masked):
        vb = vT_ref[j]
        for mp in range(2):
            p = jnp.exp2(block_scores(slot, mp, c0, c1, masked)).astype(BF16)
            acc_scr[mp, :, c0:c1] += jnp.dot(vb, p, preferred_element_type=F32)

    def pv_online(j, slot, c0, c1, masked):
        vb = vT_ref[j]
        for mp in range(2):
            s = block_scores(slot, mp, c0, c1, masked)
            m_old = m_scr[mp, :, c0:c1]
            m_new = jnp.maximum(m_old, jnp.max(s, axis=0, keepdims=True))
            alpha = jnp.exp2(m_old - m_new)
            p = jnp.exp2(s - m_new).astype(BF16)
            acc_scr[mp, :, c0:c1] = alpha * acc_scr[mp, :, c0:c1] + jnp.dot(vb, p, preferred_element_type=F32)
            m_scr[mp, :, c0:c1] = m_new

    def sweep(block):
        acc_scr[...] = jnp.zeros(acc_scr.shape, F32)
        scores(0, 0, 0)

        def body(u, carry):
            t = 2 * u + 1
            scores(t, 1, 0)
            block(t - 1, 0, 0, tq, False)
            scores(t + 1, 0, 0)
            block(t, 1, 0, tq, False)
            return carry

        lax.fori_loop(0, i, body, 0)
        scores(2 * i + 1, 1, tk)
        block(2 * i, 0, 0, tk, True)
        block(2 * i, 0, tk, tq, False)
        block(2 * i + 1, 1, tk, tq, True)

    sweep(pv_unshifted)
    den = jnp.concatenate([acc_scr[0, V_DIM:V_DIM + 1, :], acc_scr[1, V_DIM:V_DIM + 1, :]], axis=0)
    in_range = (den >= DEN_MIN) & (den <= DEN_MAX)
    n_bad = jnp.sum(jnp.where(in_range, 0.0, 1.0))

    @pl.when(n_bad > 0.0)
    def _():
        m_scr[...] = jnp.full(m_scr.shape, -jnp.inf, F32)
        sweep(pv_online)

    lam = (jnp.exp(jnp.sum(lq1_ref[...] * lk1_ref[...], axis=1, keepdims=True))
           - jnp.exp(jnp.sum(lq2_ref[...] * lk2_ref[...], axis=1, keepdims=True)) + lam_init)
    a1 = acc_scr[0]
    a2 = acc_scr[1]
    out = a1[0:V_DIM] / a1[V_DIM:V_DIM + 1] - lam * (a2[0:V_DIM] / a2[V_DIM:V_DIM + 1])
    ms = jnp.mean(out * out, axis=0, keepdims=True)
    y = out * lax.rsqrt(ms + LN_EPS) * g_ref[...] * (1.0 - lam_init)
    o_ref[...] = y.T.astype(BF16)


def _attention(qT, k, vTb, lams, g_col, lam_init):
    B, _, S = qT.shape
    nkv, tk = vTb.shape[1], vTb.shape[3]
    tq = 2 * tk
    small = pl.BlockSpec((1, QK_DIM), lambda b, h, i: (0, 0))
    return pl.pallas_call(
        functools.partial(_attn_kernel, lam_init=lam_init),
        grid=(B, N_HEADS, S // tq),
        in_specs=[
            small, small, small, small,
            pl.BlockSpec((V_DIM, 1), lambda b, h, i: (0, 0)),
            pl.BlockSpec((None, HEAD_COLS, tq), lambda b, h, i: (b, h, i)),
            pl.BlockSpec((None, S, HEAD_COLS), lambda b, h, i: (b, 0, h)),
            pl.BlockSpec((None, nkv, V_ROWS, tk), lambda b, h, i: (b, 0, h, 0)),
        ],
        out_specs=pl.BlockSpec((None, tq, V_DIM), lambda b, h, i: (b, i, h)),
        out_shape=jax.ShapeDtypeStruct((B, S, DIFF_WIDTH), BF16),
        scratch_shapes=[
            pltpu.VMEM((2, HEAD_COLS, tq), BF16),
            pltpu.VMEM((2, 2, tk, tq), F32),
            pltpu.VMEM((2, 1, tq), F32),
            pltpu.VMEM((2, V_ROWS, tq), F32),
        ],
        compiler_params=_cparams(("arbitrary", "arbitrary", "arbitrary")),
        name="diff_attn",
    )(*lams, g_col, qT, k, vTb)


def _mix_kernel(x_ref, mod_ref, att_ref, zu_ref, zvn_ref, wsp_ref, bsp_ref, wo_ref, g_ref, b_ref,
                o_ref, sg_scr, *, alpha):
    D = D_MODEL
    nc = zu_ref.shape[0]
    r = lax.broadcasted_iota(I32, (CHUNK, CHUNK), 0)
    c = lax.broadcasted_iota(I32, (CHUNK, CHUNK), 1)
    causal = r >= c
    for g in range(N_SG):
        sl = slice(g * SG_DIM, (g + 1) * SG_DIM)
        w = jnp.where(causal, wsp_ref[g], 0.0).astype(BF16)
        z = jnp.concatenate([zvn_ref[n, :, sl] for n in range(nc)], axis=1)
        mixed = jnp.dot(w, z, preferred_element_type=F32)
        for n in range(nc):
            gate = mixed[:, n * SG_DIM:(n + 1) * SG_DIM] + bsp_ref[g]
            sg_scr[n * CHUNK:(n + 1) * CHUNK, sl] = (zu_ref[n, :, sl].astype(F32) * gate).astype(BF16)
    mix = (jnp.dot(att_ref[...], wo_ref[0:DIFF_WIDTH, :], preferred_element_type=F32)
           + jnp.dot(sg_scr[...], wo_ref[DIFF_WIDTH:, :], preferred_element_type=F32))
    gt = mod_ref[:, 2 * D:3 * D]
    y = alpha * x_ref[...] + (1.0 + gt) * mix
    o_ref[...] = _layer_norm_rows(y, g_ref[...], b_ref[...])


def _mix(x, mod_l, att, zu, zvn, w_sp, b_sp_full, w_out_b, ln_g, ln_b, alpha):
    B, S, D = x.shape
    tm = min(ROW_TILE, S)
    nc = tm // CHUNK
    row = lambda b, i: (b, i, 0)
    zu4 = zu.reshape(B, S // CHUNK, CHUNK, SG_WIDTH)
    zvn4 = zvn.reshape(B, S // CHUNK, CHUNK, SG_WIDTH)
    chunked = pl.BlockSpec((None, nc, CHUNK, SG_WIDTH), lambda b, i: (b, i, 0, 0))
    full2 = lambda a: pl.BlockSpec(a.shape, lambda b, i: (0,) * a.ndim)
    return pl.pallas_call(
        functools.partial(_mix_kernel, alpha=alpha),
        grid=(B, S // tm),
        in_specs=[
            pl.BlockSpec((None, tm, D), row),
            pl.BlockSpec((None, 1, 6 * D), lambda b, i: (b, 0, 0)),
            pl.BlockSpec((None, tm, DIFF_WIDTH), row),
            chunked, chunked,
            full2(w_sp), full2(b_sp_full), full2(w_out_b), full2(ln_g), full2(ln_b),
        ],
        out_specs=pl.BlockSpec((None, tm, D), row),
        out_shape=jax.ShapeDtypeStruct((B, S, D), F32),
        scratch_shapes=[pltpu.VMEM((tm, SG_WIDTH), BF16)],
        compiler_params=_cparams(("arbitrary", "arbitrary")),
        name="sgate_outproj_ln",
    )(x, mod_l, att, zu4, zvn4, w_sp, b_sp_full, w_out_b, ln_g, ln_b)


def _route_kernel(x_ref, mod_ref, wr_hi_ref, wr_lo_ref, br_ref, hp_ref, rt_ref, cnt_ref, run_scr):
    D = D_MODEL
    tm = x_ref.shape[0]

    @pl.when((pl.program_id(0) == 0) & (pl.program_id(1) == 0))
    def _():
        run_scr[...] = jnp.zeros(run_scr.shape, F32)

    sh = mod_ref[:, 3 * D:4 * D]
    sc = mod_ref[:, 4 * D:5 * D]
    h = x_ref[...] * (1.0 + sc) + sh
    h_hi = h.astype(BF16)
    h_lo = (h - h_hi.astype(F32)).astype(BF16)
    logit = (jnp.dot(h_hi, wr_hi_ref[...], preferred_element_type=F32)
             + jnp.dot(h_lo, wr_hi_ref[...], preferred_element_type=F32)
             + jnp.dot(h_hi, wr_lo_ref[...], preferred_element_type=F32)) + br_ref[...]
    lane = lax.broadcasted_iota(I32, logit.shape, 1).astype(F32)
    neg = -jnp.inf
    big = float(LANES)

    def first_argmax(v):
        mx = jnp.max(v, axis=1, keepdims=True)
        idx = jnp.min(jnp.where(v == mx, lane, big), axis=1, keepdims=True)
        return mx, idx

    in_grp = lane < N_GROUPS
    gmax, gidx = first_argmax(jnp.where(in_grp, logit, neg))
    g_p = 1.0 / jnp.sum(jnp.where(in_grp, jnp.exp(logit - gmax), 0.0), axis=1, keepdims=True)
    lo_lane = ROUTE_LANE0 + EXP_PER_GROUP * gidx
    sel = jnp.where((lane >= lo_lane) & (lane < lo_lane + EXP_PER_GROUP), logit, neg)
    v1, i1 = first_argmax(sel)
    v2, i2 = first_argmax(jnp.where(lane == i1, neg, sel))
    t = jnp.exp(v2 - v1)
    w1 = g_p / (1.0 + t)
    w2 = g_p * t / (1.0 + t)

    first_lower = i1 < i2
    e_a = jnp.minimum(i1, i2) - lo_lane
    e_b = jnp.maximum(i1, i2) - lo_lane
    w_a = jnp.where(first_lower, w1, w2)
    w_b = jnp.where(first_lower, w2, w1)
    bucket = gidx * N_PAIRS + (e_a * (2 * EXP_PER_GROUP - 1 - e_a) * 0.5 + (e_b - e_a - 1.0))

    hp_ref[:, 0:HALF] = _pack_halves(h)
    wslab = jnp.where(lane == 0.0, w_a, jnp.where(lane == 1.0, w_b, 0.0))
    hp_ref[:, HALF:] = pltpu.bitcast(wslab, U32)

    hot = lane == bucket
    onehot = jnp.where(hot, 1.0, 0.0)
    r = lax.broadcasted_iota(I32, (tm, tm), 0)
    c = lax.broadcasted_iota(I32, (tm, tm), 1)
    strict = jnp.where(r > c, 1.0, 0.0).astype(BF16)
    before = jnp.dot(strict, onehot.astype(BF16), preferred_element_type=F32) + run_scr[...]
    rank = jnp.sum(jnp.where(hot, before, 0.0), axis=1, keepdims=True)
    run_scr[...] = run_scr[...] + jnp.sum(onehot, axis=0, keepdims=True)
    cnt_ref[...] = run_scr[...]

    slab = jnp.where(lane == 0.0, bucket, jnp.where(lane == 1.0, rank, 0.0))
    rt_ref[...] = slab.T[0:8, :]


def _route(x1, mod_l, wr_hi, wr_lo, br):
    B, S, D = x1.shape
    tm = min(ROW_TILE, S)
    row = lambda b, i: (b, i, 0)
    full2 = lambda a: pl.BlockSpec(a.shape, lambda b, i: (0,) * a.ndim)
    return pl.pallas_call(
        _route_kernel,
        grid=(B, S // tm),
        in_specs=[
            pl.BlockSpec((None, tm, D), row),
            pl.BlockSpec((None, 1, 6 * D), lambda b, i: (b, 0, 0)),
            full2(wr_hi), full2(wr_lo), full2(br),
        ],
        out_specs=[
            pl.BlockSpec((None, tm, ROW_WORDS), row),
            pl.BlockSpec((None, 8, tm), lambda b, i: (b, 0, i)),
            pl.BlockSpec((1, LANES), lambda b, i: (0, 0)),
        ],
        out_shape=[
            jax.ShapeDtypeStruct((B, S, ROW_WORDS), U32),
            jax.ShapeDtypeStruct((B, 8, S), F32),
            jax.ShapeDtypeStruct((1, LANES), F32),
        ],
        scratch_shapes=[pltpu.VMEM((1, LANES), F32)],
        compiler_params=_cparams(("arbitrary", "arbitrary")),
        name="moe_route",
    )(x1, mod_l, wr_hi, wr_lo, br)


def _dispatch_kernel(p_ref, hp_ref, xs_in_ref, xs_ref, sem):
    del xs_in_ref
    nb = p_ref.shape[2]

    def row_copy(t, p):
        return pltpu.make_async_copy(hp_ref.at[pl.ds(t, 1)], xs_ref.at[pl.ds(p, 1)], sem)

    def issue(t8, carry):
        for u in range(ROW_DMA_UNROLL):
            t = t8 * ROW_DMA_UNROLL + u
            row_copy(t, p_ref[0, 0, t]).start(priority=u % 2)
        return carry

    lax.fori_loop(0, nb // ROW_DMA_UNROLL, issue, 0)
    pltpu.make_async_copy(hp_ref, xs_ref.at[pl.ds(0, nb)], sem).wait()


def _dispatch(pos, hp, n_rows):
    T = hp.shape[0]
    nb = min(DISPATCH_BLOCK, T)
    blk = pl.BlockSpec((1, 1, nb), lambda i: (i, 0, 0), memory_space=pltpu.SMEM)
    xs0 = jnp.zeros((n_rows, ROW_WORDS), U32)
    return pl.pallas_call(
        _dispatch_kernel,
        grid=(T // nb,),
        in_specs=[blk, pl.BlockSpec((nb, ROW_WORDS), lambda i: (i, 0)), pl.BlockSpec(memory_space=pl.ANY)],
        out_specs=pl.BlockSpec(memory_space=pl.ANY),
        out_shape=jax.ShapeDtypeStruct((n_rows, ROW_WORDS), U32),
        scratch_shapes=[pltpu.SemaphoreType.DMA(())],
        input_output_aliases={2: 0},
        compiler_params=_cparams(("arbitrary",)),
        name="moe_dispatch",
    )(pos.reshape(T // nb, 1, nb), hp, xs0)


def _expert_kernel(ea_ref, eb_ref, nv_ref, xs_ref, wga_ref, wua_ref, wda_ref, wgb_ref, wub_ref, wdb_ref, ys_ref,
                   wga_scr, wua_scr, wda_scr, wgb_scr, wub_scr, wdb_scr):
    n = pl.program_id(0)
    valid = n < nv_ref[0]
    prev = jnp.maximum(n - 1, 0)
    new_a = (n == 0) | (ea_ref[n] != ea_ref[prev])
    new_b = (n == 0) | (eb_ref[n] != eb_ref[prev])

    @pl.when(valid & new_a)
    def _():
        wga_scr[...] = wga_ref[...].astype(BF16)
        wua_scr[...] = wua_ref[...].astype(BF16)
        wda_scr[...] = wda_ref[...].astype(BF16)

    @pl.when(valid & new_b)
    def _():
        wgb_scr[...] = wgb_ref[...].astype(BF16)
        wub_scr[...] = wub_ref[...].astype(BF16)
        wdb_scr[...] = wdb_ref[...].astype(BF16)

    @pl.when(valid)
    def _():
        lo, hi = _unpack_halves(xs_ref[:, 0:HALF])
        lo = lo.astype(BF16)
        hi = hi.astype(BF16)
        wts = pltpu.bitcast(xs_ref[:, HALF:], F32)

        def hidden(wg_scr, wu_scr, w):
            g = (jnp.dot(lo, wg_scr[0:HALF, :], preferred_element_type=F32)
                 + jnp.dot(hi, wg_scr[HALF:, :], preferred_element_type=F32))
            u = (jnp.dot(lo, wu_scr[0:HALF, :], preferred_element_type=F32)
                 + jnp.dot(hi, wu_scr[HALF:, :], preferred_element_type=F32))
            return (g * jax.nn.sigmoid(g) * u * w).astype(BF16)

        y = (jnp.dot(hidden(wga_scr, wua_scr, wts[:, 0:1]), wda_scr[...], preferred_element_type=F32)
             + jnp.dot(hidden(wgb_scr, wub_scr, wts[:, 1:2]), wdb_scr[...], preferred_element_type=F32))
        ys_ref[...] = _pack_halves(y)


def _experts(tile_ea, tile_eb, n_valid, xs, w_gate, w_up, w_down, layer):
    n_rows = xs.shape[0]
    tm = EXPERT_TILE
    nt = n_rows // tm
    rows = lambda n, ea, eb, nv: (jnp.minimum(n, nv[0] - 1), 0)
    sel_a = lambda n, ea, eb, nv: (layer, ea[n], 0, 0)
    sel_b = lambda n, ea, eb, nv: (layer, eb[n], 0, 0)
    up = lambda sel: pl.BlockSpec((None, None, D_MODEL, D_EXPERT), sel)
    down = lambda sel: pl.BlockSpec((None, None, D_EXPERT, D_MODEL), sel)
    return pl.pallas_call(
        _expert_kernel,
        grid_spec=pltpu.PrefetchScalarGridSpec(
            num_scalar_prefetch=3,
            grid=(nt,),
            in_specs=[pl.BlockSpec((tm, ROW_WORDS), rows),
                      up(sel_a), up(sel_a), down(sel_a), up(sel_b), up(sel_b), down(sel_b)],
            out_specs=pl.BlockSpec((tm, HALF), rows),
            scratch_shapes=[
                pltpu.VMEM((D_MODEL, D_EXPERT), BF16), pltpu.VMEM((D_MODEL, D_EXPERT), BF16),
                pltpu.VMEM((D_EXPERT, D_MODEL), BF16),
                pltpu.VMEM((D_MODEL, D_EXPERT), BF16), pltpu.VMEM((D_MODEL, D_EXPERT), BF16),
                pltpu.VMEM((D_EXPERT, D_MODEL), BF16),
            ],
        ),
        out_shape=jax.ShapeDtypeStruct((n_rows, HALF), U32),
        compiler_params=_cparams(("arbitrary",)),
        name="moe_experts",
    )(tile_ea, tile_eb, n_valid, xs, w_gate, w_up, w_down, w_gate, w_up, w_down)


def _combine_kernel(p_ref, pn_ref, ys_ref, x_ref, mod_ref, g_ref, b_ref, o_ref, buf, sems, *, alpha):
    D = D_MODEL
    tc = x_ref.shape[0]
    g = pl.program_id(0) * pl.num_programs(1) + pl.program_id(1)
    n_steps = pl.num_programs(0) * pl.num_programs(1)
    slot = g % 2

    def row_copy(p, s, t):
        return pltpu.make_async_copy(ys_ref.at[pl.ds(p, 1)], buf.at[s, pl.ds(t, 1)], sems.at[s])

    def gather(pa_ref, s):
        def issue(t8, carry):
            for u in range(ROW_DMA_UNROLL):
                t = t8 * ROW_DMA_UNROLL + u
                row_copy(pa_ref[0, 0, t], s, t).start(priority=u % 2)
            return carry

        lax.fori_loop(0, tc // ROW_DMA_UNROLL, issue, 0)

    @pl.when(g == 0)
    def _():
        gather(p_ref, 0)

    @pl.when(g + 1 < n_steps)
    def _():
        gather(pn_ref, 1 - slot)

    pltpu.make_async_copy(ys_ref.at[pl.ds(0, tc)], buf.at[slot], sems.at[slot]).wait()

    lo, hi = _unpack_halves(buf[slot])
    ffn = jnp.concatenate([lo, hi], axis=1)
    gt = mod_ref[:, 5 * D:6 * D]
    y = alpha * x_ref[...] + (1.0 + gt) * ffn
    o_ref[...] = _layer_norm_rows(y, g_ref[...], b_ref[...])


def _combine(pos, ys, x1, mod_l, ln_g, ln_b, alpha):
    B, S, D = x1.shape
    tc = min(COMBINE_TILE, S)
    nt = S // tc
    blk = pl.BlockSpec((1, 1, tc), lambda b, i: (b * nt + i, 0, 0), memory_space=pltpu.SMEM)
    nxt = pl.BlockSpec((1, 1, tc), lambda b, i: (jnp.minimum(b * nt + i + 1, B * nt - 1), 0, 0),
                       memory_space=pltpu.SMEM)
    row = lambda b, i: (b, i, 0)
    full2 = lambda a: pl.BlockSpec(a.shape, lambda b, i: (0,) * a.ndim)
    p = pos.reshape(B * nt, 1, tc)
    return pl.pallas_call(
        functools.partial(_combine_kernel, alpha=alpha),
        grid=(B, nt),
        in_specs=[
            blk, nxt,
            pl.BlockSpec(memory_space=pl.ANY),
            pl.BlockSpec((None, tc, D), row),
            pl.BlockSpec((None, 1, 6 * D), lambda b, i: (b, 0, 0)),
            full2(ln_g), full2(ln_b),
        ],
        out_specs=pl.BlockSpec((None, tc, D), row),
        out_shape=jax.ShapeDtypeStruct((B, S, D), F32),
        scratch_shapes=[pltpu.VMEM((2, tc, HALF), U32), pltpu.SemaphoreType.DMA((2,))],
        compiler_params=_cparams(("arbitrary", "arbitrary")),
        name="moe_combine_ln",
    )(p, p, ys, x1, mod_l, ln_g, ln_b)


def _rope_tables(S):
    inv = 1.0 / (ROPE_THETA ** (jnp.arange(0, QK_DIM, 2, dtype=F32) / QK_DIM))
    ang = jnp.arange(S, dtype=F32)[:, None] * inv[None, :]
    cos, sin = jnp.cos(ang), jnp.sin(ang)
    half = QK_DIM // 2
    first = (jnp.arange(LANES) % QK_DIM) < half
    cs = jnp.tile(cos, (1, LANES // half))
    sn = jnp.tile(sin, (1, LANES // half))
    sa = jnp.where(first[None, :], -sn, 0.0)
    sb = jnp.where(first[None, :], 0.0, sn)
    return cs, sa, sb


def _router_matrix(w_group, b_group, w_router, b_router):
    D = w_group.shape[0]
    w = jnp.zeros((D, LANES), F32)
    w = w.at[:, 0:N_GROUPS].set(w_group)
    wr = jnp.transpose(w_router, (1, 0, 2)).reshape(D, N_EXPERTS)
    w = w.at[:, ROUTE_LANE0:ROUTE_LANE0 + N_EXPERTS].set(wr)
    b = jnp.zeros((1, LANES), F32)
    b = b.at[0, 0:N_GROUPS].set(b_group)
    b = b.at[0, ROUTE_LANE0:ROUTE_LANE0 + N_EXPERTS].set(b_router.reshape(N_EXPERTS))
    hi = w.astype(BF16)
    lo = (w - hi.astype(F32)).astype(BF16)
    return hi, lo, b


def _routing_tables(rt, cnt, n_tiles):
    B, _, S = rt.shape
    tm = EXPERT_TILE
    bucket = rt[:, 0, :].astype(I32)
    rank = rt[:, 1, :].astype(I32)
    counts = cnt[0, 0:N_BUCKETS].astype(I32)
    tiles = (counts + tm - 1) // tm
    tile_end = jnp.cumsum(tiles)
    tile_start = tile_end - tiles
    onehot = bucket[..., None] == jnp.arange(N_BUCKETS, dtype=I32)
    pos = (jnp.sum(jnp.where(onehot, tile_start * tm, 0), axis=-1) + rank).reshape(B * S)
    tile_ids = jnp.arange(n_tiles, dtype=I32)
    tile_bucket = jnp.minimum(jnp.sum(tile_ids[:, None] >= tile_end[None, :], axis=1), N_BUCKETS - 1)
    pair_a = jnp.asarray([a for a in range(EXP_PER_GROUP) for _ in range(a + 1, EXP_PER_GROUP)], I32)
    pair_b = jnp.asarray([b for a in range(EXP_PER_GROUP) for b in range(a + 1, EXP_PER_GROUP)], I32)
    group = tile_bucket // N_PAIRS
    pair = tile_bucket % N_PAIRS
    tile_ea = (group * EXP_PER_GROUP + pair_a[pair]).astype(I32)
    tile_eb = (group * EXP_PER_GROUP + pair_b[pair]).astype(I32)
    n_valid = tile_end[-1:].astype(I32)
    return pos, tile_ea, tile_eb, n_valid


def kernel(x, c, w_ada, b_ada, w_in, lambda_q1, lambda_k1, lambda_q2, lambda_k2, subln_g, sg_ln_g, sg_ln_b, w_spatial, b_spatial, w_out, ln1_g, ln1_b, w_group, b_group, w_router, b_router, w_gate, w_up, w_down, ln2_g, ln2_b):
    B, S, D = x.shape
    depth = w_in.shape[0]
    T = B * S
    alpha = (2.0 * depth) ** 0.25
    n_tiles = T // EXPERT_TILE + N_BUCKETS
    n_rows = n_tiles * EXPERT_TILE

    mod = _ada(c, w_ada, b_ada)
    tabs = _rope_tables(S)
    for l in range(depth):
        mod_l = mod[l].reshape(B, 1, 6 * D)
        lam_init = 0.8 - 0.6 * math.exp(-0.3 * l)

        qT, k, vTb, zu, zvn = _inproj(x, mod_l, w_in[l].astype(BF16), tabs,
                                      sg_ln_g[l].reshape(1, SG_WIDTH), sg_ln_b[l].reshape(1, SG_WIDTH))
        lams = (lambda_q1[l].reshape(1, QK_DIM), lambda_k1[l].reshape(1, QK_DIM),
                lambda_q2[l].reshape(1, QK_DIM), lambda_k2[l].reshape(1, QK_DIM))
        att = _attention(qT, k, vTb, lams, subln_g[l].reshape(V_DIM, 1), lam_init)
        b_sp_full = jnp.broadcast_to(b_spatial[l][:, :, None], (N_SG, CHUNK, SG_DIM))
        x1 = _mix(x, mod_l, att, zu, zvn, w_spatial[l], b_sp_full, w_out[l].astype(BF16),
                  ln1_g[l].reshape(1, D), ln1_b[l].reshape(1, D), alpha)

        wr_hi, wr_lo, br = _router_matrix(w_group[l], b_group[l], w_router[l], b_router[l])
        hp, rt, cnt = _route(x1, mod_l, wr_hi, wr_lo, br)
        pos, tile_ea, tile_eb, n_valid = _routing_tables(rt, cnt, n_tiles)
        xs = _dispatch(pos, hp.reshape(T, ROW_WORDS), n_rows)
        ys = _experts(tile_ea, tile_eb, n_valid, xs, w_gate, w_up, w_down, l)
        x = _combine(pos, ys, x1, mod_l, ln2_g[l].reshape(1, D), ln2_b[l].reshape(1, D), alpha)
    return x
```

```python
import functools
import math

import jax
import jax.numpy as jnp
from jax import lax
from jax.experimental import pallas as pl
from jax.experimental.pallas import tpu as pltpu

F32 = jnp.float32
BF16 = jnp.bfloat16
U32 = jnp.uint32
I32 = jnp.int32

D_MODEL = 1024
N_HEADS = 4
QK_DIM = 64
V_DIM = 128
HEAD_COLS = 2 * QK_DIM
QK_COLS = N_HEADS * HEAD_COLS
DIFF_WIDTH = N_HEADS * V_DIM
N_SG = 4
SG_DIM = 128
SG_WIDTH = N_SG * SG_DIM
CHUNK = 128
N_GROUPS = 4
EXP_PER_GROUP = 8
N_EXPERTS = N_GROUPS * EXP_PER_GROUP
N_PAIRS = EXP_PER_GROUP * (EXP_PER_GROUP - 1) // 2
N_BUCKETS = N_GROUPS * N_PAIRS
D_EXPERT = 512
ROPE_THETA = 10000.0
LN_EPS = 1e-5
LANES = 128
HALF = D_MODEL // 2
ROW_WORDS = HALF + LANES

ROW_TILE = 512
V_ROWS = V_DIM + 16
EXPERT_TILE = 256
DISPATCH_BLOCK = 2048
COMBINE_TILE = 256
ROW_DMA_UNROLL = 8
ROUTE_LANE0 = 8
VMEM_LIMIT = 48 * 1024 * 1024

DEN_MIN = 2.0 ** -40
DEN_MAX = 2.0 ** 40
LOG2E = 1.4426950408889634
Q_SCALE = (QK_DIM ** -0.5) * LOG2E


def _cparams(sem):
    return pltpu.CompilerParams(dimension_semantics=sem, vmem_limit_bytes=VMEM_LIMIT)


def _layer_norm_rows(y, g, b):
    mu = jnp.mean(y, axis=-1, keepdims=True)
    yc = y - mu
    var = jnp.mean(yc * yc, axis=-1, keepdims=True)
    return yc * lax.rsqrt(var + LN_EPS) * g + b


def _gelu(x):
    return 0.5 * x * (1.0 + lax.erf(x * (2.0 ** -0.5)))


def _pack_halves(y):
    lo = pltpu.bitcast(y[:, :HALF].astype(BF16).astype(F32), U32) >> 16
    hi = pltpu.bitcast(y[:, HALF:].astype(BF16).astype(F32), U32) & jnp.uint32(0xFFFF0000)
    return lo | hi


def _unpack_halves(p):
    lo = pltpu.bitcast(p << 16, F32)
    hi = pltpu.bitcast(p & jnp.uint32(0xFFFF0000), F32)
    return lo, hi


def _ada_kernel(c_ref, w_ref, b_ref, o_ref):
    c = c_ref[...]
    sc = c * jax.nn.sigmoid(c)
    o_ref[...] = jnp.dot(sc, w_ref[...], precision=lax.Precision.HIGHEST,
                         preferred_element_type=F32) + b_ref[...]


def _ada(c, w_ada, b_ada):
    L, D, N = w_ada.shape
    B = c.shape[0]
    tn = 1536
    return pl.pallas_call(
        _ada_kernel,
        grid=(L, N // tn),
        in_specs=[
            pl.BlockSpec((B, D), lambda l, j: (0, 0)),
            pl.BlockSpec((None, D, tn), lambda l, j: (l, 0, j)),
            pl.BlockSpec((None, 1, tn), lambda l, j: (l, 0, j)),
        ],
        out_specs=pl.BlockSpec((None, B, tn), lambda l, j: (l, 0, j)),
        out_shape=jax.ShapeDtypeStruct((L, B, N), F32),
        compiler_params=_cparams(("arbitrary", "arbitrary")),
        name="ada",
    )(c, w_ada, b_ada.reshape(L, 1, N))


def _inproj_kernel(x_ref, mod_ref, w_ref, cs_ref, sa_ref, sb_ref, lng_ref, lnb_ref,
                   qT_ref, k_ref, vT_ref, zu_ref, zvn_ref):
    D = D_MODEL
    x = x_ref[...]
    sh = mod_ref[:, 0:D]
    sc = mod_ref[:, D:2 * D]
    h = (x * (1.0 + sc) + sh).astype(BF16)
    cs = cs_ref[...]
    sa = sa_ref[...]
    sb = sb_ref[...]

    def rope(t):
        return t * cs + pltpu.roll(t, 96, 1) * sa + pltpu.roll(t, 32, 1) * sb

    q = jnp.dot(h, w_ref[:, 0:QK_COLS], preferred_element_type=F32)
    for j in range(N_HEADS):
        sl = slice(j * LANES, (j + 1) * LANES)
        qT_ref[sl, :] = (rope(q[:, sl]) * Q_SCALE).T.astype(BF16)
    k = jnp.dot(h, w_ref[:, QK_COLS:2 * QK_COLS], preferred_element_type=F32)
    for j in range(N_HEADS):
        sl = slice(j * LANES, (j + 1) * LANES)
        k_ref[:, sl] = rope(k[:, sl]).astype(BF16)
    c0 = 2 * QK_COLS
    v = jnp.dot(h, w_ref[:, c0:c0 + DIFF_WIDTH], preferred_element_type=F32)
    for j in range(N_HEADS):
        vT_ref[j * V_ROWS:j * V_ROWS + V_DIM, :] = v[:, j * V_DIM:(j + 1) * V_DIM].T.astype(BF16)
        vT_ref[j * V_ROWS + V_DIM:(j + 1) * V_ROWS, :] = jnp.ones((V_ROWS - V_DIM, v.shape[0]), BF16)
    c0 += DIFF_WIDTH
    u = jnp.dot(h, w_ref[:, c0:c0 + SG_WIDTH], preferred_element_type=F32)
    zu_ref[...] = _gelu(u).astype(BF16)
    c0 += SG_WIDTH
    z = _gelu(jnp.dot(h, w_ref[:, c0:c0 + SG_WIDTH], preferred_element_type=F32))
    for g in range(N_SG):
        sl = slice(g * SG_DIM, (g + 1) * SG_DIM)
        zvn_ref[:, sl] = _layer_norm_rows(z[:, sl], lng_ref[:, sl], lnb_ref[:, sl]).astype(BF16)


def _inproj(x, mod_l, w_in_b, rope_tabs, ln_g, ln_b):
    B, S, D = x.shape
    tm = min(ROW_TILE, S)
    nt = S // tm
    cs, sa, sb = rope_tabs
    row = lambda b, i: (b, i, 0)
    tab = pl.BlockSpec((tm, LANES), lambda b, i: (i, 0))
    return pl.pallas_call(
        _inproj_kernel,
        grid=(B, nt),
        in_specs=[
            pl.BlockSpec((None, tm, D), row),
            pl.BlockSpec((None, 1, 6 * D), lambda b, i: (b, 0, 0)),
            pl.BlockSpec(w_in_b.shape, lambda b, i: (0, 0)),
            tab, tab, tab,
            pl.BlockSpec((1, SG_WIDTH), lambda b, i: (0, 0)),
            pl.BlockSpec((1, SG_WIDTH), lambda b, i: (0, 0)),
        ],
        out_specs=[
            pl.BlockSpec((None, QK_COLS, tm), lambda b, i: (b, 0, i)),
            pl.BlockSpec((None, tm, QK_COLS), row),
            pl.BlockSpec((None, None, N_HEADS * V_ROWS, tm), lambda b, i: (b, i, 0, 0)),
            pl.BlockSpec((None, tm, SG_WIDTH), row),
            pl.BlockSpec((None, tm, SG_WIDTH), row),
        ],
        out_shape=[
            jax.ShapeDtypeStruct((B, QK_COLS, S), BF16),
            jax.ShapeDtypeStruct((B, S, QK_COLS), BF16),
            jax.ShapeDtypeStruct((B, nt, N_HEADS * V_ROWS, tm), BF16),
            jax.ShapeDtypeStruct((B, S, SG_WIDTH), BF16),
            jax.ShapeDtypeStruct((B, S, SG_WIDTH), BF16),
        ],
        compiler_params=_cparams(("arbitrary", "arbitrary")),
        name="inproj",
    )(x, mod_l, w_in_b, cs, sa, sb, ln_g, ln_b)


def _attn_kernel(lq1_ref, lk1_ref, lq2_ref, lk2_ref, g_ref, qT_ref, k_ref, vT_ref, o_ref,
                 q_scr, s_scr, m_scr, acc_scr, *, lam_init):
    tq = qT_ref.shape[1]
    tk = vT_ref.shape[2]
    i = pl.program_id(2)
    qT = qT_ref[...]
    rows = lax.broadcasted_iota(I32, qT.shape, 0)
    zero = jnp.zeros_like(qT)
    q_scr[0] = jnp.where(rows < QK_DIM, qT, zero)
    q_scr[1] = jnp.where(rows >= QK_DIM, qT, zero)

    def scores(j, slot, c0):
        kb = k_ref[pl.ds(pl.multiple_of(j * tk, tk), tk), :]
        for mp in range(2):
            s_scr[slot, mp, :, c0:] = jnp.dot(kb, q_scr[mp, :, c0:], preferred_element_type=F32)

    def block_scores(slot, mp, c0, c1, masked):
        s = s_scr[slot, mp, :, c0:c1]
        if masked:
            kpos = lax.broadcasted_iota(I32, s.shape, 0)
            qpos = lax.broadcasted_iota(I32, s.shape, 1)
            s = jnp.where(kpos <= qpos, s, -jnp.inf)
        return s

    def pv_unshifted(j, slot, c0, c1, masked):
        vb = vT_ref[j]
        for mp in range(2):
            p = jnp.exp2(block_scores(slot, mp, c0, c1, masked)).astype(BF16)
            acc_scr[mp, :, c0:c1] += jnp.dot(vb, p, preferred_element_type=F32)

    def pv_online(j, slot, c0, c1, masked):
        vb = vT_ref[j]
        for mp in range(2):
            s = block_scores(slot, mp, c0, c1, masked)
            m_old = m_scr[mp, :, c0:c1]
            m_new = jnp.maximum(m_old, jnp.max(s, axis=0, keepdims=True))
            alpha = jnp.exp2(m_old - m_new)
            p = jnp.exp2(s - m_new).astype(BF16)
            acc_scr[mp, :, c0:c1] = alpha * acc_scr[mp, :, c0:c1] + jnp.dot(vb, p, preferred_element_type=F32)
            m_scr[mp, :, c0:c1] = m_new

    def sweep(block):
        acc_scr[...] = jnp.zeros(acc_scr.shape, F32)
        scores(0, 0, 0)

        def body(u, carry):
            t = 2 * u + 1
            scores(t, 1, 0)
            block(t - 1, 0, 0, tq, False)
            scores(t + 1, 0, 0)
            block(t, 1, 0, tq, False)
            return carry

        lax.fori_loop(0, i, body, 0)
        scores(2 * i + 1, 1, tk)
        block(2 * i, 0, 0, tk, True)
        block(2 * i, 0, tk, tq, False)
        block(2 * i + 1, 1, tk, tq, True)

    sweep(pv_unshifted)
    den = jnp.concatenate([acc_scr[0, V_DIM:V_DIM + 1, :], acc_scr[1, V_DIM:V_DIM + 1, :]], axis=0)
    in_range = (den >= DEN_MIN) & (den <= DEN_MAX)
    n_bad = jnp.sum(jnp.where(in_range, 0.0, 1.0))

    @pl.when(n_bad > 0.0)
    def _():
        m_scr[...] = jnp.full(m_scr.shape, -jnp.inf, F32)
        sweep(pv_online)

    lam = (jnp.exp(jnp.sum(lq1_ref[...] * lk1_ref[...], axis=1, keepdims=True))
           - jnp.exp(jnp.sum(lq2_ref[...] * lk2_ref[...], axis=1, keepdims=True)) + lam_init)
    a1 = acc_scr[0]
    a2 = acc_scr[1]
    out = a1[0:V_DIM] / a1[V_DIM:V_DIM + 1] - lam * (a2[0:V_DIM] / a2[V_DIM:V_DIM + 1])
    ms = jnp.mean(out * out, axis=0, keepdims=True)
    y = out * lax.rsqrt(ms + LN_EPS) * g_ref[...] * (1.0 - lam_init)
    o_ref[...] = y.T.astype(BF16)


def _attention(qT, k, vTb, lams, g_col, lam_init):
    B, _, S = qT.shape
    nkv, tk = vTb.shape[1], vTb.shape[3]
    tq = 2 * tk
    small = pl.BlockSpec((1, QK_DIM), lambda b, h, i: (0, 0))
    return pl.pallas_call(
        functools.partial(_attn_kernel, lam_init=lam_init),
        grid=(B, N_HEADS, S // tq),
        in_specs=[
            small, small, small, small,
            pl.BlockSpec((V_DIM, 1), lambda b, h, i: (0, 0)),
            pl.BlockSpec((None, HEAD_COLS, tq), lambda b, h, i: (b, h, i)),
            pl.BlockSpec((None, S, HEAD_COLS), lambda b, h, i: (b, 0, h)),
            pl.BlockSpec((None, nkv, V_ROWS, tk), lambda b, h, i: (b, 0, h, 0)),
        ],
        out_specs=pl.BlockSpec((None, tq, V_DIM), lambda b, h, i: (b, i, h)),
        out_shape=jax.ShapeDtypeStruct((B, S, DIFF_WIDTH), BF16),
        scratch_shapes=[
            pltpu.VMEM((2, HEAD_COLS, tq), BF16),
            pltpu.VMEM((2, 2, tk, tq), F32),
            pltpu.VMEM((2, 1, tq), F32),
            pltpu.VMEM((2, V_ROWS, tq), F32),
        ],
        compiler_params=_cparams(("arbitrary", "arbitrary", "arbitrary")),
        name="diff_attn",
    )(*lams, g_col, qT, k, vTb)


def _mix_kernel(x_ref, mod_ref, att_ref, zu_ref, zvn_ref, wsp_ref, bsp_ref, wo_ref, g_ref, b_ref,
                o_ref, sg_scr, *, alpha):
    D = D_MODEL
    nc = zu_ref.shape[0]
    r = lax.broadcasted_iota(I32, (CHUNK, CHUNK), 0)
    c = lax.broadcasted_iota(I32, (CHUNK, CHUNK), 1)
    causal = r >= c
    for g in range(N_SG):
        sl = slice(g * SG_DIM, (g + 1) * SG_DIM)
        w = jnp.where(causal, wsp_ref[g], 0.0).astype(BF16)
        z = jnp.concatenate([zvn_ref[n, :, sl] for n in range(nc)], axis=1)
        mixed = jnp.dot(w, z, preferred_element_type=F32)
        for n in range(nc):
            gate = mixed[:, n * SG_DIM:(n + 1) * SG_DIM] + bsp_ref[g]
            sg_scr[n * CHUNK:(n + 1) * CHUNK, sl] = (zu_ref[n, :, sl].astype(F32) * gate).astype(BF16)
    mix = (jnp.dot(att_ref[...], wo_ref[0:DIFF_WIDTH, :], preferred_element_type=F32)
           + jnp.dot(sg_scr[...], wo_ref[DIFF_WIDTH:, :], preferred_element_type=F32))
    gt = mod_ref[:, 2 * D:3 * D]
    y = alpha * x_ref[...] + (1.0 + gt) * mix
    o_ref[...] = _layer_norm_rows(y, g_ref[...], b_ref[...])


def _mix(x, mod_l, att, zu, zvn, w_sp, b_sp_full, w_out_b, ln_g, ln_b, alpha):
    B, S, D = x.shape
    tm = min(ROW_TILE, S)
    nc = tm // CHUNK
    row = lambda b, i: (b, i, 0)
    zu4 = zu.reshape(B, S // CHUNK, CHUNK, SG_WIDTH)
    zvn4 = zvn.reshape(B, S // CHUNK, CHUNK, SG_WIDTH)
    chunked = pl.BlockSpec((None, nc, CHUNK, SG_WIDTH), lambda b, i: (b, i, 0, 0))
    full2 = lambda a: pl.BlockSpec(a.shape, lambda b, i: (0,) * a.ndim)
    return pl.pallas_call(
        functools.partial(_mix_kernel, alpha=alpha),
        grid=(B, S // tm),
        in_specs=[
            pl.BlockSpec((None, tm, D), row),
            pl.BlockSpec((None, 1, 6 * D), lambda b, i: (b, 0, 0)),
            pl.BlockSpec((None, tm, DIFF_WIDTH), row),
            chunked, chunked,
            full2(w_sp), full2(b_sp_full), full2(w_out_b), full2(ln_g), full2(ln_b),
        ],
        out_specs=pl.BlockSpec((None, tm, D), row),
        out_shape=jax.ShapeDtypeStruct((B, S, D), F32),
        scratch_shapes=[pltpu.VMEM((tm, SG_WIDTH), BF16)],
        compiler_params=_cparams(("arbitrary", "arbitrary")),
        name="sgate_outproj_ln",
    )(x, mod_l, att, zu4, zvn4, w_sp, b_sp_full, w_out_b, ln_g, ln_b)


def _route_kernel(x_ref, mod_ref, wr_hi_ref, wr_lo_ref, br_ref, hp_ref, rt_ref, cnt_ref, run_scr):
    D = D_MODEL
    tm = x_ref.shape[0]

    @pl.when((pl.program_id(0) == 0) & (pl.program_id(1) == 0))
    def _():
        run_scr[...] = jnp.zeros(run_scr.shape, F32)

    sh = mod_ref[:, 3 * D:4 * D]
    sc = mod_ref[:, 4 * D:5 * D]
    h = x_ref[...] * (1.0 + sc) + sh
    h_hi = h.astype(BF16)
    h_lo = (h - h_hi.astype(F32)).astype(BF16)
    logit = (jnp.dot(h_hi, wr_hi_ref[...], preferred_element_type=F32)
             + jnp.dot(h_lo, wr_hi_ref[...], preferred_element_type=F32)
             + jnp.dot(h_hi, wr_lo_ref[...], preferred_element_type=F32)) + br_ref[...]
    lane = lax.broadcasted_iota(I32, logit.shape, 1).astype(F32)
    neg = -jnp.inf
    big = float(LANES)

    def first_argmax(v):
        mx = jnp.max(v, axis=1, keepdims=True)
        idx = jnp.min(jnp.where(v == mx, lane, big), axis=1, keepdims=True)
        return mx, idx

    in_grp = lane < N_GROUPS
    gmax, gidx = first_argmax(jnp.where(in_grp, logit, neg))
    g_p = 1.0 / jnp.sum(jnp.where(in_grp, jnp.exp(logit - gmax), 0.0), axis=1, keepdims=True)
    lo_lane = ROUTE_LANE0 + EXP_PER_GROUP * gidx
    sel = jnp.where((lane >= lo_lane) & (lane < lo_lane + EXP_PER_GROUP), logit, neg)
    v1, i1 = first_argmax(sel)
    v2, i2 = first_argmax(jnp.where(lane == i1, neg, sel))
    t = jnp.exp(v2 - v1)
    w1 = g_p / (1.0 + t)
    w2 = g_p * t / (1.0 + t)

    first_lower = i1 < i2
    e_a = jnp.minimum(i1, i2) - lo_lane
    e_b = jnp.maximum(i1, i2) - lo_lane
    w_a = jnp.where(first_lower, w1, w2)
    w_b = jnp.where(first_lower, w2, w1)
    bucket = gidx * N_PAIRS + (e_a * (2 * EXP_PER_GROUP - 1 - e_a) * 0.5 + (e_b - e_a - 1.0))

    hp_ref[:, 0:HALF] = _pack_halves(h)
    wslab = jnp.where(lane == 0.0, w_a, jnp.where(lane == 1.0, w_b, 0.0))
    hp_ref[:, HALF:] = pltpu.bitcast(wslab, U32)

    hot = lane == bucket
    onehot = jnp.where(hot, 1.0, 0.0)
    r = lax.broadcasted_iota(I32, (tm, tm), 0)
    c = lax.broadcasted_iota(I32, (tm, tm), 1)
    strict = jnp.where(r > c, 1.0, 0.0).astype(BF16)
    before = jnp.dot(strict, onehot.astype(BF16), preferred_element_type=F32) + run_scr[...]
    rank = jnp.sum(jnp.where(hot, before, 0.0), axis=1, keepdims=True)
    run_scr[...] = run_scr[...] + jnp.sum(onehot, axis=0, keepdims=True)
    cnt_ref[...] = run_scr[...]

    slab = jnp.where(lane == 0.0, bucket, jnp.where(lane == 1.0, rank, 0.0))
    rt_ref[...] = slab.T[0:8, :]


def _route(x1, mod_l, wr_hi, wr_lo, br):
    B, S, D = x1.shape
    tm = min(ROW_TILE, S)
    row = lambda b, i: (b, i, 0)
    full2 = lambda a: pl.BlockSpec(a.shape, lambda b, i: (0,) * a.ndim)
    return pl.pallas_call(
        _route_kernel,
        grid=(B, S // tm),
        in_specs=[
            pl.BlockSpec((None, tm, D), row),
            pl.BlockSpec((None, 1, 6 * D), lambda b, i: (b, 0, 0)),
            full2(wr_hi), full2(wr_lo), full2(br),
        ],
        out_specs=[
            pl.BlockSpec((None, tm, ROW_WORDS), row),
            pl.BlockSpec((None, 8, tm), lambda b, i: (b, 0, i)),
            pl.BlockSpec((1, LANES), lambda b, i: (0, 0)),
        ],
        out_shape=[
            jax.ShapeDtypeStruct((B, S, ROW_WORDS), U32),
            jax.ShapeDtypeStruct((B, 8, S), F32),
            jax.ShapeDtypeStruct((1, LANES), F32),
        ],
        scratch_shapes=[pltpu.VMEM((1, LANES), F32)],
        compiler_params=_cparams(("arbitrary", "arbitrary")),
        name="moe_route",
    )(x1, mod_l, wr_hi, wr_lo, br)


def _dispatch_kernel(p_ref, hp_ref, xs_in_ref, xs_ref, sem):
    del xs_in_ref
    nb = p_ref.shape[2]

    def row_copy(t, p):
        return pltpu.make_async_copy(hp_ref.at[pl.ds(t, 1)], xs_ref.at[pl.ds(p, 1)], sem)

    def issue(t8, carry):
        for u in range(ROW_DMA_UNROLL):
            t = t8 * ROW_DMA_UNROLL + u
            row_copy(t, p_ref[0, 0, t]).start(priority=u % 2)
        return carry

    lax.fori_loop(0, nb // ROW_DMA_UNROLL, issue, 0)
    pltpu.make_async_copy(hp_ref, xs_ref.at[pl.ds(0, nb)], sem).wait()


def _dispatch(pos, hp, n_rows):
    T = hp.shape[0]
    nb = min(DISPATCH_BLOCK, T)
    blk = pl.BlockSpec((1, 1, nb), lambda i: (i, 0, 0), memory_space=pltpu.SMEM)
    xs0 = jnp.zeros((n_rows, ROW_WORDS), U32)
    return pl.pallas_call(
        _dispatch_kernel,
        grid=(T // nb,),
        in_specs=[blk, pl.BlockSpec((nb, ROW_WORDS), lambda i: (i, 0)), pl.BlockSpec(memory_space=pl.ANY)],
        out_specs=pl.BlockSpec(memory_space=pl.ANY),
        out_shape=jax.ShapeDtypeStruct((n_rows, ROW_WORDS), U32),
        scratch_shapes=[pltpu.SemaphoreType.DMA(())],
        input_output_aliases={2: 0},
        compiler_params=_cparams(("arbitrary",)),
        name="moe_dispatch",
    )(pos.reshape(T // nb, 1, nb), hp, xs0)


def _expert_kernel(ea_ref, eb_ref, nv_ref, xs_ref, wga_ref, wua_ref, wda_ref, wgb_ref, wub_ref, wdb_ref, ys_ref):
    del ea_ref, eb_ref

    @pl.when(pl.program_id(0) < nv_ref[0])
    def _():
        lo, hi = _unpack_halves(xs_ref[:, 0:HALF])
        lo = lo.astype(BF16)
        hi = hi.astype(BF16)
        wts = pltpu.bitcast(xs_ref[:, HALF:], F32)

        def hidden(wg_ref, wu_ref, w):
            g = (jnp.dot(lo, wg_ref[0:HALF, :], preferred_element_type=F32)
                 + jnp.dot(hi, wg_ref[HALF:, :], preferred_element_type=F32))
            u = (jnp.dot(lo, wu_ref[0:HALF, :], preferred_element_type=F32)
                 + jnp.dot(hi, wu_ref[HALF:, :], preferred_element_type=F32))
            return (g * jax.nn.sigmoid(g) * u * w).astype(BF16)

        y = (jnp.dot(hidden(wga_ref, wua_ref, wts[:, 0:1]), wda_ref[...], preferred_element_type=F32)
             + jnp.dot(hidden(wgb_ref, wub_ref, wts[:, 1:2]), wdb_ref[...], preferred_element_type=F32))
        ys_ref[...] = _pack_halves(y)


def _experts(tile_ea, tile_eb, n_valid, xs, w_gate, w_up, w_down, layer):
    n_rows = xs.shape[0]
    tm = EXPERT_TILE
    nt = n_rows // tm
    rows = lambda n, ea, eb, nv: (jnp.minimum(n, nv[0] - 1), 0)
    sel_a = lambda n, ea, eb, nv: (layer, ea[n], 0, 0)
    sel_b = lambda n, ea, eb, nv: (layer, eb[n], 0, 0)
    up = lambda sel: pl.BlockSpec((None, None, D_MODEL, D_EXPERT), sel)
    down = lambda sel: pl.BlockSpec((None, None, D_EXPERT, D_MODEL), sel)
    return pl.pallas_call(
        _expert_kernel,
        grid_spec=pltpu.PrefetchScalarGridSpec(
            num_scalar_prefetch=3,
            grid=(nt,),
            in_specs=[pl.BlockSpec((tm, ROW_WORDS), rows),
                      up(sel_a), up(sel_a), down(sel_a), up(sel_b), up(sel_b), down(sel_b)],
            out_specs=pl.BlockSpec((tm, HALF), rows),
        ),
        out_shape=jax.ShapeDtypeStruct((n_rows, HALF), U32),
        compiler_params=_cparams(("arbitrary",)),
        name="moe_experts",
    )(tile_ea, tile_eb, n_valid, xs, w_gate, w_up, w_down, w_gate, w_up, w_down)


def _combine_kernel(p_ref, pn_ref, ys_ref, x_ref, mod_ref, g_ref, b_ref, o_ref, buf, sems, *, alpha):
    D = D_MODEL
    tc = x_ref.shape[0]
    g = pl.program_id(0) * pl.num_programs(1) + pl.program_id(1)
    n_steps = pl.num_programs(0) * pl.num_programs(1)
    slot = g % 2

    def row_copy(p, s, t):
        return pltpu.make_async_copy(ys_ref.at[pl.ds(p, 1)], buf.at[s, pl.ds(t, 1)], sems.at[s])

    def gather(pa_ref, s):
        def issue(t8, carry):
            for u in range(ROW_DMA_UNROLL):
                t = t8 * ROW_DMA_UNROLL + u
                row_copy(pa_ref[0, 0, t], s, t).start(priority=u % 2)
            return carry

        lax.fori_loop(0, tc // ROW_DMA_UNROLL, issue, 0)

    @pl.when(g == 0)
    def _():
        gather(p_ref, 0)

    @pl.when(g + 1 < n_steps)
    def _():
        gather(pn_ref, 1 - slot)

    pltpu.make_async_copy(ys_ref.at[pl.ds(0, tc)], buf.at[slot], sems.at[slot]).wait()

    lo, hi = _unpack_halves(buf[slot])
    ffn = jnp.concatenate([lo, hi], axis=1)
    gt = mod_ref[:, 5 * D:6 * D]
    y = alpha * x_ref[...] + (1.0 + gt) * ffn
    o_ref[...] = _layer_norm_rows(y, g_ref[...], b_ref[...])


def _combine(pos, ys, x1, mod_l, ln_g, ln_b, alpha):
    B, S, D = x1.shape
    tc = min(COMBINE_TILE, S)
    nt = S // tc
    blk = pl.BlockSpec((1, 1, tc), lambda b, i: (b * nt + i, 0, 0), memory_space=pltpu.SMEM)
    nxt = pl.BlockSpec((1, 1, tc), lambda b, i: (jnp.minimum(b * nt + i + 1, B * nt - 1), 0, 0),
                       memory_space=pltpu.SMEM)
    row = lambda b, i: (b, i, 0)
    full2 = lambda a: pl.BlockSpec(a.shape, lambda b, i: (0,) * a.ndim)
    p = pos.reshape(B * nt, 1, tc)
    return pl.pallas_call(
        functools.partial(_combine_kernel, alpha=alpha),
        grid=(B, nt),
        in_specs=[
            blk, nxt,
            pl.BlockSpec(memory_space=pl.ANY),
            pl.BlockSpec((None, tc, D), row),
            pl.BlockSpec((None, 1, 6 * D), lambda b, i: (b, 0, 0)),
            full2(ln_g), full2(ln_b),
        ],
        out_specs=pl.BlockSpec((None, tc, D), row),
        out_shape=jax.ShapeDtypeStruct((B, S, D), F32),
        scratch_shapes=[pltpu.VMEM((2, tc, HALF), U32), pltpu.SemaphoreType.DMA((2,))],
        compiler_params=_cparams(("arbitrary", "arbitrary")),
        name="moe_combine_ln",
    )(p, p, ys, x1, mod_l, ln_g, ln_b)


def _rope_tables(S):
    inv = 1.0 / (ROPE_THETA ** (jnp.arange(0, QK_DIM, 2, dtype=F32) / QK_DIM))
    ang = jnp.arange(S, dtype=F32)[:, None] * inv[None, :]
    cos, sin = jnp.cos(ang), jnp.sin(ang)
    half = QK_DIM // 2
    first = (jnp.arange(LANES) % QK_DIM) < half
    cs = jnp.tile(cos, (1, LANES // half))
    sn = jnp.tile(sin, (1, LANES // half))
    sa = jnp.where(first[None, :], -sn, 0.0)
    sb = jnp.where(first[None, :], 0.0, sn)
    return cs, sa, sb


def _router_matrix(w_group, b_group, w_router, b_router):
    D = w_group.shape[0]
    w = jnp.zeros((D, LANES), F32)
    w = w.at[:, 0:N_GROUPS].set(w_group)
    wr = jnp.transpose(w_router, (1, 0, 2)).reshape(D, N_EXPERTS)
    w = w.at[:, ROUTE_LANE0:ROUTE_LANE0 + N_EXPERTS].set(wr)
    b = jnp.zeros((1, LANES), F32)
    b = b.at[0, 0:N_GROUPS].set(b_group)
    b = b.at[0, ROUTE_LANE0:ROUTE_LANE0 + N_EXPERTS].set(b_router.reshape(N_EXPERTS))
    hi = w.astype(BF16)
    lo = (w - hi.astype(F32)).astype(BF16)
    return hi, lo, b


def _routing_tables(rt, cnt, n_tiles):
    B, _, S = rt.shape
    tm = EXPERT_TILE
    bucket = rt[:, 0, :].astype(I32)
    rank = rt[:, 1, :].astype(I32)
    counts = cnt[0, 0:N_BUCKETS].astype(I32)
    tiles = (counts + tm - 1) // tm
    tile_end = jnp.cumsum(tiles)
    tile_start = tile_end - tiles
    onehot = bucket[..., None] == jnp.arange(N_BUCKETS, dtype=I32)
    pos = (jnp.sum(jnp.where(onehot, tile_start * tm, 0), axis=-1) + rank).reshape(B * S)
    tile_ids = jnp.arange(n_tiles, dtype=I32)
    tile_bucket = jnp.minimum(jnp.sum(tile_ids[:, None] >= tile_end[None, :], axis=1), N_BUCKETS - 1)
    pair_a = jnp.asarray([a for a in range(EXP_PER_GROUP) for _ in range(a + 1, EXP_PER_GROUP)], I32)
    pair_b = jnp.asarray([b for a in range(EXP_PER_GROUP) for b in range(a + 1, EXP_PER_GROUP)], I32)
    group = tile_bucket // N_PAIRS
    pair = tile_bucket % N_PAIRS
    tile_ea = (group * EXP_PER_GROUP + pair_a[pair]).astype(I32)
    tile_eb = (group * EXP_PER_GROUP + pair_b[pair]).astype(I32)
    n_valid = tile_end[-1:].astype(I32)
    return pos, tile_ea, tile_eb, n_valid


def kernel(x, c, w_ada, b_ada, w_in, lambda_q1, lambda_k1, lambda_q2, lambda_k2, subln_g, sg_ln_g, sg_ln_b, w_spatial, b_spatial, w_out, ln1_g, ln1_b, w_group, b_group, w_router, b_router, w_gate, w_up, w_down, ln2_g, ln2_b):
    B, S, D = x.shape
    depth = w_in.shape[0]
    T = B * S
    alpha = (2.0 * depth) ** 0.25
    n_tiles = T // EXPERT_TILE + N_BUCKETS
    n_rows = n_tiles * EXPERT_TILE

    mod = _ada(c, w_ada, b_ada)
    tabs = _rope_tables(S)
    w_gate, w_up, w_down = w_gate.astype(BF16), w_up.astype(BF16), w_down.astype(BF16)
    for l in range(depth):
        mod_l = mod[l].reshape(B, 1, 6 * D)
        lam_init = 0.8 - 0.6 * math.exp(-0.3 * l)

        qT, k, vTb, zu, zvn = _inproj(x, mod_l, w_in[l].astype(BF16), tabs,
                                      sg_ln_g[l].reshape(1, SG_WIDTH), sg_ln_b[l].reshape(1, SG_WIDTH))
        lams = (lambda_q1[l].reshape(1, QK_DIM), lambda_k1[l].reshape(1, QK_DIM),
                lambda_q2[l].reshape(1, QK_DIM), lambda_k2[l].reshape(1, QK_DIM))
        att = _attention(qT, k, vTb, lams, subln_g[l].reshape(V_DIM, 1), lam_init)
        b_sp_full = jnp.broadcast_to(b_spatial[l][:, :, None], (N_SG, CHUNK, SG_DIM))
        x1 = _mix(x, mod_l, att, zu, zvn, w_spatial[l], b_sp_full, w_out[l].astype(BF16),
                  ln1_g[l].reshape(1, D), ln1_b[l].reshape(1, D), alpha)

        wr_hi, wr_lo, br = _router_matrix(w_group[l], b_group[l], w_router[l], b_router[l])
        hp, rt, cnt = _route(x1, mod_l, wr_hi, wr_lo, br)
        pos, tile_ea, tile_eb, n_valid = _routing_tables(rt, cnt, n_tiles)
        xs = _dispatch(pos, hp.reshape(T, ROW_WORDS), n_rows)
        ys = _experts(tile_ea, tile_eb, n_valid, xs, w_gate, w_up, w_down, l)
        x = _combine(pos, ys, x1, mod_l, ln2_g[l].reshape(1, D), ln2_b[l].reshape(1, D), alpha)
    return x
```

```python
import functools
import math

import jax
import jax.numpy as jnp
from jax import lax
from jax.experimental import pallas as pl
from jax.experimental.pallas import tpu as pltpu

F32 = jnp.float32
BF16 = jnp.bfloat16
U32 = jnp.uint32
I32 = jnp.int32

D_MODEL = 1024
N_HEADS = 4
QK_DIM = 64
V_DIM = 128
HEAD_COLS = 2 * QK_DIM
QK_COLS = N_HEADS * HEAD_COLS
DIFF_WIDTH = N_HEADS * V_DIM
N_SG = 4
SG_DIM = 128
SG_WIDTH = N_SG * SG_DIM
CHUNK = 128
N_GROUPS = 4
EXP_PER_GROUP = 8
N_EXPERTS = N_GROUPS * EXP_PER_GROUP
N_PAIRS = EXP_PER_GROUP * (EXP_PER_GROUP - 1) // 2
N_BUCKETS = N_GROUPS * N_PAIRS
D_EXPERT = 512
ROPE_THETA = 10000.0
LN_EPS = 1e-5
LANES = 128
HALF = D_MODEL // 2
ROW_WORDS = HALF + LANES

ROW_TILE = 512
EXPERT_TILE = 256
DISPATCH_BLOCK = 2048
COMBINE_TILE = 512
ROW_DMA_UNROLL = 8
ROUTE_LANE0 = 8
VMEM_LIMIT = 48 * 1024 * 1024

DEN_MIN = 2.0 ** -40
DEN_MAX = 2.0 ** 40
LOG2E = 1.4426950408889634
Q_SCALE = (QK_DIM ** -0.5) * LOG2E


def _cparams(sem):
    return pltpu.CompilerParams(dimension_semantics=sem, vmem_limit_bytes=VMEM_LIMIT)


def _layer_norm_rows(y, g, b):
    mu = jnp.mean(y, axis=-1, keepdims=True)
    yc = y - mu
    var = jnp.mean(yc * yc, axis=-1, keepdims=True)
    return yc * lax.rsqrt(var + LN_EPS) * g + b


def _gelu(x):
    return 0.5 * x * (1.0 + lax.erf(x * (2.0 ** -0.5)))


def _pack_halves(y):
    lo = pltpu.bitcast(y[:, :HALF].astype(BF16).astype(F32), U32) >> 16
    hi = pltpu.bitcast(y[:, HALF:].astype(BF16).astype(F32), U32) & jnp.uint32(0xFFFF0000)
    return lo | hi


def _unpack_halves(p):
    lo = pltpu.bitcast(p << 16, F32)
    hi = pltpu.bitcast(p & jnp.uint32(0xFFFF0000), F32)
    return lo, hi


def _ada_kernel(c_ref, w_ref, b_ref, o_ref):
    c = c_ref[...]
    sc = c * jax.nn.sigmoid(c)
    o_ref[...] = jnp.dot(sc, w_ref[...], precision=lax.Precision.HIGHEST,
                         preferred_element_type=F32) + b_ref[...]


def _ada(c, w_ada, b_ada):
    L, D, N = w_ada.shape
    B = c.shape[0]
    tn = 1536
    return pl.pallas_call(
        _ada_kernel,
        grid=(L, N // tn),
        in_specs=[
            pl.BlockSpec((B, D), lambda l, j: (0, 0)),
            pl.BlockSpec((None, D, tn), lambda l, j: (l, 0, j)),
            pl.BlockSpec((None, 1, tn), lambda l, j: (l, 0, j)),
        ],
        out_specs=pl.BlockSpec((None, B, tn), lambda l, j: (l, 0, j)),
        out_shape=jax.ShapeDtypeStruct((L, B, N), F32),
        compiler_params=_cparams(("arbitrary", "arbitrary")),
        name="ada",
    )(c, w_ada, b_ada.reshape(L, 1, N))


def _inproj_kernel(x_ref, mod_ref, w_ref, cs_ref, sa_ref, sb_ref, lng_ref, lnb_ref,
                   qT_ref, k_ref, vT_ref, zu_ref, zvn_ref):
    D = D_MODEL
    x = x_ref[...]
    sh = mod_ref[:, 0:D]
    sc = mod_ref[:, D:2 * D]
    h = (x * (1.0 + sc) + sh).astype(BF16)
    cs = cs_ref[...]
    sa = sa_ref[...]
    sb = sb_ref[...]

    def rope(t):
        return t * cs + pltpu.roll(t, 96, 1) * sa + pltpu.roll(t, 32, 1) * sb

    q = jnp.dot(h, w_ref[:, 0:QK_COLS], preferred_element_type=F32)
    for j in range(N_HEADS):
        sl = slice(j * LANES, (j + 1) * LANES)
        qT_ref[sl, :] = (rope(q[:, sl]) * Q_SCALE).T.astype(BF16)
    k = jnp.dot(h, w_ref[:, QK_COLS:2 * QK_COLS], preferred_element_type=F32)
    for j in range(N_HEADS):
        sl = slice(j * LANES, (j + 1) * LANES)
        k_ref[:, sl] = rope(k[:, sl]).astype(BF16)
    c0 = 2 * QK_COLS
    v = jnp.dot(h, w_ref[:, c0:c0 + DIFF_WIDTH], preferred_element_type=F32)
    vT_ref[...] = v.T.astype(BF16)
    c0 += DIFF_WIDTH
    u = jnp.dot(h, w_ref[:, c0:c0 + SG_WIDTH], preferred_element_type=F32)
    zu_ref[...] = _gelu(u).astype(BF16)
    c0 += SG_WIDTH
    z = _gelu(jnp.dot(h, w_ref[:, c0:c0 + SG_WIDTH], preferred_element_type=F32))
    for g in range(N_SG):
        sl = slice(g * SG_DIM, (g + 1) * SG_DIM)
        zvn_ref[:, sl] = _layer_norm_rows(z[:, sl], lng_ref[:, sl], lnb_ref[:, sl]).astype(BF16)


def _inproj(x, mod_l, w_in_b, rope_tabs, ln_g, ln_b):
    B, S, D = x.shape
    tm = min(ROW_TILE, S)
    nt = S // tm
    cs, sa, sb = rope_tabs
    row = lambda b, i: (b, i, 0)
    tab = pl.BlockSpec((tm, LANES), lambda b, i: (i, 0))
    return pl.pallas_call(
        _inproj_kernel,
        grid=(B, nt),
        in_specs=[
            pl.BlockSpec((None, tm, D), row),
            pl.BlockSpec((None, 1, 6 * D), lambda b, i: (b, 0, 0)),
            pl.BlockSpec(w_in_b.shape, lambda b, i: (0, 0)),
            tab, tab, tab,
            pl.BlockSpec((1, SG_WIDTH), lambda b, i: (0, 0)),
            pl.BlockSpec((1, SG_WIDTH), lambda b, i: (0, 0)),
        ],
        out_specs=[
            pl.BlockSpec((None, QK_COLS, tm), lambda b, i: (b, 0, i)),
            pl.BlockSpec((None, tm, QK_COLS), row),
            pl.BlockSpec((None, None, DIFF_WIDTH, tm), lambda b, i: (b, i, 0, 0)),
            pl.BlockSpec((None, tm, SG_WIDTH), row),
            pl.BlockSpec((None, tm, SG_WIDTH), row),
        ],
        out_shape=[
            jax.ShapeDtypeStruct((B, QK_COLS, S), BF16),
            jax.ShapeDtypeStruct((B, S, QK_COLS), BF16),
            jax.ShapeDtypeStruct((B, nt, DIFF_WIDTH, tm), BF16),
            jax.ShapeDtypeStruct((B, S, SG_WIDTH), BF16),
            jax.ShapeDtypeStruct((B, S, SG_WIDTH), BF16),
        ],
        compiler_params=_cparams(("arbitrary", "arbitrary")),
        name="inproj",
    )(x, mod_l, w_in_b, cs, sa, sb, ln_g, ln_b)


def _attn_kernel(lq1_ref, lk1_ref, lq2_ref, lk2_ref, g_ref, qT_ref, k_ref, vT_ref, o_ref,
                 q_scr, s_scr, m_scr, l_scr, acc_scr, *, lam_init):
    tq = qT_ref.shape[1]
    tk = vT_ref.shape[2]
    i = pl.program_id(2)
    qT = qT_ref[...]
    rows = lax.broadcasted_iota(I32, qT.shape, 0)
    zero = jnp.zeros_like(qT)
    q_scr[0] = jnp.where(rows < QK_DIM, qT, zero)
    q_scr[1] = jnp.where(rows >= QK_DIM, qT, zero)

    def scores(j, slot, c0):
        kb = k_ref[pl.ds(pl.multiple_of(j * tk, tk), tk), :]
        for mp in range(2):
            s_scr[slot, mp, :, c0:] = jnp.dot(kb, q_scr[mp, :, c0:], preferred_element_type=F32)

    def block_scores(slot, mp, c0, c1, masked):
        s = s_scr[slot, mp, :, c0:c1]
        if masked:
            kpos = lax.broadcasted_iota(I32, s.shape, 0)
            qpos = lax.broadcasted_iota(I32, s.shape, 1)
            s = jnp.where(kpos <= qpos, s, -jnp.inf)
        return s

    def pv_unshifted(j, slot, c0, c1, masked):
        vb = vT_ref[j]
        for mp in range(2):
            p = jnp.exp2(block_scores(slot, mp, c0, c1, masked))
            l_scr[mp, :, c0:c1] += jnp.sum(p, axis=0, keepdims=True)
            acc_scr[mp, :, c0:c1] += jnp.dot(vb, p.astype(BF16), preferred_element_type=F32)

    def pv_online(j, slot, c0, c1, masked):
        vb = vT_ref[j]
        for mp in range(2):
            s = block_scores(slot, mp, c0, c1, masked)
            m_old = m_scr[mp, :, c0:c1]
            m_new = jnp.maximum(m_old, jnp.max(s, axis=0, keepdims=True))
            alpha = jnp.exp2(m_old - m_new)
            p = jnp.exp2(s - m_new)
            l_scr[mp, :, c0:c1] = alpha * l_scr[mp, :, c0:c1] + jnp.sum(p, axis=0, keepdims=True)
            acc_scr[mp, :, c0:c1] = (alpha * acc_scr[mp, :, c0:c1]
                                     + jnp.dot(vb, p.astype(BF16), preferred_element_type=F32))
            m_scr[mp, :, c0:c1] = m_new

    def sweep(block):
        acc_scr[...] = jnp.zeros(acc_scr.shape, F32)
        l_scr[...] = jnp.zeros(l_scr.shape, F32)
        scores(0, 0, 0)

        def body(u, carry):
            t = 2 * u + 1
            scores(t, 1, 0)
            block(t - 1, 0, 0, tq, False)
            scores(t + 1, 0, 0)
            block(t, 1, 0, tq, False)
            return carry

        lax.fori_loop(0, i, body, 0)
        scores(2 * i + 1, 1, tk)
        block(2 * i, 0, 0, tk, True)
        block(2 * i, 0, tk, tq, False)
        block(2 * i + 1, 1, tk, tq, True)

    sweep(pv_unshifted)
    den = jnp.concatenate([l_scr[0], l_scr[1]], axis=0)
    in_range = (den >= DEN_MIN) & (den <= DEN_MAX)
    n_bad = jnp.sum(jnp.where(in_range, 0.0, 1.0))

    @pl.when(n_bad > 0.0)
    def _():
        m_scr[...] = jnp.full(m_scr.shape, -jnp.inf, F32)
        sweep(pv_online)

    lam = (jnp.exp(jnp.sum(lq1_ref[...] * lk1_ref[...], axis=1, keepdims=True))
           - jnp.exp(jnp.sum(lq2_ref[...] * lk2_ref[...], axis=1, keepdims=True)) + lam_init)
    out = acc_scr[0] / l_scr[0] - lam * (acc_scr[1] / l_scr[1])
    ms = jnp.mean(out * out, axis=0, keepdims=True)
    y = out * lax.rsqrt(ms + LN_EPS) * g_ref[...] * (1.0 - lam_init)
    o_ref[...] = y.T.astype(BF16)


def _attention(qT, k, vTb, lams, g_col, lam_init):
    B, _, S = qT.shape
    nkv, tk = vTb.shape[1], vTb.shape[3]
    tq = 2 * tk
    small = pl.BlockSpec((1, QK_DIM), lambda b, h, i: (0, 0))
    return pl.pallas_call(
        functools.partial(_attn_kernel, lam_init=lam_init),
        grid=(B, N_HEADS, S // tq),
        in_specs=[
            small, small, small, small,
            pl.BlockSpec((V_DIM, 1), lambda b, h, i: (0, 0)),
            pl.BlockSpec((None, HEAD_COLS, tq), lambda b, h, i: (b, h, i)),
            pl.BlockSpec((None, S, HEAD_COLS), lambda b, h, i: (b, 0, h)),
            pl.BlockSpec((None, nkv, V_DIM, tk), lambda b, h, i: (b, 0, h, 0)),
        ],
        out_specs=pl.BlockSpec((None, tq, V_DIM), lambda b, h, i: (b, i, h)),
        out_shape=jax.ShapeDtypeStruct((B, S, DIFF_WIDTH), BF16),
        scratch_shapes=[
            pltpu.VMEM((2, HEAD_COLS, tq), BF16),
            pltpu.VMEM((2, 2, tk, tq), F32),
            pltpu.VMEM((2, 1, tq), F32),
            pltpu.VMEM((2, 1, tq), F32),
            pltpu.VMEM((2, V_DIM, tq), F32),
        ],
        compiler_params=_cparams(("arbitrary", "arbitrary", "arbitrary")),
        name="diff_attn",
    )(*lams, g_col, qT, k, vTb)


def _mix_kernel(x_ref, mod_ref, att_ref, zu_ref, zvn_ref, wsp_ref, bsp_ref, wo_ref, g_ref, b_ref,
                o_ref, sg_scr, *, alpha):
    D = D_MODEL
    nc = zu_ref.shape[0]
    r = lax.broadcasted_iota(I32, (CHUNK, CHUNK), 0)
    c = lax.broadcasted_iota(I32, (CHUNK, CHUNK), 1)
    causal = r >= c
    for g in range(N_SG):
        sl = slice(g * SG_DIM, (g + 1) * SG_DIM)
        w = jnp.where(causal, wsp_ref[g], 0.0).astype(BF16)
        z = jnp.concatenate([zvn_ref[n, :, sl] for n in range(nc)], axis=1)
        mixed = jnp.dot(w, z, preferred_element_type=F32)
        for n in range(nc):
            gate = mixed[:, n * SG_DIM:(n + 1) * SG_DIM] + bsp_ref[g]
            sg_scr[n * CHUNK:(n + 1) * CHUNK, sl] = (zu_ref[n, :, sl].astype(F32) * gate).astype(BF16)
    mix = (jnp.dot(att_ref[...], wo_ref[0:DIFF_WIDTH, :], preferred_element_type=F32)
           + jnp.dot(sg_scr[...], wo_ref[DIFF_WIDTH:, :], preferred_element_type=F32))
    gt = mod_ref[:, 2 * D:3 * D]
    y = alpha * x_ref[...] + (1.0 + gt) * mix
    o_ref[...] = _layer_norm_rows(y, g_ref[...], b_ref[...])


def _mix(x, mod_l, att, zu, zvn, w_sp, b_sp_full, w_out_b, ln_g, ln_b, alpha):
    B, S, D = x.shape
    tm = min(ROW_TILE, S)
    nc = tm // CHUNK
    row = lambda b, i: (b, i, 0)
    zu4 = zu.reshape(B, S // CHUNK, CHUNK, SG_WIDTH)
    zvn4 = zvn.reshape(B, S // CHUNK, CHUNK, SG_WIDTH)
    chunked = pl.BlockSpec((None, nc, CHUNK, SG_WIDTH), lambda b, i: (b, i, 0, 0))
    full2 = lambda a: pl.BlockSpec(a.shape, lambda b, i: (0,) * a.ndim)
    return pl.pallas_call(
        functools.partial(_mix_kernel, alpha=alpha),
        grid=(B, S // tm),
        in_specs=[
            pl.BlockSpec((None, tm, D), row),
            pl.BlockSpec((None, 1, 6 * D), lambda b, i: (b, 0, 0)),
            pl.BlockSpec((None, tm, DIFF_WIDTH), row),
            chunked, chunked,
            full2(w_sp), full2(b_sp_full), full2(w_out_b), full2(ln_g), full2(ln_b),
        ],
        out_specs=pl.BlockSpec((None, tm, D), row),
        out_shape=jax.ShapeDtypeStruct((B, S, D), F32),
        scratch_shapes=[pltpu.VMEM((tm, SG_WIDTH), BF16)],
        compiler_params=_cparams(("arbitrary", "arbitrary")),
        name="sgate_outproj_ln",
    )(x, mod_l, att, zu4, zvn4, w_sp, b_sp_full, w_out_b, ln_g, ln_b)


def _route_kernel(x_ref, mod_ref, wr_hi_ref, wr_lo_ref, br_ref, hp_ref, rt_ref, cnt_ref, run_scr):
    D = D_MODEL
    tm = x_ref.shape[0]

    @pl.when((pl.program_id(0) == 0) & (pl.program_id(1) == 0))
    def _():
        run_scr[...] = jnp.zeros(run_scr.shape, F32)

    sh = mod_ref[:, 3 * D:4 * D]
    sc = mod_ref[:, 4 * D:5 * D]
    h = x_ref[...] * (1.0 + sc) + sh
    h_hi = h.astype(BF16)
    h_lo = (h - h_hi.astype(F32)).astype(BF16)
    logit = (jnp.dot(h_hi, wr_hi_ref[...], preferred_element_type=F32)
             + jnp.dot(h_lo, wr_hi_ref[...], preferred_element_type=F32)
             + jnp.dot(h_hi, wr_lo_ref[...], preferred_element_type=F32)) + br_ref[...]
    lane = lax.broadcasted_iota(I32, logit.shape, 1).astype(F32)
    neg = -jnp.inf
    big = float(LANES)

    def first_argmax(v):
        mx = jnp.max(v, axis=1, keepdims=True)
        idx = jnp.min(jnp.where(v == mx, lane, big), axis=1, keepdims=True)
        return mx, idx

    in_grp = lane < N_GROUPS
    gmax, gidx = first_argmax(jnp.where(in_grp, logit, neg))
    g_p = 1.0 / jnp.sum(jnp.where(in_grp, jnp.exp(logit - gmax), 0.0), axis=1, keepdims=True)
    lo_lane = ROUTE_LANE0 + EXP_PER_GROUP * gidx
    sel = jnp.where((lane >= lo_lane) & (lane < lo_lane + EXP_PER_GROUP), logit, neg)
    v1, i1 = first_argmax(sel)
    v2, i2 = first_argmax(jnp.where(lane == i1, neg, sel))
    t = jnp.exp(v2 - v1)
    w1 = g_p / (1.0 + t)
    w2 = g_p * t / (1.0 + t)

    first_lower = i1 < i2
    e_a = jnp.minimum(i1, i2) - lo_lane
    e_b = jnp.maximum(i1, i2) - lo_lane
    w_a = jnp.where(first_lower, w1, w2)
    w_b = jnp.where(first_lower, w2, w1)
    bucket = gidx * N_PAIRS + (e_a * (2 * EXP_PER_GROUP - 1 - e_a) * 0.5 + (e_b - e_a - 1.0))

    hp_ref[:, 0:HALF] = _pack_halves(h)
    wslab = jnp.where(lane == 0.0, w_a, jnp.where(lane == 1.0, w_b, 0.0))
    hp_ref[:, HALF:] = pltpu.bitcast(wslab, U32)

    hot = lane == bucket
    onehot = jnp.where(hot, 1.0, 0.0)
    r = lax.broadcasted_iota(I32, (tm, tm), 0)
    c = lax.broadcasted_iota(I32, (tm, tm), 1)
    strict = jnp.where(r > c, 1.0, 0.0).astype(BF16)
    before = jnp.dot(strict, onehot.astype(BF16), preferred_element_type=F32) + run_scr[...]
    rank = jnp.sum(jnp.where(hot, before, 0.0), axis=1, keepdims=True)
    run_scr[...] = run_scr[...] + jnp.sum(onehot, axis=0, keepdims=True)
    cnt_ref[...] = run_scr[...]

    slab = jnp.where(lane == 0.0, bucket, jnp.where(lane == 1.0, rank, 0.0))
    rt_ref[...] = slab.T[0:8, :]


def _route(x1, mod_l, wr_hi, wr_lo, br):
    B, S, D = x1.shape
    tm = min(ROW_TILE, S)
    row = lambda b, i: (b, i, 0)
    full2 = lambda a: pl.BlockSpec(a.shape, lambda b, i: (0,) * a.ndim)
    return pl.pallas_call(
        _route_kernel,
        grid=(B, S // tm),
        in_specs=[
            pl.BlockSpec((None, tm, D), row),
            pl.BlockSpec((None, 1, 6 * D), lambda b, i: (b, 0, 0)),
            full2(wr_hi), full2(wr_lo), full2(br),
        ],
        out_specs=[
            pl.BlockSpec((None, tm, ROW_WORDS), row),
            pl.BlockSpec((None, 8, tm), lambda b, i: (b, 0, i)),
            pl.BlockSpec((1, LANES), lambda b, i: (0, 0)),
        ],
        out_shape=[
            jax.ShapeDtypeStruct((B, S, ROW_WORDS), U32),
            jax.ShapeDtypeStruct((B, 8, S), F32),
            jax.ShapeDtypeStruct((1, LANES), F32),
        ],
        scratch_shapes=[pltpu.VMEM((1, LANES), F32)],
        compiler_params=_cparams(("arbitrary", "arbitrary")),
        name="moe_route",
    )(x1, mod_l, wr_hi, wr_lo, br)


def _dispatch_kernel(p_ref, hp_ref, xs_in_ref, xs_ref, sem):
    del xs_in_ref
    nb = p_ref.shape[2]

    def row_copy(t, p):
        return pltpu.make_async_copy(hp_ref.at[pl.ds(t, 1)], xs_ref.at[pl.ds(p, 1)], sem)

    def issue(t8, carry):
        for u in range(ROW_DMA_UNROLL):
            t = t8 * ROW_DMA_UNROLL + u
            row_copy(t, p_ref[0, 0, t]).start(priority=u % 2)
        return carry

    lax.fori_loop(0, nb // ROW_DMA_UNROLL, issue, 0)
    pltpu.make_async_copy(hp_ref, xs_ref.at[pl.ds(0, nb)], sem).wait()


def _dispatch(pos, hp, n_rows):
    T = hp.shape[0]
    nb = min(DISPATCH_BLOCK, T)
    blk = pl.BlockSpec((1, 1, nb), lambda i: (i, 0, 0), memory_space=pltpu.SMEM)
    xs0 = jnp.zeros((n_rows, ROW_WORDS), U32)
    return pl.pallas_call(
        _dispatch_kernel,
        grid=(T // nb,),
        in_specs=[blk, pl.BlockSpec((nb, ROW_WORDS), lambda i: (i, 0)), pl.BlockSpec(memory_space=pl.ANY)],
        out_specs=pl.BlockSpec(memory_space=pl.ANY),
        out_shape=jax.ShapeDtypeStruct((n_rows, ROW_WORDS), U32),
        scratch_shapes=[pltpu.SemaphoreType.DMA(())],
        input_output_aliases={2: 0},
        compiler_params=_cparams(("arbitrary",)),
        name="moe_dispatch",
    )(pos.reshape(T // nb, 1, nb), hp, xs0)


def _expert_kernel(ea_ref, eb_ref, nv_ref, xs_ref, wga_ref, wua_ref, wda_ref, wgb_ref, wub_ref, wdb_ref, ys_ref):
    del ea_ref, eb_ref

    @pl.when(pl.program_id(0) < nv_ref[0])
    def _():
        lo, hi = _unpack_halves(xs_ref[:, 0:HALF])
        lo = lo.astype(BF16)
        hi = hi.astype(BF16)
        wts = pltpu.bitcast(xs_ref[:, HALF:], F32)

        def hidden(wg_ref, wu_ref, w):
            g = (jnp.dot(lo, wg_ref[0:HALF, :], preferred_element_type=F32)
                 + jnp.dot(hi, wg_ref[HALF:, :], preferred_element_type=F32))
            u = (jnp.dot(lo, wu_ref[0:HALF, :], preferred_element_type=F32)
                 + jnp.dot(hi, wu_ref[HALF:, :], preferred_element_type=F32))
            return (g * jax.nn.sigmoid(g) * u * w).astype(BF16)

        y = (jnp.dot(hidden(wga_ref, wua_ref, wts[:, 0:1]), wda_ref[...], preferred_element_type=F32)
             + jnp.dot(hidden(wgb_ref, wub_ref, wts[:, 1:2]), wdb_ref[...], preferred_element_type=F32))
        ys_ref[...] = _pack_halves(y)


def _experts(tile_ea, tile_eb, n_valid, xs, w_gate, w_up, w_down, layer):
    n_rows = xs.shape[0]
    tm = EXPERT_TILE
    nt = n_rows // tm
    rows = lambda n, ea, eb, nv: (jnp.minimum(n, nv[0] - 1), 0)
    sel_a = lambda n, ea, eb, nv: (layer, ea[n], 0, 0)
    sel_b = lambda n, ea, eb, nv: (layer, eb[n], 0, 0)
    up = lambda sel: pl.BlockSpec((None, None, D_MODEL, D_EXPERT), sel)
    down = lambda sel: pl.BlockSpec((None, None, D_EXPERT, D_MODEL), sel)
    return pl.pallas_call(
        _expert_kernel,
        grid_spec=pltpu.PrefetchScalarGridSpec(
            num_scalar_prefetch=3,
            grid=(nt,),
            in_specs=[pl.BlockSpec((tm, ROW_WORDS), rows),
                      up(sel_a), up(sel_a), down(sel_a), up(sel_b), up(sel_b), down(sel_b)],
            out_specs=pl.BlockSpec((tm, HALF), rows),
        ),
        out_shape=jax.ShapeDtypeStruct((n_rows, HALF), U32),
        compiler_params=_cparams(("arbitrary",)),
        name="moe_experts",
    )(tile_ea, tile_eb, n_valid, xs, w_gate, w_up, w_down, w_gate, w_up, w_down)


def _combine_kernel(p_ref, pn_ref, ys_ref, x_ref, mod_ref, g_ref, b_ref, o_ref, buf, sems, *, alpha):
    D = D_MODEL
    tc = x_ref.shape[0]
    g = pl.program_id(0) * pl.num_programs(1) + pl.program_id(1)
    n_steps = pl.num_programs(0) * pl.num_programs(1)
    slot = g % 2

    def row_copy(p, s, t):
        return pltpu.make_async_copy(ys_ref.at[pl.ds(p, 1)], buf.at[s, pl.ds(t, 1)], sems.at[s])

    def gather(pa_ref, s):
        def issue(t8, carry):
            for u in range(ROW_DMA_UNROLL):
                t = t8 * ROW_DMA_UNROLL + u
                row_copy(pa_ref[0, 0, t], s, t).start(priority=u % 2)
            return carry

        lax.fori_loop(0, tc // ROW_DMA_UNROLL, issue, 0)

    @pl.when(g == 0)
    def _():
        gather(p_ref, 0)

    @pl.when(g + 1 < n_steps)
    def _():
        gather(pn_ref, 1 - slot)

    pltpu.make_async_copy(ys_ref.at[pl.ds(0, tc)], buf.at[slot], sems.at[slot]).wait()

    lo, hi = _unpack_halves(buf[slot])
    ffn = jnp.concatenate([lo, hi], axis=1)
    gt = mod_ref[:, 5 * D:6 * D]
    y = alpha * x_ref[...] + (1.0 + gt) * ffn
    o_ref[...] = _layer_norm_rows(y, g_ref[...], b_ref[...])


def _combine(pos, ys, x1, mod_l, ln_g, ln_b, alpha):
    B, S, D = x1.shape
    tc = min(COMBINE_TILE, S)
    nt = S // tc
    blk = pl.BlockSpec((1, 1, tc), lambda b, i: (b * nt + i, 0, 0), memory_space=pltpu.SMEM)
    nxt = pl.BlockSpec((1, 1, tc), lambda b, i: (jnp.minimum(b * nt + i + 1, B * nt - 1), 0, 0),
                       memory_space=pltpu.SMEM)
    row = lambda b, i: (b, i, 0)
    full2 = lambda a: pl.BlockSpec(a.shape, lambda b, i: (0,) * a.ndim)
    p = pos.reshape(B * nt, 1, tc)
    return pl.pallas_call(
        functools.partial(_combine_kernel, alpha=alpha),
        grid=(B, nt),
        in_specs=[
            blk, nxt,
            pl.BlockSpec(memory_space=pl.ANY),
            pl.BlockSpec((None, tc, D), row),
            pl.BlockSpec((None, 1, 6 * D), lambda b, i: (b, 0, 0)),
            full2(ln_g), full2(ln_b),
        ],
        out_specs=pl.BlockSpec((None, tc, D), row),
        out_shape=jax.ShapeDtypeStruct((B, S, D), F32),
        scratch_shapes=[pltpu.VMEM((2, tc, HALF), U32), pltpu.SemaphoreType.DMA((2,))],
        compiler_params=_cparams(("arbitrary", "arbitrary")),
        name="moe_combine_ln",
    )(p, p, ys, x1, mod_l, ln_g, ln_b)


def _rope_tables(S):
    inv = 1.0 / (ROPE_THETA ** (jnp.arange(0, QK_DIM, 2, dtype=F32) / QK_DIM))
    ang = jnp.arange(S, dtype=F32)[:, None] * inv[None, :]
    cos, sin = jnp.cos(ang), jnp.sin(ang)
    half = QK_DIM // 2
    first = (jnp.arange(LANES) % QK_DIM) < half
    cs = jnp.tile(cos, (1, LANES // half))
    sn = jnp.tile(sin, (1, LANES // half))
    sa = jnp.where(first[None, :], -sn, 0.0)
    sb = jnp.where(first[None, :], 0.0, sn)
    return cs, sa, sb


def _router_matrix(w_group, b_group, w_router, b_router):
    D = w_group.shape[0]
    w = jnp.zeros((D, LANES), F32)
    w = w.at[:, 0:N_GROUPS].set(w_group)
    wr = jnp.transpose(w_router, (1, 0, 2)).reshape(D, N_EXPERTS)
    w = w.at[:, ROUTE_LANE0:ROUTE_LANE0 + N_EXPERTS].set(wr)
    b = jnp.zeros((1, LANES), F32)
    b = b.at[0, 0:N_GROUPS].set(b_group)
    b = b.at[0, ROUTE_LANE0:ROUTE_LANE0 + N_EXPERTS].set(b_router.reshape(N_EXPERTS))
    hi = w.astype(BF16)
    lo = (w - hi.astype(F32)).astype(BF16)
    return hi, lo, b


def _routing_tables(rt, cnt, n_tiles):
    B, _, S = rt.shape
    tm = EXPERT_TILE
    bucket = rt[:, 0, :].astype(I32)
    rank = rt[:, 1, :].astype(I32)
    counts = cnt[0, 0:N_BUCKETS].astype(I32)
    tiles = (counts + tm - 1) // tm
    tile_end = jnp.cumsum(tiles)
    tile_start = tile_end - tiles
    onehot = bucket[..., None] == jnp.arange(N_BUCKETS, dtype=I32)
    pos = (jnp.sum(jnp.where(onehot, tile_start * tm, 0), axis=-1) + rank).reshape(B * S)
    tile_ids = jnp.arange(n_tiles, dtype=I32)
    tile_bucket = jnp.minimum(jnp.sum(tile_ids[:, None] >= tile_end[None, :], axis=1), N_BUCKETS - 1)
    pair_a = jnp.asarray([a for a in range(EXP_PER_GROUP) for _ in range(a + 1, EXP_PER_GROUP)], I32)
    pair_b = jnp.asarray([b for a in range(EXP_PER_GROUP) for b in range(a + 1, EXP_PER_GROUP)], I32)
    group = tile_bucket // N_PAIRS
    pair = tile_bucket % N_PAIRS
    tile_ea = (group * EXP_PER_GROUP + pair_a[pair]).astype(I32)
    tile_eb = (group * EXP_PER_GROUP + pair_b[pair]).astype(I32)
    n_valid = tile_end[-1:].astype(I32)
    return pos, tile_ea, tile_eb, n_valid


def kernel(x, c, w_ada, b_ada, w_in, lambda_q1, lambda_k1, lambda_q2, lambda_k2, subln_g, sg_ln_g, sg_ln_b, w_spatial, b_spatial, w_out, ln1_g, ln1_b, w_group, b_group, w_router, b_router, w_gate, w_up, w_down, ln2_g, ln2_b):
    B, S, D = x.shape
    depth = w_in.shape[0]
    T = B * S
    alpha = (2.0 * depth) ** 0.25
    n_tiles = T // EXPERT_TILE + N_BUCKETS
    n_rows = n_tiles * EXPERT_TILE

    mod = _ada(c, w_ada, b_ada)
    tabs = _rope_tables(S)
    w_gate, w_up, w_down = w_gate.astype(BF16), w_up.astype(BF16), w_down.astype(BF16)
    for l in range(depth):
        mod_l = mod[l].reshape(B, 1, 6 * D)
        lam_init = 0.8 - 0.6 * math.exp(-0.3 * l)

        qT, k, vTb, zu, zvn = _inproj(x, mod_l, w_in[l].astype(BF16), tabs,
                                      sg_ln_g[l].reshape(1, SG_WIDTH), sg_ln_b[l].reshape(1, SG_WIDTH))
        lams = (lambda_q1[l].reshape(1, QK_DIM), lambda_k1[l].reshape(1, QK_DIM),
                lambda_q2[l].reshape(1, QK_DIM), lambda_k2[l].reshape(1, QK_DIM))
        att = _attention(qT, k, vTb, lams, subln_g[l].reshape(V_DIM, 1), lam_init)
        b_sp_full = jnp.broadcast_to(b_spatial[l][:, :, None], (N_SG, CHUNK, SG_DIM))
        x1 = _mix(x, mod_l, att, zu, zvn, w_spatial[l], b_sp_full, w_out[l].astype(BF16),
                  ln1_g[l].reshape(1, D), ln1_b[l].reshape(1, D), alpha)

        wr_hi, wr_lo, br = _router_matrix(w_group[l], b_group[l], w_router[l], b_router[l])
        hp, rt, cnt = _route(x1, mod_l, wr_hi, wr_lo, br)
        pos, tile_ea, tile_eb, n_valid = _routing_tables(rt, cnt, n_tiles)
        xs = _dispatch(pos, hp.reshape(T, ROW_WORDS), n_rows)
        ys = _experts(tile_ea, tile_eb, n_valid, xs, w_gate, w_up, w_down, l)
        x = _combine(pos, ys, x1, mod_l, ln2_g[l].reshape(1, D), ln2_b[l].reshape(1, D), alpha)
    return x
```

```python
import functools
import math

import jax
import jax.numpy as jnp
from jax import lax
from jax.experimental import pallas as pl
from jax.experimental.pallas import tpu as pltpu

F32 = jnp.float32
BF16 = jnp.bfloat16
U32 = jnp.uint32
I32 = jnp.int32

D_MODEL = 1024
N_HEADS = 4
QK_DIM = 64
V_DIM = 128
HEAD_COLS = 2 * QK_DIM
QK_COLS = N_HEADS * HEAD_COLS
DIFF_WIDTH = N_HEADS * V_DIM
N_SG = 4
SG_DIM = 128
SG_WIDTH = N_SG * SG_DIM
CHUNK = 128
N_GROUPS = 4
EXP_PER_GROUP = 8
N_EXPERTS = N_GROUPS * EXP_PER_GROUP
N_PAIRS = EXP_PER_GROUP * (EXP_PER_GROUP - 1) // 2
N_BUCKETS = N_GROUPS * N_PAIRS
D_EXPERT = 512
ROPE_THETA = 10000.0
LN_EPS = 1e-5
LANES = 128
HALF = D_MODEL // 2
ROW_WORDS = HALF + LANES

ROW_TILE = 512
EXPERT_TILE = 320
DISPATCH_BLOCK = 2048
COMBINE_TILE = 512
ROW_DMA_UNROLL = 8
ROUTE_LANE0 = 8
VMEM_LIMIT = 48 * 1024 * 1024

DEN_MIN = 2.0 ** -40
DEN_MAX = 2.0 ** 40
LOG2E = 1.4426950408889634
Q_SCALE = (QK_DIM ** -0.5) * LOG2E


def _cparams(sem):
    return pltpu.CompilerParams(dimension_semantics=sem, vmem_limit_bytes=VMEM_LIMIT)


def _layer_norm_rows(y, g, b):
    mu = jnp.mean(y, axis=-1, keepdims=True)
    yc = y - mu
    var = jnp.mean(yc * yc, axis=-1, keepdims=True)
    return yc * lax.rsqrt(var + LN_EPS) * g + b


def _gelu(x):
    return 0.5 * x * (1.0 + lax.erf(x * (2.0 ** -0.5)))


def _pack_halves(y):
    lo = pltpu.bitcast(y[:, :HALF].astype(BF16).astype(F32), U32) >> 16
    hi = pltpu.bitcast(y[:, HALF:].astype(BF16).astype(F32), U32) & jnp.uint32(0xFFFF0000)
    return lo | hi


def _unpack_halves(p):
    lo = pltpu.bitcast(p << 16, F32)
    hi = pltpu.bitcast(p & jnp.uint32(0xFFFF0000), F32)
    return lo, hi


def _ada_kernel(c_ref, w_ref, b_ref, o_ref):
    c = c_ref[...]
    sc = c * jax.nn.sigmoid(c)
    o_ref[...] = jnp.dot(sc, w_ref[...], precision=lax.Precision.HIGHEST,
                         preferred_element_type=F32) + b_ref[...]


def _ada(c, w_ada, b_ada):
    L, D, N = w_ada.shape
    B = c.shape[0]
    tn = 1536
    return pl.pallas_call(
        _ada_kernel,
        grid=(L, N // tn),
        in_specs=[
            pl.BlockSpec((B, D), lambda l, j: (0, 0)),
            pl.BlockSpec((None, D, tn), lambda l, j: (l, 0, j)),
            pl.BlockSpec((None, 1, tn), lambda l, j: (l, 0, j)),
        ],
        out_specs=pl.BlockSpec((None, B, tn), lambda l, j: (l, 0, j)),
        out_shape=jax.ShapeDtypeStruct((L, B, N), F32),
        compiler_params=_cparams(("arbitrary", "arbitrary")),
        name="ada",
    )(c, w_ada, b_ada.reshape(L, 1, N))


def _inproj_kernel(x_ref, mod_ref, w_ref, cs_ref, sa_ref, sb_ref, lng_ref, lnb_ref,
                   qT_ref, k_ref, vT_ref, zu_ref, zvn_ref):
    D = D_MODEL
    x = x_ref[...]
    sh = mod_ref[:, 0:D]
    sc = mod_ref[:, D:2 * D]
    h = (x * (1.0 + sc) + sh).astype(BF16)
    cs = cs_ref[...]
    sa = sa_ref[...]
    sb = sb_ref[...]

    def rope(t):
        return t * cs + pltpu.roll(t, 96, 1) * sa + pltpu.roll(t, 32, 1) * sb

    q = jnp.dot(h, w_ref[:, 0:QK_COLS], preferred_element_type=F32)
    for j in range(N_HEADS):
        sl = slice(j * LANES, (j + 1) * LANES)
        qT_ref[sl, :] = (rope(q[:, sl]) * Q_SCALE).T.astype(BF16)
    k = jnp.dot(h, w_ref[:, QK_COLS:2 * QK_COLS], preferred_element_type=F32)
    for j in range(N_HEADS):
        sl = slice(j * LANES, (j + 1) * LANES)
        k_ref[:, sl] = rope(k[:, sl]).astype(BF16)
    c0 = 2 * QK_COLS
    v = jnp.dot(h, w_ref[:, c0:c0 + DIFF_WIDTH], preferred_element_type=F32)
    vT_ref[...] = v.T.astype(BF16)
    c0 += DIFF_WIDTH
    u = jnp.dot(h, w_ref[:, c0:c0 + SG_WIDTH], preferred_element_type=F32)
    zu_ref[...] = _gelu(u).astype(BF16)
    c0 += SG_WIDTH
    z = _gelu(jnp.dot(h, w_ref[:, c0:c0 + SG_WIDTH], preferred_element_type=F32))
    for g in range(N_SG):
        sl = slice(g * SG_DIM, (g + 1) * SG_DIM)
        zvn_ref[:, sl] = _layer_norm_rows(z[:, sl], lng_ref[:, sl], lnb_ref[:, sl]).astype(BF16)


def _inproj(x, mod_l, w_in_b, rope_tabs, ln_g, ln_b):
    B, S, D = x.shape
    tm = min(ROW_TILE, S)
    nt = S // tm
    cs, sa, sb = rope_tabs
    row = lambda b, i: (b, i, 0)
    tab = pl.BlockSpec((tm, LANES), lambda b, i: (i, 0))
    return pl.pallas_call(
        _inproj_kernel,
        grid=(B, nt),
        in_specs=[
            pl.BlockSpec((None, tm, D), row),
            pl.BlockSpec((None, 1, 6 * D), lambda b, i: (b, 0, 0)),
            pl.BlockSpec(w_in_b.shape, lambda b, i: (0, 0)),
            tab, tab, tab,
            pl.BlockSpec((1, SG_WIDTH), lambda b, i: (0, 0)),
            pl.BlockSpec((1, SG_WIDTH), lambda b, i: (0, 0)),
        ],
        out_specs=[
            pl.BlockSpec((None, QK_COLS, tm), lambda b, i: (b, 0, i)),
            pl.BlockSpec((None, tm, QK_COLS), row),
            pl.BlockSpec((None, None, DIFF_WIDTH, tm), lambda b, i: (b, i, 0, 0)),
            pl.BlockSpec((None, tm, SG_WIDTH), row),
            pl.BlockSpec((None, tm, SG_WIDTH), row),
        ],
        out_shape=[
            jax.ShapeDtypeStruct((B, QK_COLS, S), BF16),
            jax.ShapeDtypeStruct((B, S, QK_COLS), BF16),
            jax.ShapeDtypeStruct((B, nt, DIFF_WIDTH, tm), BF16),
            jax.ShapeDtypeStruct((B, S, SG_WIDTH), BF16),
            jax.ShapeDtypeStruct((B, S, SG_WIDTH), BF16),
        ],
        compiler_params=_cparams(("arbitrary", "arbitrary")),
        name="inproj",
    )(x, mod_l, w_in_b, cs, sa, sb, ln_g, ln_b)


def _attn_kernel(lq1_ref, lk1_ref, lq2_ref, lk2_ref, g_ref, qT_ref, k_ref, vT_ref, o_ref,
                 q_scr, s_scr, m_scr, l_scr, acc_scr, *, lam_init):
    tq = qT_ref.shape[1]
    tk = vT_ref.shape[2]
    i = pl.program_id(2)
    qT = qT_ref[...]
    rows = lax.broadcasted_iota(I32, qT.shape, 0)
    zero = jnp.zeros_like(qT)
    q_scr[0] = jnp.where(rows < QK_DIM, qT, zero)
    q_scr[1] = jnp.where(rows >= QK_DIM, qT, zero)

    def scores(j, slot, c0):
        kb = k_ref[pl.ds(pl.multiple_of(j * tk, tk), tk), :]
        for mp in range(2):
            s_scr[slot, mp, :, c0:] = jnp.dot(kb, q_scr[mp, :, c0:], preferred_element_type=F32)

    def block_scores(slot, mp, c0, c1, masked):
        s = s_scr[slot, mp, :, c0:c1]
        if masked:
            kpos = lax.broadcasted_iota(I32, s.shape, 0)
            qpos = lax.broadcasted_iota(I32, s.shape, 1)
            s = jnp.where(kpos <= qpos, s, -jnp.inf)
        return s

    def pv_unshifted(j, slot, c0, c1, masked):
        vb = vT_ref[j]
        for mp in range(2):
            p = jnp.exp2(block_scores(slot, mp, c0, c1, masked))
            l_scr[mp, :, c0:c1] += jnp.sum(p, axis=0, keepdims=True)
            acc_scr[mp, :, c0:c1] += jnp.dot(vb, p.astype(BF16), preferred_element_type=F32)

    def pv_online(j, slot, c0, c1, masked):
        vb = vT_ref[j]
        for mp in range(2):
            s = block_scores(slot, mp, c0, c1, masked)
            m_old = m_scr[mp, :, c0:c1]
            m_new = jnp.maximum(m_old, jnp.max(s, axis=0, keepdims=True))
            alpha = jnp.exp2(m_old - m_new)
            p = jnp.exp2(s - m_new)
            l_scr[mp, :, c0:c1] = alpha * l_scr[mp, :, c0:c1] + jnp.sum(p, axis=0, keepdims=True)
            acc_scr[mp, :, c0:c1] = (alpha * acc_scr[mp, :, c0:c1]
                                     + jnp.dot(vb, p.astype(BF16), preferred_element_type=F32))
            m_scr[mp, :, c0:c1] = m_new

    def sweep(block):
        acc_scr[...] = jnp.zeros(acc_scr.shape, F32)
        l_scr[...] = jnp.zeros(l_scr.shape, F32)
        scores(0, 0, 0)

        def body(u, carry):
            t = 2 * u + 1
            scores(t, 1, 0)
            block(t - 1, 0, 0, tq, False)
            scores(t + 1, 0, 0)
            block(t, 1, 0, tq, False)
            return carry

        lax.fori_loop(0, i, body, 0)
        scores(2 * i + 1, 1, tk)
        block(2 * i, 0, 0, tk, True)
        block(2 * i, 0, tk, tq, False)
        block(2 * i + 1, 1, tk, tq, True)

    sweep(pv_unshifted)
    den = jnp.concatenate([l_scr[0], l_scr[1]], axis=0)
    in_range = (den >= DEN_MIN) & (den <= DEN_MAX)
    n_bad = jnp.sum(jnp.where(in_range, 0.0, 1.0))

    @pl.when(n_bad > 0.0)
    def _():
        m_scr[...] = jnp.full(m_scr.shape, -jnp.inf, F32)
        sweep(pv_online)

    lam = (jnp.exp(jnp.sum(lq1_ref[...] * lk1_ref[...], axis=1, keepdims=True))
           - jnp.exp(jnp.sum(lq2_ref[...] * lk2_ref[...], axis=1, keepdims=True)) + lam_init)
    inv1 = 1.0 / l_scr[0]
    inv2 = lam / l_scr[1]
    out = acc_scr[0] * inv1 - acc_scr[1] * inv2
    ms = jnp.mean(out * out, axis=0, keepdims=True)
    y = out * lax.rsqrt(ms + LN_EPS) * g_ref[...] * (1.0 - lam_init)
    o_ref[...] = y.T.astype(BF16)


def _attention(qT, k, vTb, lams, g_col, lam_init):
    B, _, S = qT.shape
    nkv, tk = vTb.shape[1], vTb.shape[3]
    tq = 2 * tk
    small = pl.BlockSpec((1, QK_DIM), lambda b, h, i: (0, 0))
    return pl.pallas_call(
        functools.partial(_attn_kernel, lam_init=lam_init),
        grid=(B, N_HEADS, S // tq),
        in_specs=[
            small, small, small, small,
            pl.BlockSpec((V_DIM, 1), lambda b, h, i: (0, 0)),
            pl.BlockSpec((None, HEAD_COLS, tq), lambda b, h, i: (b, h, i)),
            pl.BlockSpec((None, S, HEAD_COLS), lambda b, h, i: (b, 0, h)),
            pl.BlockSpec((None, nkv, V_DIM, tk), lambda b, h, i: (b, 0, h, 0)),
        ],
        out_specs=pl.BlockSpec((None, tq, V_DIM), lambda b, h, i: (b, i, h)),
        out_shape=jax.ShapeDtypeStruct((B, S, DIFF_WIDTH), BF16),
        scratch_shapes=[
            pltpu.VMEM((2, HEAD_COLS, tq), BF16),
            pltpu.VMEM((2, 2, tk, tq), F32),
            pltpu.VMEM((2, 1, tq), F32),
            pltpu.VMEM((2, 1, tq), F32),
            pltpu.VMEM((2, V_DIM, tq), F32),
        ],
        compiler_params=_cparams(("arbitrary", "arbitrary", "arbitrary")),
        name="diff_attn",
    )(*lams, g_col, qT, k, vTb)


def _mix_kernel(x_ref, mod_ref, att_ref, zu_ref, zvn_ref, wsp_ref, bsp_ref, wo_ref, g_ref, b_ref,
                o_ref, sg_scr, *, alpha):
    D = D_MODEL
    nc = zu_ref.shape[0]
    r = lax.broadcasted_iota(I32, (CHUNK, CHUNK), 0)
    c = lax.broadcasted_iota(I32, (CHUNK, CHUNK), 1)
    causal = r >= c
    for g in range(N_SG):
        sl = slice(g * SG_DIM, (g + 1) * SG_DIM)
        w = jnp.where(causal, wsp_ref[g], 0.0).astype(BF16)
        z = jnp.concatenate([zvn_ref[n, :, sl] for n in range(nc)], axis=1)
        mixed = jnp.dot(w, z, preferred_element_type=F32)
        for n in range(nc):
            gate = mixed[:, n * SG_DIM:(n + 1) * SG_DIM] + bsp_ref[g]
            sg_scr[n * CHUNK:(n + 1) * CHUNK, sl] = (zu_ref[n, :, sl].astype(F32) * gate).astype(BF16)
    mix = (jnp.dot(att_ref[...], wo_ref[0:DIFF_WIDTH, :], preferred_element_type=F32)
           + jnp.dot(sg_scr[...], wo_ref[DIFF_WIDTH:, :], preferred_element_type=F32))
    gt = mod_ref[:, 2 * D:3 * D]
    y = alpha * x_ref[...] + (1.0 + gt) * mix
    o_ref[...] = _layer_norm_rows(y, g_ref[...], b_ref[...])


def _mix(x, mod_l, att, zu, zvn, w_sp, b_sp_full, w_out_b, ln_g, ln_b, alpha):
    B, S, D = x.shape
    tm = min(ROW_TILE, S)
    nc = tm // CHUNK
    row = lambda b, i: (b, i, 0)
    zu4 = zu.reshape(B, S // CHUNK, CHUNK, SG_WIDTH)
    zvn4 = zvn.reshape(B, S // CHUNK, CHUNK, SG_WIDTH)
    chunked = pl.BlockSpec((None, nc, CHUNK, SG_WIDTH), lambda b, i: (b, i, 0, 0))
    full2 = lambda a: pl.BlockSpec(a.shape, lambda b, i: (0,) * a.ndim)
    return pl.pallas_call(
        functools.partial(_mix_kernel, alpha=alpha),
        grid=(B, S // tm),
        in_specs=[
            pl.BlockSpec((None, tm, D), row),
            pl.BlockSpec((None, 1, 6 * D), lambda b, i: (b, 0, 0)),
            pl.BlockSpec((None, tm, DIFF_WIDTH), row),
            chunked, chunked,
            full2(w_sp), full2(b_sp_full), full2(w_out_b), full2(ln_g), full2(ln_b),
        ],
        out_specs=pl.BlockSpec((None, tm, D), row),
        out_shape=jax.ShapeDtypeStruct((B, S, D), F32),
        scratch_shapes=[pltpu.VMEM((tm, SG_WIDTH), BF16)],
        compiler_params=_cparams(("arbitrary", "arbitrary")),
        name="sgate_outproj_ln",
    )(x, mod_l, att, zu4, zvn4, w_sp, b_sp_full, w_out_b, ln_g, ln_b)


def _route_kernel(x_ref, mod_ref, wr_hi_ref, wr_lo_ref, br_ref, hp_ref, rt_ref, cnt_ref, run_scr):
    D = D_MODEL
    tm = x_ref.shape[0]

    @pl.when((pl.program_id(0) == 0) & (pl.program_id(1) == 0))
    def _():
        run_scr[...] = jnp.zeros(run_scr.shape, F32)

    sh = mod_ref[:, 3 * D:4 * D]
    sc = mod_ref[:, 4 * D:5 * D]
    h = x_ref[...] * (1.0 + sc) + sh
    h_hi = h.astype(BF16)
    h_lo = (h - h_hi.astype(F32)).astype(BF16)
    logit = (jnp.dot(h_hi, wr_hi_ref[...], preferred_element_type=F32)
             + jnp.dot(h_lo, wr_hi_ref[...], preferred_element_type=F32)
             + jnp.dot(h_hi, wr_lo_ref[...], preferred_element_type=F32)) + br_ref[...]
    lane = lax.broadcasted_iota(I32, logit.shape, 1).astype(F32)
    neg = -jnp.inf
    big = float(LANES)

    def first_argmax(v):
        mx = jnp.max(v, axis=1, keepdims=True)
        idx = jnp.min(jnp.where(v == mx, lane, big), axis=1, keepdims=True)
        return mx, idx

    in_grp = lane < N_GROUPS
    gmax, gidx = first_argmax(jnp.where(in_grp, logit, neg))
    g_p = 1.0 / jnp.sum(jnp.where(in_grp, jnp.exp(logit - gmax), 0.0), axis=1, keepdims=True)
    lo_lane = ROUTE_LANE0 + EXP_PER_GROUP * gidx
    sel = jnp.where((lane >= lo_lane) & (lane < lo_lane + EXP_PER_GROUP), logit, neg)
    v1, i1 = first_argmax(sel)
    v2, i2 = first_argmax(jnp.where(lane == i1, neg, sel))
    t = jnp.exp(v2 - v1)
    w1 = g_p / (1.0 + t)
    w2 = g_p * t / (1.0 + t)

    first_lower = i1 < i2
    e_a = jnp.minimum(i1, i2) - lo_lane
    e_b = jnp.maximum(i1, i2) - lo_lane
    w_a = jnp.where(first_lower, w1, w2)
    w_b = jnp.where(first_lower, w2, w1)
    bucket = gidx * N_PAIRS + (e_a * (2 * EXP_PER_GROUP - 1 - e_a) * 0.5 + (e_b - e_a - 1.0))

    hp_ref[:, 0:HALF] = _pack_halves(h)
    wslab = jnp.where(lane == 0.0, w_a, jnp.where(lane == 1.0, w_b, 0.0))
    hp_ref[:, HALF:] = pltpu.bitcast(wslab, U32)

    hot = lane == bucket
    onehot = jnp.where(hot, 1.0, 0.0)
    r = lax.broadcasted_iota(I32, (tm, tm), 0)
    c = lax.broadcasted_iota(I32, (tm, tm), 1)
    strict = jnp.where(r > c, 1.0, 0.0).astype(BF16)
    before = jnp.dot(strict, onehot.astype(BF16), preferred_element_type=F32) + run_scr[...]
    rank = jnp.sum(jnp.where(hot, before, 0.0), axis=1, keepdims=True)
    run_scr[...] = run_scr[...] + jnp.sum(onehot, axis=0, keepdims=True)
    cnt_ref[...] = run_scr[...]

    slab = jnp.where(lane == 0.0, bucket, jnp.where(lane == 1.0, rank, 0.0))
    rt_ref[...] = slab.T[0:8, :]


def _route(x1, mod_l, wr_hi, wr_lo, br):
    B, S, D = x1.shape
    tm = min(ROW_TILE, S)
    row = lambda b, i: (b, i, 0)
    full2 = lambda a: pl.BlockSpec(a.shape, lambda b, i: (0,) * a.ndim)
    return pl.pallas_call(
        _route_kernel,
        grid=(B, S // tm),
        in_specs=[
            pl.BlockSpec((None, tm, D), row),
            pl.BlockSpec((None, 1, 6 * D), lambda b, i: (b, 0, 0)),
            full2(wr_hi), full2(wr_lo), full2(br),
        ],
        out_specs=[
            pl.BlockSpec((None, tm, ROW_WORDS), row),
            pl.BlockSpec((None, 8, tm), lambda b, i: (b, 0, i)),
            pl.BlockSpec((1, LANES), lambda b, i: (0, 0)),
        ],
        out_shape=[
            jax.ShapeDtypeStruct((B, S, ROW_WORDS), U32),
            jax.ShapeDtypeStruct((B, 8, S), F32),
            jax.ShapeDtypeStruct((1, LANES), F32),
        ],
        scratch_shapes=[pltpu.VMEM((1, LANES), F32)],
        compiler_params=_cparams(("arbitrary", "arbitrary")),
        name="moe_route",
    )(x1, mod_l, wr_hi, wr_lo, br)


def _dispatch_kernel(p_ref, hp_ref, xs_in_ref, xs_ref, sem):
    del xs_in_ref
    nb = p_ref.shape[2]

    def row_copy(t, p):
        return pltpu.make_async_copy(hp_ref.at[pl.ds(t, 1)], xs_ref.at[pl.ds(p, 1)], sem)

    def issue(t8, carry):
        for u in range(ROW_DMA_UNROLL):
            t = t8 * ROW_DMA_UNROLL + u
            row_copy(t, p_ref[0, 0, t]).start(priority=u % 2)
        return carry

    lax.fori_loop(0, nb // ROW_DMA_UNROLL, issue, 0)
    pltpu.make_async_copy(hp_ref, xs_ref.at[pl.ds(0, nb)], sem).wait()


def _dispatch(pos, hp, n_rows):
    T = hp.shape[0]
    nb = min(DISPATCH_BLOCK, T)
    blk = pl.BlockSpec((1, 1, nb), lambda i: (i, 0, 0), memory_space=pltpu.SMEM)
    xs0 = jnp.zeros((n_rows, ROW_WORDS), U32)
    return pl.pallas_call(
        _dispatch_kernel,
        grid=(T // nb,),
        in_specs=[blk, pl.BlockSpec((nb, ROW_WORDS), lambda i: (i, 0)), pl.BlockSpec(memory_space=pl.ANY)],
        out_specs=pl.BlockSpec(memory_space=pl.ANY),
        out_shape=jax.ShapeDtypeStruct((n_rows, ROW_WORDS), U32),
        scratch_shapes=[pltpu.SemaphoreType.DMA(())],
        input_output_aliases={2: 0},
        compiler_params=_cparams(("arbitrary",)),
        name="moe_dispatch",
    )(pos.reshape(T // nb, 1, nb), hp, xs0)


def _expert_kernel(ea_ref, eb_ref, nv_ref, xs_ref, wga_ref, wua_ref, wda_ref, wgb_ref, wub_ref, wdb_ref, ys_ref):
    del ea_ref, eb_ref

    @pl.when(pl.program_id(0) < nv_ref[0])
    def _():
        lo, hi = _unpack_halves(xs_ref[:, 0:HALF])
        lo = lo.astype(BF16)
        hi = hi.astype(BF16)
        wts = pltpu.bitcast(xs_ref[:, HALF:], F32)

        def hidden(wg_ref, wu_ref, w):
            g = (jnp.dot(lo, wg_ref[0:HALF, :], preferred_element_type=F32)
                 + jnp.dot(hi, wg_ref[HALF:, :], preferred_element_type=F32))
            u = (jnp.dot(lo, wu_ref[0:HALF, :], preferred_element_type=F32)
                 + jnp.dot(hi, wu_ref[HALF:, :], preferred_element_type=F32))
            return (g * jax.nn.sigmoid(g) * u * w).astype(BF16)

        y = (jnp.dot(hidden(wga_ref, wua_ref, wts[:, 0:1]), wda_ref[...], preferred_element_type=F32)
             + jnp.dot(hidden(wgb_ref, wub_ref, wts[:, 1:2]), wdb_ref[...], preferred_element_type=F32))
        ys_ref[...] = _pack_halves(y)


def _experts(tile_ea, tile_eb, n_valid, xs, w_gate, w_up, w_down, layer):
    n_rows = xs.shape[0]
    tm = EXPERT_TILE
    nt = n_rows // tm
    rows = lambda n, ea, eb, nv: (jnp.minimum(n, nv[0] - 1), 0)
    sel_a = lambda n, ea, eb, nv: (layer, ea[n], 0, 0)
    sel_b = lambda n, ea, eb, nv: (layer, eb[n], 0, 0)
    up = lambda sel: pl.BlockSpec((None, None, D_MODEL, D_EXPERT), sel)
    down = lambda sel: pl.BlockSpec((None, None, D_EXPERT, D_MODEL), sel)
    return pl.pallas_call(
        _expert_kernel,
        grid_spec=pltpu.PrefetchScalarGridSpec(
            num_scalar_prefetch=3,
            grid=(nt,),
            in_specs=[pl.BlockSpec((tm, ROW_WORDS), rows),
                      up(sel_a), up(sel_a), down(sel_a), up(sel_b), up(sel_b), down(sel_b)],
            out_specs=pl.BlockSpec((tm, HALF), rows),
        ),
        out_shape=jax.ShapeDtypeStruct((n_rows, HALF), U32),
        compiler_params=_cparams(("arbitrary",)),
        name="moe_experts",
    )(tile_ea, tile_eb, n_valid, xs, w_gate, w_up, w_down, w_gate, w_up, w_down)


def _combine_kernel(p_ref, pn_ref, ys_ref, x_ref, mod_ref, g_ref, b_ref, o_ref, buf, sems, *, alpha):
    D = D_MODEL
    tc = x_ref.shape[0]
    g = pl.program_id(0) * pl.num_programs(1) + pl.program_id(1)
    n_steps = pl.num_programs(0) * pl.num_programs(1)
    slot = g % 2

    def row_copy(p, s, t):
        return pltpu.make_async_copy(ys_ref.at[pl.ds(p, 1)], buf.at[s, pl.ds(t, 1)], sems.at[s])

    def gather(pa_ref, s):
        def issue(t8, carry):
            for u in range(ROW_DMA_UNROLL):
                t = t8 * ROW_DMA_UNROLL + u
                row_copy(pa_ref[0, 0, t], s, t).start(priority=u % 2)
            return carry

        lax.fori_loop(0, tc // ROW_DMA_UNROLL, issue, 0)

    @pl.when(g == 0)
    def _():
        gather(p_ref, 0)

    @pl.when(g + 1 < n_steps)
    def _():
        gather(pn_ref, 1 - slot)

    pltpu.make_async_copy(ys_ref.at[pl.ds(0, tc)], buf.at[slot], sems.at[slot]).wait()

    lo, hi = _unpack_halves(buf[slot])
    ffn = jnp.concatenate([lo, hi], axis=1)
    gt = mod_ref[:, 5 * D:6 * D]
    y = alpha * x_ref[...] + (1.0 + gt) * ffn
    o_ref[...] = _layer_norm_rows(y, g_ref[...], b_ref[...])


def _combine(pos, ys, x1, mod_l, ln_g, ln_b, alpha):
    B, S, D = x1.shape
    tc = min(COMBINE_TILE, S)
    nt = S // tc
    blk = pl.BlockSpec((1, 1, tc), lambda b, i: (b * nt + i, 0, 0), memory_space=pltpu.SMEM)
    nxt = pl.BlockSpec((1, 1, tc), lambda b, i: (jnp.minimum(b * nt + i + 1, B * nt - 1), 0, 0),
                       memory_space=pltpu.SMEM)
    row = lambda b, i: (b, i, 0)
    full2 = lambda a: pl.BlockSpec(a.shape, lambda b, i: (0,) * a.ndim)
    p = pos.reshape(B * nt, 1, tc)
    return pl.pallas_call(
        functools.partial(_combine_kernel, alpha=alpha),
        grid=(B, nt),
        in_specs=[
            blk, nxt,
            pl.BlockSpec(memory_space=pl.ANY),
            pl.BlockSpec((None, tc, D), row),
            pl.BlockSpec((None, 1, 6 * D), lambda b, i: (b, 0, 0)),
            full2(ln_g), full2(ln_b),
        ],
        out_specs=pl.BlockSpec((None, tc, D), row),
        out_shape=jax.ShapeDtypeStruct((B, S, D), F32),
        scratch_shapes=[pltpu.VMEM((2, tc, HALF), U32), pltpu.SemaphoreType.DMA((2,))],
        compiler_params=_cparams(("arbitrary", "arbitrary")),
        name="moe_combine_ln",
    )(p, p, ys, x1, mod_l, ln_g, ln_b)


def _rope_tables(S):
    inv = 1.0 / (ROPE_THETA ** (jnp.arange(0, QK_DIM, 2, dtype=F32) / QK_DIM))
    ang = jnp.arange(S, dtype=F32)[:, None] * inv[None, :]
    cos, sin = jnp.cos(ang), jnp.sin(ang)
    half = QK_DIM // 2
    first = (jnp.arange(LANES) % QK_DIM) < half
    cs = jnp.tile(cos, (1, LANES // half))
    sn = jnp.tile(sin, (1, LANES // half))
    sa = jnp.where(first[None, :], -sn, 0.0)
    sb = jnp.where(first[None, :], 0.0, sn)
    return cs, sa, sb


def _router_matrix(w_group, b_group, w_router, b_router):
    D = w_group.shape[0]
    w = jnp.zeros((D, LANES), F32)
    w = w.at[:, 0:N_GROUPS].set(w_group)
    wr = jnp.transpose(w_router, (1, 0, 2)).reshape(D, N_EXPERTS)
    w = w.at[:, ROUTE_LANE0:ROUTE_LANE0 + N_EXPERTS].set(wr)
    b = jnp.zeros((1, LANES), F32)
    b = b.at[0, 0:N_GROUPS].set(b_group)
    b = b.at[0, ROUTE_LANE0:ROUTE_LANE0 + N_EXPERTS].set(b_router.reshape(N_EXPERTS))
    hi = w.astype(BF16)
    lo = (w - hi.astype(F32)).astype(BF16)
    return hi, lo, b


def _routing_tables(rt, cnt, n_tiles):
    B, _, S = rt.shape
    tm = EXPERT_TILE
    bucket = rt[:, 0, :].astype(I32)
    rank = rt[:, 1, :].astype(I32)
    counts = cnt[0, 0:N_BUCKETS].astype(I32)
    tiles = (counts + tm - 1) // tm
    tile_end = jnp.cumsum(tiles)
    tile_start = tile_end - tiles
    onehot = bucket[..., None] == jnp.arange(N_BUCKETS, dtype=I32)
    pos = (jnp.sum(jnp.where(onehot, tile_start * tm, 0), axis=-1) + rank).reshape(B * S)
    tile_ids = jnp.arange(n_tiles, dtype=I32)
    tile_bucket = jnp.minimum(jnp.sum(tile_ids[:, None] >= tile_end[None, :], axis=1), N_BUCKETS - 1)
    pair_a = jnp.asarray([a for a in range(EXP_PER_GROUP) for _ in range(a + 1, EXP_PER_GROUP)], I32)
    pair_b = jnp.asarray([b for a in range(EXP_PER_GROUP) for b in range(a + 1, EXP_PER_GROUP)], I32)
    group = tile_bucket // N_PAIRS
    pair = tile_bucket % N_PAIRS
    tile_ea = (group * EXP_PER_GROUP + pair_a[pair]).astype(I32)
    tile_eb = (group * EXP_PER_GROUP + pair_b[pair]).astype(I32)
    n_valid = tile_end[-1:].astype(I32)
    return pos, tile_ea, tile_eb, n_valid


def kernel(x, c, w_ada, b_ada, w_in, lambda_q1, lambda_k1, lambda_q2, lambda_k2, subln_g, sg_ln_g, sg_ln_b, w_spatial, b_spatial, w_out, ln1_g, ln1_b, w_group, b_group, w_router, b_router, w_gate, w_up, w_down, ln2_g, ln2_b):
    B, S, D = x.shape
    depth = w_in.shape[0]
    T = B * S
    alpha = (2.0 * depth) ** 0.25
    n_tiles = -(-T // EXPERT_TILE) + N_BUCKETS
    n_rows = n_tiles * EXPERT_TILE

    mod = _ada(c, w_ada, b_ada)
    tabs = _rope_tables(S)
    w_gate, w_up, w_down = w_gate.astype(BF16), w_up.astype(BF16), w_down.astype(BF16)
    for l in range(depth):
        mod_l = mod[l].reshape(B, 1, 6 * D)
        lam_init = 0.8 - 0.6 * math.exp(-0.3 * l)

        qT, k, vTb, zu, zvn = _inproj(x, mod_l, w_in[l].astype(BF16), tabs,
                                      sg_ln_g[l].reshape(1, SG_WIDTH), sg_ln_b[l].reshape(1, SG_WIDTH))
        lams = (lambda_q1[l].reshape(1, QK_DIM), lambda_k1[l].reshape(1, QK_DIM),
                lambda_q2[l].reshape(1, QK_DIM), lambda_k2[l].reshape(1, QK_DIM))
        att = _attention(qT, k, vTb, lams, subln_g[l].reshape(V_DIM, 1), lam_init)
        b_sp_full = jnp.broadcast_to(b_spatial[l][:, :, None], (N_SG, CHUNK, SG_DIM))
        x1 = _mix(x, mod_l, att, zu, zvn, w_spatial[l], b_sp_full, w_out[l].astype(BF16),
                  ln1_g[l].reshape(1, D), ln1_b[l].reshape(1, D), alpha)

        wr_hi, wr_lo, br = _router_matrix(w_group[l], b_group[l], w_router[l], b_router[l])
        hp, rt, cnt = _route(x1, mod_l, wr_hi, wr_lo, br)
        pos, tile_ea, tile_eb, n_valid = _routing_tables(rt, cnt, n_tiles)
        xs = _dispatch(pos, hp.reshape(T, ROW_WORDS), n_rows)
        ys = _experts(tile_ea, tile_eb, n_valid, xs, w_gate, w_up, w_down, l)
        x = _combine(pos, ys, x1, mod_l, ln2_g[l].reshape(1, D), ln2_b[l].reshape(1, D), alpha)
    return x
```

```python
import functools
import math

import jax
import jax.numpy as jnp
from jax import lax
from jax.experimental import pallas as pl
from jax.experimental.pallas import tpu as pltpu

F32 = jnp.float32
BF16 = jnp.bfloat16
U32 = jnp.uint32
I32 = jnp.int32

D_MODEL = 1024
N_HEADS = 4
QK_DIM = 64
V_DIM = 128
HEAD_COLS = 2 * QK_DIM
QK_COLS = N_HEADS * HEAD_COLS
DIFF_WIDTH = N_HEADS * V_DIM
N_SG = 4
SG_DIM = 128
SG_WIDTH = N_SG * SG_DIM
CHUNK = 128
N_GROUPS = 4
EXP_PER_GROUP = 8
N_EXPERTS = N_GROUPS * EXP_PER_GROUP
N_PAIRS = EXP_PER_GROUP * (EXP_PER_GROUP - 1) // 2
N_BUCKETS = N_GROUPS * N_PAIRS
D_EXPERT = 512
ROPE_THETA = 10000.0
LN_EPS = 1e-5
LANES = 128
HALF = D_MODEL // 2
ROW_WORDS = HALF + LANES

ROW_TILE = 512
EXPERT_TILE = 256
DISPATCH_BLOCK = 2048
COMBINE_TILE = 512
ROW_DMA_UNROLL = 8
ROUTE_LANE0 = 8
VMEM_LIMIT = 48 * 1024 * 1024

DEN_MIN = 2.0 ** -40
DEN_MAX = 2.0 ** 40
LOG2E = 1.4426950408889634
Q_SCALE = (QK_DIM ** -0.5) * LOG2E


def _cparams(sem):
    return pltpu.CompilerParams(dimension_semantics=sem, vmem_limit_bytes=VMEM_LIMIT)


def _layer_norm_rows(y, g, b):
    mu = jnp.mean(y, axis=-1, keepdims=True)
    yc = y - mu
    var = jnp.mean(yc * yc, axis=-1, keepdims=True)
    return yc * lax.rsqrt(var + LN_EPS) * g + b


def _gelu(x):
    return 0.5 * x * (1.0 + lax.erf(x * (2.0 ** -0.5)))


def _pack_halves(y):
    lo = pltpu.bitcast(y[:, :HALF].astype(BF16).astype(F32), U32) >> 16
    hi = pltpu.bitcast(y[:, HALF:].astype(BF16).astype(F32), U32) & jnp.uint32(0xFFFF0000)
    return lo | hi


def _unpack_halves(p):
    lo = pltpu.bitcast(p << 16, F32)
    hi = pltpu.bitcast(p & jnp.uint32(0xFFFF0000), F32)
    return lo, hi


def _ada_kernel(c_ref, w_ref, b_ref, o_ref):
    c = c_ref[...]
    sc = c * jax.nn.sigmoid(c)
    o_ref[...] = jnp.dot(sc, w_ref[...], precision=lax.Precision.HIGHEST,
                         preferred_element_type=F32) + b_ref[...]


def _ada(c, w_ada, b_ada):
    L, D, N = w_ada.shape
    B = c.shape[0]
    tn = 1536
    return pl.pallas_call(
        _ada_kernel,
        grid=(L, N // tn),
        in_specs=[
            pl.BlockSpec((B, D), lambda l, j: (0, 0)),
            pl.BlockSpec((None, D, tn), lambda l, j: (l, 0, j)),
            pl.BlockSpec((None, 1, tn), lambda l, j: (l, 0, j)),
        ],
        out_specs=pl.BlockSpec((None, B, tn), lambda l, j: (l, 0, j)),
        out_shape=jax.ShapeDtypeStruct((L, B, N), F32),
        compiler_params=_cparams(("arbitrary", "arbitrary")),
        name="ada",
    )(c, w_ada, b_ada.reshape(L, 1, N))


def _inproj_kernel(x_ref, mod_ref, w_ref, cs_ref, sa_ref, sb_ref, lng_ref, lnb_ref,
                   qT_ref, k_ref, vT_ref, zu_ref, zvn_ref):
    D = D_MODEL
    x = x_ref[...]
    sh = mod_ref[:, 0:D]
    sc = mod_ref[:, D:2 * D]
    h = (x * (1.0 + sc) + sh).astype(BF16)
    cs = cs_ref[...]
    sa = sa_ref[...]
    sb = sb_ref[...]

    def rope(t):
        return t * cs + pltpu.roll(t, 96, 1) * sa + pltpu.roll(t, 32, 1) * sb

    q = jnp.dot(h, w_ref[:, 0:QK_COLS], preferred_element_type=F32)
    for j in range(N_HEADS):
        sl = slice(j * LANES, (j + 1) * LANES)
        qT_ref[sl, :] = (rope(q[:, sl]) * Q_SCALE).T.astype(BF16)
    k = jnp.dot(h, w_ref[:, QK_COLS:2 * QK_COLS], preferred_element_type=F32)
    for j in range(N_HEADS):
        sl = slice(j * LANES, (j + 1) * LANES)
        k_ref[:, sl] = rope(k[:, sl]).astype(BF16)
    c0 = 2 * QK_COLS
    v = jnp.dot(h, w_ref[:, c0:c0 + DIFF_WIDTH], preferred_element_type=F32)
    vT_ref[...] = v.T.astype(BF16)
    c0 += DIFF_WIDTH
    u = jnp.dot(h, w_ref[:, c0:c0 + SG_WIDTH], preferred_element_type=F32)
    zu_ref[...] = _gelu(u).astype(BF16)
    c0 += SG_WIDTH
    z = _gelu(jnp.dot(h, w_ref[:, c0:c0 + SG_WIDTH], preferred_element_type=F32))
    for g in range(N_SG):
        sl = slice(g * SG_DIM, (g + 1) * SG_DIM)
        zvn_ref[:, sl] = _layer_norm_rows(z[:, sl], lng_ref[:, sl], lnb_ref[:, sl]).astype(BF16)


def _inproj(x, mod_l, w_in_b, rope_tabs, ln_g, ln_b):
    B, S, D = x.shape
    tm = min(ROW_TILE, S)
    nt = S // tm
    cs, sa, sb = rope_tabs
    row = lambda b, i: (b, i, 0)
    tab = pl.BlockSpec((tm, LANES), lambda b, i: (i, 0))
    return pl.pallas_call(
        _inproj_kernel,
        grid=(B, nt),
        in_specs=[
            pl.BlockSpec((None, tm, D), row),
            pl.BlockSpec((None, 1, 6 * D), lambda b, i: (b, 0, 0)),
            pl.BlockSpec(w_in_b.shape, lambda b, i: (0, 0)),
            tab, tab, tab,
            pl.BlockSpec((1, SG_WIDTH), lambda b, i: (0, 0)),
            pl.BlockSpec((1, SG_WIDTH), lambda b, i: (0, 0)),
        ],
        out_specs=[
            pl.BlockSpec((None, QK_COLS, tm), lambda b, i: (b, 0, i)),
            pl.BlockSpec((None, tm, QK_COLS), row),
            pl.BlockSpec((None, None, DIFF_WIDTH, tm), lambda b, i: (b, i, 0, 0)),
            pl.BlockSpec((None, tm, SG_WIDTH), row),
            pl.BlockSpec((None, tm, SG_WIDTH), row),
        ],
        out_shape=[
            jax.ShapeDtypeStruct((B, QK_COLS, S), BF16),
            jax.ShapeDtypeStruct((B, S, QK_COLS), BF16),
            jax.ShapeDtypeStruct((B, nt, DIFF_WIDTH, tm), BF16),
            jax.ShapeDtypeStruct((B, S, SG_WIDTH), BF16),
            jax.ShapeDtypeStruct((B, S, SG_WIDTH), BF16),
        ],
        compiler_params=_cparams(("arbitrary", "arbitrary")),
        name="inproj",
    )(x, mod_l, w_in_b, cs, sa, sb, ln_g, ln_b)


def _attn_kernel(lq1_ref, lk1_ref, lq2_ref, lk2_ref, g_ref, qT_ref, k_ref, vT_ref, o_ref,
                 q_scr, s_scr, m_scr, l_scr, acc_scr, *, lam_init):
    tq = qT_ref.shape[1]
    tk = vT_ref.shape[2]
    i = pl.program_id(2)
    qT = qT_ref[...]
    rows = lax.broadcasted_iota(I32, qT.shape, 0)
    zero = jnp.zeros_like(qT)
    q_scr[0] = jnp.where(rows < QK_DIM, qT, zero)
    q_scr[1] = jnp.where(rows >= QK_DIM, qT, zero)

    def scores(j, slot, c0):
        kb = k_ref[pl.ds(pl.multiple_of(j * tk, tk), tk), :]
        for mp in range(2):
            s_scr[slot, mp, :, c0:] = jnp.dot(kb, q_scr[mp, :, c0:], preferred_element_type=F32)

    def block_scores(slot, mp, c0, c1, masked):
        s = s_scr[slot, mp, :, c0:c1]
        if masked:
            kpos = lax.broadcasted_iota(I32, s.shape, 0)
            qpos = lax.broadcasted_iota(I32, s.shape, 1)
            s = jnp.where(kpos <= qpos, s, -jnp.inf)
        return s

    def pv_unshifted(j, slot, c0, c1, masked):
        vb = vT_ref[j]
        for mp in range(2):
            p = jnp.exp2(block_scores(slot, mp, c0, c1, masked))
            l_scr[mp, :, c0:c1] += jnp.sum(p, axis=0, keepdims=True)
            acc_scr[mp, :, c0:c1] += jnp.dot(vb, p.astype(BF16), preferred_element_type=F32)

    def pv_online(j, slot, c0, c1, masked):
        vb = vT_ref[j]
        for mp in range(2):
            s = block_scores(slot, mp, c0, c1, masked)
            m_old = m_scr[mp, :, c0:c1]
            m_new = jnp.maximum(m_old, jnp.max(s, axis=0, keepdims=True))
            alpha = jnp.exp2(m_old - m_new)
            p = jnp.exp2(s - m_new)
            l_scr[mp, :, c0:c1] = alpha * l_scr[mp, :, c0:c1] + jnp.sum(p, axis=0, keepdims=True)
            acc_scr[mp, :, c0:c1] = (alpha * acc_scr[mp, :, c0:c1]
                                     + jnp.dot(vb, p.astype(BF16), preferred_element_type=F32))
            m_scr[mp, :, c0:c1] = m_new

    def sweep(block):
        acc_scr[...] = jnp.zeros(acc_scr.shape, F32)
        l_scr[...] = jnp.zeros(l_scr.shape, F32)
        scores(0, 0, 0)

        def pair(t):
            scores(t, 1, 0)
            block(t - 1, 0, 0, tq, False)
            scores(t + 1, 0, 0)
            block(t, 1, 0, tq, False)

        def body(u, carry):
            pair(4 * u + 1)
            pair(4 * u + 3)
            return carry

        lax.fori_loop(0, i // 2, body, 0)

        @pl.when(i % 2 == 1)
        def _():
            pair(2 * i - 1)

        scores(2 * i + 1, 1, tk)
        block(2 * i, 0, 0, tk, True)
        block(2 * i, 0, tk, tq, False)
        block(2 * i + 1, 1, tk, tq, True)

    sweep(pv_unshifted)
    den = jnp.concatenate([l_scr[0], l_scr[1]], axis=0)
    in_range = (den >= DEN_MIN) & (den <= DEN_MAX)
    n_bad = jnp.sum(jnp.where(in_range, 0.0, 1.0))

    @pl.when(n_bad > 0.0)
    def _():
        m_scr[...] = jnp.full(m_scr.shape, -jnp.inf, F32)
        sweep(pv_online)

    lam = (jnp.exp(jnp.sum(lq1_ref[...] * lk1_ref[...], axis=1, keepdims=True))
           - jnp.exp(jnp.sum(lq2_ref[...] * lk2_ref[...], axis=1, keepdims=True)) + lam_init)
    out = acc_scr[0] / l_scr[0] - lam * (acc_scr[1] / l_scr[1])
    ms = jnp.mean(out * out, axis=0, keepdims=True)
    y = out * lax.rsqrt(ms + LN_EPS) * g_ref[...] * (1.0 - lam_init)
    o_ref[...] = y.T.astype(BF16)


def _attention(qT, k, vTb, lams, g_col, lam_init):
    B, _, S = qT.shape
    nkv, tk = vTb.shape[1], vTb.shape[3]
    tq = 2 * tk
    small = pl.BlockSpec((1, QK_DIM), lambda b, h, i: (0, 0))
    return pl.pallas_call(
        functools.partial(_attn_kernel, lam_init=lam_init),
        grid=(B, N_HEADS, S // tq),
        in_specs=[
            small, small, small, small,
            pl.BlockSpec((V_DIM, 1), lambda b, h, i: (0, 0)),
            pl.BlockSpec((None, HEAD_COLS, tq), lambda b, h, i: (b, h, i)),
            pl.BlockSpec((None, S, HEAD_COLS), lambda b, h, i: (b, 0, h)),
            pl.BlockSpec((None, nkv, V_DIM, tk), lambda b, h, i: (b, 0, h, 0)),
        ],
        out_specs=pl.BlockSpec((None, tq, V_DIM), lambda b, h, i: (b, i, h)),
        out_shape=jax.ShapeDtypeStruct((B, S, DIFF_WIDTH), BF16),
        scratch_shapes=[
            pltpu.VMEM((2, HEAD_COLS, tq), BF16),
            pltpu.VMEM((2, 2, tk, tq), F32),
            pltpu.VMEM((2, 1, tq), F32),
            pltpu.VMEM((2, 1, tq), F32),
            pltpu.VMEM((2, V_DIM, tq), F32),
        ],
        compiler_params=_cparams(("arbitrary", "arbitrary", "arbitrary")),
        name="diff_attn",
    )(*lams, g_col, qT, k, vTb)


def _mix_kernel(x_ref, mod_ref, att_ref, zu_ref, zvn_ref, wsp_ref, bsp_ref, wo_ref, g_ref, b_ref,
                o_ref, sg_scr, *, alpha):
    D = D_MODEL
    nc = zu_ref.shape[0]
    r = lax.broadcasted_iota(I32, (CHUNK, CHUNK), 0)
    c = lax.broadcasted_iota(I32, (CHUNK, CHUNK), 1)
    causal = r >= c
    for g in range(N_SG):
        sl = slice(g * SG_DIM, (g + 1) * SG_DIM)
        w = jnp.where(causal, wsp_ref[g], 0.0).astype(BF16)
        z = jnp.concatenate([zvn_ref[n, :, sl] for n in range(nc)], axis=1)
        mixed = jnp.dot(w, z, preferred_element_type=F32)
        for n in range(nc):
            gate = mixed[:, n * SG_DIM:(n + 1) * SG_DIM] + bsp_ref[g]
            sg_scr[n * CHUNK:(n + 1) * CHUNK, sl] = (zu_ref[n, :, sl].astype(F32) * gate).astype(BF16)
    mix = (jnp.dot(att_ref[...], wo_ref[0:DIFF_WIDTH, :], preferred_element_type=F32)
           + jnp.dot(sg_scr[...], wo_ref[DIFF_WIDTH:, :], preferred_element_type=F32))
    gt = mod_ref[:, 2 * D:3 * D]
    y = alpha * x_ref[...] + (1.0 + gt) * mix
    o_ref[...] = _layer_norm_rows(y, g_ref[...], b_ref[...])


def _mix(x, mod_l, att, zu, zvn, w_sp, b_sp_full, w_out_b, ln_g, ln_b, alpha):
    B, S, D = x.shape
    tm = min(ROW_TILE, S)
    nc = tm // CHUNK
    row = lambda b, i: (b, i, 0)
    zu4 = zu.reshape(B, S // CHUNK, CHUNK, SG_WIDTH)
    zvn4 = zvn.reshape(B, S // CHUNK, CHUNK, SG_WIDTH)
    chunked = pl.BlockSpec((None, nc, CHUNK, SG_WIDTH), lambda b, i: (b, i, 0, 0))
    full2 = lambda a: pl.BlockSpec(a.shape, lambda b, i: (0,) * a.ndim)
    return pl.pallas_call(
        functools.partial(_mix_kernel, alpha=alpha),
        grid=(B, S // tm),
        in_specs=[
            pl.BlockSpec((None, tm, D), row),
            pl.BlockSpec((None, 1, 6 * D), lambda b, i: (b, 0, 0)),
            pl.BlockSpec((None, tm, DIFF_WIDTH), row),
            chunked, chunked,
            full2(w_sp), full2(b_sp_full), full2(w_out_b), full2(ln_g), full2(ln_b),
        ],
        out_specs=pl.BlockSpec((None, tm, D), row),
        out_shape=jax.ShapeDtypeStruct((B, S, D), F32),
        scratch_shapes=[pltpu.VMEM((tm, SG_WIDTH), BF16)],
        compiler_params=_cparams(("arbitrary", "arbitrary")),
        name="sgate_outproj_ln",
    )(x, mod_l, att, zu4, zvn4, w_sp, b_sp_full, w_out_b, ln_g, ln_b)


def _route_kernel(x_ref, mod_ref, wr_hi_ref, wr_lo_ref, br_ref, hp_ref, rt_ref, cnt_ref, run_scr):
    D = D_MODEL
    tm = x_ref.shape[0]

    @pl.when((pl.program_id(0) == 0) & (pl.program_id(1) == 0))
    def _():
        run_scr[...] = jnp.zeros(run_scr.shape, F32)

    sh = mod_ref[:, 3 * D:4 * D]
    sc = mod_ref[:, 4 * D:5 * D]
    h = x_ref[...] * (1.0 + sc) + sh
    h_hi = h.astype(BF16)
    h_lo = (h - h_hi.astype(F32)).astype(BF16)
    logit = (jnp.dot(h_hi, wr_hi_ref[...], preferred_element_type=F32)
             + jnp.dot(h_lo, wr_hi_ref[...], preferred_element_type=F32)
             + jnp.dot(h_hi, wr_lo_ref[...], preferred_element_type=F32)) + br_ref[...]
    lane = lax.broadcasted_iota(I32, logit.shape, 1).astype(F32)
    neg = -jnp.inf
    big = float(LANES)

    def first_argmax(v):
        mx = jnp.max(v, axis=1, keepdims=True)
        idx = jnp.min(jnp.where(v == mx, lane, big), axis=1, keepdims=True)
        return mx, idx

    in_grp = lane < N_GROUPS
    gmax, gidx = first_argmax(jnp.where(in_grp, logit, neg))
    g_p = 1.0 / jnp.sum(jnp.where(in_grp, jnp.exp(logit - gmax), 0.0), axis=1, keepdims=True)
    lo_lane = ROUTE_LANE0 + EXP_PER_GROUP * gidx
    sel = jnp.where((lane >= lo_lane) & (lane < lo_lane + EXP_PER_GROUP), logit, neg)
    v1, i1 = first_argmax(sel)
    v2, i2 = first_argmax(jnp.where(lane == i1, neg, sel))
    t = jnp.exp(v2 - v1)
    w1 = g_p / (1.0 + t)
    w2 = g_p * t / (1.0 + t)

    first_lower = i1 < i2
    e_a = jnp.minimum(i1, i2) - lo_lane
    e_b = jnp.maximum(i1, i2) - lo_lane
    w_a = jnp.where(first_lower, w1, w2)
    w_b = jnp.where(first_lower, w2, w1)
    bucket = gidx * N_PAIRS + (e_a * (2 * EXP_PER_GROUP - 1 - e_a) * 0.5 + (e_b - e_a - 1.0))

    hp_ref[:, 0:HALF] = _pack_halves(h)
    wslab = jnp.where(lane == 0.0, w_a, jnp.where(lane == 1.0, w_b, 0.0))
    hp_ref[:, HALF:] = pltpu.bitcast(wslab, U32)

    hot = lane == bucket
    onehot = jnp.where(hot, 1.0, 0.0)
    r = lax.broadcasted_iota(I32, (tm, tm), 0)
    c = lax.broadcasted_iota(I32, (tm, tm), 1)
    strict = jnp.where(r > c, 1.0, 0.0).astype(BF16)
    before = jnp.dot(strict, onehot.astype(BF16), preferred_element_type=F32) + run_scr[...]
    rank = jnp.sum(jnp.where(hot, before, 0.0), axis=1, keepdims=True)
    run_scr[...] = run_scr[...] + jnp.sum(onehot, axis=0, keepdims=True)
    cnt_ref[...] = run_scr[...]

    slab = jnp.where(lane == 0.0, bucket, jnp.where(lane == 1.0, rank, 0.0))
    rt_ref[...] = slab.T[0:8, :]


def _route(x1, mod_l, wr_hi, wr_lo, br):
    B, S, D = x1.shape
    tm = min(ROW_TILE, S)
    row = lambda b, i: (b, i, 0)
    full2 = lambda a: pl.BlockSpec(a.shape, lambda b, i: (0,) * a.ndim)
    return pl.pallas_call(
        _route_kernel,
        grid=(B, S // tm),
        in_specs=[
            pl.BlockSpec((None, tm, D), row),
            pl.BlockSpec((None, 1, 6 * D), lambda b, i: (b, 0, 0)),
            full2(wr_hi), full2(wr_lo), full2(br),
        ],
        out_specs=[
            pl.BlockSpec((None, tm, ROW_WORDS), row),
            pl.BlockSpec((None, 8, tm), lambda b, i: (b, 0, i)),
            pl.BlockSpec((1, LANES), lambda b, i: (0, 0)),
        ],
        out_shape=[
            jax.ShapeDtypeStruct((B, S, ROW_WORDS), U32),
            jax.ShapeDtypeStruct((B, 8, S), F32),
            jax.ShapeDtypeStruct((1, LANES), F32),
        ],
        scratch_shapes=[pltpu.VMEM((1, LANES), F32)],
        compiler_params=_cparams(("arbitrary", "arbitrary")),
        name="moe_route",
    )(x1, mod_l, wr_hi, wr_lo, br)


def _dispatch_kernel(p_ref, hp_ref, xs_in_ref, xs_ref, sem):
    del xs_in_ref
    nb = p_ref.shape[2]

    def row_copy(t, p):
        return pltpu.make_async_copy(hp_ref.at[pl.ds(t, 1)], xs_ref.at[pl.ds(p, 1)], sem)

    def issue(t8, carry):
        for u in range(ROW_DMA_UNROLL):
            t = t8 * ROW_DMA_UNROLL + u
            row_copy(t, p_ref[0, 0, t]).start(priority=u % 2)
        return carry

    lax.fori_loop(0, nb // ROW_DMA_UNROLL, issue, 0)
    pltpu.make_async_copy(hp_ref, xs_ref.at[pl.ds(0, nb)], sem).wait()


def _dispatch(pos, hp, n_rows):
    T = hp.shape[0]
    nb = min(DISPATCH_BLOCK, T)
    blk = pl.BlockSpec((1, 1, nb), lambda i: (i, 0, 0), memory_space=pltpu.SMEM)
    xs0 = jnp.zeros((n_rows, ROW_WORDS), U32)
    return pl.pallas_call(
        _dispatch_kernel,
        grid=(T // nb,),
        in_specs=[blk, pl.BlockSpec((nb, ROW_WORDS), lambda i: (i, 0)), pl.BlockSpec(memory_space=pl.ANY)],
        out_specs=pl.BlockSpec(memory_space=pl.ANY),
        out_shape=jax.ShapeDtypeStruct((n_rows, ROW_WORDS), U32),
        scratch_shapes=[pltpu.SemaphoreType.DMA(())],
        input_output_aliases={2: 0},
        compiler_params=_cparams(("arbitrary",)),
        name="moe_dispatch",
    )(pos.reshape(T // nb, 1, nb), hp, xs0)


def _expert_kernel(ea_ref, eb_ref, nv_ref, xs_ref, wga_ref, wua_ref, wda_ref, wgb_ref, wub_ref, wdb_ref, ys_ref):
    del ea_ref, eb_ref

    @pl.when(pl.program_id(0) < nv_ref[0])
    def _():
        lo, hi = _unpack_halves(xs_ref[:, 0:HALF])
        lo = lo.astype(BF16)
        hi = hi.astype(BF16)
        wts = pltpu.bitcast(xs_ref[:, HALF:], F32)

        def hidden(wg_ref, wu_ref, w):
            g = (jnp.dot(lo, wg_ref[0:HALF, :], preferred_element_type=F32)
                 + jnp.dot(hi, wg_ref[HALF:, :], preferred_element_type=F32))
            u = (jnp.dot(lo, wu_ref[0:HALF, :], preferred_element_type=F32)
                 + jnp.dot(hi, wu_ref[HALF:, :], preferred_element_type=F32))
            return (g * jax.nn.sigmoid(g) * u * w).astype(BF16)

        y = (jnp.dot(hidden(wga_ref, wua_ref, wts[:, 0:1]), wda_ref[...], preferred_element_type=F32)
             + jnp.dot(hidden(wgb_ref, wub_ref, wts[:, 1:2]), wdb_ref[...], preferred_element_type=F32))
        ys_ref[...] = _pack_halves(y)


def _experts(tile_ea, tile_eb, n_valid, xs, w_gate, w_up, w_down, layer):
    n_rows = xs.shape[0]
    tm = EXPERT_TILE
    nt = n_rows // tm
    rows = lambda n, ea, eb, nv: (jnp.minimum(n, nv[0] - 1), 0)
    sel_a = lambda n, ea, eb, nv: (layer, ea[n], 0, 0)
    sel_b = lambda n, ea, eb, nv: (layer, eb[n], 0, 0)
    up = lambda sel: pl.BlockSpec((None, None, D_MODEL, D_EXPERT), sel)
    down = lambda sel: pl.BlockSpec((None, None, D_EXPERT, D_MODEL), sel)
    return pl.pallas_call(
        _expert_kernel,
        grid_spec=pltpu.PrefetchScalarGridSpec(
            num_scalar_prefetch=3,
            grid=(nt,),
            in_specs=[pl.BlockSpec((tm, ROW_WORDS), rows),
                      up(sel_a), up(sel_a), down(sel_a), up(sel_b), up(sel_b), down(sel_b)],
            out_specs=pl.BlockSpec((tm, HALF), rows),
        ),
        out_shape=jax.ShapeDtypeStruct((n_rows, HALF), U32),
        compiler_params=_cparams(("arbitrary",)),
        name="moe_experts",
    )(tile_ea, tile_eb, n_valid, xs, w_gate, w_up, w_down, w_gate, w_up, w_down)


def _combine_kernel(p_ref, pn_ref, ys_ref, x_ref, mod_ref, g_ref, b_ref, o_ref, buf, sems, *, alpha):
    D = D_MODEL
    tc = x_ref.shape[0]
    g = pl.program_id(0) * pl.num_programs(1) + pl.program_id(1)
    n_steps = pl.num_programs(0) * pl.num_programs(1)
    slot = g % 2

    def row_copy(p, s, t):
        return pltpu.make_async_copy(ys_ref.at[pl.ds(p, 1)], buf.at[s, pl.ds(t, 1)], sems.at[s])

    def gather(pa_ref, s):
        def issue(t8, carry):
            for u in range(ROW_DMA_UNROLL):
                t = t8 * ROW_DMA_UNROLL + u
                row_copy(pa_ref[0, 0, t], s, t).start(priority=u % 2)
            return carry

        lax.fori_loop(0, tc // ROW_DMA_UNROLL, issue, 0)

    @pl.when(g == 0)
    def _():
        gather(p_ref, 0)

    @pl.when(g + 1 < n_steps)
    def _():
        gather(pn_ref, 1 - slot)

    pltpu.make_async_copy(ys_ref.at[pl.ds(0, tc)], buf.at[slot], sems.at[slot]).wait()

    lo, hi = _unpack_halves(buf[slot])
    ffn = jnp.concatenate([lo, hi], axis=1)
    gt = mod_ref[:, 5 * D:6 * D]
    y = alpha * x_ref[...] + (1.0 + gt) * ffn
    o_ref[...] = _layer_norm_rows(y, g_ref[...], b_ref[...])


def _combine(pos, ys, x1, mod_l, ln_g, ln_b, alpha):
    B, S, D = x1.shape
    tc = min(COMBINE_TILE, S)
    nt = S // tc
    blk = pl.BlockSpec((1, 1, tc), lambda b, i: (b * nt + i, 0, 0), memory_space=pltpu.SMEM)
    nxt = pl.BlockSpec((1, 1, tc), lambda b, i: (jnp.minimum(b * nt + i + 1, B * nt - 1), 0, 0),
                       memory_space=pltpu.SMEM)
    row = lambda b, i: (b, i, 0)
    full2 = lambda a: pl.BlockSpec(a.shape, lambda b, i: (0,) * a.ndim)
    p = pos.reshape(B * nt, 1, tc)
    return pl.pallas_call(
        functools.partial(_combine_kernel, alpha=alpha),
        grid=(B, nt),
        in_specs=[
            blk, nxt,
            pl.BlockSpec(memory_space=pl.ANY),
            pl.BlockSpec((None, tc, D), row),
            pl.BlockSpec((None, 1, 6 * D), lambda b, i: (b, 0, 0)),
            full2(ln_g), full2(ln_b),
        ],
        out_specs=pl.BlockSpec((None, tc, D), row),
        out_shape=jax.ShapeDtypeStruct((B, S, D), F32),
        scratch_shapes=[pltpu.VMEM((2, tc, HALF), U32), pltpu.SemaphoreType.DMA((2,))],
        compiler_params=_cparams(("arbitrary", "arbitrary")),
        name="moe_combine_ln",
    )(p, p, ys, x1, mod_l, ln_g, ln_b)


def _rope_tables(S):
    inv = 1.0 / (ROPE_THETA ** (jnp.arange(0, QK_DIM, 2, dtype=F32) / QK_DIM))
    ang = jnp.arange(S, dtype=F32)[:, None] * inv[None, :]
    cos, sin = jnp.cos(ang), jnp.sin(ang)
    half = QK_DIM // 2
    first = (jnp.arange(LANES) % QK_DIM) < half
    cs = jnp.tile(cos, (1, LANES // half))
    sn = jnp.tile(sin, (1, LANES // half))
    sa = jnp.where(first[None, :], -sn, 0.0)
    sb = jnp.where(first[None, :], 0.0, sn)
    return cs, sa, sb


def _router_matrix(w_group, b_group, w_router, b_router):
    D = w_group.shape[0]
    w = jnp.zeros((D, LANES), F32)
    w = w.at[:, 0:N_GROUPS].set(w_group)
    wr = jnp.transpose(w_router, (1, 0, 2)).reshape(D, N_EXPERTS)
    w = w.at[:, ROUTE_LANE0:ROUTE_LANE0 + N_EXPERTS].set(wr)
    b = jnp.zeros((1, LANES), F32)
    b = b.at[0, 0:N_GROUPS].set(b_group)
    b = b.at[0, ROUTE_LANE0:ROUTE_LANE0 + N_EXPERTS].set(b_router.reshape(N_EXPERTS))
    hi = w.astype(BF16)
    lo = (w - hi.astype(F32)).astype(BF16)
    return hi, lo, b


def _routing_tables(rt, cnt, n_tiles):
    B, _, S = rt.shape
    tm = EXPERT_TILE
    bucket = rt[:, 0, :].astype(I32)
    rank = rt[:, 1, :].astype(I32)
    counts = cnt[0, 0:N_BUCKETS].astype(I32)
    tiles = (counts + tm - 1) // tm
    tile_end = jnp.cumsum(tiles)
    tile_start = tile_end - tiles
    onehot = bucket[..., None] == jnp.arange(N_BUCKETS, dtype=I32)
    pos = (jnp.sum(jnp.where(onehot, tile_start * tm, 0), axis=-1) + rank).reshape(B * S)
    tile_ids = jnp.arange(n_tiles, dtype=I32)
    tile_bucket = jnp.minimum(jnp.sum(tile_ids[:, None] >= tile_end[None, :], axis=1), N_BUCKETS - 1)
    pair_a = jnp.asarray([a for a in range(EXP_PER_GROUP) for _ in range(a + 1, EXP_PER_GROUP)], I32)
    pair_b = jnp.asarray([b for a in range(EXP_PER_GROUP) for b in range(a + 1, EXP_PER_GROUP)], I32)
    group = tile_bucket // N_PAIRS
    pair = tile_bucket % N_PAIRS
    tile_ea = (group * EXP_PER_GROUP + pair_a[pair]).astype(I32)
    tile_eb = (group * EXP_PER_GROUP + pair_b[pair]).astype(I32)
    n_valid = tile_end[-1:].astype(I32)
    return pos, tile_ea, tile_eb, n_valid


def kernel(x, c, w_ada, b_ada, w_in, lambda_q1, lambda_k1, lambda_q2, lambda_k2, subln_g, sg_ln_g, sg_ln_b, w_spatial, b_spatial, w_out, ln1_g, ln1_b, w_group, b_group, w_router, b_router, w_gate, w_up, w_down, ln2_g, ln2_b):
    B, S, D = x.shape
    depth = w_in.shape[0]
    T = B * S
    alpha = (2.0 * depth) ** 0.25
    n_tiles = T // EXPERT_TILE + N_BUCKETS
    n_rows = n_tiles * EXPERT_TILE

    mod = _ada(c, w_ada, b_ada)
    tabs = _rope_tables(S)
    w_gate, w_up, w_down = w_gate.astype(BF16), w_up.astype(BF16), w_down.astype(BF16)
    for l in range(depth):
        mod_l = mod[l].reshape(B, 1, 6 * D)
        lam_init = 0.8 - 0.6 * math.exp(-0.3 * l)

        qT, k, vTb, zu, zvn = _inproj(x, mod_l, w_in[l].astype(BF16), tabs,
                                      sg_ln_g[l].reshape(1, SG_WIDTH), sg_ln_b[l].reshape(1, SG_WIDTH))
        lams = (lambda_q1[l].reshape(1, QK_DIM), lambda_k1[l].reshape(1, QK_DIM),
                lambda_q2[l].reshape(1, QK_DIM), lambda_k2[l].reshape(1, QK_DIM))
        att = _attention(qT, k, vTb, lams, subln_g[l].reshape(V_DIM, 1), lam_init)
        b_sp_full = jnp.broadcast_to(b_spatial[l][:, :, None], (N_SG, CHUNK, SG_DIM))
        x1 = _mix(x, mod_l, att, zu, zvn, w_spatial[l], b_sp_full, w_out[l].astype(BF16),
                  ln1_g[l].reshape(1, D), ln1_b[l].reshape(1, D), alpha)

        wr_hi, wr_lo, br = _router_matrix(w_group[l], b_group[l], w_router[l], b_router[l])
        hp, rt, cnt = _route(x1, mod_l, wr_hi, wr_lo, br)
        pos, tile_ea, tile_eb, n_valid = _routing_tables(rt, cnt, n_tiles)
        xs = _dispatch(pos, hp.reshape(T, ROW_WORDS), n_rows)
        ys = _experts(tile_ea, tile_eb, n_valid, xs, w_gate, w_up, w_down, l)
        x = _combine(pos, ys, x1, mod_l, ln2_g[l].reshape(1, D), ln2_b[l].reshape(1, D), alpha)
    return x
```

```python
import functools
import math

import jax
import jax.numpy as jnp
from jax import lax
from jax.experimental import pallas as pl
from jax.experimental.pallas import tpu as pltpu

F32 = jnp.float32
BF16 = jnp.bfloat16
U32 = jnp.uint32
I32 = jnp.int32

D_MODEL = 1024
N_HEADS = 4
QK_DIM = 64
V_DIM = 128
HEAD_COLS = 2 * QK_DIM
QK_COLS = N_HEADS * HEAD_COLS
DIFF_WIDTH = N_HEADS * V_DIM
N_SG = 4
SG_DIM = 128
SG_WIDTH = N_SG * SG_DIM
CHUNK = 128
N_GROUPS = 4
EXP_PER_GROUP = 8
N_EXPERTS = N_GROUPS * EXP_PER_GROUP
N_PAIRS = EXP_PER_GROUP * (EXP_PER_GROUP - 1) // 2
N_BUCKETS = N_GROUPS * N_PAIRS
D_EXPERT = 512
ROPE_THETA = 10000.0
LN_EPS = 1e-5
LANES = 128
HALF = D_MODEL // 2
ROW_WORDS = HALF + LANES

ROW_TILE = 512
EXPERT_TILE = 256
DISPATCH_BLOCK = 2048
COMBINE_TILE = 512
SUBLANES = 8
ROUTE_LANE0 = 8
VMEM_LIMIT = 48 * 1024 * 1024

DEN_MIN = 2.0 ** -40
DEN_MAX = 2.0 ** 40
LOG2E = 1.4426950408889634
Q_SCALE = (QK_DIM ** -0.5) * LOG2E


def _cparams(sem):
    return pltpu.CompilerParams(dimension_semantics=sem, vmem_limit_bytes=VMEM_LIMIT)


def _layer_norm_rows(y, g, b):
    mu = jnp.mean(y, axis=-1, keepdims=True)
    yc = y - mu
    var = jnp.mean(yc * yc, axis=-1, keepdims=True)
    return yc * lax.rsqrt(var + LN_EPS) * g + b


def _gelu(x):
    return 0.5 * x * (1.0 + lax.erf(x * (2.0 ** -0.5)))


def _pack_halves(y):
    lo = pltpu.bitcast(y[:, :HALF].astype(BF16).astype(F32), U32) >> 16
    hi = pltpu.bitcast(y[:, HALF:].astype(BF16).astype(F32), U32) & jnp.uint32(0xFFFF0000)
    return lo | hi


def _unpack_halves(p):
    lo = pltpu.bitcast(p << 16, F32)
    hi = pltpu.bitcast(p & jnp.uint32(0xFFFF0000), F32)
    return lo, hi


def _ada_kernel(c_ref, w_ref, b_ref, o_ref):
    c = c_ref[...]
    sc = c * jax.nn.sigmoid(c)
    o_ref[...] = jnp.dot(sc, w_ref[...], precision=lax.Precision.HIGHEST,
                         preferred_element_type=F32) + b_ref[...]


def _ada(c, w_ada, b_ada):
    L, D, N = w_ada.shape
    B = c.shape[0]
    tn = 1536
    return pl.pallas_call(
        _ada_kernel,
        grid=(L, N // tn),
        in_specs=[
            pl.BlockSpec((B, D), lambda l, j: (0, 0)),
            pl.BlockSpec((None, D, tn), lambda l, j: (l, 0, j)),
            pl.BlockSpec((None, 1, tn), lambda l, j: (l, 0, j)),
        ],
        out_specs=pl.BlockSpec((None, B, tn), lambda l, j: (l, 0, j)),
        out_shape=jax.ShapeDtypeStruct((L, B, N), F32),
        compiler_params=_cparams(("arbitrary", "arbitrary")),
        name="ada",
    )(c, w_ada, b_ada.reshape(L, 1, N))


def _inproj_kernel(x_ref, mod_ref, w_ref, cs_ref, sa_ref, sb_ref, lng_ref, lnb_ref,
                   qT_ref, k_ref, vT_ref, zu_ref, zvn_ref):
    D = D_MODEL
    x = x_ref[...]
    sh = mod_ref[:, 0:D]
    sc = mod_ref[:, D:2 * D]
    h = (x * (1.0 + sc) + sh).astype(BF16)
    cs = cs_ref[...]
    sa = sa_ref[...]
    sb = sb_ref[...]

    def rope(t):
        return t * cs + pltpu.roll(t, 96, 1) * sa + pltpu.roll(t, 32, 1) * sb

    q = jnp.dot(h, w_ref[:, 0:QK_COLS], preferred_element_type=F32)
    for j in range(N_HEADS):
        sl = slice(j * LANES, (j + 1) * LANES)
        qT_ref[sl, :] = (rope(q[:, sl]) * Q_SCALE).T.astype(BF16)
    k = jnp.dot(h, w_ref[:, QK_COLS:2 * QK_COLS], preferred_element_type=F32)
    for j in range(N_HEADS):
        sl = slice(j * LANES, (j + 1) * LANES)
        k_ref[:, sl] = rope(k[:, sl]).astype(BF16)
    c0 = 2 * QK_COLS
    v = jnp.dot(h, w_ref[:, c0:c0 + DIFF_WIDTH], preferred_element_type=F32)
    vT_ref[...] = v.T.astype(BF16)
    c0 += DIFF_WIDTH
    u = jnp.dot(h, w_ref[:, c0:c0 + SG_WIDTH], preferred_element_type=F32)
    zu_ref[...] = _gelu(u).astype(BF16)
    c0 += SG_WIDTH
    z = _gelu(jnp.dot(h, w_ref[:, c0:c0 + SG_WIDTH], preferred_element_type=F32))
    for g in range(N_SG):
        sl = slice(g * SG_DIM, (g + 1) * SG_DIM)
        zvn_ref[:, sl] = _layer_norm_rows(z[:, sl], lng_ref[:, sl], lnb_ref[:, sl]).astype(BF16)


def _inproj(x, mod_l, w_in_b, rope_tabs, ln_g, ln_b):
    B, S, D = x.shape
    tm = min(ROW_TILE, S)
    nt = S // tm
    cs, sa, sb = rope_tabs
    row = lambda b, i: (b, i, 0)
    tab = pl.BlockSpec((tm, LANES), lambda b, i: (i, 0))
    return pl.pallas_call(
        _inproj_kernel,
        grid=(B, nt),
        in_specs=[
            pl.BlockSpec((None, tm, D), row),
            pl.BlockSpec((None, 1, 6 * D), lambda b, i: (b, 0, 0)),
            pl.BlockSpec(w_in_b.shape, lambda b, i: (0, 0)),
            tab, tab, tab,
            pl.BlockSpec((1, SG_WIDTH), lambda b, i: (0, 0)),
            pl.BlockSpec((1, SG_WIDTH), lambda b, i: (0, 0)),
        ],
        out_specs=[
            pl.BlockSpec((None, QK_COLS, tm), lambda b, i: (b, 0, i)),
            pl.BlockSpec((None, tm, QK_COLS), row),
            pl.BlockSpec((None, None, DIFF_WIDTH, tm), lambda b, i: (b, i, 0, 0)),
            pl.BlockSpec((None, tm, SG_WIDTH), row),
            pl.BlockSpec((None, tm, SG_WIDTH), row),
        ],
        out_shape=[
            jax.ShapeDtypeStruct((B, QK_COLS, S), BF16),
            jax.ShapeDtypeStruct((B, S, QK_COLS), BF16),
            jax.ShapeDtypeStruct((B, nt, DIFF_WIDTH, tm), BF16),
            jax.ShapeDtypeStruct((B, S, SG_WIDTH), BF16),
            jax.ShapeDtypeStruct((B, S, SG_WIDTH), BF16),
        ],
        compiler_params=_cparams(("arbitrary", "arbitrary")),
        name="inproj",
    )(x, mod_l, w_in_b, cs, sa, sb, ln_g, ln_b)


def _attn_kernel(lq1_ref, lk1_ref, lq2_ref, lk2_ref, g_ref, qT_ref, k_ref, vT_ref, o_ref,
                 q_scr, s_scr, m_scr, l_scr, acc_scr, *, lam_init):
    tq = qT_ref.shape[1]
    tk = vT_ref.shape[2]
    i = pl.program_id(2)
    qT = qT_ref[...]
    rows = lax.broadcasted_iota(I32, qT.shape, 0)
    zero = jnp.zeros_like(qT)
    q_scr[0] = jnp.where(rows < QK_DIM, qT, zero)
    q_scr[1] = jnp.where(rows >= QK_DIM, qT, zero)

    def scores(j, slot, c0):
        kb = k_ref[pl.ds(pl.multiple_of(j * tk, tk), tk), :]
        for mp in range(2):
            s_scr[slot, mp, :, c0:] = jnp.dot(kb, q_scr[mp, :, c0:], preferred_element_type=F32)

    def block_scores(slot, mp, c0, c1, masked):
        s = s_scr[slot, mp, :, c0:c1]
        if masked:
            kpos = lax.broadcasted_iota(I32, s.shape, 0)
            qpos = lax.broadcasted_iota(I32, s.shape, 1)
            s = jnp.where(kpos <= qpos, s, -jnp.inf)
        return s

    def pv_unshifted(j, slot, c0, c1, masked):
        vb = vT_ref[j]
        for mp in range(2):
            p = jnp.exp2(block_scores(slot, mp, c0, c1, masked))
            l_scr[mp, :, c0:c1] += jnp.sum(p, axis=0, keepdims=True)
            acc_scr[mp, :, c0:c1] += jnp.dot(vb, p.astype(BF16), preferred_element_type=F32)

    def pv_online(j, slot, c0, c1, masked):
        vb = vT_ref[j]
        for mp in range(2):
            s = block_scores(slot, mp, c0, c1, masked)
            m_old = m_scr[mp, :, c0:c1]
            m_new = jnp.maximum(m_old, jnp.max(s, axis=0, keepdims=True))
            alpha = jnp.exp2(m_old - m_new)
            p = jnp.exp2(s - m_new)
            l_scr[mp, :, c0:c1] = alpha * l_scr[mp, :, c0:c1] + jnp.sum(p, axis=0, keepdims=True)
            acc_scr[mp, :, c0:c1] = (alpha * acc_scr[mp, :, c0:c1]
                                     + jnp.dot(vb, p.astype(BF16), preferred_element_type=F32))
            m_scr[mp, :, c0:c1] = m_new

    def sweep(block):
        acc_scr[...] = jnp.zeros(acc_scr.shape, F32)
        l_scr[...] = jnp.zeros(l_scr.shape, F32)
        scores(0, 0, 0)

        def pair(t):
            scores(t, 1, 0)
            block(t - 1, 0, 0, tq, False)
            scores(t + 1, 0, 0)
            block(t, 1, 0, tq, False)

        def body(u, carry):
            pair(4 * u + 1)
            pair(4 * u + 3)
            return carry

        lax.fori_loop(0, i // 2, body, 0)

        @pl.when(i % 2 == 1)
        def _():
            pair(2 * i - 1)

        scores(2 * i + 1, 1, tk)
        block(2 * i, 0, 0, tk, True)
        block(2 * i, 0, tk, tq, False)
        block(2 * i + 1, 1, tk, tq, True)

    sweep(pv_unshifted)
    den = jnp.concatenate([l_scr[0], l_scr[1]], axis=0)
    in_range = (den >= DEN_MIN) & (den <= DEN_MAX)
    n_bad = jnp.sum(jnp.where(in_range, 0.0, 1.0))

    @pl.when(n_bad > 0.0)
    def _():
        m_scr[...] = jnp.full(m_scr.shape, -jnp.inf, F32)
        sweep(pv_online)

    lam = (jnp.exp(jnp.sum(lq1_ref[...] * lk1_ref[...], axis=1, keepdims=True))
           - jnp.exp(jnp.sum(lq2_ref[...] * lk2_ref[...], axis=1, keepdims=True)) + lam_init)
    out = acc_scr[0] / l_scr[0] - lam * (acc_scr[1] / l_scr[1])
    ms = jnp.mean(out * out, axis=0, keepdims=True)
    y = out * lax.rsqrt(ms + LN_EPS) * g_ref[...] * (1.0 - lam_init)
    o_ref[...] = y.T.astype(BF16)


def _attention(qT, k, vTb, lams, g_col, lam_init):
    B, _, S = qT.shape
    nkv, tk = vTb.shape[1], vTb.shape[3]
    tq = 2 * tk
    small = pl.BlockSpec((1, QK_DIM), lambda b, h, i: (0, 0))
    return pl.pallas_call(
        functools.partial(_attn_kernel, lam_init=lam_init),
        grid=(B, N_HEADS, S // tq),
        in_specs=[
            small, small, small, small,
            pl.BlockSpec((V_DIM, 1), lambda b, h, i: (0, 0)),
            pl.BlockSpec((None, HEAD_COLS, tq), lambda b, h, i: (b, h, i)),
            pl.BlockSpec((None, S, HEAD_COLS), lambda b, h, i: (b, 0, h)),
            pl.BlockSpec((None, nkv, V_DIM, tk), lambda b, h, i: (b, 0, h, 0)),
        ],
        out_specs=pl.BlockSpec((None, tq, V_DIM), lambda b, h, i: (b, i, h)),
        out_shape=jax.ShapeDtypeStruct((B, S, DIFF_WIDTH), BF16),
        scratch_shapes=[
            pltpu.VMEM((2, HEAD_COLS, tq), BF16),
            pltpu.VMEM((2, 2, tk, tq), F32),
            pltpu.VMEM((2, 1, tq), F32),
            pltpu.VMEM((2, 1, tq), F32),
            pltpu.VMEM((2, V_DIM, tq), F32),
        ],
        compiler_params=_cparams(("arbitrary", "arbitrary", "arbitrary")),
        name="diff_attn",
    )(*lams, g_col, qT, k, vTb)


def _mix_kernel(x_ref, mod_ref, att_ref, zu_ref, zvn_ref, wsp_ref, bsp_ref, wo_ref, g_ref, b_ref,
                o_ref, sg_scr, *, alpha):
    D = D_MODEL
    nc = zu_ref.shape[0]
    r = lax.broadcasted_iota(I32, (CHUNK, CHUNK), 0)
    c = lax.broadcasted_iota(I32, (CHUNK, CHUNK), 1)
    causal = r >= c
    for g in range(N_SG):
        sl = slice(g * SG_DIM, (g + 1) * SG_DIM)
        w = jnp.where(causal, wsp_ref[g], 0.0).astype(BF16)
        z = jnp.concatenate([zvn_ref[n, :, sl] for n in range(nc)], axis=1)
        mixed = jnp.dot(w, z, preferred_element_type=F32)
        for n in range(nc):
            gate = mixed[:, n * SG_DIM:(n + 1) * SG_DIM] + bsp_ref[g]
            sg_scr[n * CHUNK:(n + 1) * CHUNK, sl] = (zu_ref[n, :, sl].astype(F32) * gate).astype(BF16)
    mix = (jnp.dot(att_ref[...], wo_ref[0:DIFF_WIDTH, :], preferred_element_type=F32)
           + jnp.dot(sg_scr[...], wo_ref[DIFF_WIDTH:, :], preferred_element_type=F32))
    gt = mod_ref[:, 2 * D:3 * D]
    y = alpha * x_ref[...] + (1.0 + gt) * mix
    o_ref[...] = _layer_norm_rows(y, g_ref[...], b_ref[...])


def _mix(x, mod_l, att, zu, zvn, w_sp, b_sp_full, w_out_b, ln_g, ln_b, alpha):
    B, S, D = x.shape
    tm = min(ROW_TILE, S)
    nc = tm // CHUNK
    row = lambda b, i: (b, i, 0)
    zu4 = zu.reshape(B, S // CHUNK, CHUNK, SG_WIDTH)
    zvn4 = zvn.reshape(B, S // CHUNK, CHUNK, SG_WIDTH)
    chunked = pl.BlockSpec((None, nc, CHUNK, SG_WIDTH), lambda b, i: (b, i, 0, 0))
    full2 = lambda a: pl.BlockSpec(a.shape, lambda b, i: (0,) * a.ndim)
    return pl.pallas_call(
        functools.partial(_mix_kernel, alpha=alpha),
        grid=(B, S // tm),
        in_specs=[
            pl.BlockSpec((None, tm, D), row),
            pl.BlockSpec((None, 1, 6 * D), lambda b, i: (b, 0, 0)),
            pl.BlockSpec((None, tm, DIFF_WIDTH), row),
            chunked, chunked,
            full2(w_sp), full2(b_sp_full), full2(w_out_b), full2(ln_g), full2(ln_b),
        ],
        out_specs=pl.BlockSpec((None, tm, D), row),
        out_shape=jax.ShapeDtypeStruct((B, S, D), F32),
        scratch_shapes=[pltpu.VMEM((tm, SG_WIDTH), BF16)],
        compiler_params=_cparams(("arbitrary", "arbitrary")),
        name="sgate_outproj_ln",
    )(x, mod_l, att, zu4, zvn4, w_sp, b_sp_full, w_out_b, ln_g, ln_b)


def _route_kernel(x_ref, mod_ref, wr_hi_ref, wr_lo_ref, br_ref, hp_ref, rt_ref, cnt_ref, run_scr):
    D = D_MODEL
    tm = x_ref.shape[0]

    @pl.when((pl.program_id(0) == 0) & (pl.program_id(1) == 0))
    def _():
        run_scr[...] = jnp.zeros(run_scr.shape, F32)

    sh = mod_ref[:, 3 * D:4 * D]
    sc = mod_ref[:, 4 * D:5 * D]
    h = x_ref[...] * (1.0 + sc) + sh
    h_hi = h.astype(BF16)
    h_lo = (h - h_hi.astype(F32)).astype(BF16)
    logit = (jnp.dot(h_hi, wr_hi_ref[...], preferred_element_type=F32)
             + jnp.dot(h_lo, wr_hi_ref[...], preferred_element_type=F32)
             + jnp.dot(h_hi, wr_lo_ref[...], preferred_element_type=F32)) + br_ref[...]
    lane = lax.broadcasted_iota(I32, logit.shape, 1).astype(F32)
    neg = -jnp.inf
    big = float(LANES)

    def first_argmax(v):
        mx = jnp.max(v, axis=1, keepdims=True)
        idx = jnp.min(jnp.where(v == mx, lane, big), axis=1, keepdims=True)
        return mx, idx

    in_grp = lane < N_GROUPS
    gmax, gidx = first_argmax(jnp.where(in_grp, logit, neg))
    g_p = 1.0 / jnp.sum(jnp.where(in_grp, jnp.exp(logit - gmax), 0.0), axis=1, keepdims=True)
    lo_lane = ROUTE_LANE0 + EXP_PER_GROUP * gidx
    sel = jnp.where((lane >= lo_lane) & (lane < lo_lane + EXP_PER_GROUP), logit, neg)
    v1, i1 = first_argmax(sel)
    v2, i2 = first_argmax(jnp.where(lane == i1, neg, sel))
    t = jnp.exp(v2 - v1)
    w1 = g_p / (1.0 + t)
    w2 = g_p * t / (1.0 + t)

    first_lower = i1 < i2
    e_a = jnp.minimum(i1, i2) - lo_lane
    e_b = jnp.maximum(i1, i2) - lo_lane
    w_a = jnp.where(first_lower, w1, w2)
    w_b = jnp.where(first_lower, w2, w1)
    bucket = gidx * N_PAIRS + (e_a * (2 * EXP_PER_GROUP - 1 - e_a) * 0.5 + (e_b - e_a - 1.0))

    hp_ref[:, 0:HALF] = _pack_halves(h)
    wslab = jnp.where(lane == 0.0, w_a, jnp.where(lane == 1.0, w_b, 0.0))
    hp_ref[:, HALF:] = pltpu.bitcast(wslab, U32)

    hot = lane == bucket
    onehot = jnp.where(hot, 1.0, 0.0)
    r = lax.broadcasted_iota(I32, (tm, tm), 0)
    c = lax.broadcasted_iota(I32, (tm, tm), 1)
    strict = jnp.where(r > c, 1.0, 0.0).astype(BF16)
    before = jnp.dot(strict, onehot.astype(BF16), preferred_element_type=F32) + run_scr[...]
    rank = jnp.sum(jnp.where(hot, before, 0.0), axis=1, keepdims=True)
    run_scr[...] = run_scr[...] + jnp.sum(onehot, axis=0, keepdims=True)
    cnt_ref[...] = run_scr[...]

    slab = jnp.where(lane == 0.0, bucket, jnp.where(lane == 1.0, rank, 0.0))
    rt_ref[...] = slab.T[0:8, :]


def _route(x1, mod_l, wr_hi, wr_lo, br):
    B, S, D = x1.shape
    tm = min(ROW_TILE, S)
    row = lambda b, i: (b, i, 0)
    full2 = lambda a: pl.BlockSpec(a.shape, lambda b, i: (0,) * a.ndim)
    return pl.pallas_call(
        _route_kernel,
        grid=(B, S // tm),
        in_specs=[
            pl.BlockSpec((None, tm, D), row),
            pl.BlockSpec((None, 1, 6 * D), lambda b, i: (b, 0, 0)),
            full2(wr_hi), full2(wr_lo), full2(br),
        ],
        out_specs=[
            pl.BlockSpec((None, tm, ROW_WORDS), row),
            pl.BlockSpec((None, 8, tm), lambda b, i: (b, 0, i)),
            pl.BlockSpec((1, LANES), lambda b, i: (0, 0)),
        ],
        out_shape=[
            jax.ShapeDtypeStruct((B, S, ROW_WORDS), U32),
            jax.ShapeDtypeStruct((B, 8, S), F32),
            jax.ShapeDtypeStruct((1, LANES), F32),
        ],
        scratch_shapes=[pltpu.VMEM((1, LANES), F32)],
        compiler_params=_cparams(("arbitrary", "arbitrary")),
        name="moe_route",
    )(x1, mod_l, wr_hi, wr_lo, br)


def _row_of(ref, p):
    return ref.at[p >> 3, pl.ds(p & (SUBLANES - 1), 1)]


def _dispatch_kernel(p_ref, hp_ref, xs_in_ref, xs_ref, sem):
    del xs_in_ref
    ng = hp_ref.shape[0]

    def issue(g, carry):
        for u in range(SUBLANES):
            p = p_ref[0, 0, g * SUBLANES + u]
            pltpu.make_async_copy(hp_ref.at[g, pl.ds(u, 1)], _row_of(xs_ref, p), sem).start(priority=u % 2)
        return carry

    lax.fori_loop(0, ng, issue, 0)
    pltpu.make_async_copy(hp_ref, xs_ref.at[pl.ds(0, ng)], sem).wait()


def _dispatch(pos, hp, n_rows):
    T = hp.shape[0]
    nb = min(DISPATCH_BLOCK, T)
    blk = pl.BlockSpec((1, 1, nb), lambda i: (i, 0, 0), memory_space=pltpu.SMEM)
    xs0 = jnp.zeros((n_rows // SUBLANES, SUBLANES, ROW_WORDS), U32)
    xs = pl.pallas_call(
        _dispatch_kernel,
        grid=(T // nb,),
        in_specs=[blk, pl.BlockSpec((nb // SUBLANES, SUBLANES, ROW_WORDS), lambda i: (i, 0, 0)),
                  pl.BlockSpec(memory_space=pl.ANY)],
        out_specs=pl.BlockSpec(memory_space=pl.ANY),
        out_shape=jax.ShapeDtypeStruct(xs0.shape, U32),
        scratch_shapes=[pltpu.SemaphoreType.DMA(())],
        input_output_aliases={2: 0},
        compiler_params=_cparams(("arbitrary",)),
        name="moe_dispatch",
    )(pos.reshape(T // nb, 1, nb), hp.reshape(T // SUBLANES, SUBLANES, ROW_WORDS), xs0)
    return xs.reshape(n_rows, ROW_WORDS)


def _expert_kernel(ea_ref, eb_ref, nv_ref, xs_ref, wga_ref, wua_ref, wda_ref, wgb_ref, wub_ref, wdb_ref, ys_ref):
    del ea_ref, eb_ref

    @pl.when(pl.program_id(0) < nv_ref[0])
    def _():
        lo, hi = _unpack_halves(xs_ref[:, 0:HALF])
        lo = lo.astype(BF16)
        hi = hi.astype(BF16)
        wts = pltpu.bitcast(xs_ref[:, HALF:], F32)

        def hidden(wg_ref, wu_ref, w):
            g = (jnp.dot(lo, wg_ref[0:HALF, :], preferred_element_type=F32)
                 + jnp.dot(hi, wg_ref[HALF:, :], preferred_element_type=F32))
            u = (jnp.dot(lo, wu_ref[0:HALF, :], preferred_element_type=F32)
                 + jnp.dot(hi, wu_ref[HALF:, :], preferred_element_type=F32))
            return (g * jax.nn.sigmoid(g) * u * w).astype(BF16)

        y = (jnp.dot(hidden(wga_ref, wua_ref, wts[:, 0:1]), wda_ref[...], preferred_element_type=F32)
             + jnp.dot(hidden(wgb_ref, wub_ref, wts[:, 1:2]), wdb_ref[...], preferred_element_type=F32))
        ys_ref[...] = _pack_halves(y)


def _experts(tile_ea, tile_eb, n_valid, xs, w_gate, w_up, w_down, layer):
    n_rows = xs.shape[0]
    tm = EXPERT_TILE
    nt = n_rows // tm
    rows = lambda n, ea, eb, nv: (jnp.minimum(n, nv[0] - 1), 0)
    sel_a = lambda n, ea, eb, nv: (layer, ea[n], 0, 0)
    sel_b = lambda n, ea, eb, nv: (layer, eb[n], 0, 0)
    up = lambda sel: pl.BlockSpec((None, None, D_MODEL, D_EXPERT), sel)
    down = lambda sel: pl.BlockSpec((None, None, D_EXPERT, D_MODEL), sel)
    return pl.pallas_call(
        _expert_kernel,
        grid_spec=pltpu.PrefetchScalarGridSpec(
            num_scalar_prefetch=3,
            grid=(nt,),
            in_specs=[pl.BlockSpec((tm, ROW_WORDS), rows),
                      up(sel_a), up(sel_a), down(sel_a), up(sel_b), up(sel_b), down(sel_b)],
            out_specs=pl.BlockSpec((tm, HALF), rows),
        ),
        out_shape=jax.ShapeDtypeStruct((n_rows, HALF), U32),
        compiler_params=_cparams(("arbitrary",)),
        name="moe_experts",
    )(tile_ea, tile_eb, n_valid, xs, w_gate, w_up, w_down, w_gate, w_up, w_down)


def _combine_kernel(p_ref, pn_ref, ys_ref, x_ref, mod_ref, g_ref, b_ref, o_ref, buf, sems, *, alpha):
    D = D_MODEL
    tc = x_ref.shape[0]
    g = pl.program_id(0) * pl.num_programs(1) + pl.program_id(1)
    n_steps = pl.num_programs(0) * pl.num_programs(1)
    slot = g % 2

    def gather(pa_ref, s):
        def issue(g8, carry):
            for u in range(SUBLANES):
                p = pa_ref[0, 0, g8 * SUBLANES + u]
                pltpu.make_async_copy(_row_of(ys_ref, p), buf.at[s, g8, pl.ds(u, 1)], sems.at[s]).start(
                    priority=u % 2)
            return carry

        lax.fori_loop(0, tc // SUBLANES, issue, 0)

    @pl.when(g == 0)
    def _():
        gather(p_ref, 0)

    @pl.when(g + 1 < n_steps)
    def _():
        gather(pn_ref, 1 - slot)

    pltpu.make_async_copy(ys_ref.at[pl.ds(0, tc // SUBLANES)], buf.at[slot], sems.at[slot]).wait()

    lo, hi = _unpack_halves(buf[slot].reshape(tc, HALF))
    ffn = jnp.concatenate([lo, hi], axis=1)
    gt = mod_ref[:, 5 * D:6 * D]
    y = alpha * x_ref[...] + (1.0 + gt) * ffn
    o_ref[...] = _layer_norm_rows(y, g_ref[...], b_ref[...])


def _combine(pos, ys, x1, mod_l, ln_g, ln_b, alpha):
    B, S, D = x1.shape
    tc = min(COMBINE_TILE, S)
    nt = S // tc
    blk = pl.BlockSpec((1, 1, tc), lambda b, i: (b * nt + i, 0, 0), memory_space=pltpu.SMEM)
    nxt = pl.BlockSpec((1, 1, tc), lambda b, i: (jnp.minimum(b * nt + i + 1, B * nt - 1), 0, 0),
                       memory_space=pltpu.SMEM)
    row = lambda b, i: (b, i, 0)
    full2 = lambda a: pl.BlockSpec(a.shape, lambda b, i: (0,) * a.ndim)
    p = pos.reshape(B * nt, 1, tc)
    return pl.pallas_call(
        functools.partial(_combine_kernel, alpha=alpha),
        grid=(B, nt),
        in_specs=[
            blk, nxt,
            pl.BlockSpec(memory_space=pl.ANY),
            pl.BlockSpec((None, tc, D), row),
            pl.BlockSpec((None, 1, 6 * D), lambda b, i: (b, 0, 0)),
            full2(ln_g), full2(ln_b),
        ],
        out_specs=pl.BlockSpec((None, tc, D), row),
        out_shape=jax.ShapeDtypeStruct((B, S, D), F32),
        scratch_shapes=[pltpu.VMEM((2, tc // SUBLANES, SUBLANES, HALF), U32), pltpu.SemaphoreType.DMA((2,))],
        compiler_params=_cparams(("arbitrary", "arbitrary")),
        name="moe_combine_ln",
    )(p, p, ys.reshape(ys.shape[0] // SUBLANES, SUBLANES, HALF), x1, mod_l, ln_g, ln_b)


def _rope_tables(S):
    inv = 1.0 / (ROPE_THETA ** (jnp.arange(0, QK_DIM, 2, dtype=F32) / QK_DIM))
    ang = jnp.arange(S, dtype=F32)[:, None] * inv[None, :]
    cos, sin = jnp.cos(ang), jnp.sin(ang)
    half = QK_DIM // 2
    first = (jnp.arange(LANES) % QK_DIM) < half
    cs = jnp.tile(cos, (1, LANES // half))
    sn = jnp.tile(sin, (1, LANES // half))
    sa = jnp.where(first[None, :], -sn, 0.0)
    sb = jnp.where(first[None, :], 0.0, sn)
    return cs, sa, sb


def _router_matrix(w_group, b_group, w_router, b_router):
    D = w_group.shape[0]
    w = jnp.zeros((D, LANES), F32)
    w = w.at[:, 0:N_GROUPS].set(w_group)
    wr = jnp.transpose(w_router, (1, 0, 2)).reshape(D, N_EXPERTS)
    w = w.at[:, ROUTE_LANE0:ROUTE_LANE0 + N_EXPERTS].set(wr)
    b = jnp.zeros((1, LANES), F32)
    b = b.at[0, 0:N_GROUPS].set(b_group)
    b = b.at[0, ROUTE_LANE0:ROUTE_LANE0 + N_EXPERTS].set(b_router.reshape(N_EXPERTS))
    hi = w.astype(BF16)
    lo = (w - hi.astype(F32)).astype(BF16)
    return hi, lo, b


def _routing_tables(rt, cnt, n_tiles):
    B, _, S = rt.shape
    tm = EXPERT_TILE
    bucket = rt[:, 0, :].astype(I32)
    rank = rt[:, 1, :].astype(I32)
    counts = cnt[0, 0:N_BUCKETS].astype(I32)
    tiles = (counts + tm - 1) // tm
    tile_end = jnp.cumsum(tiles)
    tile_start = tile_end - tiles
    onehot = bucket[..., None] == jnp.arange(N_BUCKETS, dtype=I32)
    pos = (jnp.sum(jnp.where(onehot, tile_start * tm, 0), axis=-1) + rank).reshape(B * S)
    tile_ids = jnp.arange(n_tiles, dtype=I32)
    tile_bucket = jnp.minimum(jnp.sum(tile_ids[:, None] >= tile_end[None, :], axis=1), N_BUCKETS - 1)
    pair_a = jnp.asarray([a for a in range(EXP_PER_GROUP) for _ in range(a + 1, EXP_PER_GROUP)], I32)
    pair_b = jnp.asarray([b for a in range(EXP_PER_GROUP) for b in range(a + 1, EXP_PER_GROUP)], I32)
    group = tile_bucket // N_PAIRS
    pair = tile_bucket % N_PAIRS
    tile_ea = (group * EXP_PER_GROUP + pair_a[pair]).astype(I32)
    tile_eb = (group * EXP_PER_GROUP + pair_b[pair]).astype(I32)
    n_valid = tile_end[-1:].astype(I32)
    return pos, tile_ea, tile_eb, n_valid


def kernel(x, c, w_ada, b_ada, w_in, lambda_q1, lambda_k1, lambda_q2, lambda_k2, subln_g, sg_ln_g, sg_ln_b, w_spatial, b_spatial, w_out, ln1_g, ln1_b, w_group, b_group, w_router, b_router, w_gate, w_up, w_down, ln2_g, ln2_b):
    B, S, D = x.shape
    depth = w_in.shape[0]
    T = B * S
    alpha = (2.0 * depth) ** 0.25
    n_tiles = T // EXPERT_TILE + N_BUCKETS
    n_rows = n_tiles * EXPERT_TILE

    mod = _ada(c, w_ada, b_ada)
    tabs = _rope_tables(S)
    w_gate, w_up, w_down = w_gate.astype(BF16), w_up.astype(BF16), w_down.astype(BF16)
    for l in range(depth):
        mod_l = mod[l].reshape(B, 1, 6 * D)
        lam_init = 0.8 - 0.6 * math.exp(-0.3 * l)

        qT, k, vTb, zu, zvn = _inproj(x, mod_l, w_in[l].astype(BF16), tabs,
                                      sg_ln_g[l].reshape(1, SG_WIDTH), sg_ln_b[l].reshape(1, SG_WIDTH))
        lams = (lambda_q1[l].reshape(1, QK_DIM), lambda_k1[l].reshape(1, QK_DIM),
                lambda_q2[l].reshape(1, QK_DIM), lambda_k2[l].reshape(1, QK_DIM))
        att = _attention(qT, k, vTb, lams, subln_g[l].reshape(V_DIM, 1), lam_init)
        b_sp_full = jnp.broadcast_to(b_spatial[l][:, :, None], (N_SG, CHUNK, SG_DIM))
        x1 = _mix(x, mod_l, att, zu, zvn, w_spatial[l], b_sp_full, w_out[l].astype(BF16),
                  ln1_g[l].reshape(1, D), ln1_b[l].reshape(1, D), alpha)

        wr_hi, wr_lo, br = _router_matrix(w_group[l], b_group[l], w_router[l], b_router[l])
        hp, rt, cnt = _route(x1, mod_l, wr_hi, wr_lo, br)
        pos, tile_ea, tile_eb, n_valid = _routing_tables(rt, cnt, n_tiles)
        xs = _dispatch(pos, hp.reshape(T, ROW_WORDS), n_rows)
        ys = _experts(tile_ea, tile_eb, n_valid, xs, w_gate, w_up, w_down, l)
        x = _combine(pos, ys, x1, mod_l, ln2_g[l].reshape(1, D), ln2_b[l].reshape(1, D), alpha)
    return x
```

```python
import functools
import math

import jax
import jax.numpy as jnp
from jax import lax
from jax.experimental import pallas as pl
from jax.experimental.pallas import tpu as pltpu

F32 = jnp.float32
BF16 = jnp.bfloat16
U32 = jnp.uint32
I32 = jnp.int32

D_MODEL = 1024
N_HEADS = 4
QK_DIM = 64
V_DIM = 128
HEAD_COLS = 2 * QK_DIM
QK_COLS = N_HEADS * HEAD_COLS
DIFF_WIDTH = N_HEADS * V_DIM
N_SG = 4
SG_DIM = 128
SG_WIDTH = N_SG * SG_DIM
CHUNK = 128
N_GROUPS = 4
EXP_PER_GROUP = 8
N_EXPERTS = N_GROUPS * EXP_PER_GROUP
N_PAIRS = EXP_PER_GROUP * (EXP_PER_GROUP - 1) // 2
N_BUCKETS = N_GROUPS * N_PAIRS
D_EXPERT = 512
ROPE_THETA = 10000.0
LN_EPS = 1e-5
LANES = 128
HALF = D_MODEL // 2
ROW_WORDS = HALF + LANES

ROW_TILE = 512
EXPERT_TILE = 256
DISPATCH_BLOCK = 2048
COMBINE_TILE = 512
SUBLANES = 8
ROUTE_LANE0 = 8
VMEM_LIMIT = 48 * 1024 * 1024

DEN_MIN = 2.0 ** -40
DEN_MAX = 2.0 ** 40
LOG2E = 1.4426950408889634
Q_SCALE = (QK_DIM ** -0.5) * LOG2E


def _cparams(sem):
    return pltpu.CompilerParams(dimension_semantics=sem, vmem_limit_bytes=VMEM_LIMIT)


def _layer_norm_rows(y, g, b):
    mu = jnp.mean(y, axis=-1, keepdims=True)
    yc = y - mu
    var = jnp.mean(yc * yc, axis=-1, keepdims=True)
    return yc * lax.rsqrt(var + LN_EPS) * g + b


def _gelu(x):
    return 0.5 * x * (1.0 + lax.erf(x * (2.0 ** -0.5)))


def _pack_halves(y):
    lo = pltpu.bitcast(y[:, :HALF].astype(BF16).astype(F32), U32) >> 16
    hi = pltpu.bitcast(y[:, HALF:].astype(BF16).astype(F32), U32) & jnp.uint32(0xFFFF0000)
    return lo | hi


def _unpack_halves(p):
    lo = pltpu.bitcast(p << 16, F32)
    hi = pltpu.bitcast(p & jnp.uint32(0xFFFF0000), F32)
    return lo, hi


def _ada_kernel(c_ref, w_ref, b_ref, o_ref):
    c = c_ref[...]
    sc = c * jax.nn.sigmoid(c)
    o_ref[...] = jnp.dot(sc, w_ref[...], precision=lax.Precision.HIGHEST,
                         preferred_element_type=F32) + b_ref[...]


def _ada(c, w_ada, b_ada):
    L, D, N = w_ada.shape
    B = c.shape[0]
    tn = 1536
    return pl.pallas_call(
        _ada_kernel,
        grid=(L, N // tn),
        in_specs=[
            pl.BlockSpec((B, D), lambda l, j: (0, 0)),
            pl.BlockSpec((None, D, tn), lambda l, j: (l, 0, j)),
            pl.BlockSpec((None, 1, tn), lambda l, j: (l, 0, j)),
        ],
        out_specs=pl.BlockSpec((None, B, tn), lambda l, j: (l, 0, j)),
        out_shape=jax.ShapeDtypeStruct((L, B, N), F32),
        compiler_params=_cparams(("arbitrary", "arbitrary")),
        name="ada",
    )(c, w_ada, b_ada.reshape(L, 1, N))


def _inproj_kernel(x_ref, mod_ref, w_ref, cs_ref, sa_ref, sb_ref, lng_ref, lnb_ref,
                   qT_ref, k_ref, vT_ref, zu_ref, zvn_ref):
    D = D_MODEL
    x = x_ref[...]
    sh = mod_ref[:, 0:D]
    sc = mod_ref[:, D:2 * D]
    h = (x * (1.0 + sc) + sh).astype(BF16)
    cs = cs_ref[...]
    sa = sa_ref[...]
    sb = sb_ref[...]

    def rope(t):
        return t * cs + pltpu.roll(t, 96, 1) * sa + pltpu.roll(t, 32, 1) * sb

    q = jnp.dot(h, w_ref[:, 0:QK_COLS], preferred_element_type=F32)
    for j in range(N_HEADS):
        sl = slice(j * LANES, (j + 1) * LANES)
        qT_ref[sl, :] = (rope(q[:, sl]) * Q_SCALE).T.astype(BF16)
    k = jnp.dot(h, w_ref[:, QK_COLS:2 * QK_COLS], preferred_element_type=F32)
    for j in range(N_HEADS):
        sl = slice(j * LANES, (j + 1) * LANES)
        k_ref[:, sl] = rope(k[:, sl]).astype(BF16)
    c0 = 2 * QK_COLS
    v = jnp.dot(h, w_ref[:, c0:c0 + DIFF_WIDTH], preferred_element_type=F32)
    vT_ref[...] = v.T.astype(BF16)
    c0 += DIFF_WIDTH
    u = jnp.dot(h, w_ref[:, c0:c0 + SG_WIDTH], preferred_element_type=F32)
    zu_ref[...] = _gelu(u).astype(BF16)
    c0 += SG_WIDTH
    z = _gelu(jnp.dot(h, w_ref[:, c0:c0 + SG_WIDTH], preferred_element_type=F32))
    for g in range(N_SG):
        sl = slice(g * SG_DIM, (g + 1) * SG_DIM)
        zvn_ref[:, sl] = _layer_norm_rows(z[:, sl], lng_ref[:, sl], lnb_ref[:, sl]).astype(BF16)


def _inproj(x, mod_l, w_in_b, rope_tabs, ln_g, ln_b):
    B, S, D = x.shape
    tm = min(ROW_TILE, S)
    nt = S // tm
    cs, sa, sb = rope_tabs
    row = lambda b, i: (b, i, 0)
    tab = pl.BlockSpec((tm, LANES), lambda b, i: (i, 0))
    return pl.pallas_call(
        _inproj_kernel,
        grid=(B, nt),
        in_specs=[
            pl.BlockSpec((None, tm, D), row),
            pl.BlockSpec((None, 1, 6 * D), lambda b, i: (b, 0, 0)),
            pl.BlockSpec(w_in_b.shape, lambda b, i: (0, 0)),
            tab, tab, tab,
            pl.BlockSpec((1, SG_WIDTH), lambda b, i: (0, 0)),
            pl.BlockSpec((1, SG_WIDTH), lambda b, i: (0, 0)),
        ],
        out_specs=[
            pl.BlockSpec((None, QK_COLS, tm), lambda b, i: (b, 0, i)),
            pl.BlockSpec((None, tm, QK_COLS), row),
            pl.BlockSpec((None, None, DIFF_WIDTH, tm), lambda b, i: (b, i, 0, 0)),
            pl.BlockSpec((None, tm, SG_WIDTH), row),
            pl.BlockSpec((None, tm, SG_WIDTH), row),
        ],
        out_shape=[
            jax.ShapeDtypeStruct((B, QK_COLS, S), BF16),
            jax.ShapeDtypeStruct((B, S, QK_COLS), BF16),
            jax.ShapeDtypeStruct((B, nt, DIFF_WIDTH, tm), BF16),
            jax.ShapeDtypeStruct((B, S, SG_WIDTH), BF16),
            jax.ShapeDtypeStruct((B, S, SG_WIDTH), BF16),
        ],
        compiler_params=_cparams(("arbitrary", "arbitrary")),
        name="inproj",
    )(x, mod_l, w_in_b, cs, sa, sb, ln_g, ln_b)


def _attn_kernel(lq1_ref, lk1_ref, lq2_ref, lk2_ref, g_ref, qT_ref, k_ref, vT_ref, o_ref,
                 q_scr, s_scr, m_scr, l_scr, acc_scr, *, lam_init):
    tq = qT_ref.shape[1]
    tk = vT_ref.shape[2]
    i = pl.program_id(2)
    qT = qT_ref[...]
    rows = lax.broadcasted_iota(I32, qT.shape, 0)
    zero = jnp.zeros_like(qT)
    q_scr[0] = jnp.where(rows < QK_DIM, qT, zero)
    q_scr[1] = jnp.where(rows >= QK_DIM, qT, zero)

    def scores(j, slot, c0):
        kb = k_ref[pl.ds(pl.multiple_of(j * tk, tk), tk), :]
        for mp in range(2):
            s_scr[slot, mp, :, c0:] = jnp.dot(kb, q_scr[mp, :, c0:], preferred_element_type=F32)

    def block_scores(slot, mp, c0, c1, masked):
        s = s_scr[slot, mp, :, c0:c1]
        if masked:
            kpos = lax.broadcasted_iota(I32, s.shape, 0)
            qpos = lax.broadcasted_iota(I32, s.shape, 1)
            s = jnp.where(kpos <= qpos, s, -jnp.inf)
        return s

    def pv_unshifted(j, slot, c0, c1, masked):
        vb = vT_ref[j]
        for mp in range(2):
            p = jnp.exp2(block_scores(slot, mp, c0, c1, masked))
            l_scr[mp, :, c0:c1] += jnp.sum(p, axis=0, keepdims=True)
            acc_scr[mp, :, c0:c1] += jnp.dot(vb, p.astype(BF16), preferred_element_type=F32)

    def pv_online(j, slot, c0, c1, masked):
        vb = vT_ref[j]
        for mp in range(2):
            s = block_scores(slot, mp, c0, c1, masked)
            m_old = m_scr[mp, :, c0:c1]
            m_new = jnp.maximum(m_old, jnp.max(s, axis=0, keepdims=True))
            alpha = jnp.exp2(m_old - m_new)
            p = jnp.exp2(s - m_new)
            l_scr[mp, :, c0:c1] = alpha * l_scr[mp, :, c0:c1] + jnp.sum(p, axis=0, keepdims=True)
            acc_scr[mp, :, c0:c1] = (alpha * acc_scr[mp, :, c0:c1]
                                     + jnp.dot(vb, p.astype(BF16), preferred_element_type=F32))
            m_scr[mp, :, c0:c1] = m_new

    def sweep(block):
        acc_scr[...] = jnp.zeros(acc_scr.shape, F32)
        l_scr[...] = jnp.zeros(l_scr.shape, F32)
        scores(0, 0, 0)

        def pair(t):
            scores(t, 1, 0)
            block(t - 1, 0, 0, tq, False)
            scores(t + 1, 0, 0)
            block(t, 1, 0, tq, False)

        def body(u, carry):
            pair(4 * u + 1)
            pair(4 * u + 3)
            return carry

        lax.fori_loop(0, i // 2, body, 0)

        @pl.when(i % 2 == 1)
        def _():
            pair(2 * i - 1)

        scores(2 * i + 1, 1, tk)
        block(2 * i, 0, 0, tk, True)
        block(2 * i, 0, tk, tq, False)
        block(2 * i + 1, 1, tk, tq, True)

    sweep(pv_unshifted)
    den = jnp.concatenate([l_scr[0], l_scr[1]], axis=0)
    in_range = (den >= DEN_MIN) & (den <= DEN_MAX)
    n_bad = jnp.sum(jnp.where(in_range, 0.0, 1.0))

    @pl.when(n_bad > 0.0)
    def _():
        m_scr[...] = jnp.full(m_scr.shape, -jnp.inf, F32)
        sweep(pv_online)

    lam = (jnp.exp(jnp.sum(lq1_ref[...] * lk1_ref[...], axis=1, keepdims=True))
           - jnp.exp(jnp.sum(lq2_ref[...] * lk2_ref[...], axis=1, keepdims=True)) + lam_init)
    out = acc_scr[0] / l_scr[0] - lam * (acc_scr[1] / l_scr[1])
    ms = jnp.mean(out * out, axis=0, keepdims=True)
    y = out * lax.rsqrt(ms + LN_EPS) * g_ref[...] * (1.0 - lam_init)
    o_ref[...] = y.T.astype(BF16)


def _attention(qT, k, vTb, lams, g_col, lam_init):
    B, _, S = qT.shape
    nkv, tk = vTb.shape[1], vTb.shape[3]
    tq = 2 * tk
    small = pl.BlockSpec((1, QK_DIM), lambda b, h, i: (0, 0))
    return pl.pallas_call(
        functools.partial(_attn_kernel, lam_init=lam_init),
        grid=(B, N_HEADS, S // tq),
        in_specs=[
            small, small, small, small,
            pl.BlockSpec((V_DIM, 1), lambda b, h, i: (0, 0)),
            pl.BlockSpec((None, HEAD_COLS, tq), lambda b, h, i: (b, h, i)),
            pl.BlockSpec((None, S, HEAD_COLS), lambda b, h, i: (b, 0, h)),
            pl.BlockSpec((None, nkv, V_DIM, tk), lambda b, h, i: (b, 0, h, 0)),
        ],
        out_specs=pl.BlockSpec((None, tq, V_DIM), lambda b, h, i: (b, i, h)),
        out_shape=jax.ShapeDtypeStruct((B, S, DIFF_WIDTH), BF16),
        scratch_shapes=[
            pltpu.VMEM((2, HEAD_COLS, tq), BF16),
            pltpu.VMEM((2, 2, tk, tq), F32),
            pltpu.VMEM((2, 1, tq), F32),
            pltpu.VMEM((2, 1, tq), F32),
            pltpu.VMEM((2, V_DIM, tq), F32),
        ],
        compiler_params=_cparams(("arbitrary", "arbitrary", "arbitrary")),
        name="diff_attn",
    )(*lams, g_col, qT, k, vTb)


def _mix_kernel(x_ref, mod_ref, att_ref, zu_ref, zvn_ref, wsp_ref, bsp_ref, wo_ref, g_ref, b_ref,
                o_ref, sg_scr, *, alpha):
    D = D_MODEL
    nc = zu_ref.shape[0]
    r = lax.broadcasted_iota(I32, (CHUNK, CHUNK), 0)
    c = lax.broadcasted_iota(I32, (CHUNK, CHUNK), 1)
    causal = r >= c
    for g in range(N_SG):
        sl = slice(g * SG_DIM, (g + 1) * SG_DIM)
        w = jnp.where(causal, wsp_ref[g], 0.0).astype(BF16)
        z = jnp.concatenate([zvn_ref[n, :, sl] for n in range(nc)], axis=1)
        mixed = jnp.dot(w, z, preferred_element_type=F32)
        for n in range(nc):
            gate = mixed[:, n * SG_DIM:(n + 1) * SG_DIM] + bsp_ref[g]
            sg_scr[n * CHUNK:(n + 1) * CHUNK, sl] = (zu_ref[n, :, sl].astype(F32) * gate).astype(BF16)
    mix = (jnp.dot(att_ref[...], wo_ref[0:DIFF_WIDTH, :], preferred_element_type=F32)
           + jnp.dot(sg_scr[...], wo_ref[DIFF_WIDTH:, :], preferred_element_type=F32))
    gt = mod_ref[:, 2 * D:3 * D]
    y = alpha * x_ref[...] + (1.0 + gt) * mix
    o_ref[...] = _layer_norm_rows(y, g_ref[...], b_ref[...])


def _mix(x, mod_l, att, zu, zvn, w_sp, b_sp_full, w_out_b, ln_g, ln_b, alpha):
    B, S, D = x.shape
    tm = min(ROW_TILE, S)
    nc = tm // CHUNK
    row = lambda b, i: (b, i, 0)
    zu4 = zu.reshape(B, S // CHUNK, CHUNK, SG_WIDTH)
    zvn4 = zvn.reshape(B, S // CHUNK, CHUNK, SG_WIDTH)
    chunked = pl.BlockSpec((None, nc, CHUNK, SG_WIDTH), lambda b, i: (b, i, 0, 0))
    full2 = lambda a: pl.BlockSpec(a.shape, lambda b, i: (0,) * a.ndim)
    return pl.pallas_call(
        functools.partial(_mix_kernel, alpha=alpha),
        grid=(B, S // tm),
        in_specs=[
            pl.BlockSpec((None, tm, D), row),
            pl.BlockSpec((None, 1, 6 * D), lambda b, i: (b, 0, 0)),
            pl.BlockSpec((None, tm, DIFF_WIDTH), row),
            chunked, chunked,
            full2(w_sp), full2(b_sp_full), full2(w_out_b), full2(ln_g), full2(ln_b),
        ],
        out_specs=pl.BlockSpec((None, tm, D), row),
        out_shape=jax.ShapeDtypeStruct((B, S, D), F32),
        scratch_shapes=[pltpu.VMEM((tm, SG_WIDTH), BF16)],
        compiler_params=_cparams(("arbitrary", "arbitrary")),
        name="sgate_outproj_ln",
    )(x, mod_l, att, zu4, zvn4, w_sp, b_sp_full, w_out_b, ln_g, ln_b)


def _route_kernel(x_ref, mod_ref, wr_hi_ref, wr_lo_ref, br_ref, hp_ref, rt_ref, cnt_ref, run_scr):
    D = D_MODEL
    tm = x_ref.shape[0]

    @pl.when((pl.program_id(0) == 0) & (pl.program_id(1) == 0))
    def _():
        run_scr[...] = jnp.zeros(run_scr.shape, F32)

    sh = mod_ref[:, 3 * D:4 * D]
    sc = mod_ref[:, 4 * D:5 * D]
    h = x_ref[...] * (1.0 + sc) + sh
    h_hi = h.astype(BF16)
    h_lo = (h - h_hi.astype(F32)).astype(BF16)
    logit = (jnp.dot(h_hi, wr_hi_ref[...], preferred_element_type=F32)
             + jnp.dot(h_lo, wr_hi_ref[...], preferred_element_type=F32)
             + jnp.dot(h_hi, wr_lo_ref[...], preferred_element_type=F32)) + br_ref[...]
    lane = lax.broadcasted_iota(I32, logit.shape, 1).astype(F32)
    neg = -jnp.inf
    big = float(LANES)

    def first_argmax(v):
        mx = jnp.max(v, axis=1, keepdims=True)
        idx = jnp.min(jnp.where(v == mx, lane, big), axis=1, keepdims=True)
        return mx, idx

    in_grp = lane < N_GROUPS
    gmax, gidx = first_argmax(jnp.where(in_grp, logit, neg))
    g_p = 1.0 / jnp.sum(jnp.where(in_grp, jnp.exp(logit - gmax), 0.0), axis=1, keepdims=True)
    lo_lane = ROUTE_LANE0 + EXP_PER_GROUP * gidx
    sel = jnp.where((lane >= lo_lane) & (lane < lo_lane + EXP_PER_GROUP), logit, neg)
    v1, i1 = first_argmax(sel)
    v2, i2 = first_argmax(jnp.where(lane == i1, neg, sel))
    t = jnp.exp(v2 - v1)
    w1 = g_p / (1.0 + t)
    w2 = g_p * t / (1.0 + t)

    first_lower = i1 < i2
    e_a = jnp.minimum(i1, i2) - lo_lane
    e_b = jnp.maximum(i1, i2) - lo_lane
    w_a = jnp.where(first_lower, w1, w2)
    w_b = jnp.where(first_lower, w2, w1)
    bucket = gidx * N_PAIRS + (e_a * (2 * EXP_PER_GROUP - 1 - e_a) * 0.5 + (e_b - e_a - 1.0))

    hp_ref[:, 0:HALF] = _pack_halves(h)
    wslab = jnp.where(lane == 0.0, w_a, jnp.where(lane == 1.0, w_b, 0.0))
    hp_ref[:, HALF:] = pltpu.bitcast(wslab, U32)

    hot = lane == bucket
    onehot = jnp.where(hot, 1.0, 0.0)
    r = lax.broadcasted_iota(I32, (tm, tm), 0)
    c = lax.broadcasted_iota(I32, (tm, tm), 1)
    strict = jnp.where(r > c, 1.0, 0.0).astype(BF16)
    before = jnp.dot(strict, onehot.astype(BF16), preferred_element_type=F32) + run_scr[...]
    rank = jnp.sum(jnp.where(hot, before, 0.0), axis=1, keepdims=True)
    run_scr[...] = run_scr[...] + jnp.sum(onehot, axis=0, keepdims=True)
    cnt_ref[...] = run_scr[...]

    slab = jnp.where(lane == 0.0, bucket, jnp.where(lane == 1.0, rank, 0.0))
    rt_ref[...] = slab.T[0:8, :]


def _route(x1, mod_l, wr_hi, wr_lo, br):
    B, S, D = x1.shape
    tm = min(ROW_TILE, S)
    row = lambda b, i: (b, i, 0)
    full2 = lambda a: pl.BlockSpec(a.shape, lambda b, i: (0,) * a.ndim)
    return pl.pallas_call(
        _route_kernel,
        grid=(B, S // tm),
        in_specs=[
            pl.BlockSpec((None, tm, D), row),
            pl.BlockSpec((None, 1, 6 * D), lambda b, i: (b, 0, 0)),
            full2(wr_hi), full2(wr_lo), full2(br),
        ],
        out_specs=[
            pl.BlockSpec((None, tm, ROW_WORDS), row),
            pl.BlockSpec((None, 8, tm), lambda b, i: (b, 0, i)),
            pl.BlockSpec((1, LANES), lambda b, i: (0, 0)),
        ],
        out_shape=[
            jax.ShapeDtypeStruct((B, S, ROW_WORDS), U32),
            jax.ShapeDtypeStruct((B, 8, S), F32),
            jax.ShapeDtypeStruct((1, LANES), F32),
        ],
        scratch_shapes=[pltpu.VMEM((1, LANES), F32)],
        compiler_params=_cparams(("arbitrary", "arbitrary")),
        name="moe_route",
    )(x1, mod_l, wr_hi, wr_lo, br)


def _row_of(ref, p):
    return ref.at[p >> 3, pl.ds(p & (SUBLANES - 1), 1)]


def _dispatch_kernel(p_ref, hp_ref, xs_in_ref, xs_ref, sem):
    del xs_in_ref
    ng = hp_ref.shape[0]

    def issue(g, carry):
        for u in range(SUBLANES):
            p = p_ref[0, 0, g * SUBLANES + u]
            pltpu.make_async_copy(hp_ref.at[g, pl.ds(u, 1)], _row_of(xs_ref, p), sem).start(priority=u % 2)
        return carry

    lax.fori_loop(0, ng, issue, 0)
    pltpu.make_async_copy(hp_ref, xs_ref.at[pl.ds(0, ng)], sem).wait()


def _dispatch(pos, hp, xs_buf):
    T = hp.shape[0]
    n_rows = xs_buf.shape[0]
    nb = min(DISPATCH_BLOCK, T)
    blk = pl.BlockSpec((1, 1, nb), lambda i: (i, 0, 0), memory_space=pltpu.SMEM)
    xs0 = xs_buf.reshape(n_rows // SUBLANES, SUBLANES, ROW_WORDS)
    xs = pl.pallas_call(
        _dispatch_kernel,
        grid=(T // nb,),
        in_specs=[blk, pl.BlockSpec((nb // SUBLANES, SUBLANES, ROW_WORDS), lambda i: (i, 0, 0)),
                  pl.BlockSpec(memory_space=pl.ANY)],
        out_specs=pl.BlockSpec(memory_space=pl.ANY),
        out_shape=jax.ShapeDtypeStruct(xs0.shape, U32),
        scratch_shapes=[pltpu.SemaphoreType.DMA(())],
        input_output_aliases={2: 0},
        compiler_params=_cparams(("arbitrary",)),
        name="moe_dispatch",
    )(pos.reshape(T // nb, 1, nb), hp.reshape(T // SUBLANES, SUBLANES, ROW_WORDS), xs0)
    return xs.reshape(n_rows, ROW_WORDS)


def _expert_kernel(ea_ref, eb_ref, nv_ref, xs_ref, wga_ref, wua_ref, wda_ref, wgb_ref, wub_ref, wdb_ref, ys_ref):
    del ea_ref, eb_ref

    @pl.when(pl.program_id(0) < nv_ref[0])
    def _():
        lo, hi = _unpack_halves(xs_ref[:, 0:HALF])
        lo = lo.astype(BF16)
        hi = hi.astype(BF16)
        wts = pltpu.bitcast(xs_ref[:, HALF:], F32)

        def hidden(wg_ref, wu_ref, w):
            g = (jnp.dot(lo, wg_ref[0:HALF, :], preferred_element_type=F32)
                 + jnp.dot(hi, wg_ref[HALF:, :], preferred_element_type=F32))
            u = (jnp.dot(lo, wu_ref[0:HALF, :], preferred_element_type=F32)
                 + jnp.dot(hi, wu_ref[HALF:, :], preferred_element_type=F32))
            return (g * jax.nn.sigmoid(g) * u * w).astype(BF16)

        y = (jnp.dot(hidden(wga_ref, wua_ref, wts[:, 0:1]), wda_ref[...], preferred_element_type=F32)
             + jnp.dot(hidden(wgb_ref, wub_ref, wts[:, 1:2]), wdb_ref[...], preferred_element_type=F32))
        ys_ref[...] = _pack_halves(y)


def _experts(tile_ea, tile_eb, n_valid, xs, w_gate, w_up, w_down, layer):
    n_rows = xs.shape[0]
    tm = EXPERT_TILE
    nt = n_rows // tm
    rows = lambda n, ea, eb, nv: (jnp.minimum(n, nv[0] - 1), 0)
    sel_a = lambda n, ea, eb, nv: (layer, ea[n], 0, 0)
    sel_b = lambda n, ea, eb, nv: (layer, eb[n], 0, 0)
    up = lambda sel: pl.BlockSpec((None, None, D_MODEL, D_EXPERT), sel)
    down = lambda sel: pl.BlockSpec((None, None, D_EXPERT, D_MODEL), sel)
    return pl.pallas_call(
        _expert_kernel,
        grid_spec=pltpu.PrefetchScalarGridSpec(
            num_scalar_prefetch=3,
            grid=(nt,),
            in_specs=[pl.BlockSpec((tm, ROW_WORDS), rows),
                      up(sel_a), up(sel_a), down(sel_a), up(sel_b), up(sel_b), down(sel_b)],
            out_specs=pl.BlockSpec((tm, HALF), rows),
        ),
        out_shape=jax.ShapeDtypeStruct((n_rows, HALF), U32),
        compiler_params=_cparams(("arbitrary",)),
        name="moe_experts",
    )(tile_ea, tile_eb, n_valid, xs, w_gate, w_up, w_down, w_gate, w_up, w_down)


def _combine_kernel(p_ref, pn_ref, ys_ref, x_ref, mod_ref, g_ref, b_ref, o_ref, buf, sems, *, alpha):
    D = D_MODEL
    tc = x_ref.shape[0]
    g = pl.program_id(0) * pl.num_programs(1) + pl.program_id(1)
    n_steps = pl.num_programs(0) * pl.num_programs(1)
    slot = g % 2

    def gather(pa_ref, s):
        def issue(g8, carry):
            for u in range(SUBLANES):
                p = pa_ref[0, 0, g8 * SUBLANES + u]
                pltpu.make_async_copy(_row_of(ys_ref, p), buf.at[s, g8, pl.ds(u, 1)], sems.at[s]).start(
                    priority=u % 2)
            return carry

        lax.fori_loop(0, tc // SUBLANES, issue, 0)

    @pl.when(g == 0)
    def _():
        gather(p_ref, 0)

    @pl.when(g + 1 < n_steps)
    def _():
        gather(pn_ref, 1 - slot)

    pltpu.make_async_copy(ys_ref.at[pl.ds(0, tc // SUBLANES)], buf.at[slot], sems.at[slot]).wait()

    lo, hi = _unpack_halves(buf[slot].reshape(tc, HALF))
    ffn = jnp.concatenate([lo, hi], axis=1)
    gt = mod_ref[:, 5 * D:6 * D]
    y = alpha * x_ref[...] + (1.0 + gt) * ffn
    o_ref[...] = _layer_norm_rows(y, g_ref[...], b_ref[...])


def _combine(pos, ys, x1, mod_l, ln_g, ln_b, alpha):
    B, S, D = x1.shape
    tc = min(COMBINE_TILE, S)
    nt = S // tc
    blk = pl.BlockSpec((1, 1, tc), lambda b, i: (b * nt + i, 0, 0), memory_space=pltpu.SMEM)
    nxt = pl.BlockSpec((1, 1, tc), lambda b, i: (jnp.minimum(b * nt + i + 1, B * nt - 1), 0, 0),
                       memory_space=pltpu.SMEM)
    row = lambda b, i: (b, i, 0)
    full2 = lambda a: pl.BlockSpec(a.shape, lambda b, i: (0,) * a.ndim)
    p = pos.reshape(B * nt, 1, tc)
    return pl.pallas_call(
        functools.partial(_combine_kernel, alpha=alpha),
        grid=(B, nt),
        in_specs=[
            blk, nxt,
            pl.BlockSpec(memory_space=pl.ANY),
            pl.BlockSpec((None, tc, D), row),
            pl.BlockSpec((None, 1, 6 * D), lambda b, i: (b, 0, 0)),
            full2(ln_g), full2(ln_b),
        ],
        out_specs=pl.BlockSpec((None, tc, D), row),
        out_shape=jax.ShapeDtypeStruct((B, S, D), F32),
        scratch_shapes=[pltpu.VMEM((2, tc // SUBLANES, SUBLANES, HALF), U32), pltpu.SemaphoreType.DMA((2,))],
        compiler_params=_cparams(("arbitrary", "arbitrary")),
        name="moe_combine_ln",
    )(p, p, ys.reshape(ys.shape[0] // SUBLANES, SUBLANES, HALF), x1, mod_l, ln_g, ln_b)


def _rope_tables(S):
    inv = 1.0 / (ROPE_THETA ** (jnp.arange(0, QK_DIM, 2, dtype=F32) / QK_DIM))
    ang = jnp.arange(S, dtype=F32)[:, None] * inv[None, :]
    cos, sin = jnp.cos(ang), jnp.sin(ang)
    half = QK_DIM // 2
    first = (jnp.arange(LANES) % QK_DIM) < half
    cs = jnp.tile(cos, (1, LANES // half))
    sn = jnp.tile(sin, (1, LANES // half))
    sa = jnp.where(first[None, :], -sn, 0.0)
    sb = jnp.where(first[None, :], 0.0, sn)
    return cs, sa, sb


def _router_matrix(w_group, b_group, w_router, b_router):
    D = w_group.shape[0]
    w = jnp.zeros((D, LANES), F32)
    w = w.at[:, 0:N_GROUPS].set(w_group)
    wr = jnp.transpose(w_router, (1, 0, 2)).reshape(D, N_EXPERTS)
    w = w.at[:, ROUTE_LANE0:ROUTE_LANE0 + N_EXPERTS].set(wr)
    b = jnp.zeros((1, LANES), F32)
    b = b.at[0, 0:N_GROUPS].set(b_group)
    b = b.at[0, ROUTE_LANE0:ROUTE_LANE0 + N_EXPERTS].set(b_router.reshape(N_EXPERTS))
    hi = w.astype(BF16)
    lo = (w - hi.astype(F32)).astype(BF16)
    return hi, lo, b


def _routing_tables(rt, cnt, n_tiles):
    B, _, S = rt.shape
    tm = EXPERT_TILE
    bucket = rt[:, 0, :].astype(I32)
    rank = rt[:, 1, :].astype(I32)
    counts = cnt[0, 0:N_BUCKETS].astype(I32)
    tiles = (counts + tm - 1) // tm
    tile_end = jnp.cumsum(tiles)
    tile_start = tile_end - tiles
    onehot = bucket[..., None] == jnp.arange(N_BUCKETS, dtype=I32)
    pos = (jnp.sum(jnp.where(onehot, tile_start * tm, 0), axis=-1) + rank).reshape(B * S)
    tile_ids = jnp.arange(n_tiles, dtype=I32)
    tile_bucket = jnp.minimum(jnp.sum(tile_ids[:, None] >= tile_end[None, :], axis=1), N_BUCKETS - 1)
    pair_a = jnp.asarray([a for a in range(EXP_PER_GROUP) for _ in range(a + 1, EXP_PER_GROUP)], I32)
    pair_b = jnp.asarray([b for a in range(EXP_PER_GROUP) for b in range(a + 1, EXP_PER_GROUP)], I32)
    group = tile_bucket // N_PAIRS
    pair = tile_bucket % N_PAIRS
    tile_ea = (group * EXP_PER_GROUP + pair_a[pair]).astype(I32)
    tile_eb = (group * EXP_PER_GROUP + pair_b[pair]).astype(I32)
    n_valid = tile_end[-1:].astype(I32)
    return pos, tile_ea, tile_eb, n_valid


def kernel(x, c, w_ada, b_ada, w_in, lambda_q1, lambda_k1, lambda_q2, lambda_k2, subln_g, sg_ln_g, sg_ln_b, w_spatial, b_spatial, w_out, ln1_g, ln1_b, w_group, b_group, w_router, b_router, w_gate, w_up, w_down, ln2_g, ln2_b):
    B, S, D = x.shape
    depth = w_in.shape[0]
    T = B * S
    alpha = (2.0 * depth) ** 0.25
    n_tiles = T // EXPERT_TILE + N_BUCKETS
    n_rows = n_tiles * EXPERT_TILE

    mod = _ada(c, w_ada, b_ada)
    tabs = _rope_tables(S)
    w_gate, w_up, w_down = w_gate.astype(BF16), w_up.astype(BF16), w_down.astype(BF16)
    xs = jnp.zeros((n_rows, ROW_WORDS), U32)
    for l in range(depth):
        mod_l = mod[l].reshape(B, 1, 6 * D)
        lam_init = 0.8 - 0.6 * math.exp(-0.3 * l)

        qT, k, vTb, zu, zvn = _inproj(x, mod_l, w_in[l].astype(BF16), tabs,
                                      sg_ln_g[l].reshape(1, SG_WIDTH), sg_ln_b[l].reshape(1, SG_WIDTH))
        lams = (lambda_q1[l].reshape(1, QK_DIM), lambda_k1[l].reshape(1, QK_DIM),
                lambda_q2[l].reshape(1, QK_DIM), lambda_k2[l].reshape(1, QK_DIM))
        att = _attention(qT, k, vTb, lams, subln_g[l].reshape(V_DIM, 1), lam_init)
        b_sp_full = jnp.broadcast_to(b_spatial[l][:, :, None], (N_SG, CHUNK, SG_DIM))
        x1 = _mix(x, mod_l, att, zu, zvn, w_spatial[l], b_sp_full, w_out[l].astype(BF16),
                  ln1_g[l].reshape(1, D), ln1_b[l].reshape(1, D), alpha)

        wr_hi, wr_lo, br = _router_matrix(w_group[l], b_group[l], w_router[l], b_router[l])
        hp, rt, cnt = _route(x1, mod_l, wr_hi, wr_lo, br)
        pos, tile_ea, tile_eb, n_valid = _routing_tables(rt, cnt, n_tiles)
        xs = _dispatch(pos, hp.reshape(T, ROW_WORDS), xs)
        ys = _experts(tile_ea, tile_eb, n_valid, xs, w_gate, w_up, w_down, l)
        x = _combine(pos, ys, x1, mod_l, ln2_g[l].reshape(1, D), ln2_b[l].reshape(1, D), alpha)
    return x
```

```python
import functools
import math

import jax
import jax.numpy as jnp
from jax import lax
from jax.experimental import pallas as pl
from jax.experimental.pallas import tpu as pltpu

F32 = jnp.float32
BF16 = jnp.bfloat16
U32 = jnp.uint32
I32 = jnp.int32

D_MODEL = 1024
N_HEADS = 4
QK_DIM = 64
V_DIM = 128
HEAD_COLS = 2 * QK_DIM
QK_COLS = N_HEADS * HEAD_COLS
DIFF_WIDTH = N_HEADS * V_DIM
N_SG = 4
SG_DIM = 128
SG_WIDTH = N_SG * SG_DIM
CHUNK = 128
N_GROUPS = 4
EXP_PER_GROUP = 8
N_EXPERTS = N_GROUPS * EXP_PER_GROUP
N_PAIRS = EXP_PER_GROUP * (EXP_PER_GROUP - 1) // 2
N_BUCKETS = N_GROUPS * N_PAIRS
D_EXPERT = 512
ROPE_THETA = 10000.0
LN_EPS = 1e-5
LANES = 128
HALF = D_MODEL // 2
ROW_WORDS = HALF + LANES

ROW_TILE = 512
EXPERT_TILE = 256
DISPATCH_BLOCK = 2048
COMBINE_TILE = 512
SUBLANES = 8
ROUTE_LANE0 = 8
VMEM_LIMIT = 48 * 1024 * 1024

DEN_MIN = 2.0 ** -40
DEN_MAX = 2.0 ** 40
LOG2E = 1.4426950408889634
Q_SCALE = (QK_DIM ** -0.5) * LOG2E


def _cparams(sem):
    return pltpu.CompilerParams(dimension_semantics=sem, vmem_limit_bytes=VMEM_LIMIT)


def _layer_norm_rows(y, g, b):
    mu = jnp.mean(y, axis=-1, keepdims=True)
    yc = y - mu
    var = jnp.mean(yc * yc, axis=-1, keepdims=True)
    return yc * lax.rsqrt(var + LN_EPS) * g + b


def _gelu(x):
    return 0.5 * x * (1.0 + lax.erf(x * (2.0 ** -0.5)))


def _pack_halves(y):
    lo = pltpu.bitcast(y[:, :HALF].astype(BF16).astype(F32), U32) >> 16
    hi = pltpu.bitcast(y[:, HALF:].astype(BF16).astype(F32), U32) & jnp.uint32(0xFFFF0000)
    return lo | hi


def _unpack_halves(p):
    lo = pltpu.bitcast(p << 16, F32)
    hi = pltpu.bitcast(p & jnp.uint32(0xFFFF0000), F32)
    return lo, hi


def _ada_kernel(c_ref, w_ref, b_ref, o_ref):
    c = c_ref[...]
    sc = c * jax.nn.sigmoid(c)
    o_ref[...] = jnp.dot(sc, w_ref[...], precision=lax.Precision.HIGHEST,
                         preferred_element_type=F32) + b_ref[...]


def _ada(c, w_ada, b_ada):
    L, D, N = w_ada.shape
    B = c.shape[0]
    tn = 1536
    return pl.pallas_call(
        _ada_kernel,
        grid=(L, N // tn),
        in_specs=[
            pl.BlockSpec((B, D), lambda l, j: (0, 0)),
            pl.BlockSpec((None, D, tn), lambda l, j: (l, 0, j)),
            pl.BlockSpec((None, 1, tn), lambda l, j: (l, 0, j)),
        ],
        out_specs=pl.BlockSpec((None, B, tn), lambda l, j: (l, 0, j)),
        out_shape=jax.ShapeDtypeStruct((L, B, N), F32),
        compiler_params=_cparams(("arbitrary", "arbitrary")),
        name="ada",
    )(c, w_ada, b_ada.reshape(L, 1, N))


def _inproj_kernel(x_ref, mod_ref, w_ref, cs_ref, sa_ref, sb_ref, lng_ref, lnb_ref,
                   qT_ref, k_ref, vT_ref, zu_ref, zvn_ref):
    D = D_MODEL
    x = x_ref[...]
    sh = mod_ref[:, 0:D]
    sc = mod_ref[:, D:2 * D]
    h = (x * (1.0 + sc) + sh).astype(BF16)
    cs = cs_ref[...]
    sa = sa_ref[...]
    sb = sb_ref[...]

    def rope(t):
        return t * cs + pltpu.roll(t, 96, 1) * sa + pltpu.roll(t, 32, 1) * sb

    q = jnp.dot(h, w_ref[:, 0:QK_COLS], preferred_element_type=F32)
    for j in range(N_HEADS):
        sl = slice(j * LANES, (j + 1) * LANES)
        qT_ref[sl, :] = (rope(q[:, sl]) * Q_SCALE).T.astype(BF16)
    k = jnp.dot(h, w_ref[:, QK_COLS:2 * QK_COLS], preferred_element_type=F32)
    for j in range(N_HEADS):
        sl = slice(j * LANES, (j + 1) * LANES)
        k_ref[:, sl] = rope(k[:, sl]).astype(BF16)
    c0 = 2 * QK_COLS
    v = jnp.dot(h, w_ref[:, c0:c0 + DIFF_WIDTH], preferred_element_type=F32)
    vT_ref[...] = v.T.astype(BF16)
    c0 += DIFF_WIDTH
    u = jnp.dot(h, w_ref[:, c0:c0 + SG_WIDTH], preferred_element_type=F32)
    zu_ref[...] = _gelu(u).astype(BF16)
    c0 += SG_WIDTH
    z = _gelu(jnp.dot(h, w_ref[:, c0:c0 + SG_WIDTH], preferred_element_type=F32))
    for g in range(N_SG):
        sl = slice(g * SG_DIM, (g + 1) * SG_DIM)
        zvn_ref[:, sl] = _layer_norm_rows(z[:, sl], lng_ref[:, sl], lnb_ref[:, sl]).astype(BF16)


def _inproj(x, mod_l, w_in_b, rope_tabs, ln_g, ln_b):
    B, S, D = x.shape
    tm = min(ROW_TILE, S)
    nt = S // tm
    cs, sa, sb = rope_tabs
    row = lambda b, i: (b, i, 0)
    tab = pl.BlockSpec((tm, LANES), lambda b, i: (i, 0))
    return pl.pallas_call(
        _inproj_kernel,
        grid=(B, nt),
        in_specs=[
            pl.BlockSpec((None, tm, D), row),
            pl.BlockSpec((None, 1, 6 * D), lambda b, i: (b, 0, 0)),
            pl.BlockSpec(w_in_b.shape, lambda b, i: (0, 0)),
            tab, tab, tab,
            pl.BlockSpec((1, SG_WIDTH), lambda b, i: (0, 0)),
            pl.BlockSpec((1, SG_WIDTH), lambda b, i: (0, 0)),
        ],
        out_specs=[
            pl.BlockSpec((None, QK_COLS, tm), lambda b, i: (b, 0, i)),
            pl.BlockSpec((None, tm, QK_COLS), row),
            pl.BlockSpec((None, None, DIFF_WIDTH, tm), lambda b, i: (b, i, 0, 0)),
            pl.BlockSpec((None, tm, SG_WIDTH), row),
            pl.BlockSpec((None, tm, SG_WIDTH), row),
        ],
        out_shape=[
            jax.ShapeDtypeStruct((B, QK_COLS, S), BF16),
            jax.ShapeDtypeStruct((B, S, QK_COLS), BF16),
            jax.ShapeDtypeStruct((B, nt, DIFF_WIDTH, tm), BF16),
            jax.ShapeDtypeStruct((B, S, SG_WIDTH), BF16),
            jax.ShapeDtypeStruct((B, S, SG_WIDTH), BF16),
        ],
        compiler_params=_cparams(("arbitrary", "arbitrary")),
        name="inproj",
    )(x, mod_l, w_in_b, cs, sa, sb, ln_g, ln_b)


def _attn_kernel(lq1_ref, lk1_ref, lq2_ref, lk2_ref, g_ref, qT_ref, k_ref, vT_ref, o_ref,
                 q_scr, s_scr, m_scr, l_scr, acc_scr, *, lam_init):
    tq = qT_ref.shape[1]
    tk = vT_ref.shape[2]
    i = pl.program_id(2)
    qT = qT_ref[...]
    rows = lax.broadcasted_iota(I32, qT.shape, 0)
    zero = jnp.zeros_like(qT)
    q_scr[0] = jnp.where(rows < QK_DIM, qT, zero)
    q_scr[1] = jnp.where(rows >= QK_DIM, qT, zero)

    def scores(j, slot, c0):
        kb = k_ref[pl.ds(pl.multiple_of(j * tk, tk), tk), :]
        for mp in range(2):
            s_scr[slot, mp, :, c0:] = jnp.dot(kb, q_scr[mp, :, c0:], preferred_element_type=F32)

    def block_scores(slot, mp, c0, c1, masked):
        s = s_scr[slot, mp, :, c0:c1]
        if masked:
            kpos = lax.broadcasted_iota(I32, s.shape, 0)
            qpos = lax.broadcasted_iota(I32, s.shape, 1)
            s = jnp.where(kpos <= qpos, s, -jnp.inf)
        return s

    def pv_unshifted(j, slot, c0, c1, masked):
        vb = vT_ref[j]
        for mp in range(2):
            p = jnp.exp2(block_scores(slot, mp, c0, c1, masked))
            l_scr[mp, :, c0:c1] += jnp.sum(p, axis=0, keepdims=True)
            acc_scr[mp, :, c0:c1] += jnp.dot(vb, p.astype(BF16), preferred_element_type=F32)

    def pv_online(j, slot, c0, c1, masked):
        vb = vT_ref[j]
        for mp in range(2):
            s = block_scores(slot, mp, c0, c1, masked)
            m_old = m_scr[mp, :, c0:c1]
            m_new = jnp.maximum(m_old, jnp.max(s, axis=0, keepdims=True))
            alpha = jnp.exp2(m_old - m_new)
            p = jnp.exp2(s - m_new)
            l_scr[mp, :, c0:c1] = alpha * l_scr[mp, :, c0:c1] + jnp.sum(p, axis=0, keepdims=True)
            acc_scr[mp, :, c0:c1] = (alpha * acc_scr[mp, :, c0:c1]
                                     + jnp.dot(vb, p.astype(BF16), preferred_element_type=F32))
            m_scr[mp, :, c0:c1] = m_new

    def sweep(block):
        acc_scr[...] = jnp.zeros(acc_scr.shape, F32)
        l_scr[...] = jnp.zeros(l_scr.shape, F32)
        scores(0, 0, 0)

        def pair(t):
            scores(t, 1, 0)
            block(t - 1, 0, 0, tq, False)
            scores(t + 1, 0, 0)
            block(t, 1, 0, tq, False)

        def body(u, carry):
            pair(4 * u + 1)
            pair(4 * u + 3)
            return carry

        lax.fori_loop(0, i // 2, body, 0)

        @pl.when(i % 2 == 1)
        def _():
            pair(2 * i - 1)

        scores(2 * i + 1, 1, tk)
        block(2 * i, 0, 0, tk, True)
        block(2 * i, 0, tk, tq, False)
        block(2 * i + 1, 1, tk, tq, True)

    def finalize():
        lam = (jnp.exp(jnp.sum(lq1_ref[...] * lk1_ref[...], axis=1, keepdims=True))
               - jnp.exp(jnp.sum(lq2_ref[...] * lk2_ref[...], axis=1, keepdims=True)) + lam_init)
        out = acc_scr[0] / l_scr[0] - lam * (acc_scr[1] / l_scr[1])
        ms = jnp.mean(out * out, axis=0, keepdims=True)
        y = out * lax.rsqrt(ms + LN_EPS) * g_ref[...] * (1.0 - lam_init)
        o_ref[...] = y.T.astype(BF16)

    sweep(pv_unshifted)
    finalize()
    den = jnp.concatenate([l_scr[0], l_scr[1]], axis=0)
    in_range = (den >= DEN_MIN) & (den <= DEN_MAX)
    n_bad = jnp.sum(jnp.where(in_range, 0.0, 1.0))

    @pl.when(n_bad > 0.0)
    def _():
        m_scr[...] = jnp.full(m_scr.shape, -jnp.inf, F32)
        sweep(pv_online)
        finalize()


def _attention(qT, k, vTb, lams, g_col, lam_init):
    B, _, S = qT.shape
    nkv, tk = vTb.shape[1], vTb.shape[3]
    tq = 2 * tk
    small = pl.BlockSpec((1, QK_DIM), lambda b, h, i: (0, 0))
    return pl.pallas_call(
        functools.partial(_attn_kernel, lam_init=lam_init),
        grid=(B, N_HEADS, S // tq),
        in_specs=[
            small, small, small, small,
            pl.BlockSpec((V_DIM, 1), lambda b, h, i: (0, 0)),
            pl.BlockSpec((None, HEAD_COLS, tq), lambda b, h, i: (b, h, i)),
            pl.BlockSpec((None, S, HEAD_COLS), lambda b, h, i: (b, 0, h)),
            pl.BlockSpec((None, nkv, V_DIM, tk), lambda b, h, i: (b, 0, h, 0)),
        ],
        out_specs=pl.BlockSpec((None, tq, V_DIM), lambda b, h, i: (b, i, h)),
        out_shape=jax.ShapeDtypeStruct((B, S, DIFF_WIDTH), BF16),
        scratch_shapes=[
            pltpu.VMEM((2, HEAD_COLS, tq), BF16),
            pltpu.VMEM((2, 2, tk, tq), F32),
            pltpu.VMEM((2, 1, tq), F32),
            pltpu.VMEM((2, 1, tq), F32),
            pltpu.VMEM((2, V_DIM, tq), F32),
        ],
        compiler_params=_cparams(("arbitrary", "arbitrary", "arbitrary")),
        name="diff_attn",
    )(*lams, g_col, qT, k, vTb)


def _mix_kernel(x_ref, mod_ref, att_ref, zu_ref, zvn_ref, wsp_ref, bsp_ref, wo_ref, g_ref, b_ref,
                o_ref, sg_scr, *, alpha):
    D = D_MODEL
    nc = zu_ref.shape[0]
    r = lax.broadcasted_iota(I32, (CHUNK, CHUNK), 0)
    c = lax.broadcasted_iota(I32, (CHUNK, CHUNK), 1)
    causal = r >= c
    for g in range(N_SG):
        sl = slice(g * SG_DIM, (g + 1) * SG_DIM)
        w = jnp.where(causal, wsp_ref[g], 0.0).astype(BF16)
        z = jnp.concatenate([zvn_ref[n, :, sl] for n in range(nc)], axis=1)
        mixed = jnp.dot(w, z, preferred_element_type=F32)
        for n in range(nc):
            gate = mixed[:, n * SG_DIM:(n + 1) * SG_DIM] + bsp_ref[g]
            sg_scr[n * CHUNK:(n + 1) * CHUNK, sl] = (zu_ref[n, :, sl].astype(F32) * gate).astype(BF16)
    mix = (jnp.dot(att_ref[...], wo_ref[0:DIFF_WIDTH, :], preferred_element_type=F32)
           + jnp.dot(sg_scr[...], wo_ref[DIFF_WIDTH:, :], preferred_element_type=F32))
    gt = mod_ref[:, 2 * D:3 * D]
    y = alpha * x_ref[...] + (1.0 + gt) * mix
    o_ref[...] = _layer_norm_rows(y, g_ref[...], b_ref[...])


def _mix(x, mod_l, att, zu, zvn, w_sp, b_sp_full, w_out_b, ln_g, ln_b, alpha):
    B, S, D = x.shape
    tm = min(ROW_TILE, S)
    nc = tm // CHUNK
    row = lambda b, i: (b, i, 0)
    zu4 = zu.reshape(B, S // CHUNK, CHUNK, SG_WIDTH)
    zvn4 = zvn.reshape(B, S // CHUNK, CHUNK, SG_WIDTH)
    chunked = pl.BlockSpec((None, nc, CHUNK, SG_WIDTH), lambda b, i: (b, i, 0, 0))
    full2 = lambda a: pl.BlockSpec(a.shape, lambda b, i: (0,) * a.ndim)
    return pl.pallas_call(
        functools.partial(_mix_kernel, alpha=alpha),
        grid=(B, S // tm),
        in_specs=[
            pl.BlockSpec((None, tm, D), row),
            pl.BlockSpec((None, 1, 6 * D), lambda b, i: (b, 0, 0)),
            pl.BlockSpec((None, tm, DIFF_WIDTH), row),
            chunked, chunked,
            full2(w_sp), full2(b_sp_full), full2(w_out_b), full2(ln_g), full2(ln_b),
        ],
        out_specs=pl.BlockSpec((None, tm, D), row),
        out_shape=jax.ShapeDtypeStruct((B, S, D), F32),
        scratch_shapes=[pltpu.VMEM((tm, SG_WIDTH), BF16)],
        compiler_params=_cparams(("arbitrary", "arbitrary")),
        name="sgate_outproj_ln",
    )(x, mod_l, att, zu4, zvn4, w_sp, b_sp_full, w_out_b, ln_g, ln_b)


def _route_kernel(x_ref, mod_ref, wr_hi_ref, wr_lo_ref, br_ref, hp_ref, rt_ref, cnt_ref, run_scr):
    D = D_MODEL
    tm = x_ref.shape[0]

    @pl.when((pl.program_id(0) == 0) & (pl.program_id(1) == 0))
    def _():
        run_scr[...] = jnp.zeros(run_scr.shape, F32)

    sh = mod_ref[:, 3 * D:4 * D]
    sc = mod_ref[:, 4 * D:5 * D]
    h = x_ref[...] * (1.0 + sc) + sh
    h_hi = h.astype(BF16)
    h_lo = (h - h_hi.astype(F32)).astype(BF16)
    logit = (jnp.dot(h_hi, wr_hi_ref[...], preferred_element_type=F32)
             + jnp.dot(h_lo, wr_hi_ref[...], preferred_element_type=F32)
             + jnp.dot(h_hi, wr_lo_ref[...], preferred_element_type=F32)) + br_ref[...]
    lane = lax.broadcasted_iota(I32, logit.shape, 1).astype(F32)
    neg = -jnp.inf
    big = float(LANES)

    def first_argmax(v):
        mx = jnp.max(v, axis=1, keepdims=True)
        idx = jnp.min(jnp.where(v == mx, lane, big), axis=1, keepdims=True)
        return mx, idx

    in_grp = lane < N_GROUPS
    gmax, gidx = first_argmax(jnp.where(in_grp, logit, neg))
    g_p = 1.0 / jnp.sum(jnp.where(in_grp, jnp.exp(logit - gmax), 0.0), axis=1, keepdims=True)
    lo_lane = ROUTE_LANE0 + EXP_PER_GROUP * gidx
    sel = jnp.where((lane >= lo_lane) & (lane < lo_lane + EXP_PER_GROUP), logit, neg)
    v1, i1 = first_argmax(sel)
    v2, i2 = first_argmax(jnp.where(lane == i1, neg, sel))
    t = jnp.exp(v2 - v1)
    w1 = g_p / (1.0 + t)
    w2 = g_p * t / (1.0 + t)

    first_lower = i1 < i2
    e_a = jnp.minimum(i1, i2) - lo_lane
    e_b = jnp.maximum(i1, i2) - lo_lane
    w_a = jnp.where(first_lower, w1, w2)
    w_b = jnp.where(first_lower, w2, w1)
    bucket = gidx * N_PAIRS + (e_a * (2 * EXP_PER_GROUP - 1 - e_a) * 0.5 + (e_b - e_a - 1.0))

    hp_ref[:, 0:HALF] = _pack_halves(h)
    wslab = jnp.where(lane == 0.0, w_a, jnp.where(lane == 1.0, w_b, 0.0))
    hp_ref[:, HALF:] = pltpu.bitcast(wslab, U32)

    hot = lane == bucket
    onehot = jnp.where(hot, 1.0, 0.0)
    r = lax.broadcasted_iota(I32, (tm, tm), 0)
    c = lax.broadcasted_iota(I32, (tm, tm), 1)
    strict = jnp.where(r > c, 1.0, 0.0).astype(BF16)
    before = jnp.dot(strict, onehot.astype(BF16), preferred_element_type=F32) + run_scr[...]
    rank = jnp.sum(jnp.where(hot, before, 0.0), axis=1, keepdims=True)
    run_scr[...] = run_scr[...] + jnp.sum(onehot, axis=0, keepdims=True)
    cnt_ref[...] = run_scr[...]

    slab = jnp.where(lane == 0.0, bucket, jnp.where(lane == 1.0, rank, 0.0))
    rt_ref[...] = slab.T[0:8, :]


def _route(x1, mod_l, wr_hi, wr_lo, br):
    B, S, D = x1.shape
    tm = min(ROW_TILE, S)
    row = lambda b, i: (b, i, 0)
    full2 = lambda a: pl.BlockSpec(a.shape, lambda b, i: (0,) * a.ndim)
    return pl.pallas_call(
        _route_kernel,
        grid=(B, S // tm),
        in_specs=[
            pl.BlockSpec((None, tm, D), row),
            pl.BlockSpec((None, 1, 6 * D), lambda b, i: (b, 0, 0)),
            full2(wr_hi), full2(wr_lo), full2(br),
        ],
        out_specs=[
            pl.BlockSpec((None, tm, ROW_WORDS), row),
            pl.BlockSpec((None, 8, tm), lambda b, i: (b, 0, i)),
            pl.BlockSpec((1, LANES), lambda b, i: (0, 0)),
        ],
        out_shape=[
            jax.ShapeDtypeStruct((B, S, ROW_WORDS), U32),
            jax.ShapeDtypeStruct((B, 8, S), F32),
            jax.ShapeDtypeStruct((1, LANES), F32),
        ],
        scratch_shapes=[pltpu.VMEM((1, LANES), F32)],
        compiler_params=_cparams(("arbitrary", "arbitrary")),
        name="moe_route",
    )(x1, mod_l, wr_hi, wr_lo, br)


def _row_of(ref, p):
    return ref.at[p >> 3, pl.ds(p & (SUBLANES - 1), 1)]


def _dispatch_kernel(p_ref, hp_ref, xs_in_ref, xs_ref, sem):
    del xs_in_ref
    ng = hp_ref.shape[0]

    def issue(g, carry):
        for u in range(SUBLANES):
            p = p_ref[0, 0, g * SUBLANES + u]
            pltpu.make_async_copy(hp_ref.at[g, pl.ds(u, 1)], _row_of(xs_ref, p), sem).start(priority=u % 2)
        return carry

    lax.fori_loop(0, ng, issue, 0)
    pltpu.make_async_copy(hp_ref, xs_ref.at[pl.ds(0, ng)], sem).wait()


def _dispatch(pos, hp, xs_buf):
    T = hp.shape[0]
    n_rows = xs_buf.shape[0]
    nb = min(DISPATCH_BLOCK, T)
    blk = pl.BlockSpec((1, 1, nb), lambda i: (i, 0, 0), memory_space=pltpu.SMEM)
    xs0 = xs_buf.reshape(n_rows // SUBLANES, SUBLANES, ROW_WORDS)
    xs = pl.pallas_call(
        _dispatch_kernel,
        grid=(T // nb,),
        in_specs=[blk, pl.BlockSpec((nb // SUBLANES, SUBLANES, ROW_WORDS), lambda i: (i, 0, 0)),
                  pl.BlockSpec(memory_space=pl.ANY)],
        out_specs=pl.BlockSpec(memory_space=pl.ANY),
        out_shape=jax.ShapeDtypeStruct(xs0.shape, U32),
        scratch_shapes=[pltpu.SemaphoreType.DMA(())],
        input_output_aliases={2: 0},
        compiler_params=_cparams(("arbitrary",)),
        name="moe_dispatch",
    )(pos.reshape(T // nb, 1, nb), hp.reshape(T // SUBLANES, SUBLANES, ROW_WORDS), xs0)
    return xs.reshape(n_rows, ROW_WORDS)


def _expert_kernel(ea_ref, eb_ref, nv_ref, xs_ref, wga_ref, wua_ref, wda_ref, wgb_ref, wub_ref, wdb_ref, ys_ref):
    del ea_ref, eb_ref

    @pl.when(pl.program_id(0) < nv_ref[0])
    def _():
        lo, hi = _unpack_halves(xs_ref[:, 0:HALF])
        lo = lo.astype(BF16)
        hi = hi.astype(BF16)
        wts = pltpu.bitcast(xs_ref[:, HALF:], F32)

        def hidden(wg_ref, wu_ref, w):
            g = (jnp.dot(lo, wg_ref[0:HALF, :], preferred_element_type=F32)
                 + jnp.dot(hi, wg_ref[HALF:, :], preferred_element_type=F32))
            u = (jnp.dot(lo, wu_ref[0:HALF, :], preferred_element_type=F32)
                 + jnp.dot(hi, wu_ref[HALF:, :], preferred_element_type=F32))
            return (g * jax.nn.sigmoid(g) * u * w).astype(BF16)

        y = (jnp.dot(hidden(wga_ref, wua_ref, wts[:, 0:1]), wda_ref[...], preferred_element_type=F32)
             + jnp.dot(hidden(wgb_ref, wub_ref, wts[:, 1:2]), wdb_ref[...], preferred_element_type=F32))
        ys_ref[...] = _pack_halves(y)


def _experts(tile_ea, tile_eb, n_valid, xs, w_gate, w_up, w_down, layer):
    n_rows = xs.shape[0]
    tm = EXPERT_TILE
    nt = n_rows // tm
    rows = lambda n, ea, eb, nv: (jnp.minimum(n, nv[0] - 1), 0)
    sel_a = lambda n, ea, eb, nv: (layer, ea[n], 0, 0)
    sel_b = lambda n, ea, eb, nv: (layer, eb[n], 0, 0)
    up = lambda sel: pl.BlockSpec((None, None, D_MODEL, D_EXPERT), sel)
    down = lambda sel: pl.BlockSpec((None, None, D_EXPERT, D_MODEL), sel)
    return pl.pallas_call(
        _expert_kernel,
        grid_spec=pltpu.PrefetchScalarGridSpec(
            num_scalar_prefetch=3,
            grid=(nt,),
            in_specs=[pl.BlockSpec((tm, ROW_WORDS), rows),
                      up(sel_a), up(sel_a), down(sel_a), up(sel_b), up(sel_b), down(sel_b)],
            out_specs=pl.BlockSpec((tm, HALF), rows),
        ),
        out_shape=jax.ShapeDtypeStruct((n_rows, HALF), U32),
        compiler_params=_cparams(("arbitrary",)),
        name="moe_experts",
    )(tile_ea, tile_eb, n_valid, xs, w_gate, w_up, w_down, w_gate, w_up, w_down)


def _combine_kernel(p_ref, pn_ref, ys_ref, x_ref, mod_ref, g_ref, b_ref, o_ref, buf, sems, *, alpha):
    D = D_MODEL
    tc = x_ref.shape[0]
    g = pl.program_id(0) * pl.num_programs(1) + pl.program_id(1)
    n_steps = pl.num_programs(0) * pl.num_programs(1)
    slot = g % 2

    def gather(pa_ref, s):
        def issue(g8, carry):
            for u in range(SUBLANES):
                p = pa_ref[0, 0, g8 * SUBLANES + u]
                pltpu.make_async_copy(_row_of(ys_ref, p), buf.at[s, g8, pl.ds(u, 1)], sems.at[s]).start(
                    priority=u % 2)
            return carry

        lax.fori_loop(0, tc // SUBLANES, issue, 0)

    @pl.when(g == 0)
    def _():
        gather(p_ref, 0)

    @pl.when(g + 1 < n_steps)
    def _():
        gather(pn_ref, 1 - slot)

    pltpu.make_async_copy(ys_ref.at[pl.ds(0, tc // SUBLANES)], buf.at[slot], sems.at[slot]).wait()

    lo, hi = _unpack_halves(buf[slot].reshape(tc, HALF))
    ffn = jnp.concatenate([lo, hi], axis=1)
    gt = mod_ref[:, 5 * D:6 * D]
    y = alpha * x_ref[...] + (1.0 + gt) * ffn
    o_ref[...] = _layer_norm_rows(y, g_ref[...], b_ref[...])


def _combine(pos, ys, x1, mod_l, ln_g, ln_b, alpha):
    B, S, D = x1.shape
    tc = min(COMBINE_TILE, S)
    nt = S // tc
    blk = pl.BlockSpec((1, 1, tc), lambda b, i: (b * nt + i, 0, 0), memory_space=pltpu.SMEM)
    nxt = pl.BlockSpec((1, 1, tc), lambda b, i: (jnp.minimum(b * nt + i + 1, B * nt - 1), 0, 0),
                       memory_space=pltpu.SMEM)
    row = lambda b, i: (b, i, 0)
    full2 = lambda a: pl.BlockSpec(a.shape, lambda b, i: (0,) * a.ndim)
    p = pos.reshape(B * nt, 1, tc)
    return pl.pallas_call(
        functools.partial(_combine_kernel, alpha=alpha),
        grid=(B, nt),
        in_specs=[
            blk, nxt,
            pl.BlockSpec(memory_space=pl.ANY),
            pl.BlockSpec((None, tc, D), row),
            pl.BlockSpec((None, 1, 6 * D), lambda b, i: (b, 0, 0)),
            full2(ln_g), full2(ln_b),
        ],
        out_specs=pl.BlockSpec((None, tc, D), row),
        out_shape=jax.ShapeDtypeStruct((B, S, D), F32),
        scratch_shapes=[pltpu.VMEM((2, tc // SUBLANES, SUBLANES, HALF), U32), pltpu.SemaphoreType.DMA((2,))],
        compiler_params=_cparams(("arbitrary", "arbitrary")),
        name="moe_combine_ln",
    )(p, p, ys.reshape(ys.shape[0] // SUBLANES, SUBLANES, HALF), x1, mod_l, ln_g, ln_b)


def _rope_tables(S):
    inv = 1.0 / (ROPE_THETA ** (jnp.arange(0, QK_DIM, 2, dtype=F32) / QK_DIM))
    ang = jnp.arange(S, dtype=F32)[:, None] * inv[None, :]
    cos, sin = jnp.cos(ang), jnp.sin(ang)
    half = QK_DIM // 2
    first = (jnp.arange(LANES) % QK_DIM) < half
    cs = jnp.tile(cos, (1, LANES // half))
    sn = jnp.tile(sin, (1, LANES // half))
    sa = jnp.where(first[None, :], -sn, 0.0)
    sb = jnp.where(first[None, :], 0.0, sn)
    return cs, sa, sb


def _router_matrix(w_group, b_group, w_router, b_router):
    D = w_group.shape[0]
    w = jnp.zeros((D, LANES), F32)
    w = w.at[:, 0:N_GROUPS].set(w_group)
    wr = jnp.transpose(w_router, (1, 0, 2)).reshape(D, N_EXPERTS)
    w = w.at[:, ROUTE_LANE0:ROUTE_LANE0 + N_EXPERTS].set(wr)
    b = jnp.zeros((1, LANES), F32)
    b = b.at[0, 0:N_GROUPS].set(b_group)
    b = b.at[0, ROUTE_LANE0:ROUTE_LANE0 + N_EXPERTS].set(b_router.reshape(N_EXPERTS))
    hi = w.astype(BF16)
    lo = (w - hi.astype(F32)).astype(BF16)
    return hi, lo, b


def _routing_tables(rt, cnt, n_tiles):
    B, _, S = rt.shape
    tm = EXPERT_TILE
    bucket = rt[:, 0, :].astype(I32)
    rank = rt[:, 1, :].astype(I32)
    counts = cnt[0, 0:N_BUCKETS].astype(I32)
    tiles = (counts + tm - 1) // tm
    tile_end = jnp.cumsum(tiles)
    tile_start = tile_end - tiles
    onehot = bucket[..., None] == jnp.arange(N_BUCKETS, dtype=I32)
    pos = (jnp.sum(jnp.where(onehot, tile_start * tm, 0), axis=-1) + rank).reshape(B * S)
    tile_ids = jnp.arange(n_tiles, dtype=I32)
    tile_bucket = jnp.minimum(jnp.sum(tile_ids[:, None] >= tile_end[None, :], axis=1), N_BUCKETS - 1)
    pair_a = jnp.asarray([a for a in range(EXP_PER_GROUP) for _ in range(a + 1, EXP_PER_GROUP)], I32)
    pair_b = jnp.asarray([b for a in range(EXP_PER_GROUP) for b in range(a + 1, EXP_PER_GROUP)], I32)
    group = tile_bucket // N_PAIRS
    pair = tile_bucket % N_PAIRS
    tile_ea = (group * EXP_PER_GROUP + pair_a[pair]).astype(I32)
    tile_eb = (group * EXP_PER_GROUP + pair_b[pair]).astype(I32)
    n_valid = tile_end[-1:].astype(I32)
    return pos, tile_ea, tile_eb, n_valid


def kernel(x, c, w_ada, b_ada, w_in, lambda_q1, lambda_k1, lambda_q2, lambda_k2, subln_g, sg_ln_g, sg_ln_b, w_spatial, b_spatial, w_out, ln1_g, ln1_b, w_group, b_group, w_router, b_router, w_gate, w_up, w_down, ln2_g, ln2_b):
    B, S, D = x.shape
    depth = w_in.shape[0]
    T = B * S
    alpha = (2.0 * depth) ** 0.25
    n_tiles = T // EXPERT_TILE + N_BUCKETS
    n_rows = n_tiles * EXPERT_TILE

    mod = _ada(c, w_ada, b_ada)
    tabs = _rope_tables(S)
    w_gate, w_up, w_down = w_gate.astype(BF16), w_up.astype(BF16), w_down.astype(BF16)
    xs = jnp.zeros((n_rows, ROW_WORDS), U32)
    for l in range(depth):
        mod_l = mod[l].reshape(B, 1, 6 * D)
        lam_init = 0.8 - 0.6 * math.exp(-0.3 * l)

        qT, k, vTb, zu, zvn = _inproj(x, mod_l, w_in[l].astype(BF16), tabs,
                                      sg_ln_g[l].reshape(1, SG_WIDTH), sg_ln_b[l].reshape(1, SG_WIDTH))
        lams = (lambda_q1[l].reshape(1, QK_DIM), lambda_k1[l].reshape(1, QK_DIM),
                lambda_q2[l].reshape(1, QK_DIM), lambda_k2[l].reshape(1, QK_DIM))
        att = _attention(qT, k, vTb, lams, subln_g[l].reshape(V_DIM, 1), lam_init)
        b_sp_full = jnp.broadcast_to(b_spatial[l][:, :, None], (N_SG, CHUNK, SG_DIM))
        x1 = _mix(x, mod_l, att, zu, zvn, w_spatial[l], b_sp_full, w_out[l].astype(BF16),
                  ln1_g[l].reshape(1, D), ln1_b[l].reshape(1, D), alpha)

        wr_hi, wr_lo, br = _router_matrix(w_group[l], b_group[l], w_router[l], b_router[l])
        hp, rt, cnt = _route(x1, mod_l, wr_hi, wr_lo, br)
        pos, tile_ea, tile_eb, n_valid = _routing_tables(rt, cnt, n_tiles)
        xs = _dispatch(pos, hp.reshape(T, ROW_WORDS), xs)
        ys = _experts(tile_ea, tile_eb, n_valid, xs, w_gate, w_up, w_down, l)
        x = _combine(pos, ys, x1, mod_l, ln2_g[l].reshape(1, D), ln2_b[l].reshape(1, D), alpha)
    return x
```

```python
import functools
import math

import jax
import jax.numpy as jnp
from jax import lax
from jax.experimental import pallas as pl
from jax.experimental.pallas import tpu as pltpu

F32 = jnp.float32
BF16 = jnp.bfloat16
U32 = jnp.uint32
I32 = jnp.int32

D_MODEL = 1024
N_HEADS = 4
QK_DIM = 64
V_DIM = 128
HEAD_COLS = 2 * QK_DIM
QK_COLS = N_HEADS * HEAD_COLS
DIFF_WIDTH = N_HEADS * V_DIM
N_SG = 4
SG_DIM = 128
SG_WIDTH = N_SG * SG_DIM
CHUNK = 128
N_GROUPS = 4
EXP_PER_GROUP = 8
N_EXPERTS = N_GROUPS * EXP_PER_GROUP
N_PAIRS = EXP_PER_GROUP * (EXP_PER_GROUP - 1) // 2
N_BUCKETS = N_GROUPS * N_PAIRS
D_EXPERT = 512
ROPE_THETA = 10000.0
LN_EPS = 1e-5
LANES = 128
HALF = D_MODEL // 2
ROW_WORDS = HALF + LANES

ROW_TILE = 512
EXPERT_TILE = 256
DISPATCH_BLOCK = 2048
COMBINE_TILE = 512
SUBLANES = 8
ROUTE_LANE0 = 8
VMEM_LIMIT = 48 * 1024 * 1024

DEN_MIN = 2.0 ** -40
DEN_MAX = 2.0 ** 40
LOG2E = 1.4426950408889634
Q_SCALE = (QK_DIM ** -0.5) * LOG2E


def _cparams(sem):
    return pltpu.CompilerParams(dimension_semantics=sem, vmem_limit_bytes=VMEM_LIMIT)


def _layer_norm_rows(y, g, b):
    mu = jnp.mean(y, axis=-1, keepdims=True)
    yc = y - mu
    var = jnp.mean(yc * yc, axis=-1, keepdims=True)
    return yc * lax.rsqrt(var + LN_EPS) * g + b


def _gelu(x):
    return 0.5 * x * (1.0 + lax.erf(x * (2.0 ** -0.5)))


def _pack_halves(y):
    lo = pltpu.bitcast(y[:, :HALF].astype(BF16).astype(F32), U32) >> 16
    hi = pltpu.bitcast(y[:, HALF:].astype(BF16).astype(F32), U32) & jnp.uint32(0xFFFF0000)
    return lo | hi


def _unpack_halves(p):
    lo = pltpu.bitcast(p << 16, F32)
    hi = pltpu.bitcast(p & jnp.uint32(0xFFFF0000), F32)
    return lo, hi


def _ada_kernel(c_ref, w_ref, b_ref, o_ref):
    c = c_ref[...]
    sc = c * jax.nn.sigmoid(c)
    o_ref[...] = jnp.dot(sc, w_ref[...], precision=lax.Precision.HIGHEST,
                         preferred_element_type=F32) + b_ref[...]


def _ada(c, w_ada, b_ada):
    L, D, N = w_ada.shape
    B = c.shape[0]
    tn = 1536
    return pl.pallas_call(
        _ada_kernel,
        grid=(L, N // tn),
        in_specs=[
            pl.BlockSpec((B, D), lambda l, j: (0, 0)),
            pl.BlockSpec((None, D, tn), lambda l, j: (l, 0, j)),
            pl.BlockSpec((None, 1, tn), lambda l, j: (l, 0, j)),
        ],
        out_specs=pl.BlockSpec((None, B, tn), lambda l, j: (l, 0, j)),
        out_shape=jax.ShapeDtypeStruct((L, B, N), F32),
        compiler_params=_cparams(("arbitrary", "arbitrary")),
        name="ada",
    )(c, w_ada, b_ada.reshape(L, 1, N))


def _inproj_kernel(x_ref, mod_ref, w_ref, cs_ref, sa_ref, sb_ref, lng_ref, lnb_ref,
                   qT_ref, k_ref, vT_ref, zu_ref, zvn_ref):
    D = D_MODEL
    x = x_ref[...]
    sh = mod_ref[:, 0:D]
    sc = mod_ref[:, D:2 * D]
    h = (x * (1.0 + sc) + sh).astype(BF16)
    cs = cs_ref[...]
    sa = sa_ref[...]
    sb = sb_ref[...]

    def rope(t):
        return t * cs + pltpu.roll(t, 96, 1) * sa + pltpu.roll(t, 32, 1) * sb

    q = jnp.dot(h, w_ref[:, 0:QK_COLS], preferred_element_type=F32)
    for j in range(N_HEADS):
        sl = slice(j * LANES, (j + 1) * LANES)
        qT_ref[sl, :] = (rope(q[:, sl]) * Q_SCALE).T.astype(BF16)
    k = jnp.dot(h, w_ref[:, QK_COLS:2 * QK_COLS], preferred_element_type=F32)
    for j in range(N_HEADS):
        sl = slice(j * LANES, (j + 1) * LANES)
        k_ref[:, sl] = rope(k[:, sl]).astype(BF16)
    c0 = 2 * QK_COLS
    v = jnp.dot(h, w_ref[:, c0:c0 + DIFF_WIDTH], preferred_element_type=F32)
    vT_ref[...] = v.T.astype(BF16)
    c0 += DIFF_WIDTH
    u = jnp.dot(h, w_ref[:, c0:c0 + SG_WIDTH], preferred_element_type=F32)
    zu_ref[...] = _gelu(u).astype(BF16)
    c0 += SG_WIDTH
    z = _gelu(jnp.dot(h, w_ref[:, c0:c0 + SG_WIDTH], preferred_element_type=F32))
    for g in range(N_SG):
        sl = slice(g * SG_DIM, (g + 1) * SG_DIM)
        zvn_ref[:, sl] = _layer_norm_rows(z[:, sl], lng_ref[:, sl], lnb_ref[:, sl]).astype(BF16)


def _inproj(x, mod_l, w_in_b, rope_tabs, ln_g, ln_b):
    B, S, D = x.shape
    tm = min(ROW_TILE, S)
    nt = S // tm
    cs, sa, sb = rope_tabs
    row = lambda b, i: (b, i, 0)
    tab = pl.BlockSpec((tm, LANES), lambda b, i: (i, 0))
    return pl.pallas_call(
        _inproj_kernel,
        grid=(B, nt),
        in_specs=[
            pl.BlockSpec((None, tm, D), row),
            pl.BlockSpec((None, 1, 6 * D), lambda b, i: (b, 0, 0)),
            pl.BlockSpec(w_in_b.shape, lambda b, i: (0, 0)),
            tab, tab, tab,
            pl.BlockSpec((1, SG_WIDTH), lambda b, i: (0, 0)),
            pl.BlockSpec((1, SG_WIDTH), lambda b, i: (0, 0)),
        ],
        out_specs=[
            pl.BlockSpec((None, QK_COLS, tm), lambda b, i: (b, 0, i)),
            pl.BlockSpec((None, tm, QK_COLS), row),
            pl.BlockSpec((None, None, DIFF_WIDTH, tm), lambda b, i: (b, i, 0, 0)),
            pl.BlockSpec((None, tm, SG_WIDTH), row),
            pl.BlockSpec((None, tm, SG_WIDTH), row),
        ],
        out_shape=[
            jax.ShapeDtypeStruct((B, QK_COLS, S), BF16),
            jax.ShapeDtypeStruct((B, S, QK_COLS), BF16),
            jax.ShapeDtypeStruct((B, nt, DIFF_WIDTH, tm), BF16),
            jax.ShapeDtypeStruct((B, S, SG_WIDTH), BF16),
            jax.ShapeDtypeStruct((B, S, SG_WIDTH), BF16),
        ],
        compiler_params=_cparams(("arbitrary", "arbitrary")),
        name="inproj",
    )(x, mod_l, w_in_b, cs, sa, sb, ln_g, ln_b)


def _attn_kernel(lq1_ref, lk1_ref, lq2_ref, lk2_ref, g_ref, qT_ref, k_ref, vT_ref, o_ref,
                 q_scr, s_scr, m_scr, l_scr, acc_scr, *, lam_init):
    tq = qT_ref.shape[1]
    tk = vT_ref.shape[2]
    i = pl.program_id(2)
    qT = qT_ref[...]
    rows = lax.broadcasted_iota(I32, qT.shape, 0)
    zero = jnp.zeros_like(qT)
    q_scr[0] = jnp.where(rows < QK_DIM, qT, zero)
    q_scr[1] = jnp.where(rows >= QK_DIM, qT, zero)

    def scores(j, slot, c0):
        kb = k_ref[pl.ds(pl.multiple_of(j * tk, tk), tk), :]
        for mp in range(2):
            s_scr[slot, mp, :, c0:] = jnp.dot(kb, q_scr[mp, :, c0:], preferred_element_type=F32)

    def block_scores(slot, mp, c0, c1, masked):
        s = s_scr[slot, mp, :, c0:c1]
        if masked:
            kpos = lax.broadcasted_iota(I32, s.shape, 0)
            qpos = lax.broadcasted_iota(I32, s.shape, 1)
            s = jnp.where(kpos <= qpos, s, -jnp.inf)
        return s

    def pv_unshifted(j, slot, c0, c1, masked):
        vb = vT_ref[j]
        for mp in range(2):
            p = jnp.exp2(block_scores(slot, mp, c0, c1, masked))
            l_scr[mp, :, c0:c1] += jnp.sum(p, axis=0, keepdims=True)
            acc_scr[mp, :, c0:c1] += jnp.dot(vb, p.astype(BF16), preferred_element_type=F32)

    def pv_online(j, slot, c0, c1, masked):
        vb = vT_ref[j]
        for mp in range(2):
            s = block_scores(slot, mp, c0, c1, masked)
            m_old = m_scr[mp, :, c0:c1]
            m_new = jnp.maximum(m_old, jnp.max(s, axis=0, keepdims=True))
            alpha = jnp.exp2(m_old - m_new)
            p = jnp.exp2(s - m_new)
            l_scr[mp, :, c0:c1] = alpha * l_scr[mp, :, c0:c1] + jnp.sum(p, axis=0, keepdims=True)
            acc_scr[mp, :, c0:c1] = (alpha * acc_scr[mp, :, c0:c1]
                                     + jnp.dot(vb, p.astype(BF16), preferred_element_type=F32))
            m_scr[mp, :, c0:c1] = m_new

    def sweep(block):
        acc_scr[...] = jnp.zeros(acc_scr.shape, F32)
        l_scr[...] = jnp.zeros(l_scr.shape, F32)
        scores(0, 0, 0)

        def pair(t):
            scores(t, 1, 0)
            block(t - 1, 0, 0, tq, False)
            scores(t + 1, 0, 0)
            block(t, 1, 0, tq, False)

        def body(u, carry):
            pair(4 * u + 1)
            pair(4 * u + 3)
            return carry

        lax.fori_loop(0, i // 2, body, 0)

        @pl.when(i % 2 == 1)
        def _():
            pair(2 * i - 1)

        scores(2 * i + 1, 1, tk)
        block(2 * i, 0, 0, tk, True)
        block(2 * i, 0, tk, tq, False)
        block(2 * i + 1, 1, tk, tq, True)

    def finalize():
        lam = (jnp.exp(jnp.sum(lq1_ref[...] * lk1_ref[...], axis=1, keepdims=True))
               - jnp.exp(jnp.sum(lq2_ref[...] * lk2_ref[...], axis=1, keepdims=True)) + lam_init)
        out = acc_scr[0] / l_scr[0] - lam * (acc_scr[1] / l_scr[1])
        ms = jnp.mean(out * out, axis=0, keepdims=True)
        y = out * lax.rsqrt(ms + LN_EPS) * g_ref[...] * (1.0 - lam_init)
        o_ref[...] = y.T.astype(BF16)

    sweep(pv_unshifted)
    finalize()
    den = jnp.concatenate([l_scr[0], l_scr[1]], axis=0)
    in_range = (den >= DEN_MIN) & (den <= DEN_MAX)
    n_bad = jnp.sum(jnp.where(in_range, 0.0, 1.0))

    @pl.when(n_bad > 0.0)
    def _():
        m_scr[...] = jnp.full(m_scr.shape, -jnp.inf, F32)
        sweep(pv_online)
        finalize()


def _attention(qT, k, vTb, lams, g_col, lam_init):
    B, _, S = qT.shape
    nkv, tk = vTb.shape[1], vTb.shape[3]
    tq = 2 * tk
    small = pl.BlockSpec((1, QK_DIM), lambda b, h, i: (0, 0))
    return pl.pallas_call(
        functools.partial(_attn_kernel, lam_init=lam_init),
        grid=(B, N_HEADS, S // tq),
        in_specs=[
            small, small, small, small,
            pl.BlockSpec((V_DIM, 1), lambda b, h, i: (0, 0)),
            pl.BlockSpec((None, HEAD_COLS, tq), lambda b, h, i: (b, h, i)),
            pl.BlockSpec((None, S, HEAD_COLS), lambda b, h, i: (b, 0, h)),
            pl.BlockSpec((None, nkv, V_DIM, tk), lambda b, h, i: (b, 0, h, 0)),
        ],
        out_specs=pl.BlockSpec((None, tq, V_DIM), lambda b, h, i: (b, i, h)),
        out_shape=jax.ShapeDtypeStruct((B, S, DIFF_WIDTH), BF16),
        scratch_shapes=[
            pltpu.VMEM((2, HEAD_COLS, tq), BF16),
            pltpu.VMEM((2, 2, tk, tq), F32),
            pltpu.VMEM((2, 1, tq), F32),
            pltpu.VMEM((2, 1, tq), F32),
            pltpu.VMEM((2, V_DIM, tq), F32),
        ],
        compiler_params=_cparams(("arbitrary", "arbitrary", "arbitrary")),
        name="diff_attn",
    )(*lams, g_col, qT, k, vTb)


def _mix_kernel(x_ref, mod_ref, att_ref, zu_ref, zvn_ref, wsp_ref, bsp_ref, wo_ref, g_ref, b_ref,
                o_ref, sg_scr, *, alpha):
    D = D_MODEL
    nc = zu_ref.shape[0]
    r = lax.broadcasted_iota(I32, (CHUNK, CHUNK), 0)
    c = lax.broadcasted_iota(I32, (CHUNK, CHUNK), 1)
    causal = r >= c
    for g in range(N_SG):
        sl = slice(g * SG_DIM, (g + 1) * SG_DIM)
        w = jnp.where(causal, wsp_ref[g], 0.0).astype(BF16)
        z = jnp.concatenate([zvn_ref[n, :, sl] for n in range(nc)], axis=1)
        mixed = jnp.dot(w, z, preferred_element_type=F32)
        for n in range(nc):
            gate = mixed[:, n * SG_DIM:(n + 1) * SG_DIM] + bsp_ref[g]
            sg_scr[n * CHUNK:(n + 1) * CHUNK, sl] = (zu_ref[n, :, sl].astype(F32) * gate).astype(BF16)
    mix = (jnp.dot(att_ref[...], wo_ref[0:DIFF_WIDTH, :], preferred_element_type=F32)
           + jnp.dot(sg_scr[...], wo_ref[DIFF_WIDTH:, :], preferred_element_type=F32))
    gt = mod_ref[:, 2 * D:3 * D]
    y = alpha * x_ref[...] + (1.0 + gt) * mix
    o_ref[...] = _layer_norm_rows(y, g_ref[...], b_ref[...])


def _mix(x, mod_l, att, zu, zvn, w_sp, b_sp_full, w_out_b, ln_g, ln_b, alpha):
    B, S, D = x.shape
    tm = min(ROW_TILE, S)
    nc = tm // CHUNK
    row = lambda b, i: (b, i, 0)
    zu4 = zu.reshape(B, S // CHUNK, CHUNK, SG_WIDTH)
    zvn4 = zvn.reshape(B, S // CHUNK, CHUNK, SG_WIDTH)
    chunked = pl.BlockSpec((None, nc, CHUNK, SG_WIDTH), lambda b, i: (b, i, 0, 0))
    full2 = lambda a: pl.BlockSpec(a.shape, lambda b, i: (0,) * a.ndim)
    return pl.pallas_call(
        functools.partial(_mix_kernel, alpha=alpha),
        grid=(B, S // tm),
        in_specs=[
            pl.BlockSpec((None, tm, D), row),
            pl.BlockSpec((None, 1, 6 * D), lambda b, i: (b, 0, 0)),
            pl.BlockSpec((None, tm, DIFF_WIDTH), row),
            chunked, chunked,
            full2(w_sp), full2(b_sp_full), full2(w_out_b), full2(ln_g), full2(ln_b),
        ],
        out_specs=pl.BlockSpec((None, tm, D), row),
        out_shape=jax.ShapeDtypeStruct((B, S, D), F32),
        scratch_shapes=[pltpu.VMEM((tm, SG_WIDTH), BF16)],
        compiler_params=_cparams(("arbitrary", "arbitrary")),
        name="sgate_outproj_ln",
    )(x, mod_l, att, zu4, zvn4, w_sp, b_sp_full, w_out_b, ln_g, ln_b)


def _route_kernel(x_ref, mod_ref, wr_hi_ref, wr_lo_ref, br_ref, hp_ref, rt_ref, cnt_ref, run_scr):
    D = D_MODEL
    tm = x_ref.shape[0]

    @pl.when((pl.program_id(0) == 0) & (pl.program_id(1) == 0))
    def _():
        run_scr[...] = jnp.zeros(run_scr.shape, F32)

    sh = mod_ref[:, 3 * D:4 * D]
    sc = mod_ref[:, 4 * D:5 * D]
    h = x_ref[...] * (1.0 + sc) + sh
    h_hi = h.astype(BF16)
    h_lo = (h - h_hi.astype(F32)).astype(BF16)
    logit = (jnp.dot(h_hi, wr_hi_ref[...], preferred_element_type=F32)
             + jnp.dot(h_lo, wr_hi_ref[...], preferred_element_type=F32)
             + jnp.dot(h_hi, wr_lo_ref[...], preferred_element_type=F32)) + br_ref[...]
    lt = logit.T[0:ROUTE_LANE0 + N_EXPERTS, :]
    row = lax.broadcasted_iota(I32, lt.shape, 0).astype(F32)
    neg = -jnp.inf
    big = float(LANES)

    def first_argmax(v):
        mx = jnp.max(v, axis=0, keepdims=True)
        idx = jnp.min(jnp.where(v == mx, row, big), axis=0, keepdims=True)
        return mx, idx

    in_grp = row < N_GROUPS
    gmax, gidx = first_argmax(jnp.where(in_grp, lt, neg))
    g_p = 1.0 / jnp.sum(jnp.where(in_grp, jnp.exp(lt - gmax), 0.0), axis=0, keepdims=True)
    lo_row = ROUTE_LANE0 + EXP_PER_GROUP * gidx
    sel = jnp.where((row >= lo_row) & (row < lo_row + EXP_PER_GROUP), lt, neg)
    v1, i1 = first_argmax(sel)
    v2, i2 = first_argmax(jnp.where(row == i1, neg, sel))
    t = jnp.exp(v2 - v1)
    w1 = g_p / (1.0 + t)
    w2 = g_p * t / (1.0 + t)

    first_lower = i1 < i2
    e_a = jnp.minimum(i1, i2) - lo_row
    e_b = jnp.maximum(i1, i2) - lo_row
    w_a = jnp.where(first_lower, w1, w2)
    w_b = jnp.where(first_lower, w2, w1)
    bucket = gidx * N_PAIRS + (e_a * (2 * EXP_PER_GROUP - 1 - e_a) * 0.5 + (e_b - e_a - 1.0))

    brow = lax.broadcasted_iota(I32, (LANES, tm), 0).astype(F32)
    hp_ref[:, 0:HALF] = _pack_halves(h)
    wslab = jnp.where(brow == 0.0, w_a, jnp.where(brow == 1.0, w_b, 0.0))
    hp_ref[:, HALF:] = pltpu.bitcast(wslab.T, U32)

    hot = brow == bucket
    onehot = jnp.where(hot, 1.0, 0.0)
    r = lax.broadcasted_iota(I32, (tm, tm), 0)
    c = lax.broadcasted_iota(I32, (tm, tm), 1)
    earlier = jnp.where(r < c, 1.0, 0.0).astype(BF16)
    before = jnp.dot(onehot.astype(BF16), earlier, preferred_element_type=F32) + run_scr[...]
    rank = jnp.sum(jnp.where(hot, before, 0.0), axis=0, keepdims=True)
    run_scr[...] = run_scr[...] + jnp.sum(onehot, axis=1, keepdims=True)
    cnt_ref[...] = run_scr[...]

    r8 = lax.broadcasted_iota(I32, (8, tm), 0)
    rt_ref[...] = jnp.where(r8 == 0, bucket, jnp.where(r8 == 1, rank, 0.0))


def _route(x1, mod_l, wr_hi, wr_lo, br):
    B, S, D = x1.shape
    tm = min(ROW_TILE, S)
    row = lambda b, i: (b, i, 0)
    full2 = lambda a: pl.BlockSpec(a.shape, lambda b, i: (0,) * a.ndim)
    return pl.pallas_call(
        _route_kernel,
        grid=(B, S // tm),
        in_specs=[
            pl.BlockSpec((None, tm, D), row),
            pl.BlockSpec((None, 1, 6 * D), lambda b, i: (b, 0, 0)),
            full2(wr_hi), full2(wr_lo), full2(br),
        ],
        out_specs=[
            pl.BlockSpec((None, tm, ROW_WORDS), row),
            pl.BlockSpec((None, 8, tm), lambda b, i: (b, 0, i)),
            pl.BlockSpec((LANES, 1), lambda b, i: (0, 0)),
        ],
        out_shape=[
            jax.ShapeDtypeStruct((B, S, ROW_WORDS), U32),
            jax.ShapeDtypeStruct((B, 8, S), F32),
            jax.ShapeDtypeStruct((LANES, 1), F32),
        ],
        scratch_shapes=[pltpu.VMEM((LANES, 1), F32)],
        compiler_params=_cparams(("arbitrary", "arbitrary")),
        name="moe_route",
    )(x1, mod_l, wr_hi, wr_lo, br)


def _row_of(ref, p):
    return ref.at[p >> 3, pl.ds(p & (SUBLANES - 1), 1)]


def _dispatch_kernel(p_ref, hp_ref, xs_in_ref, xs_ref, sem):
    del xs_in_ref
    ng = hp_ref.shape[0]

    def issue(g, carry):
        for u in range(SUBLANES):
            p = p_ref[0, 0, g * SUBLANES + u]
            pltpu.make_async_copy(hp_ref.at[g, pl.ds(u, 1)], _row_of(xs_ref, p), sem).start(priority=u % 2)
        return carry

    lax.fori_loop(0, ng, issue, 0)
    pltpu.make_async_copy(hp_ref, xs_ref.at[pl.ds(0, ng)], sem).wait()


def _dispatch(pos, hp, xs_buf):
    T = hp.shape[0]
    n_rows = xs_buf.shape[0]
    nb = min(DISPATCH_BLOCK, T)
    blk = pl.BlockSpec((1, 1, nb), lambda i: (i, 0, 0), memory_space=pltpu.SMEM)
    xs0 = xs_buf.reshape(n_rows // SUBLANES, SUBLANES, ROW_WORDS)
    xs = pl.pallas_call(
        _dispatch_kernel,
        grid=(T // nb,),
        in_specs=[blk, pl.BlockSpec((nb // SUBLANES, SUBLANES, ROW_WORDS), lambda i: (i, 0, 0)),
                  pl.BlockSpec(memory_space=pl.ANY)],
        out_specs=pl.BlockSpec(memory_space=pl.ANY),
        out_shape=jax.ShapeDtypeStruct(xs0.shape, U32),
        scratch_shapes=[pltpu.SemaphoreType.DMA(())],
        input_output_aliases={2: 0},
        compiler_params=_cparams(("arbitrary",)),
        name="moe_dispatch",
    )(pos.reshape(T // nb, 1, nb), hp.reshape(T // SUBLANES, SUBLANES, ROW_WORDS), xs0)
    return xs.reshape(n_rows, ROW_WORDS)


def _expert_kernel(ea_ref, eb_ref, nv_ref, xs_ref, wga_ref, wua_ref, wda_ref, wgb_ref, wub_ref, wdb_ref, ys_ref):
    del ea_ref, eb_ref

    @pl.when(pl.program_id(0) < nv_ref[0])
    def _():
        lo, hi = _unpack_halves(xs_ref[:, 0:HALF])
        lo = lo.astype(BF16)
        hi = hi.astype(BF16)
        wts = pltpu.bitcast(xs_ref[:, HALF:], F32)

        def hidden(wg_ref, wu_ref, w):
            g = (jnp.dot(lo, wg_ref[0:HALF, :], preferred_element_type=F32)
                 + jnp.dot(hi, wg_ref[HALF:, :], preferred_element_type=F32))
            u = (jnp.dot(lo, wu_ref[0:HALF, :], preferred_element_type=F32)
                 + jnp.dot(hi, wu_ref[HALF:, :], preferred_element_type=F32))
            return (g * jax.nn.sigmoid(g) * u * w).astype(BF16)

        y = (jnp.dot(hidden(wga_ref, wua_ref, wts[:, 0:1]), wda_ref[...], preferred_element_type=F32)
             + jnp.dot(hidden(wgb_ref, wub_ref, wts[:, 1:2]), wdb_ref[...], preferred_element_type=F32))
        ys_ref[...] = _pack_halves(y)


def _experts(tile_ea, tile_eb, n_valid, xs, w_gate, w_up, w_down, layer):
    n_rows = xs.shape[0]
    tm = EXPERT_TILE
    nt = n_rows // tm
    rows = lambda n, ea, eb, nv: (jnp.minimum(n, nv[0] - 1), 0)
    sel_a = lambda n, ea, eb, nv: (layer, ea[n], 0, 0)
    sel_b = lambda n, ea, eb, nv: (layer, eb[n], 0, 0)
    up = lambda sel: pl.BlockSpec((None, None, D_MODEL, D_EXPERT), sel)
    down = lambda sel: pl.BlockSpec((None, None, D_EXPERT, D_MODEL), sel)
    return pl.pallas_call(
        _expert_kernel,
        grid_spec=pltpu.PrefetchScalarGridSpec(
            num_scalar_prefetch=3,
            grid=(nt,),
            in_specs=[pl.BlockSpec((tm, ROW_WORDS), rows),
                      up(sel_a), up(sel_a), down(sel_a), up(sel_b), up(sel_b), down(sel_b)],
            out_specs=pl.BlockSpec((tm, HALF), rows),
        ),
        out_shape=jax.ShapeDtypeStruct((n_rows, HALF), U32),
        compiler_params=_cparams(("arbitrary",)),
        name="moe_experts",
    )(tile_ea, tile_eb, n_valid, xs, w_gate, w_up, w_down, w_gate, w_up, w_down)


def _combine_kernel(p_ref, pn_ref, ys_ref, x_ref, mod_ref, g_ref, b_ref, o_ref, buf, sems, *, alpha):
    D = D_MODEL
    tc = x_ref.shape[0]
    g = pl.program_id(0) * pl.num_programs(1) + pl.program_id(1)
    n_steps = pl.num_programs(0) * pl.num_programs(1)
    slot = g % 2

    def gather(pa_ref, s):
        def issue(g8, carry):
            for u in range(SUBLANES):
                p = pa_ref[0, 0, g8 * SUBLANES + u]
                pltpu.make_async_copy(_row_of(ys_ref, p), buf.at[s, g8, pl.ds(u, 1)], sems.at[s]).start(
                    priority=u % 2)
            return carry

        lax.fori_loop(0, tc // SUBLANES, issue, 0)

    @pl.when(g == 0)
    def _():
        gather(p_ref, 0)

    @pl.when(g + 1 < n_steps)
    def _():
        gather(pn_ref, 1 - slot)

    pltpu.make_async_copy(ys_ref.at[pl.ds(0, tc // SUBLANES)], buf.at[slot], sems.at[slot]).wait()

    lo, hi = _unpack_halves(buf[slot].reshape(tc, HALF))
    ffn = jnp.concatenate([lo, hi], axis=1)
    gt = mod_ref[:, 5 * D:6 * D]
    y = alpha * x_ref[...] + (1.0 + gt) * ffn
    o_ref[...] = _layer_norm_rows(y, g_ref[...], b_ref[...])


def _combine(pos, ys, x1, mod_l, ln_g, ln_b, alpha):
    B, S, D = x1.shape
    tc = min(COMBINE_TILE, S)
    nt = S // tc
    blk = pl.BlockSpec((1, 1, tc), lambda b, i: (b * nt + i, 0, 0), memory_space=pltpu.SMEM)
    nxt = pl.BlockSpec((1, 1, tc), lambda b, i: (jnp.minimum(b * nt + i + 1, B * nt - 1), 0, 0),
                       memory_space=pltpu.SMEM)
    row = lambda b, i: (b, i, 0)
    full2 = lambda a: pl.BlockSpec(a.shape, lambda b, i: (0,) * a.ndim)
    p = pos.reshape(B * nt, 1, tc)
    return pl.pallas_call(
        functools.partial(_combine_kernel, alpha=alpha),
        grid=(B, nt),
        in_specs=[
            blk, nxt,
            pl.BlockSpec(memory_space=pl.ANY),
            pl.BlockSpec((None, tc, D), row),
            pl.BlockSpec((None, 1, 6 * D), lambda b, i: (b, 0, 0)),
            full2(ln_g), full2(ln_b),
        ],
        out_specs=pl.BlockSpec((None, tc, D), row),
        out_shape=jax.ShapeDtypeStruct((B, S, D), F32),
        scratch_shapes=[pltpu.VMEM((2, tc // SUBLANES, SUBLANES, HALF), U32), pltpu.SemaphoreType.DMA((2,))],
        compiler_params=_cparams(("arbitrary", "arbitrary")),
        name="moe_combine_ln",
    )(p, p, ys.reshape(ys.shape[0] // SUBLANES, SUBLANES, HALF), x1, mod_l, ln_g, ln_b)


def _rope_tables(S):
    inv = 1.0 / (ROPE_THETA ** (jnp.arange(0, QK_DIM, 2, dtype=F32) / QK_DIM))
    ang = jnp.arange(S, dtype=F32)[:, None] * inv[None, :]
    cos, sin = jnp.cos(ang), jnp.sin(ang)
    half = QK_DIM // 2
    first = (jnp.arange(LANES) % QK_DIM) < half
    cs = jnp.tile(cos, (1, LANES // half))
    sn = jnp.tile(sin, (1, LANES // half))
    sa = jnp.where(first[None, :], -sn, 0.0)
    sb = jnp.where(first[None, :], 0.0, sn)
    return cs, sa, sb


def _router_matrix(w_group, b_group, w_router, b_router):
    D = w_group.shape[0]
    w = jnp.zeros((D, LANES), F32)
    w = w.at[:, 0:N_GROUPS].set(w_group)
    wr = jnp.transpose(w_router, (1, 0, 2)).reshape(D, N_EXPERTS)
    w = w.at[:, ROUTE_LANE0:ROUTE_LANE0 + N_EXPERTS].set(wr)
    b = jnp.zeros((1, LANES), F32)
    b = b.at[0, 0:N_GROUPS].set(b_group)
    b = b.at[0, ROUTE_LANE0:ROUTE_LANE0 + N_EXPERTS].set(b_router.reshape(N_EXPERTS))
    hi = w.astype(BF16)
    lo = (w - hi.astype(F32)).astype(BF16)
    return hi, lo, b


def _routing_tables(rt, cnt, n_tiles):
    B, _, S = rt.shape
    tm = EXPERT_TILE
    bucket = rt[:, 0, :].astype(I32)
    rank = rt[:, 1, :].astype(I32)
    counts = cnt[0:N_BUCKETS, 0].astype(I32)
    tiles = (counts + tm - 1) // tm
    tile_end = jnp.cumsum(tiles)
    tile_start = tile_end - tiles
    onehot = bucket[..., None] == jnp.arange(N_BUCKETS, dtype=I32)
    pos = (jnp.sum(jnp.where(onehot, tile_start * tm, 0), axis=-1) + rank).reshape(B * S)
    tile_ids = jnp.arange(n_tiles, dtype=I32)
    tile_bucket = jnp.minimum(jnp.sum(tile_ids[:, None] >= tile_end[None, :], axis=1), N_BUCKETS - 1)
    pair_a = jnp.asarray([a for a in range(EXP_PER_GROUP) for _ in range(a + 1, EXP_PER_GROUP)], I32)
    pair_b = jnp.asarray([b for a in range(EXP_PER_GROUP) for b in range(a + 1, EXP_PER_GROUP)], I32)
    group = tile_bucket // N_PAIRS
    pair = tile_bucket % N_PAIRS
    tile_ea = (group * EXP_PER_GROUP + pair_a[pair]).astype(I32)
    tile_eb = (group * EXP_PER_GROUP + pair_b[pair]).astype(I32)
    n_valid = tile_end[-1:].astype(I32)
    return pos, tile_ea, tile_eb, n_valid


def kernel(x, c, w_ada, b_ada, w_in, lambda_q1, lambda_k1, lambda_q2, lambda_k2, subln_g, sg_ln_g, sg_ln_b, w_spatial, b_spatial, w_out, ln1_g, ln1_b, w_group, b_group, w_router, b_router, w_gate, w_up, w_down, ln2_g, ln2_b):
    B, S, D = x.shape
    depth = w_in.shape[0]
    T = B * S
    alpha = (2.0 * depth) ** 0.25
    n_tiles = T // EXPERT_TILE + N_BUCKETS
    n_rows = n_tiles * EXPERT_TILE

    mod = _ada(c, w_ada, b_ada)
    tabs = _rope_tables(S)
    w_gate, w_up, w_down = w_gate.astype(BF16), w_up.astype(BF16), w_down.astype(BF16)
    xs = jnp.zeros((n_rows, ROW_WORDS), U32)
    for l in range(depth):
        mod_l = mod[l].reshape(B, 1, 6 * D)
        lam_init = 0.8 - 0.6 * math.exp(-0.3 * l)

        qT, k, vTb, zu, zvn = _inproj(x, mod_l, w_in[l].astype(BF16), tabs,
                                      sg_ln_g[l].reshape(1, SG_WIDTH), sg_ln_b[l].reshape(1, SG_WIDTH))
        lams = (lambda_q1[l].reshape(1, QK_DIM), lambda_k1[l].reshape(1, QK_DIM),
                lambda_q2[l].reshape(1, QK_DIM), lambda_k2[l].reshape(1, QK_DIM))
        att = _attention(qT, k, vTb, lams, subln_g[l].reshape(V_DIM, 1), lam_init)
        b_sp_full = jnp.broadcast_to(b_spatial[l][:, :, None], (N_SG, CHUNK, SG_DIM))
        x1 = _mix(x, mod_l, att, zu, zvn, w_spatial[l], b_sp_full, w_out[l].astype(BF16),
                  ln1_g[l].reshape(1, D), ln1_b[l].reshape(1, D), alpha)

        wr_hi, wr_lo, br = _router_matrix(w_group[l], b_group[l], w_router[l], b_router[l])
        hp, rt, cnt = _route(x1, mod_l, wr_hi, wr_lo, br)
        pos, tile_ea, tile_eb, n_valid = _routing_tables(rt, cnt, n_tiles)
        xs = _dispatch(pos, hp.reshape(T, ROW_WORDS), xs)
        ys = _experts(tile_ea, tile_eb, n_valid, xs, w_gate, w_up, w_down, l)
        x = _combine(pos, ys, x1, mod_l, ln2_g[l].reshape(1, D), ln2_b[l].reshape(1, D), alpha)
    return x
```

```python
import functools
import math

import jax
import jax.numpy as jnp
from jax import lax
from jax.experimental import pallas as pl
from jax.experimental.pallas import tpu as pltpu

F32 = jnp.float32
BF16 = jnp.bfloat16
U32 = jnp.uint32
I32 = jnp.int32

D_MODEL = 1024
N_HEADS = 4
QK_DIM = 64
V_DIM = 128
HEAD_COLS = 2 * QK_DIM
QK_COLS = N_HEADS * HEAD_COLS
DIFF_WIDTH = N_HEADS * V_DIM
N_SG = 4
SG_DIM = 128
SG_WIDTH = N_SG * SG_DIM
CHUNK = 128
N_GROUPS = 4
EXP_PER_GROUP = 8
N_EXPERTS = N_GROUPS * EXP_PER_GROUP
N_PAIRS = EXP_PER_GROUP * (EXP_PER_GROUP - 1) // 2
N_BUCKETS = N_GROUPS * N_PAIRS
D_EXPERT = 512
ROPE_THETA = 10000.0
LN_EPS = 1e-5
LANES = 128
HALF = D_MODEL // 2
ROW_WORDS = HALF + LANES

ROW_TILE = 512
EXPERT_TILE = 256
DISPATCH_BLOCK = 2048
COMBINE_TILE = 512
SUBLANES = 8
ROUTE_LANE0 = 8
VMEM_LIMIT = 48 * 1024 * 1024

DEN_MIN = 2.0 ** -40
DEN_MAX = 2.0 ** 40
LOG2E = 1.4426950408889634
Q_SCALE = (QK_DIM ** -0.5) * LOG2E


def _cparams(sem):
    return pltpu.CompilerParams(dimension_semantics=sem, vmem_limit_bytes=VMEM_LIMIT)


def _layer_norm_rows(y, g, b):
    mu = jnp.mean(y, axis=-1, keepdims=True)
    yc = y - mu
    var = jnp.mean(yc * yc, axis=-1, keepdims=True)
    return yc * lax.rsqrt(var + LN_EPS) * g + b


def _gelu(x):
    return 0.5 * x * (1.0 + lax.erf(x * (2.0 ** -0.5)))


def _pack_halves(y):
    lo = pltpu.bitcast(y[:, :HALF].astype(BF16).astype(F32), U32) >> 16
    hi = pltpu.bitcast(y[:, HALF:].astype(BF16).astype(F32), U32) & jnp.uint32(0xFFFF0000)
    return lo | hi


def _unpack_halves(p):
    lo = pltpu.bitcast(p << 16, F32)
    hi = pltpu.bitcast(p & jnp.uint32(0xFFFF0000), F32)
    return lo, hi


def _ada_kernel(c_ref, w_ref, b_ref, o_ref):
    c = c_ref[...]
    sc = c * jax.nn.sigmoid(c)
    o_ref[...] = jnp.dot(sc, w_ref[...], precision=lax.Precision.HIGHEST,
                         preferred_element_type=F32) + b_ref[...]


def _ada(c, w_ada, b_ada):
    L, D, N = w_ada.shape
    B = c.shape[0]
    tn = 1536
    return pl.pallas_call(
        _ada_kernel,
        grid=(L, N // tn),
        in_specs=[
            pl.BlockSpec((B, D), lambda l, j: (0, 0)),
            pl.BlockSpec((None, D, tn), lambda l, j: (l, 0, j)),
            pl.BlockSpec((None, 1, tn), lambda l, j: (l, 0, j)),
        ],
        out_specs=pl.BlockSpec((None, B, tn), lambda l, j: (l, 0, j)),
        out_shape=jax.ShapeDtypeStruct((L, B, N), F32),
        compiler_params=_cparams(("arbitrary", "arbitrary")),
        name="ada",
    )(c, w_ada, b_ada.reshape(L, 1, N))


def _inproj_kernel(x_ref, mod_ref, w_ref, cs_ref, ss_ref, lng_ref, lnb_ref,
                   qT_ref, k_ref, vT_ref, zu_ref, zvn_ref):
    D = D_MODEL
    x = x_ref[...]
    sh = mod_ref[:, 0:D]
    sc = mod_ref[:, D:2 * D]
    h = (x * (1.0 + sc) + sh).astype(BF16)
    cs = cs_ref[...]
    ss = ss_ref[...]

    def rope(t):
        return t * cs + pltpu.roll(t, LANES // 2, 1) * ss

    q = jnp.dot(h, w_ref[:, 0:QK_COLS], preferred_element_type=F32)
    for j in range(N_HEADS):
        sl = slice(j * LANES, (j + 1) * LANES)
        qT_ref[sl, :] = (rope(q[:, sl]) * Q_SCALE).T.astype(BF16)
    k = jnp.dot(h, w_ref[:, QK_COLS:2 * QK_COLS], preferred_element_type=F32)
    for j in range(N_HEADS):
        sl = slice(j * LANES, (j + 1) * LANES)
        k_ref[:, sl] = rope(k[:, sl]).astype(BF16)
    c0 = 2 * QK_COLS
    v = jnp.dot(h, w_ref[:, c0:c0 + DIFF_WIDTH], preferred_element_type=F32)
    vT_ref[...] = v.T.astype(BF16)
    c0 += DIFF_WIDTH
    u = jnp.dot(h, w_ref[:, c0:c0 + SG_WIDTH], preferred_element_type=F32)
    zu_ref[...] = _gelu(u).astype(BF16)
    c0 += SG_WIDTH
    z = _gelu(jnp.dot(h, w_ref[:, c0:c0 + SG_WIDTH], preferred_element_type=F32))
    for g in range(N_SG):
        sl = slice(g * SG_DIM, (g + 1) * SG_DIM)
        zvn_ref[:, sl] = _layer_norm_rows(z[:, sl], lng_ref[:, sl], lnb_ref[:, sl]).astype(BF16)


def _inproj(x, mod_l, w_in_b, rope_tabs, ln_g, ln_b):
    B, S, D = x.shape
    tm = min(ROW_TILE, S)
    nt = S // tm
    cs, ss = rope_tabs
    row = lambda b, i: (b, i, 0)
    tab = pl.BlockSpec((tm, LANES), lambda b, i: (i, 0))
    return pl.pallas_call(
        _inproj_kernel,
        grid=(B, nt),
        in_specs=[
            pl.BlockSpec((None, tm, D), row),
            pl.BlockSpec((None, 1, 6 * D), lambda b, i: (b, 0, 0)),
            pl.BlockSpec(w_in_b.shape, lambda b, i: (0, 0)),
            tab, tab,
            pl.BlockSpec((1, SG_WIDTH), lambda b, i: (0, 0)),
            pl.BlockSpec((1, SG_WIDTH), lambda b, i: (0, 0)),
        ],
        out_specs=[
            pl.BlockSpec((None, QK_COLS, tm), lambda b, i: (b, 0, i)),
            pl.BlockSpec((None, tm, QK_COLS), row),
            pl.BlockSpec((None, None, DIFF_WIDTH, tm), lambda b, i: (b, i, 0, 0)),
            pl.BlockSpec((None, tm, SG_WIDTH), row),
            pl.BlockSpec((None, tm, SG_WIDTH), row),
        ],
        out_shape=[
            jax.ShapeDtypeStruct((B, QK_COLS, S), BF16),
            jax.ShapeDtypeStruct((B, S, QK_COLS), BF16),
            jax.ShapeDtypeStruct((B, nt, DIFF_WIDTH, tm), BF16),
            jax.ShapeDtypeStruct((B, S, SG_WIDTH), BF16),
            jax.ShapeDtypeStruct((B, S, SG_WIDTH), BF16),
        ],
        compiler_params=_cparams(("arbitrary", "arbitrary")),
        name="inproj",
    )(x, mod_l, w_in_b, cs, ss, ln_g, ln_b)


def _attn_kernel(lq1_ref, lk1_ref, lq2_ref, lk2_ref, g_ref, qT_ref, k_ref, vT_ref, o_ref,
                 q_scr, s_scr, m_scr, l_scr, acc_scr, *, lam_init):
    tq = qT_ref.shape[1]
    tk = vT_ref.shape[2]
    i = pl.program_id(2)
    qT = qT_ref[...]
    rows = lax.broadcasted_iota(I32, qT.shape, 0)
    first_map = (rows & (QK_DIM // 2)) == 0
    zero = jnp.zeros_like(qT)
    q_scr[0] = jnp.where(first_map, qT, zero)
    q_scr[1] = jnp.where(first_map, zero, qT)

    def scores(j, slot, c0):
        kb = k_ref[pl.ds(pl.multiple_of(j * tk, tk), tk), :]
        for mp in range(2):
            s_scr[slot, mp, :, c0:] = jnp.dot(kb, q_scr[mp, :, c0:], preferred_element_type=F32)

    def block_scores(slot, mp, c0, c1, masked):
        s = s_scr[slot, mp, :, c0:c1]
        if masked:
            kpos = lax.broadcasted_iota(I32, s.shape, 0)
            qpos = lax.broadcasted_iota(I32, s.shape, 1)
            s = jnp.where(kpos <= qpos, s, -jnp.inf)
        return s

    def pv_unshifted(j, slot, c0, c1, masked):
        vb = vT_ref[j]
        for mp in range(2):
            p = jnp.exp2(block_scores(slot, mp, c0, c1, masked))
            l_scr[mp, :, c0:c1] += jnp.sum(p, axis=0, keepdims=True)
            acc_scr[mp, :, c0:c1] += jnp.dot(vb, p.astype(BF16), preferred_element_type=F32)

    def pv_online(j, slot, c0, c1, masked):
        vb = vT_ref[j]
        for mp in range(2):
            s = block_scores(slot, mp, c0, c1, masked)
            m_old = m_scr[mp, :, c0:c1]
            m_new = jnp.maximum(m_old, jnp.max(s, axis=0, keepdims=True))
            alpha = jnp.exp2(m_old - m_new)
            p = jnp.exp2(s - m_new)
            l_scr[mp, :, c0:c1] = alpha * l_scr[mp, :, c0:c1] + jnp.sum(p, axis=0, keepdims=True)
            acc_scr[mp, :, c0:c1] = (alpha * acc_scr[mp, :, c0:c1]
                                     + jnp.dot(vb, p.astype(BF16), preferred_element_type=F32))
            m_scr[mp, :, c0:c1] = m_new

    def sweep(block):
        acc_scr[...] = jnp.zeros(acc_scr.shape, F32)
        l_scr[...] = jnp.zeros(l_scr.shape, F32)
        scores(0, 0, 0)

        def pair(t):
            scores(t, 1, 0)
            block(t - 1, 0, 0, tq, False)
            scores(t + 1, 0, 0)
            block(t, 1, 0, tq, False)

        def body(u, carry):
            pair(4 * u + 1)
            pair(4 * u + 3)
            return carry

        lax.fori_loop(0, i // 2, body, 0)

        @pl.when(i % 2 == 1)
        def _():
            pair(2 * i - 1)

        scores(2 * i + 1, 1, tk)
        block(2 * i, 0, 0, tk, True)
        block(2 * i, 0, tk, tq, False)
        block(2 * i + 1, 1, tk, tq, True)

    def finalize():
        lam = (jnp.exp(jnp.sum(lq1_ref[...] * lk1_ref[...], axis=1, keepdims=True))
               - jnp.exp(jnp.sum(lq2_ref[...] * lk2_ref[...], axis=1, keepdims=True)) + lam_init)
        out = acc_scr[0] / l_scr[0] - lam * (acc_scr[1] / l_scr[1])
        ms = jnp.mean(out * out, axis=0, keepdims=True)
        y = out * lax.rsqrt(ms + LN_EPS) * g_ref[...] * (1.0 - lam_init)
        o_ref[...] = y.T.astype(BF16)

    sweep(pv_unshifted)
    finalize()
    den = jnp.concatenate([l_scr[0], l_scr[1]], axis=0)
    in_range = (den >= DEN_MIN) & (den <= DEN_MAX)
    n_bad = jnp.sum(jnp.where(in_range, 0.0, 1.0))

    @pl.when(n_bad > 0.0)
    def _():
        m_scr[...] = jnp.full(m_scr.shape, -jnp.inf, F32)
        sweep(pv_online)
        finalize()


def _attention(qT, k, vTb, lams, g_col, lam_init):
    B, _, S = qT.shape
    nkv, tk = vTb.shape[1], vTb.shape[3]
    tq = 2 * tk
    small = pl.BlockSpec((1, QK_DIM), lambda b, h, i: (0, 0))
    return pl.pallas_call(
        functools.partial(_attn_kernel, lam_init=lam_init),
        grid=(B, N_HEADS, S // tq),
        in_specs=[
            small, small, small, small,
            pl.BlockSpec((V_DIM, 1), lambda b, h, i: (0, 0)),
            pl.BlockSpec((None, HEAD_COLS, tq), lambda b, h, i: (b, h, i)),
            pl.BlockSpec((None, S, HEAD_COLS), lambda b, h, i: (b, 0, h)),
            pl.BlockSpec((None, nkv, V_DIM, tk), lambda b, h, i: (b, 0, h, 0)),
        ],
        out_specs=pl.BlockSpec((None, tq, V_DIM), lambda b, h, i: (b, i, h)),
        out_shape=jax.ShapeDtypeStruct((B, S, DIFF_WIDTH), BF16),
        scratch_shapes=[
            pltpu.VMEM((2, HEAD_COLS, tq), BF16),
            pltpu.VMEM((2, 2, tk, tq), F32),
            pltpu.VMEM((2, 1, tq), F32),
            pltpu.VMEM((2, 1, tq), F32),
            pltpu.VMEM((2, V_DIM, tq), F32),
        ],
        compiler_params=_cparams(("arbitrary", "arbitrary", "arbitrary")),
        name="diff_attn",
    )(*lams, g_col, qT, k, vTb)


def _mix_kernel(x_ref, mod_ref, att_ref, zu_ref, zvn_ref, wsp_ref, bsp_ref, wo_ref, g_ref, b_ref,
                o_ref, sg_scr, *, alpha):
    D = D_MODEL
    nc = zu_ref.shape[0]
    r = lax.broadcasted_iota(I32, (CHUNK, CHUNK), 0)
    c = lax.broadcasted_iota(I32, (CHUNK, CHUNK), 1)
    causal = r >= c
    for g in range(N_SG):
        sl = slice(g * SG_DIM, (g + 1) * SG_DIM)
        w = jnp.where(causal, wsp_ref[g], 0.0).astype(BF16)
        z = jnp.concatenate([zvn_ref[n, :, sl] for n in range(nc)], axis=1)
        mixed = jnp.dot(w, z, preferred_element_type=F32)
        for n in range(nc):
            gate = mixed[:, n * SG_DIM:(n + 1) * SG_DIM] + bsp_ref[g]
            sg_scr[n * CHUNK:(n + 1) * CHUNK, sl] = (zu_ref[n, :, sl].astype(F32) * gate).astype(BF16)
    gt1 = 1.0 + mod_ref[:, 2 * D:3 * D]
    half = x_ref.shape[0] // 2
    for hf in range(2):
        rows = slice(hf * half, (hf + 1) * half)
        mix = (jnp.dot(att_ref[rows, :], wo_ref[0:DIFF_WIDTH, :], preferred_element_type=F32)
               + jnp.dot(sg_scr[rows, :], wo_ref[DIFF_WIDTH:, :], preferred_element_type=F32))
        y = alpha * x_ref[rows, :] + gt1 * mix
        o_ref[rows, :] = _layer_norm_rows(y, g_ref[...], b_ref[...])


def _mix(x, mod_l, att, zu, zvn, w_sp, b_sp_full, w_out_b, ln_g, ln_b, alpha):
    B, S, D = x.shape
    tm = min(ROW_TILE, S)
    nc = tm // CHUNK
    row = lambda b, i: (b, i, 0)
    zu4 = zu.reshape(B, S // CHUNK, CHUNK, SG_WIDTH)
    zvn4 = zvn.reshape(B, S // CHUNK, CHUNK, SG_WIDTH)
    chunked = pl.BlockSpec((None, nc, CHUNK, SG_WIDTH), lambda b, i: (b, i, 0, 0))
    full2 = lambda a: pl.BlockSpec(a.shape, lambda b, i: (0,) * a.ndim)
    return pl.pallas_call(
        functools.partial(_mix_kernel, alpha=alpha),
        grid=(B, S // tm),
        in_specs=[
            pl.BlockSpec((None, tm, D), row),
            pl.BlockSpec((None, 1, 6 * D), lambda b, i: (b, 0, 0)),
            pl.BlockSpec((None, tm, DIFF_WIDTH), row),
            chunked, chunked,
            full2(w_sp), full2(b_sp_full), full2(w_out_b), full2(ln_g), full2(ln_b),
        ],
        out_specs=pl.BlockSpec((None, tm, D), row),
        out_shape=jax.ShapeDtypeStruct((B, S, D), F32),
        scratch_shapes=[pltpu.VMEM((tm, SG_WIDTH), BF16)],
        compiler_params=_cparams(("arbitrary", "arbitrary")),
        name="sgate_outproj_ln",
    )(x, mod_l, att, zu4, zvn4, w_sp, b_sp_full, w_out_b, ln_g, ln_b)


def _route_kernel(x_ref, mod_ref, wr_hi_ref, wr_lo_ref, br_ref, hp_ref, rt_ref, cnt_ref, run_scr):
    D = D_MODEL
    tm = x_ref.shape[0]

    @pl.when((pl.program_id(0) == 0) & (pl.program_id(1) == 0))
    def _():
        run_scr[...] = jnp.zeros(run_scr.shape, F32)

    sh = mod_ref[:, 3 * D:4 * D]
    sc = mod_ref[:, 4 * D:5 * D]
    h = x_ref[...] * (1.0 + sc) + sh
    h_hi = h.astype(BF16)
    h_lo = (h - h_hi.astype(F32)).astype(BF16)
    logit = (jnp.dot(h_hi, wr_hi_ref[...], preferred_element_type=F32)
             + jnp.dot(h_lo, wr_hi_ref[...], preferred_element_type=F32)
             + jnp.dot(h_hi, wr_lo_ref[...], preferred_element_type=F32)) + br_ref[...]
    lt = logit.T[0:ROUTE_LANE0 + N_EXPERTS, :]
    row = lax.broadcasted_iota(I32, lt.shape, 0).astype(F32)
    neg = -jnp.inf
    big = float(LANES)

    def first_argmax(v):
        mx = jnp.max(v, axis=0, keepdims=True)
        idx = jnp.min(jnp.where(v == mx, row, big), axis=0, keepdims=True)
        return mx, idx

    in_grp = row < N_GROUPS
    gmax, gidx = first_argmax(jnp.where(in_grp, lt, neg))
    g_p = 1.0 / jnp.sum(jnp.where(in_grp, jnp.exp(lt - gmax), 0.0), axis=0, keepdims=True)
    lo_row = ROUTE_LANE0 + EXP_PER_GROUP * gidx
    sel = jnp.where((row >= lo_row) & (row < lo_row + EXP_PER_GROUP), lt, neg)
    v1, i1 = first_argmax(sel)
    v2, i2 = first_argmax(jnp.where(row == i1, neg, sel))
    t = jnp.exp(v2 - v1)
    w1 = g_p / (1.0 + t)
    w2 = g_p * t / (1.0 + t)

    first_lower = i1 < i2
    e_a = jnp.minimum(i1, i2) - lo_row
    e_b = jnp.maximum(i1, i2) - lo_row
    w_a = jnp.where(first_lower, w1, w2)
    w_b = jnp.where(first_lower, w2, w1)
    bucket = gidx * N_PAIRS + (e_a * (2 * EXP_PER_GROUP - 1 - e_a) * 0.5 + (e_b - e_a - 1.0))

    brow = lax.broadcasted_iota(I32, (LANES, tm), 0).astype(F32)
    hp_ref[:, 0:HALF] = _pack_halves(h)
    wslab = jnp.where(brow == 0.0, w_a, jnp.where(brow == 1.0, w_b, 0.0))
    hp_ref[:, HALF:] = pltpu.bitcast(wslab.T, U32)

    hot = brow == bucket
    onehot = jnp.where(hot, 1.0, 0.0)
    r = lax.broadcasted_iota(I32, (tm, tm), 0)
    c = lax.broadcasted_iota(I32, (tm, tm), 1)
    earlier = jnp.where(r < c, 1.0, 0.0).astype(BF16)
    before = jnp.dot(onehot.astype(BF16), earlier, preferred_element_type=F32) + run_scr[...]
    rank = jnp.sum(jnp.where(hot, before, 0.0), axis=0, keepdims=True)
    run_scr[...] = run_scr[...] + jnp.sum(onehot, axis=1, keepdims=True)
    cnt_ref[...] = run_scr[...]

    r8 = lax.broadcasted_iota(I32, (8, tm), 0)
    rt_ref[...] = jnp.where(r8 == 0, bucket, jnp.where(r8 == 1, rank, 0.0))


def _route(x1, mod_l, wr_hi, wr_lo, br):
    B, S, D = x1.shape
    tm = min(ROW_TILE, S)
    row = lambda b, i: (b, i, 0)
    full2 = lambda a: pl.BlockSpec(a.shape, lambda b, i: (0,) * a.ndim)
    return pl.pallas_call(
        _route_kernel,
        grid=(B, S // tm),
        in_specs=[
            pl.BlockSpec((None, tm, D), row),
            pl.BlockSpec((None, 1, 6 * D), lambda b, i: (b, 0, 0)),
            full2(wr_hi), full2(wr_lo), full2(br),
        ],
        out_specs=[
            pl.BlockSpec((None, tm, ROW_WORDS), row),
            pl.BlockSpec((None, 8, tm), lambda b, i: (b, 0, i)),
            pl.BlockSpec((LANES, 1), lambda b, i: (0, 0)),
        ],
        out_shape=[
            jax.ShapeDtypeStruct((B, S, ROW_WORDS), U32),
            jax.ShapeDtypeStruct((B, 8, S), F32),
            jax.ShapeDtypeStruct((LANES, 1), F32),
        ],
        scratch_shapes=[pltpu.VMEM((LANES, 1), F32)],
        compiler_params=_cparams(("arbitrary", "arbitrary")),
        name="moe_route",
    )(x1, mod_l, wr_hi, wr_lo, br)


def _row_of(ref, p):
    return ref.at[p >> 3, pl.ds(p & (SUBLANES - 1), 1)]


def _dispatch_kernel(p_ref, hp_ref, xs_in_ref, xs_ref, sem):
    del xs_in_ref
    ng = hp_ref.shape[0]

    def issue(g, carry):
        for u in range(SUBLANES):
            p = p_ref[0, 0, g * SUBLANES + u]
            pltpu.make_async_copy(hp_ref.at[g, pl.ds(u, 1)], _row_of(xs_ref, p), sem).start(priority=u % 2)
        return carry

    lax.fori_loop(0, ng, issue, 0)
    pltpu.make_async_copy(hp_ref, xs_ref.at[pl.ds(0, ng)], sem).wait()


def _dispatch(pos, hp, xs_buf):
    T = hp.shape[0]
    n_rows = xs_buf.shape[0]
    nb = min(DISPATCH_BLOCK, T)
    blk = pl.BlockSpec((1, 1, nb), lambda i: (i, 0, 0), memory_space=pltpu.SMEM)
    xs0 = xs_buf.reshape(n_rows // SUBLANES, SUBLANES, ROW_WORDS)
    xs = pl.pallas_call(
        _dispatch_kernel,
        grid=(T // nb,),
        in_specs=[blk, pl.BlockSpec((nb // SUBLANES, SUBLANES, ROW_WORDS), lambda i: (i, 0, 0)),
                  pl.BlockSpec(memory_space=pl.ANY)],
        out_specs=pl.BlockSpec(memory_space=pl.ANY),
        out_shape=jax.ShapeDtypeStruct(xs0.shape, U32),
        scratch_shapes=[pltpu.SemaphoreType.DMA(())],
        input_output_aliases={2: 0},
        compiler_params=_cparams(("arbitrary",)),
        name="moe_dispatch",
    )(pos.reshape(T // nb, 1, nb), hp.reshape(T // SUBLANES, SUBLANES, ROW_WORDS), xs0)
    return xs.reshape(n_rows, ROW_WORDS)


def _expert_kernel(ea_ref, eb_ref, nv_ref, xs_ref, wga_ref, wua_ref, wda_ref, wgb_ref, wub_ref, wdb_ref, ys_ref):
    del ea_ref, eb_ref

    @pl.when(pl.program_id(0) < nv_ref[0])
    def _():
        lo, hi = _unpack_halves(xs_ref[:, 0:HALF])
        lo = lo.astype(BF16)
        hi = hi.astype(BF16)
        wts = pltpu.bitcast(xs_ref[:, HALF:], F32)

        def hidden(wg_ref, wu_ref, w):
            g = (jnp.dot(lo, wg_ref[0:HALF, :], preferred_element_type=F32)
                 + jnp.dot(hi, wg_ref[HALF:, :], preferred_element_type=F32))
            u = (jnp.dot(lo, wu_ref[0:HALF, :], preferred_element_type=F32)
                 + jnp.dot(hi, wu_ref[HALF:, :], preferred_element_type=F32))
            return (g * jax.nn.sigmoid(g) * u * w).astype(BF16)

        y = (jnp.dot(hidden(wga_ref, wua_ref, wts[:, 0:1]), wda_ref[...], preferred_element_type=F32)
             + jnp.dot(hidden(wgb_ref, wub_ref, wts[:, 1:2]), wdb_ref[...], preferred_element_type=F32))
        ys_ref[...] = _pack_halves(y)


def _experts(tile_ea, tile_eb, n_valid, xs, w_gate, w_up, w_down, layer):
    n_rows = xs.shape[0]
    tm = EXPERT_TILE
    nt = n_rows // tm
    rows = lambda n, ea, eb, nv: (jnp.minimum(n, nv[0] - 1), 0)
    sel_a = lambda n, ea, eb, nv: (layer, ea[n], 0, 0)
    sel_b = lambda n, ea, eb, nv: (layer, eb[n], 0, 0)
    up = lambda sel: pl.BlockSpec((None, None, D_MODEL, D_EXPERT), sel)
    down = lambda sel: pl.BlockSpec((None, None, D_EXPERT, D_MODEL), sel)
    return pl.pallas_call(
        _expert_kernel,
        grid_spec=pltpu.PrefetchScalarGridSpec(
            num_scalar_prefetch=3,
            grid=(nt,),
            in_specs=[pl.BlockSpec((tm, ROW_WORDS), rows),
                      up(sel_a), up(sel_a), down(sel_a), up(sel_b), up(sel_b), down(sel_b)],
            out_specs=pl.BlockSpec((tm, HALF), rows),
        ),
        out_shape=jax.ShapeDtypeStruct((n_rows, HALF), U32),
        compiler_params=_cparams(("arbitrary",)),
        name="moe_experts",
    )(tile_ea, tile_eb, n_valid, xs, w_gate, w_up, w_down, w_gate, w_up, w_down)


def _combine_kernel(p_ref, pn_ref, ys_ref, x_ref, mod_ref, g_ref, b_ref, o_ref, buf, sems, *, alpha):
    D = D_MODEL
    tc = x_ref.shape[0]
    g = pl.program_id(0) * pl.num_programs(1) + pl.program_id(1)
    n_steps = pl.num_programs(0) * pl.num_programs(1)
    slot = g % 2

    def gather(pa_ref, s):
        def issue(g8, carry):
            for u in range(SUBLANES):
                p = pa_ref[0, 0, g8 * SUBLANES + u]
                pltpu.make_async_copy(_row_of(ys_ref, p), buf.at[s, g8, pl.ds(u, 1)], sems.at[s]).start(
                    priority=u % 2)
            return carry

        lax.fori_loop(0, tc // SUBLANES, issue, 0)

    @pl.when(g == 0)
    def _():
        gather(p_ref, 0)

    @pl.when(g + 1 < n_steps)
    def _():
        gather(pn_ref, 1 - slot)

    pltpu.make_async_copy(ys_ref.at[pl.ds(0, tc // SUBLANES)], buf.at[slot], sems.at[slot]).wait()

    lo, hi = _unpack_halves(buf[slot].reshape(tc, HALF))
    ffn = jnp.concatenate([lo, hi], axis=1)
    gt = mod_ref[:, 5 * D:6 * D]
    y = alpha * x_ref[...] + (1.0 + gt) * ffn
    o_ref[...] = _layer_norm_rows(y, g_ref[...], b_ref[...])


def _combine(pos, ys, x1, mod_l, ln_g, ln_b, alpha):
    B, S, D = x1.shape
    tc = min(COMBINE_TILE, S)
    nt = S // tc
    blk = pl.BlockSpec((1, 1, tc), lambda b, i: (b * nt + i, 0, 0), memory_space=pltpu.SMEM)
    nxt = pl.BlockSpec((1, 1, tc), lambda b, i: (jnp.minimum(b * nt + i + 1, B * nt - 1), 0, 0),
                       memory_space=pltpu.SMEM)
    row = lambda b, i: (b, i, 0)
    full2 = lambda a: pl.BlockSpec(a.shape, lambda b, i: (0,) * a.ndim)
    p = pos.reshape(B * nt, 1, tc)
    return pl.pallas_call(
        functools.partial(_combine_kernel, alpha=alpha),
        grid=(B, nt),
        in_specs=[
            blk, nxt,
            pl.BlockSpec(memory_space=pl.ANY),
            pl.BlockSpec((None, tc, D), row),
            pl.BlockSpec((None, 1, 6 * D), lambda b, i: (b, 0, 0)),
            full2(ln_g), full2(ln_b),
        ],
        out_specs=pl.BlockSpec((None, tc, D), row),
        out_shape=jax.ShapeDtypeStruct((B, S, D), F32),
        scratch_shapes=[pltpu.VMEM((2, tc // SUBLANES, SUBLANES, HALF), U32), pltpu.SemaphoreType.DMA((2,))],
        compiler_params=_cparams(("arbitrary", "arbitrary")),
        name="moe_combine_ln",
    )(p, p, ys.reshape(ys.shape[0] // SUBLANES, SUBLANES, HALF), x1, mod_l, ln_g, ln_b)


def _rope_tables(S):
    inv = 1.0 / (ROPE_THETA ** (jnp.arange(0, QK_DIM, 2, dtype=F32) / QK_DIM))
    ang = jnp.arange(S, dtype=F32)[:, None] * inv[None, :]
    cos, sin = jnp.cos(ang), jnp.sin(ang)
    half = QK_DIM // 2
    cs = jnp.tile(cos, (1, LANES // half))
    sn = jnp.tile(sin, (1, LANES // half))
    first = jnp.arange(LANES) < LANES // 2
    ss = jnp.where(first[None, :], -sn, sn)
    return cs, ss


def _head_lane_order():
    half = QK_DIM // 2
    lane = jnp.arange(LANES)
    src = ((lane % QK_DIM) // half) * QK_DIM + (lane // QK_DIM) * half + lane % half
    return (jnp.arange(N_HEADS)[:, None] * LANES + src[None, :]).reshape(QK_COLS)


def _in_weights(w_in_l):
    order = _head_lane_order()
    w = w_in_l.astype(BF16)
    return jnp.concatenate([w[:, 0:QK_COLS][:, order], w[:, QK_COLS:2 * QK_COLS][:, order], w[:, 2 * QK_COLS:]],
                           axis=1)


def _router_matrix(w_group, b_group, w_router, b_router):
    D = w_group.shape[0]
    w = jnp.zeros((D, LANES), F32)
    w = w.at[:, 0:N_GROUPS].set(w_group)
    wr = jnp.transpose(w_router, (1, 0, 2)).reshape(D, N_EXPERTS)
    w = w.at[:, ROUTE_LANE0:ROUTE_LANE0 + N_EXPERTS].set(wr)
    b = jnp.zeros((1, LANES), F32)
    b = b.at[0, 0:N_GROUPS].set(b_group)
    b = b.at[0, ROUTE_LANE0:ROUTE_LANE0 + N_EXPERTS].set(b_router.reshape(N_EXPERTS))
    hi = w.astype(BF16)
    lo = (w - hi.astype(F32)).astype(BF16)
    return hi, lo, b


def _routing_tables(rt, cnt, n_tiles):
    B, _, S = rt.shape
    tm = EXPERT_TILE
    bucket = rt[:, 0, :].astype(I32)
    rank = rt[:, 1, :].astype(I32)
    counts = cnt[0:N_BUCKETS, 0].astype(I32)
    tiles = (counts + tm - 1) // tm
    tile_end = jnp.cumsum(tiles)
    tile_start = tile_end - tiles
    onehot = bucket[..., None] == jnp.arange(N_BUCKETS, dtype=I32)
    pos = (jnp.sum(jnp.where(onehot, tile_start * tm, 0), axis=-1) + rank).reshape(B * S)
    tile_ids = jnp.arange(n_tiles, dtype=I32)
    tile_bucket = jnp.minimum(jnp.sum(tile_ids[:, None] >= tile_end[None, :], axis=1), N_BUCKETS - 1)
    pair_a = jnp.asarray([a for a in range(EXP_PER_GROUP) for _ in range(a + 1, EXP_PER_GROUP)], I32)
    pair_b = jnp.asarray([b for a in range(EXP_PER_GROUP) for b in range(a + 1, EXP_PER_GROUP)], I32)
    group = tile_bucket // N_PAIRS
    pair = tile_bucket % N_PAIRS
    tile_ea = (group * EXP_PER_GROUP + pair_a[pair]).astype(I32)
    tile_eb = (group * EXP_PER_GROUP + pair_b[pair]).astype(I32)
    n_valid = tile_end[-1:].astype(I32)
    return pos, tile_ea, tile_eb, n_valid


def kernel(x, c, w_ada, b_ada, w_in, lambda_q1, lambda_k1, lambda_q2, lambda_k2, subln_g, sg_ln_g, sg_ln_b, w_spatial, b_spatial, w_out, ln1_g, ln1_b, w_group, b_group, w_router, b_router, w_gate, w_up, w_down, ln2_g, ln2_b):
    B, S, D = x.shape
    depth = w_in.shape[0]
    T = B * S
    alpha = (2.0 * depth) ** 0.25
    n_tiles = T // EXPERT_TILE + N_BUCKETS
    n_rows = n_tiles * EXPERT_TILE

    mod = _ada(c, w_ada, b_ada)
    tabs = _rope_tables(S)
    w_gate, w_up, w_down = w_gate.astype(BF16), w_up.astype(BF16), w_down.astype(BF16)
    xs = jnp.zeros((n_rows, ROW_WORDS), U32)
    for l in range(depth):
        mod_l = mod[l].reshape(B, 1, 6 * D)
        lam_init = 0.8 - 0.6 * math.exp(-0.3 * l)

        qT, k, vTb, zu, zvn = _inproj(x, mod_l, _in_weights(w_in[l]), tabs,
                                      sg_ln_g[l].reshape(1, SG_WIDTH), sg_ln_b[l].reshape(1, SG_WIDTH))
        lams = (lambda_q1[l].reshape(1, QK_DIM), lambda_k1[l].reshape(1, QK_DIM),
                lambda_q2[l].reshape(1, QK_DIM), lambda_k2[l].reshape(1, QK_DIM))
        att = _attention(qT, k, vTb, lams, subln_g[l].reshape(V_DIM, 1), lam_init)
        b_sp_full = jnp.broadcast_to(b_spatial[l][:, :, None], (N_SG, CHUNK, SG_DIM))
        x1 = _mix(x, mod_l, att, zu, zvn, w_spatial[l], b_sp_full, w_out[l].astype(BF16),
                  ln1_g[l].reshape(1, D), ln1_b[l].reshape(1, D), alpha)

        wr_hi, wr_lo, br = _router_matrix(w_group[l], b_group[l], w_router[l], b_router[l])
        hp, rt, cnt = _route(x1, mod_l, wr_hi, wr_lo, br)
        pos, tile_ea, tile_eb, n_valid = _routing_tables(rt, cnt, n_tiles)
        xs = _dispatch(pos, hp.reshape(T, ROW_WORDS), xs)
        ys = _experts(tile_ea, tile_eb, n_valid, xs, w_gate, w_up, w_down, l)
        x = _combine(pos, ys, x1, mod_l, ln2_g[l].reshape(1, D), ln2_b[l].reshape(1, D), alpha)
    return x
```

```python
import functools
import math

import jax
import jax.numpy as jnp
from jax import lax
from jax.experimental import pallas as pl
from jax.experimental.pallas import tpu as pltpu

F32 = jnp.float32
BF16 = jnp.bfloat16
U32 = jnp.uint32
I32 = jnp.int32

D_MODEL = 1024
N_HEADS = 4
QK_DIM = 64
V_DIM = 128
HEAD_COLS = 2 * QK_DIM
QK_COLS = N_HEADS * HEAD_COLS
DIFF_WIDTH = N_HEADS * V_DIM
N_SG = 4
SG_DIM = 128
SG_WIDTH = N_SG * SG_DIM
CHUNK = 128
N_GROUPS = 4
EXP_PER_GROUP = 8
N_EXPERTS = N_GROUPS * EXP_PER_GROUP
N_PAIRS = EXP_PER_GROUP * (EXP_PER_GROUP - 1) // 2
N_BUCKETS = N_GROUPS * N_PAIRS
D_EXPERT = 512
ROPE_THETA = 10000.0
LN_EPS = 1e-5
LANES = 128
HALF = D_MODEL // 2
ROW_WORDS = HALF + LANES

ROW_TILE = 512
EXPERT_TILE = 256
DISPATCH_BLOCK = 2048
COMBINE_TILE = 512
SUBLANES = 8
ROUTE_LANE0 = 8
VMEM_LIMIT = 48 * 1024 * 1024

DEN_MIN = 2.0 ** -40
DEN_MAX = 2.0 ** 40
LOG2E = 1.4426950408889634
Q_SCALE = (QK_DIM ** -0.5) * LOG2E


def _cparams(sem):
    return pltpu.CompilerParams(dimension_semantics=sem, vmem_limit_bytes=VMEM_LIMIT)


def _layer_norm_rows(y, g, b):
    mu = jnp.mean(y, axis=-1, keepdims=True)
    yc = y - mu
    var = jnp.mean(yc * yc, axis=-1, keepdims=True)
    return yc * lax.rsqrt(var + LN_EPS) * g + b


def _gelu(x):
    return 0.5 * x * (1.0 + lax.erf(x * (2.0 ** -0.5)))


def _pack_halves(y):
    lo = pltpu.bitcast(y[:, :HALF].astype(BF16).astype(F32), U32) >> 16
    hi = pltpu.bitcast(y[:, HALF:].astype(BF16).astype(F32), U32) & jnp.uint32(0xFFFF0000)
    return lo | hi


def _unpack_halves(p):
    lo = pltpu.bitcast(p << 16, F32)
    hi = pltpu.bitcast(p & jnp.uint32(0xFFFF0000), F32)
    return lo, hi


def _ada_kernel(c_ref, w_ref, b_ref, o_ref):
    c = c_ref[...]
    sc = c * jax.nn.sigmoid(c)
    o_ref[...] = jnp.dot(sc, w_ref[...], precision=lax.Precision.HIGHEST,
                         preferred_element_type=F32) + b_ref[...]


def _ada(c, w_ada, b_ada):
    L, D, N = w_ada.shape
    B = c.shape[0]
    tn = 1536
    return pl.pallas_call(
        _ada_kernel,
        grid=(L, N // tn),
        in_specs=[
            pl.BlockSpec((B, D), lambda l, j: (0, 0)),
            pl.BlockSpec((None, D, tn), lambda l, j: (l, 0, j)),
            pl.BlockSpec((None, 1, tn), lambda l, j: (l, 0, j)),
        ],
        out_specs=pl.BlockSpec((None, B, tn), lambda l, j: (l, 0, j)),
        out_shape=jax.ShapeDtypeStruct((L, B, N), F32),
        compiler_params=_cparams(("arbitrary", "arbitrary")),
        name="ada",
    )(c, w_ada, b_ada.reshape(L, 1, N))


def _inproj_kernel(x_ref, mod_ref, w_ref, cs_ref, sa_ref, sb_ref, lng_ref, lnb_ref,
                   qT_ref, k_ref, vT_ref, zu_ref, zvn_ref):
    D = D_MODEL
    x = x_ref[...]
    sh = mod_ref[:, 0:D]
    sc = mod_ref[:, D:2 * D]
    h = (x * (1.0 + sc) + sh).astype(BF16)
    cs = cs_ref[...]
    sa = sa_ref[...]
    sb = sb_ref[...]

    def rope(t, rows):
        return t * cs[rows] + pltpu.roll(t, 96, 1) * sa[rows] + pltpu.roll(t, 32, 1) * sb[rows]

    half = x.shape[0] // 2
    for hf in range(2):
        rows = slice(hf * half, (hf + 1) * half)
        hh = h[rows, :]
        q = jnp.dot(hh, w_ref[:, 0:QK_COLS], preferred_element_type=F32)
        for j in range(N_HEADS):
            sl = slice(j * LANES, (j + 1) * LANES)
            qT_ref[sl, rows] = (rope(q[:, sl], rows) * Q_SCALE).T.astype(BF16)
        k = jnp.dot(hh, w_ref[:, QK_COLS:2 * QK_COLS], preferred_element_type=F32)
        for j in range(N_HEADS):
            sl = slice(j * LANES, (j + 1) * LANES)
            k_ref[rows, sl] = rope(k[:, sl], rows).astype(BF16)
        c0 = 2 * QK_COLS
        v = jnp.dot(hh, w_ref[:, c0:c0 + DIFF_WIDTH], preferred_element_type=F32)
        vT_ref[:, rows] = v.T.astype(BF16)
        c0 += DIFF_WIDTH
        u = jnp.dot(hh, w_ref[:, c0:c0 + SG_WIDTH], preferred_element_type=F32)
        zu_ref[rows, :] = _gelu(u).astype(BF16)
        c0 += SG_WIDTH
        z = _gelu(jnp.dot(hh, w_ref[:, c0:c0 + SG_WIDTH], preferred_element_type=F32))
        for g in range(N_SG):
            sl = slice(g * SG_DIM, (g + 1) * SG_DIM)
            zvn_ref[rows, sl] = _layer_norm_rows(z[:, sl], lng_ref[:, sl], lnb_ref[:, sl]).astype(BF16)


def _inproj(x, mod_l, w_in_b, rope_tabs, ln_g, ln_b):
    B, S, D = x.shape
    tm = min(ROW_TILE, S)
    nt = S // tm
    cs, sa, sb = rope_tabs
    row = lambda b, i: (b, i, 0)
    tab = pl.BlockSpec((tm, LANES), lambda b, i: (i, 0))
    return pl.pallas_call(
        _inproj_kernel,
        grid=(B, nt),
        in_specs=[
            pl.BlockSpec((None, tm, D), row),
            pl.BlockSpec((None, 1, 6 * D), lambda b, i: (b, 0, 0)),
            pl.BlockSpec(w_in_b.shape, lambda b, i: (0, 0)),
            tab, tab, tab,
            pl.BlockSpec((1, SG_WIDTH), lambda b, i: (0, 0)),
            pl.BlockSpec((1, SG_WIDTH), lambda b, i: (0, 0)),
        ],
        out_specs=[
            pl.BlockSpec((None, QK_COLS, tm), lambda b, i: (b, 0, i)),
            pl.BlockSpec((None, tm, QK_COLS), row),
            pl.BlockSpec((None, None, DIFF_WIDTH, tm), lambda b, i: (b, i, 0, 0)),
            pl.BlockSpec((None, tm, SG_WIDTH), row),
            pl.BlockSpec((None, tm, SG_WIDTH), row),
        ],
        out_shape=[
            jax.ShapeDtypeStruct((B, QK_COLS, S), BF16),
            jax.ShapeDtypeStruct((B, S, QK_COLS), BF16),
            jax.ShapeDtypeStruct((B, nt, DIFF_WIDTH, tm), BF16),
            jax.ShapeDtypeStruct((B, S, SG_WIDTH), BF16),
            jax.ShapeDtypeStruct((B, S, SG_WIDTH), BF16),
        ],
        compiler_params=_cparams(("arbitrary", "arbitrary")),
        name="inproj",
    )(x, mod_l, w_in_b, cs, sa, sb, ln_g, ln_b)


def _attn_kernel(lq1_ref, lk1_ref, lq2_ref, lk2_ref, g_ref, qT_ref, k_ref, vT_ref, o_ref,
                 q_scr, s_scr, m_scr, l_scr, acc_scr, *, lam_init):
    tq = qT_ref.shape[1]
    tk = vT_ref.shape[2]
    i = pl.program_id(2)
    qT = qT_ref[...]
    rows = lax.broadcasted_iota(I32, qT.shape, 0)
    zero = jnp.zeros_like(qT)
    q_scr[0] = jnp.where(rows < QK_DIM, qT, zero)
    q_scr[1] = jnp.where(rows >= QK_DIM, qT, zero)

    def scores(j, slot, c0):
        kb = k_ref[pl.ds(pl.multiple_of(j * tk, tk), tk), :]
        for mp in range(2):
            s_scr[slot, mp, :, c0:] = jnp.dot(kb, q_scr[mp, :, c0:], preferred_element_type=F32)

    def block_scores(slot, mp, c0, c1, masked):
        s = s_scr[slot, mp, :, c0:c1]
        if masked:
            kpos = lax.broadcasted_iota(I32, s.shape, 0)
            qpos = lax.broadcasted_iota(I32, s.shape, 1)
            s = jnp.where(kpos <= qpos, s, -jnp.inf)
        return s

    def pv_unshifted(j, slot, c0, c1, masked):
        vb = vT_ref[j]
        for mp in range(2):
            p = jnp.exp2(block_scores(slot, mp, c0, c1, masked))
            l_scr[mp, :, c0:c1] += jnp.sum(p, axis=0, keepdims=True)
            acc_scr[mp, :, c0:c1] += jnp.dot(vb, p.astype(BF16), preferred_element_type=F32)

    def pv_online(j, slot, c0, c1, masked):
        vb = vT_ref[j]
        for mp in range(2):
            s = block_scores(slot, mp, c0, c1, masked)
            m_old = m_scr[mp, :, c0:c1]
            m_new = jnp.maximum(m_old, jnp.max(s, axis=0, keepdims=True))
            alpha = jnp.exp2(m_old - m_new)
            p = jnp.exp2(s - m_new)
            l_scr[mp, :, c0:c1] = alpha * l_scr[mp, :, c0:c1] + jnp.sum(p, axis=0, keepdims=True)
            acc_scr[mp, :, c0:c1] = (alpha * acc_scr[mp, :, c0:c1]
                                     + jnp.dot(vb, p.astype(BF16), preferred_element_type=F32))
            m_scr[mp, :, c0:c1] = m_new

    def sweep(block):
        acc_scr[...] = jnp.zeros(acc_scr.shape, F32)
        l_scr[...] = jnp.zeros(l_scr.shape, F32)
        scores(0, 0, 0)

        def pair(t):
            scores(t, 1, 0)
            block(t - 1, 0, 0, tq, False)
            scores(t + 1, 0, 0)
            block(t, 1, 0, tq, False)

        def body(u, carry):
            pair(4 * u + 1)
            pair(4 * u + 3)
            return carry

        lax.fori_loop(0, i // 2, body, 0)

        @pl.when(i % 2 == 1)
        def _():
            pair(2 * i - 1)

        scores(2 * i + 1, 1, tk)
        block(2 * i, 0, 0, tk, True)
        block(2 * i, 0, tk, tq, False)
        block(2 * i + 1, 1, tk, tq, True)

    def finalize():
        lam = (jnp.exp(jnp.sum(lq1_ref[...] * lk1_ref[...], axis=1, keepdims=True))
               - jnp.exp(jnp.sum(lq2_ref[...] * lk2_ref[...], axis=1, keepdims=True)) + lam_init)
        out = acc_scr[0] / l_scr[0] - lam * (acc_scr[1] / l_scr[1])
        ms = jnp.mean(out * out, axis=0, keepdims=True)
        y = out * lax.rsqrt(ms + LN_EPS) * g_ref[...] * (1.0 - lam_init)
        o_ref[...] = y.T.astype(BF16)

    sweep(pv_unshifted)
    finalize()
    den = jnp.concatenate([l_scr[0], l_scr[1]], axis=0)
    in_range = (den >= DEN_MIN) & (den <= DEN_MAX)
    n_bad = jnp.sum(jnp.where(in_range, 0.0, 1.0))

    @pl.when(n_bad > 0.0)
    def _():
        m_scr[...] = jnp.full(m_scr.shape, -jnp.inf, F32)
        sweep(pv_online)
        finalize()


def _attention(qT, k, vTb, lams, g_col, lam_init):
    B, _, S = qT.shape
    nkv, tk = vTb.shape[1], vTb.shape[3]
    tq = 2 * tk
    small = pl.BlockSpec((1, QK_DIM), lambda b, h, i: (0, 0))
    return pl.pallas_call(
        functools.partial(_attn_kernel, lam_init=lam_init),
        grid=(B, N_HEADS, S // tq),
        in_specs=[
            small, small, small, small,
            pl.BlockSpec((V_DIM, 1), lambda b, h, i: (0, 0)),
            pl.BlockSpec((None, HEAD_COLS, tq), lambda b, h, i: (b, h, i)),
            pl.BlockSpec((None, S, HEAD_COLS), lambda b, h, i: (b, 0, h)),
            pl.BlockSpec((None, nkv, V_DIM, tk), lambda b, h, i: (b, 0, h, 0)),
        ],
        out_specs=pl.BlockSpec((None, tq, V_DIM), lambda b, h, i: (b, i, h)),
        out_shape=jax.ShapeDtypeStruct((B, S, DIFF_WIDTH), BF16),
        scratch_shapes=[
            pltpu.VMEM((2, HEAD_COLS, tq), BF16),
            pltpu.VMEM((2, 2, tk, tq), F32),
            pltpu.VMEM((2, 1, tq), F32),
            pltpu.VMEM((2, 1, tq), F32),
            pltpu.VMEM((2, V_DIM, tq), F32),
        ],
        compiler_params=_cparams(("arbitrary", "arbitrary", "arbitrary")),
        name="diff_attn",
    )(*lams, g_col, qT, k, vTb)


def _mix_kernel(x_ref, mod_ref, att_ref, zu_ref, zvn_ref, wsp_ref, bsp_ref, wo_ref, g_ref, b_ref,
                o_ref, sg_scr, *, alpha):
    D = D_MODEL
    nc = zu_ref.shape[0]
    r = lax.broadcasted_iota(I32, (CHUNK, CHUNK), 0)
    c = lax.broadcasted_iota(I32, (CHUNK, CHUNK), 1)
    causal = r >= c
    for g in range(N_SG):
        sl = slice(g * SG_DIM, (g + 1) * SG_DIM)
        w = jnp.where(causal, wsp_ref[g], 0.0).astype(BF16)
        z = jnp.concatenate([zvn_ref[n, :, sl] for n in range(nc)], axis=1)
        mixed = jnp.dot(w, z, preferred_element_type=F32)
        for n in range(nc):
            gate = mixed[:, n * SG_DIM:(n + 1) * SG_DIM] + bsp_ref[g]
            sg_scr[n * CHUNK:(n + 1) * CHUNK, sl] = (zu_ref[n, :, sl].astype(F32) * gate).astype(BF16)
    gt1 = 1.0 + mod_ref[:, 2 * D:3 * D]
    half = x_ref.shape[0] // 2
    for hf in range(2):
        rows = slice(hf * half, (hf + 1) * half)
        mix = (jnp.dot(att_ref[rows, :], wo_ref[0:DIFF_WIDTH, :], preferred_element_type=F32)
               + jnp.dot(sg_scr[rows, :], wo_ref[DIFF_WIDTH:, :], preferred_element_type=F32))
        y = alpha * x_ref[rows, :] + gt1 * mix
        o_ref[rows, :] = _layer_norm_rows(y, g_ref[...], b_ref[...])


def _mix(x, mod_l, att, zu, zvn, w_sp, b_sp_full, w_out_b, ln_g, ln_b, alpha):
    B, S, D = x.shape
    tm = min(ROW_TILE, S)
    nc = tm // CHUNK
    row = lambda b, i: (b, i, 0)
    zu4 = zu.reshape(B, S // CHUNK, CHUNK, SG_WIDTH)
    zvn4 = zvn.reshape(B, S // CHUNK, CHUNK, SG_WIDTH)
    chunked = pl.BlockSpec((None, nc, CHUNK, SG_WIDTH), lambda b, i: (b, i, 0, 0))
    full2 = lambda a: pl.BlockSpec(a.shape, lambda b, i: (0,) * a.ndim)
    return pl.pallas_call(
        functools.partial(_mix_kernel, alpha=alpha),
        grid=(B, S // tm),
        in_specs=[
            pl.BlockSpec((None, tm, D), row),
            pl.BlockSpec((None, 1, 6 * D), lambda b, i: (b, 0, 0)),
            pl.BlockSpec((None, tm, DIFF_WIDTH), row),
            chunked, chunked,
            full2(w_sp), full2(b_sp_full), full2(w_out_b), full2(ln_g), full2(ln_b),
        ],
        out_specs=pl.BlockSpec((None, tm, D), row),
        out_shape=jax.ShapeDtypeStruct((B, S, D), F32),
        scratch_shapes=[pltpu.VMEM((tm, SG_WIDTH), BF16)],
        compiler_params=_cparams(("arbitrary", "arbitrary")),
        name="sgate_outproj_ln",
    )(x, mod_l, att, zu4, zvn4, w_sp, b_sp_full, w_out_b, ln_g, ln_b)


def _route_kernel(x_ref, mod_ref, wr_hi_ref, wr_lo_ref, br_ref, hp_ref, rt_ref, cnt_ref, run_scr):
    D = D_MODEL
    tm = x_ref.shape[0]

    @pl.when((pl.program_id(0) == 0) & (pl.program_id(1) == 0))
    def _():
        run_scr[...] = jnp.zeros(run_scr.shape, F32)

    sh = mod_ref[:, 3 * D:4 * D]
    sc = mod_ref[:, 4 * D:5 * D]
    h = x_ref[...] * (1.0 + sc) + sh
    h_hi = h.astype(BF16)
    h_lo = (h - h_hi.astype(F32)).astype(BF16)
    logit = (jnp.dot(h_hi, wr_hi_ref[...], preferred_element_type=F32)
             + jnp.dot(h_lo, wr_hi_ref[...], preferred_element_type=F32)
             + jnp.dot(h_hi, wr_lo_ref[...], preferred_element_type=F32)) + br_ref[...]
    lt = logit.T[0:ROUTE_LANE0 + N_EXPERTS, :]
    row = lax.broadcasted_iota(I32, lt.shape, 0).astype(F32)
    neg = -jnp.inf
    big = float(LANES)

    def first_argmax(v):
        mx = jnp.max(v, axis=0, keepdims=True)
        idx = jnp.min(jnp.where(v == mx, row, big), axis=0, keepdims=True)
        return mx, idx

    in_grp = row < N_GROUPS
    gmax, gidx = first_argmax(jnp.where(in_grp, lt, neg))
    g_p = 1.0 / jnp.sum(jnp.where(in_grp, jnp.exp(lt - gmax), 0.0), axis=0, keepdims=True)
    lo_row = ROUTE_LANE0 + EXP_PER_GROUP * gidx
    sel = jnp.where((row >= lo_row) & (row < lo_row + EXP_PER_GROUP), lt, neg)
    v1, i1 = first_argmax(sel)
    v2, i2 = first_argmax(jnp.where(row == i1, neg, sel))
    t = jnp.exp(v2 - v1)
    w1 = g_p / (1.0 + t)
    w2 = g_p * t / (1.0 + t)

    first_lower = i1 < i2
    e_a = jnp.minimum(i1, i2) - lo_row
    e_b = jnp.maximum(i1, i2) - lo_row
    w_a = jnp.where(first_lower, w1, w2)
    w_b = jnp.where(first_lower, w2, w1)
    bucket = gidx * N_PAIRS + (e_a * (2 * EXP_PER_GROUP - 1 - e_a) * 0.5 + (e_b - e_a - 1.0))

    brow = lax.broadcasted_iota(I32, (LANES, tm), 0).astype(F32)
    hp_ref[:, 0:HALF] = _pack_halves(h)
    wslab = jnp.where(brow == 0.0, w_a, jnp.where(brow == 1.0, w_b, 0.0))
    hp_ref[:, HALF:] = pltpu.bitcast(wslab.T, U32)

    hot = brow == bucket
    onehot = jnp.where(hot, 1.0, 0.0)
    r = lax.broadcasted_iota(I32, (tm, tm), 0)
    c = lax.broadcasted_iota(I32, (tm, tm), 1)
    earlier = jnp.where(r < c, 1.0, 0.0).astype(BF16)
    before = jnp.dot(onehot.astype(BF16), earlier, preferred_element_type=F32) + run_scr[...]
    rank = jnp.sum(jnp.where(hot, before, 0.0), axis=0, keepdims=True)
    run_scr[...] = run_scr[...] + jnp.sum(onehot, axis=1, keepdims=True)
    cnt_ref[...] = run_scr[...]

    r8 = lax.broadcasted_iota(I32, (8, tm), 0)
    rt_ref[...] = jnp.where(r8 == 0, bucket, jnp.where(r8 == 1, rank, 0.0))


def _route(x1, mod_l, wr_hi, wr_lo, br):
    B, S, D = x1.shape
    tm = min(ROW_TILE, S)
    row = lambda b, i: (b, i, 0)
    full2 = lambda a: pl.BlockSpec(a.shape, lambda b, i: (0,) * a.ndim)
    return pl.pallas_call(
        _route_kernel,
        grid=(B, S // tm),
        in_specs=[
            pl.BlockSpec((None, tm, D), row),
            pl.BlockSpec((None, 1, 6 * D), lambda b, i: (b, 0, 0)),
            full2(wr_hi), full2(wr_lo), full2(br),
        ],
        out_specs=[
            pl.BlockSpec((None, tm, ROW_WORDS), row),
            pl.BlockSpec((None, 8, tm), lambda b, i: (b, 0, i)),
            pl.BlockSpec((LANES, 1), lambda b, i: (0, 0)),
        ],
        out_shape=[
            jax.ShapeDtypeStruct((B, S, ROW_WORDS), U32),
            jax.ShapeDtypeStruct((B, 8, S), F32),
            jax.ShapeDtypeStruct((LANES, 1), F32),
        ],
        scratch_shapes=[pltpu.VMEM((LANES, 1), F32)],
        compiler_params=_cparams(("arbitrary", "arbitrary")),
        name="moe_route",
    )(x1, mod_l, wr_hi, wr_lo, br)


def _row_of(ref, p):
    return ref.at[p >> 3, pl.ds(p & (SUBLANES - 1), 1)]


def _dispatch_kernel(p_ref, hp_ref, xs_in_ref, xs_ref, sem):
    del xs_in_ref
    ng = hp_ref.shape[0]

    def issue(g, carry):
        for u in range(SUBLANES):
            p = p_ref[0, 0, g * SUBLANES + u]
            pltpu.make_async_copy(hp_ref.at[g, pl.ds(u, 1)], _row_of(xs_ref, p), sem).start(priority=u % 2)
        return carry

    lax.fori_loop(0, ng, issue, 0)
    pltpu.make_async_copy(hp_ref, xs_ref.at[pl.ds(0, ng)], sem).wait()


def _dispatch(pos, hp, xs_buf):
    T = hp.shape[0]
    n_rows = xs_buf.shape[0]
    nb = min(DISPATCH_BLOCK, T)
    blk = pl.BlockSpec((1, 1, nb), lambda i: (i, 0, 0), memory_space=pltpu.SMEM)
    xs0 = xs_buf.reshape(n_rows // SUBLANES, SUBLANES, ROW_WORDS)
    xs = pl.pallas_call(
        _dispatch_kernel,
        grid=(T // nb,),
        in_specs=[blk, pl.BlockSpec((nb // SUBLANES, SUBLANES, ROW_WORDS), lambda i: (i, 0, 0)),
                  pl.BlockSpec(memory_space=pl.ANY)],
        out_specs=pl.BlockSpec(memory_space=pl.ANY),
        out_shape=jax.ShapeDtypeStruct(xs0.shape, U32),
        scratch_shapes=[pltpu.SemaphoreType.DMA(())],
        input_output_aliases={2: 0},
        compiler_params=_cparams(("arbitrary",)),
        name="moe_dispatch",
    )(pos.reshape(T // nb, 1, nb), hp.reshape(T // SUBLANES, SUBLANES, ROW_WORDS), xs0)
    return xs.reshape(n_rows, ROW_WORDS)


def _expert_kernel(ea_ref, eb_ref, nv_ref, xs_ref, wga_ref, wua_ref, wda_ref, wgb_ref, wub_ref, wdb_ref, ys_ref):
    del ea_ref, eb_ref

    @pl.when(pl.program_id(0) < nv_ref[0])
    def _():
        lo, hi = _unpack_halves(xs_ref[:, 0:HALF])
        lo = lo.astype(BF16)
        hi = hi.astype(BF16)
        wts = pltpu.bitcast(xs_ref[:, HALF:], F32)

        def hidden(wg_ref, wu_ref, w):
            g = (jnp.dot(lo, wg_ref[0:HALF, :], preferred_element_type=F32)
                 + jnp.dot(hi, wg_ref[HALF:, :], preferred_element_type=F32))
            u = (jnp.dot(lo, wu_ref[0:HALF, :], preferred_element_type=F32)
                 + jnp.dot(hi, wu_ref[HALF:, :], preferred_element_type=F32))
            return (g * jax.nn.sigmoid(g) * u * w).astype(BF16)

        y = (jnp.dot(hidden(wga_ref, wua_ref, wts[:, 0:1]), wda_ref[...], preferred_element_type=F32)
             + jnp.dot(hidden(wgb_ref, wub_ref, wts[:, 1:2]), wdb_ref[...], preferred_element_type=F32))
        ys_ref[...] = _pack_halves(y)


def _experts(tile_ea, tile_eb, n_valid, xs, w_gate, w_up, w_down, layer):
    n_rows = xs.shape[0]
    tm = EXPERT_TILE
    nt = n_rows // tm
    rows = lambda n, ea, eb, nv: (jnp.minimum(n, nv[0] - 1), 0)
    sel_a = lambda n, ea, eb, nv: (layer, ea[n], 0, 0)
    sel_b = lambda n, ea, eb, nv: (layer, eb[n], 0, 0)
    up = lambda sel: pl.BlockSpec((None, None, D_MODEL, D_EXPERT), sel)
    down = lambda sel: pl.BlockSpec((None, None, D_EXPERT, D_MODEL), sel)
    return pl.pallas_call(
        _expert_kernel,
        grid_spec=pltpu.PrefetchScalarGridSpec(
            num_scalar_prefetch=3,
            grid=(nt,),
            in_specs=[pl.BlockSpec((tm, ROW_WORDS), rows),
                      up(sel_a), up(sel_a), down(sel_a), up(sel_b), up(sel_b), down(sel_b)],
            out_specs=pl.BlockSpec((tm, HALF), rows),
        ),
        out_shape=jax.ShapeDtypeStruct((n_rows, HALF), U32),
        compiler_params=_cparams(("arbitrary",)),
        name="moe_experts",
    )(tile_ea, tile_eb, n_valid, xs, w_gate, w_up, w_down, w_gate, w_up, w_down)


def _combine_kernel(p_ref, pn_ref, ys_ref, x_ref, mod_ref, g_ref, b_ref, o_ref, buf, sems, *, alpha):
    D = D_MODEL
    tc = x_ref.shape[0]
    g = pl.program_id(0) * pl.num_programs(1) + pl.program_id(1)
    n_steps = pl.num_programs(0) * pl.num_programs(1)
    slot = g % 2

    def gather(pa_ref, s):
        def issue(g8, carry):
            for u in range(SUBLANES):
                p = pa_ref[0, 0, g8 * SUBLANES + u]
                pltpu.make_async_copy(_row_of(ys_ref, p), buf.at[s, g8, pl.ds(u, 1)], sems.at[s]).start(
                    priority=u % 2)
            return carry

        lax.fori_loop(0, tc // SUBLANES, issue, 0)

    @pl.when(g == 0)
    def _():
        gather(p_ref, 0)

    @pl.when(g + 1 < n_steps)
    def _():
        gather(pn_ref, 1 - slot)

    pltpu.make_async_copy(ys_ref.at[pl.ds(0, tc // SUBLANES)], buf.at[slot], sems.at[slot]).wait()

    lo, hi = _unpack_halves(buf[slot].reshape(tc, HALF))
    ffn = jnp.concatenate([lo, hi], axis=1)
    gt = mod_ref[:, 5 * D:6 * D]
    y = alpha * x_ref[...] + (1.0 + gt) * ffn
    o_ref[...] = _layer_norm_rows(y, g_ref[...], b_ref[...])


def _combine(pos, ys, x1, mod_l, ln_g, ln_b, alpha):
    B, S, D = x1.shape
    tc = min(COMBINE_TILE, S)
    nt = S // tc
    blk = pl.BlockSpec((1, 1, tc), lambda b, i: (b * nt + i, 0, 0), memory_space=pltpu.SMEM)
    nxt = pl.BlockSpec((1, 1, tc), lambda b, i: (jnp.minimum(b * nt + i + 1, B * nt - 1), 0, 0),
                       memory_space=pltpu.SMEM)
    row = lambda b, i: (b, i, 0)
    full2 = lambda a: pl.BlockSpec(a.shape, lambda b, i: (0,) * a.ndim)
    p = pos.reshape(B * nt, 1, tc)
    return pl.pallas_call(
        functools.partial(_combine_kernel, alpha=alpha),
        grid=(B, nt),
        in_specs=[
            blk, nxt,
            pl.BlockSpec(memory_space=pl.ANY),
            pl.BlockSpec((None, tc, D), row),
            pl.BlockSpec((None, 1, 6 * D), lambda b, i: (b, 0, 0)),
            full2(ln_g), full2(ln_b),
        ],
        out_specs=pl.BlockSpec((None, tc, D), row),
        out_shape=jax.ShapeDtypeStruct((B, S, D), F32),
        scratch_shapes=[pltpu.VMEM((2, tc // SUBLANES, SUBLANES, HALF), U32), pltpu.SemaphoreType.DMA((2,))],
        compiler_params=_cparams(("arbitrary", "arbitrary")),
        name="moe_combine_ln",
    )(p, p, ys.reshape(ys.shape[0] // SUBLANES, SUBLANES, HALF), x1, mod_l, ln_g, ln_b)


def _rope_tables(S):
    inv = 1.0 / (ROPE_THETA ** (jnp.arange(0, QK_DIM, 2, dtype=F32) / QK_DIM))
    ang = jnp.arange(S, dtype=F32)[:, None] * inv[None, :]
    cos, sin = jnp.cos(ang), jnp.sin(ang)
    half = QK_DIM // 2
    first = (jnp.arange(LANES) % QK_DIM) < half
    cs = jnp.tile(cos, (1, LANES // half))
    sn = jnp.tile(sin, (1, LANES // half))
    sa = jnp.where(first[None, :], -sn, 0.0)
    sb = jnp.where(first[None, :], 0.0, sn)
    return cs, sa, sb


def _router_matrix(w_group, b_group, w_router, b_router):
    D = w_group.shape[0]
    w = jnp.zeros((D, LANES), F32)
    w = w.at[:, 0:N_GROUPS].set(w_group)
    wr = jnp.transpose(w_router, (1, 0, 2)).reshape(D, N_EXPERTS)
    w = w.at[:, ROUTE_LANE0:ROUTE_LANE0 + N_EXPERTS].set(wr)
    b = jnp.zeros((1, LANES), F32)
    b = b.at[0, 0:N_GROUPS].set(b_group)
    b = b.at[0, ROUTE_LANE0:ROUTE_LANE0 + N_EXPERTS].set(b_router.reshape(N_EXPERTS))
    hi = w.astype(BF16)
    lo = (w - hi.astype(F32)).astype(BF16)
    return hi, lo, b


def _routing_tables(rt, cnt, n_tiles):
    B, _, S = rt.shape
    tm = EXPERT_TILE
    bucket = rt[:, 0, :].astype(I32)
    rank = rt[:, 1, :].astype(I32)
    counts = cnt[0:N_BUCKETS, 0].astype(I32)
    tiles = (counts + tm - 1) // tm
    tile_end = jnp.cumsum(tiles)
    tile_start = tile_end - tiles
    onehot = bucket[..., None] == jnp.arange(N_BUCKETS, dtype=I32)
    pos = (jnp.sum(jnp.where(onehot, tile_start * tm, 0), axis=-1) + rank).reshape(B * S)
    tile_ids = jnp.arange(n_tiles, dtype=I32)
    tile_bucket = jnp.minimum(jnp.sum(tile_ids[:, None] >= tile_end[None, :], axis=1), N_BUCKETS - 1)
    pair_a = jnp.asarray([a for a in range(EXP_PER_GROUP) for _ in range(a + 1, EXP_PER_GROUP)], I32)
    pair_b = jnp.asarray([b for a in range(EXP_PER_GROUP) for b in range(a + 1, EXP_PER_GROUP)], I32)
    group = tile_bucket // N_PAIRS
    pair = tile_bucket % N_PAIRS
    tile_ea = (group * EXP_PER_GROUP + pair_a[pair]).astype(I32)
    tile_eb = (group * EXP_PER_GROUP + pair_b[pair]).astype(I32)
    n_valid = tile_end[-1:].astype(I32)
    return pos, tile_ea, tile_eb, n_valid


def kernel(x, c, w_ada, b_ada, w_in, lambda_q1, lambda_k1, lambda_q2, lambda_k2, subln_g, sg_ln_g, sg_ln_b, w_spatial, b_spatial, w_out, ln1_g, ln1_b, w_group, b_group, w_router, b_router, w_gate, w_up, w_down, ln2_g, ln2_b):
    B, S, D = x.shape
    depth = w_in.shape[0]
    T = B * S
    alpha = (2.0 * depth) ** 0.25
    n_tiles = T // EXPERT_TILE + N_BUCKETS
    n_rows = n_tiles * EXPERT_TILE

    mod = _ada(c, w_ada, b_ada)
    tabs = _rope_tables(S)
    w_gate, w_up, w_down = w_gate.astype(BF16), w_up.astype(BF16), w_down.astype(BF16)
    xs = jnp.zeros((n_rows, ROW_WORDS), U32)
    for l in range(depth):
        mod_l = mod[l].reshape(B, 1, 6 * D)
        lam_init = 0.8 - 0.6 * math.exp(-0.3 * l)

        qT, k, vTb, zu, zvn = _inproj(x, mod_l, w_in[l].astype(BF16), tabs,
                                      sg_ln_g[l].reshape(1, SG_WIDTH), sg_ln_b[l].reshape(1, SG_WIDTH))
        lams = (lambda_q1[l].reshape(1, QK_DIM), lambda_k1[l].reshape(1, QK_DIM),
                lambda_q2[l].reshape(1, QK_DIM), lambda_k2[l].reshape(1, QK_DIM))
        att = _attention(qT, k, vTb, lams, subln_g[l].reshape(V_DIM, 1), lam_init)
        b_sp_full = jnp.broadcast_to(b_spatial[l][:, :, None], (N_SG, CHUNK, SG_DIM))
        x1 = _mix(x, mod_l, att, zu, zvn, w_spatial[l], b_sp_full, w_out[l].astype(BF16),
                  ln1_g[l].reshape(1, D), ln1_b[l].reshape(1, D), alpha)

        wr_hi, wr_lo, br = _router_matrix(w_group[l], b_group[l], w_router[l], b_router[l])
        hp, rt, cnt = _route(x1, mod_l, wr_hi, wr_lo, br)
        pos, tile_ea, tile_eb, n_valid = _routing_tables(rt, cnt, n_tiles)
        xs = _dispatch(pos, hp.reshape(T, ROW_WORDS), xs)
        ys = _experts(tile_ea, tile_eb, n_valid, xs, w_gate, w_up, w_down, l)
        x = _combine(pos, ys, x1, mod_l, ln2_g[l].reshape(1, D), ln2_b[l].reshape(1, D), alpha)
    return x
```

```python
import functools
import math

import jax
import jax.numpy as jnp
from jax import lax
from jax.experimental import pallas as pl
from jax.experimental.pallas import tpu as pltpu

F32 = jnp.float32
BF16 = jnp.bfloat16
U32 = jnp.uint32
I32 = jnp.int32

D_MODEL = 1024
N_HEADS = 4
QK_DIM = 64
V_DIM = 128
HEAD_COLS = 2 * QK_DIM
QK_COLS = N_HEADS * HEAD_COLS
DIFF_WIDTH = N_HEADS * V_DIM
N_SG = 4
SG_DIM = 128
SG_WIDTH = N_SG * SG_DIM
CHUNK = 128
N_GROUPS = 4
EXP_PER_GROUP = 8
N_EXPERTS = N_GROUPS * EXP_PER_GROUP
N_PAIRS = EXP_PER_GROUP * (EXP_PER_GROUP - 1) // 2
N_BUCKETS = N_GROUPS * N_PAIRS
D_EXPERT = 512
ROPE_THETA = 10000.0
LN_EPS = 1e-5
LANES = 128
HALF = D_MODEL // 2
ROW_WORDS = HALF + LANES

ROW_TILE = 512
EXPERT_TILE = 256
DISPATCH_BLOCK = 2048
COMBINE_TILE = 512
SUBLANES = 8
ROUTE_LANE0 = 8
VMEM_LIMIT = 48 * 1024 * 1024

DEN_MIN = 2.0 ** -40
DEN_MAX = 2.0 ** 40
LOG2E = 1.4426950408889634
Q_SCALE = (QK_DIM ** -0.5) * LOG2E


def _cparams(sem):
    return pltpu.CompilerParams(dimension_semantics=sem, vmem_limit_bytes=VMEM_LIMIT)


def _layer_norm_rows(y, g, b):
    mu = jnp.mean(y, axis=-1, keepdims=True)
    yc = y - mu
    var = jnp.mean(yc * yc, axis=-1, keepdims=True)
    return yc * lax.rsqrt(var + LN_EPS) * g + b


def _gelu(x):
    return 0.5 * x * (1.0 + lax.erf(x * (2.0 ** -0.5)))


def _pack_halves(y):
    lo = pltpu.bitcast(y[:, :HALF].astype(BF16).astype(F32), U32) >> 16
    hi = pltpu.bitcast(y[:, HALF:].astype(BF16).astype(F32), U32) & jnp.uint32(0xFFFF0000)
    return lo | hi


def _unpack_halves(p):
    lo = pltpu.bitcast(p << 16, F32)
    hi = pltpu.bitcast(p & jnp.uint32(0xFFFF0000), F32)
    return lo, hi


def _ada_kernel(c_ref, w_ref, b_ref, o_ref):
    c = c_ref[...]
    sc = c * jax.nn.sigmoid(c)
    o_ref[...] = jnp.dot(sc, w_ref[...], precision=lax.Precision.HIGHEST,
                         preferred_element_type=F32) + b_ref[...]


def _ada(c, w_ada, b_ada):
    L, D, N = w_ada.shape
    B = c.shape[0]
    tn = 1536
    return pl.pallas_call(
        _ada_kernel,
        grid=(L, N // tn),
        in_specs=[
            pl.BlockSpec((B, D), lambda l, j: (0, 0)),
            pl.BlockSpec((None, D, tn), lambda l, j: (l, 0, j)),
            pl.BlockSpec((None, 1, tn), lambda l, j: (l, 0, j)),
        ],
        out_specs=pl.BlockSpec((None, B, tn), lambda l, j: (l, 0, j)),
        out_shape=jax.ShapeDtypeStruct((L, B, N), F32),
        compiler_params=_cparams(("arbitrary", "arbitrary")),
        name="ada",
    )(c, w_ada, b_ada.reshape(L, 1, N))


def _inproj_kernel(x_ref, mod_ref, w_ref, cs_ref, sa_ref, sb_ref, lng_ref, lnb_ref,
                   qT_ref, k_ref, vT_ref, zu_ref, zvn_ref):
    D = D_MODEL
    x = x_ref[...]
    sh = mod_ref[:, 0:D]
    sc = mod_ref[:, D:2 * D]
    h = (x * (1.0 + sc) + sh).astype(BF16)
    cs = cs_ref[...]
    sa = sa_ref[...]
    sb = sb_ref[...]

    def rope(t, rows):
        return t * cs[rows] + pltpu.roll(t, 96, 1) * sa[rows] + pltpu.roll(t, 32, 1) * sb[rows]

    half = x.shape[0] // 2
    for hf in range(2):
        rows = slice(hf * half, (hf + 1) * half)
        hh = h[rows, :]
        q = jnp.dot(hh, w_ref[:, 0:QK_COLS], preferred_element_type=F32)
        for j in range(N_HEADS):
            sl = slice(j * LANES, (j + 1) * LANES)
            qT_ref[sl, rows] = (rope(q[:, sl], rows) * Q_SCALE).T.astype(BF16)
        k = jnp.dot(hh, w_ref[:, QK_COLS:2 * QK_COLS], preferred_element_type=F32)
        for j in range(N_HEADS):
            sl = slice(j * LANES, (j + 1) * LANES)
            k_ref[rows, sl] = rope(k[:, sl], rows).astype(BF16)
        c0 = 2 * QK_COLS
        v = jnp.dot(hh, w_ref[:, c0:c0 + DIFF_WIDTH], preferred_element_type=F32)
        vT_ref[:, rows] = v.T.astype(BF16)
        c0 += DIFF_WIDTH
        u = jnp.dot(hh, w_ref[:, c0:c0 + SG_WIDTH], preferred_element_type=F32)
        zu_ref[rows, :] = _gelu(u).astype(BF16)
        c0 += SG_WIDTH
        z = _gelu(jnp.dot(hh, w_ref[:, c0:c0 + SG_WIDTH], preferred_element_type=F32))
        for g in range(N_SG):
            sl = slice(g * SG_DIM, (g + 1) * SG_DIM)
            zvn_ref[rows, sl] = _layer_norm_rows(z[:, sl], lng_ref[:, sl], lnb_ref[:, sl]).astype(BF16)


def _inproj(x, mod_l, w_in_b, rope_tabs, ln_g, ln_b):
    B, S, D = x.shape
    tm = min(ROW_TILE, S)
    nt = S // tm
    cs, sa, sb = rope_tabs
    row = lambda b, i: (b, i, 0)
    tab = pl.BlockSpec((tm, LANES), lambda b, i: (i, 0))
    return pl.pallas_call(
        _inproj_kernel,
        grid=(B, nt),
        in_specs=[
            pl.BlockSpec((None, tm, D), row),
            pl.BlockSpec((None, 1, 6 * D), lambda b, i: (b, 0, 0)),
            pl.BlockSpec(w_in_b.shape, lambda b, i: (0, 0)),
            tab, tab, tab,
            pl.BlockSpec((1, SG_WIDTH), lambda b, i: (0, 0)),
            pl.BlockSpec((1, SG_WIDTH), lambda b, i: (0, 0)),
        ],
        out_specs=[
            pl.BlockSpec((None, QK_COLS, tm), lambda b, i: (b, 0, i)),
            pl.BlockSpec((None, tm, QK_COLS), row),
            pl.BlockSpec((None, None, DIFF_WIDTH, tm), lambda b, i: (b, i, 0, 0)),
            pl.BlockSpec((None, tm, SG_WIDTH), row),
            pl.BlockSpec((None, tm, SG_WIDTH), row),
        ],
        out_shape=[
            jax.ShapeDtypeStruct((B, QK_COLS, S), BF16),
            jax.ShapeDtypeStruct((B, S, QK_COLS), BF16),
            jax.ShapeDtypeStruct((B, nt, DIFF_WIDTH, tm), BF16),
            jax.ShapeDtypeStruct((B, S, SG_WIDTH), BF16),
            jax.ShapeDtypeStruct((B, S, SG_WIDTH), BF16),
        ],
        compiler_params=_cparams(("arbitrary", "arbitrary")),
        name="inproj",
    )(x, mod_l, w_in_b, cs, sa, sb, ln_g, ln_b)


def _attn_kernel(lq1_ref, lk1_ref, lq2_ref, lk2_ref, g_ref, qT_ref, k_ref, vT_ref, o_ref,
                 q_scr, s_scr, m_scr, l_scr, acc_scr, *, lam_init):
    tq = qT_ref.shape[1]
    tk = vT_ref.shape[2]
    i = pl.program_id(2)
    qT = qT_ref[...]
    rows = lax.broadcasted_iota(I32, qT.shape, 0)
    zero = jnp.zeros_like(qT)
    q_scr[0] = jnp.where(rows < QK_DIM, qT, zero)
    q_scr[1] = jnp.where(rows >= QK_DIM, qT, zero)

    def scores(j, slot, c0):
        kb = k_ref[pl.ds(pl.multiple_of(j * tk, tk), tk), :]
        for mp in range(2):
            s_scr[slot, mp, :, c0:] = jnp.dot(kb, q_scr[mp, :, c0:], preferred_element_type=F32)

    def block_scores(slot, mp, c0, c1, masked):
        s = s_scr[slot, mp, :, c0:c1]
        if masked:
            kpos = lax.broadcasted_iota(I32, s.shape, 0)
            qpos = lax.broadcasted_iota(I32, s.shape, 1)
            s = jnp.where(kpos <= qpos, s, -jnp.inf)
        return s

    def pv_unshifted(j, slot, c0, c1, masked):
        vb = vT_ref[j]
        for mp in range(2):
            p = jnp.exp2(block_scores(slot, mp, c0, c1, masked))
            l_scr[mp, :, c0:c1] += jnp.sum(p, axis=0, keepdims=True)
            acc_scr[mp, :, c0:c1] += jnp.dot(vb, p.astype(BF16), preferred_element_type=F32)

    def pv_online(j, slot, c0, c1, masked):
        vb = vT_ref[j]
        for mp in range(2):
            s = block_scores(slot, mp, c0, c1, masked)
            m_old = m_scr[mp, :, c0:c1]
            m_new = jnp.maximum(m_old, jnp.max(s, axis=0, keepdims=True))
            alpha = jnp.exp2(m_old - m_new)
            p = jnp.exp2(s - m_new)
            l_scr[mp, :, c0:c1] = alpha * l_scr[mp, :, c0:c1] + jnp.sum(p, axis=0, keepdims=True)
            acc_scr[mp, :, c0:c1] = (alpha * acc_scr[mp, :, c0:c1]
                                     + jnp.dot(vb, p.astype(BF16), preferred_element_type=F32))
            m_scr[mp, :, c0:c1] = m_new

    def sweep(block):
        acc_scr[...] = jnp.zeros(acc_scr.shape, F32)
        l_scr[...] = jnp.zeros(l_scr.shape, F32)
        scores(0, 0, 0)

        def pair(t):
            scores(t, 1, 0)
            block(t - 1, 0, 0, tq, False)
            scores(t + 1, 0, 0)
            block(t, 1, 0, tq, False)

        def body(u, carry):
            pair(4 * u + 1)
            pair(4 * u + 3)
            return carry

        lax.fori_loop(0, i // 2, body, 0)

        @pl.when(i % 2 == 1)
        def _():
            pair(2 * i - 1)

        scores(2 * i + 1, 1, tk)
        block(2 * i, 0, 0, tk, True)
        block(2 * i, 0, tk, tq, False)
        block(2 * i + 1, 1, tk, tq, True)

    def finalize():
        lam = (jnp.exp(jnp.sum(lq1_ref[...] * lk1_ref[...], axis=1, keepdims=True))
               - jnp.exp(jnp.sum(lq2_ref[...] * lk2_ref[...], axis=1, keepdims=True)) + lam_init)
        out = acc_scr[0] / l_scr[0] - lam * (acc_scr[1] / l_scr[1])
        ms = jnp.mean(out * out, axis=0, keepdims=True)
        y = out * lax.rsqrt(ms + LN_EPS) * g_ref[...] * (1.0 - lam_init)
        o_ref[...] = y.T.astype(BF16)

    sweep(pv_unshifted)
    finalize()
    den = jnp.concatenate([l_scr[0], l_scr[1]], axis=0)
    in_range = (den >= DEN_MIN) & (den <= DEN_MAX)
    n_bad = jnp.sum(jnp.where(in_range, 0.0, 1.0))

    @pl.when(n_bad > 0.0)
    def _():
        m_scr[...] = jnp.full(m_scr.shape, -jnp.inf, F32)
        sweep(pv_online)
        finalize()


def _attention(qT, k, vTb, lams, g_col, lam_init):
    B, _, S = qT.shape
    nkv, tk = vTb.shape[1], vTb.shape[3]
    tq = 2 * tk
    small = pl.BlockSpec((1, QK_DIM), lambda b, h, i: (0, 0))
    return pl.pallas_call(
        functools.partial(_attn_kernel, lam_init=lam_init),
        grid=(B, N_HEADS, S // tq),
        in_specs=[
            small, small, small, small,
            pl.BlockSpec((V_DIM, 1), lambda b, h, i: (0, 0)),
            pl.BlockSpec((None, HEAD_COLS, tq), lambda b, h, i: (b, h, i)),
            pl.BlockSpec((None, S, HEAD_COLS), lambda b, h, i: (b, 0, h)),
            pl.BlockSpec((None, nkv, V_DIM, tk), lambda b, h, i: (b, 0, h, 0)),
        ],
        out_specs=pl.BlockSpec((None, tq, V_DIM), lambda b, h, i: (b, i, h)),
        out_shape=jax.ShapeDtypeStruct((B, S, DIFF_WIDTH), BF16),
        scratch_shapes=[
            pltpu.VMEM((2, HEAD_COLS, tq), BF16),
            pltpu.VMEM((2, 2, tk, tq), F32),
            pltpu.VMEM((2, 1, tq), F32),
            pltpu.VMEM((2, 1, tq), F32),
            pltpu.VMEM((2, V_DIM, tq), F32),
        ],
        compiler_params=_cparams(("arbitrary", "arbitrary", "arbitrary")),
        name="diff_attn",
    )(*lams, g_col, qT, k, vTb)


def _mix_kernel(x_ref, mod_ref, att_ref, zu_ref, zvn_ref, wsp_ref, bsp_ref, wo_ref, g_ref, b_ref,
                o_ref, sg_scr, *, alpha):
    D = D_MODEL
    nc = zu_ref.shape[0]
    r = lax.broadcasted_iota(I32, (CHUNK, CHUNK), 0)
    c = lax.broadcasted_iota(I32, (CHUNK, CHUNK), 1)
    causal = r >= c
    for g in range(N_SG):
        sl = slice(g * SG_DIM, (g + 1) * SG_DIM)
        w = jnp.where(causal, wsp_ref[g], 0.0).astype(BF16)
        z = jnp.concatenate([zvn_ref[n, :, sl] for n in range(nc)], axis=1)
        mixed = jnp.dot(w, z, preferred_element_type=F32)
        for n in range(nc):
            gate = mixed[:, n * SG_DIM:(n + 1) * SG_DIM] + bsp_ref[g]
            sg_scr[n * CHUNK:(n + 1) * CHUNK, sl] = (zu_ref[n, :, sl].astype(F32) * gate).astype(BF16)
    gt1 = 1.0 + mod_ref[:, 2 * D:3 * D]
    half = x_ref.shape[0] // 2
    for hf in range(2):
        rows = slice(hf * half, (hf + 1) * half)
        mix = (jnp.dot(att_ref[rows, :], wo_ref[0:DIFF_WIDTH, :], preferred_element_type=F32)
               + jnp.dot(sg_scr[rows, :], wo_ref[DIFF_WIDTH:, :], preferred_element_type=F32))
        y = alpha * x_ref[rows, :] + gt1 * mix
        o_ref[rows, :] = _layer_norm_rows(y, g_ref[...], b_ref[...])


def _mix(x, mod_l, att, zu, zvn, w_sp, b_sp_full, w_out_b, ln_g, ln_b, alpha):
    B, S, D = x.shape
    tm = min(ROW_TILE, S)
    nc = tm // CHUNK
    row = lambda b, i: (b, i, 0)
    zu4 = zu.reshape(B, S // CHUNK, CHUNK, SG_WIDTH)
    zvn4 = zvn.reshape(B, S // CHUNK, CHUNK, SG_WIDTH)
    chunked = pl.BlockSpec((None, nc, CHUNK, SG_WIDTH), lambda b, i: (b, i, 0, 0))
    full2 = lambda a: pl.BlockSpec(a.shape, lambda b, i: (0,) * a.ndim)
    return pl.pallas_call(
        functools.partial(_mix_kernel, alpha=alpha),
        grid=(B, S // tm),
        in_specs=[
            pl.BlockSpec((None, tm, D), row),
            pl.BlockSpec((None, 1, 6 * D), lambda b, i: (b, 0, 0)),
            pl.BlockSpec((None, tm, DIFF_WIDTH), row),
            chunked, chunked,
            full2(w_sp), full2(b_sp_full), full2(w_out_b), full2(ln_g), full2(ln_b),
        ],
        out_specs=pl.BlockSpec((None, tm, D), row),
        out_shape=jax.ShapeDtypeStruct((B, S, D), F32),
        scratch_shapes=[pltpu.VMEM((tm, SG_WIDTH), BF16)],
        compiler_params=_cparams(("arbitrary", "arbitrary")),
        name="sgate_outproj_ln",
    )(x, mod_l, att, zu4, zvn4, w_sp, b_sp_full, w_out_b, ln_g, ln_b)


def _route_kernel(x_ref, mod_ref, wr_ref, br_ref, hp_ref, rt_ref, cnt_ref, run_scr):
    D = D_MODEL
    tm = x_ref.shape[0]

    @pl.when((pl.program_id(0) == 0) & (pl.program_id(1) == 0))
    def _():
        run_scr[...] = jnp.zeros(run_scr.shape, F32)

    sh = mod_ref[:, 3 * D:4 * D]
    sc = mod_ref[:, 4 * D:5 * D]
    h = x_ref[...] * (1.0 + sc) + sh
    h_hi = h.astype(BF16)
    h_lo = (h - h_hi.astype(F32)).astype(BF16)
    both = jnp.dot(h_hi, wr_ref[...], preferred_element_type=F32)
    logit = (both[:, 0:LANES] + both[:, LANES:]
             + jnp.dot(h_lo, wr_ref[:, 0:LANES], preferred_element_type=F32)) + br_ref[...]
    lt = logit.T[0:ROUTE_LANE0 + N_EXPERTS, :]
    row = lax.broadcasted_iota(I32, lt.shape, 0).astype(F32)
    neg = -jnp.inf
    big = float(LANES)

    def first_argmax(v):
        mx = jnp.max(v, axis=0, keepdims=True)
        idx = jnp.min(jnp.where(v == mx, row, big), axis=0, keepdims=True)
        return mx, idx

    in_grp = row < N_GROUPS
    gmax, gidx = first_argmax(jnp.where(in_grp, lt, neg))
    g_p = 1.0 / jnp.sum(jnp.where(in_grp, jnp.exp(lt - gmax), 0.0), axis=0, keepdims=True)
    lo_row = ROUTE_LANE0 + EXP_PER_GROUP * gidx
    sel = jnp.where((row >= lo_row) & (row < lo_row + EXP_PER_GROUP), lt, neg)
    v1, i1 = first_argmax(sel)
    v2, i2 = first_argmax(jnp.where(row == i1, neg, sel))
    t = jnp.exp(v2 - v1)
    w1 = g_p / (1.0 + t)
    w2 = g_p * t / (1.0 + t)

    first_lower = i1 < i2
    e_a = jnp.minimum(i1, i2) - lo_row
    e_b = jnp.maximum(i1, i2) - lo_row
    w_a = jnp.where(first_lower, w1, w2)
    w_b = jnp.where(first_lower, w2, w1)
    bucket = gidx * N_PAIRS + (e_a * (2 * EXP_PER_GROUP - 1 - e_a) * 0.5 + (e_b - e_a - 1.0))

    brow = lax.broadcasted_iota(I32, (LANES, tm), 0).astype(F32)
    hp_ref[:, 0:HALF] = _pack_halves(h)
    wslab = jnp.where(brow == 0.0, w_a, jnp.where(brow == 1.0, w_b, 0.0))
    hp_ref[:, HALF:] = pltpu.bitcast(wslab.T, U32)

    hot = brow == bucket
    onehot = jnp.where(hot, 1.0, 0.0)
    r = lax.broadcasted_iota(I32, (tm, tm), 0)
    c = lax.broadcasted_iota(I32, (tm, tm), 1)
    earlier = jnp.where(r < c, 1.0, 0.0).astype(BF16)
    before = jnp.dot(onehot.astype(BF16), earlier, preferred_element_type=F32) + run_scr[...]
    rank = jnp.sum(jnp.where(hot, before, 0.0), axis=0, keepdims=True)
    run_scr[...] = run_scr[...] + jnp.sum(onehot, axis=1, keepdims=True)
    cnt_ref[...] = run_scr[...]

    r8 = lax.broadcasted_iota(I32, (8, tm), 0)
    rt_ref[...] = jnp.where(r8 == 0, bucket, jnp.where(r8 == 1, rank, 0.0))


def _route(x1, mod_l, wr, br):
    B, S, D = x1.shape
    tm = min(ROW_TILE, S)
    row = lambda b, i: (b, i, 0)
    full2 = lambda a: pl.BlockSpec(a.shape, lambda b, i: (0,) * a.ndim)
    return pl.pallas_call(
        _route_kernel,
        grid=(B, S // tm),
        in_specs=[
            pl.BlockSpec((None, tm, D), row),
            pl.BlockSpec((None, 1, 6 * D), lambda b, i: (b, 0, 0)),
            full2(wr), full2(br),
        ],
        out_specs=[
            pl.BlockSpec((None, tm, ROW_WORDS), row),
            pl.BlockSpec((None, 8, tm), lambda b, i: (b, 0, i)),
            pl.BlockSpec((LANES, 1), lambda b, i: (0, 0)),
        ],
        out_shape=[
            jax.ShapeDtypeStruct((B, S, ROW_WORDS), U32),
            jax.ShapeDtypeStruct((B, 8, S), F32),
            jax.ShapeDtypeStruct((LANES, 1), F32),
        ],
        scratch_shapes=[pltpu.VMEM((LANES, 1), F32)],
        compiler_params=_cparams(("arbitrary", "arbitrary")),
        name="moe_route",
    )(x1, mod_l, wr, br)


def _row_of(ref, p):
    return ref.at[p >> 3, pl.ds(p & (SUBLANES - 1), 1)]


def _dispatch_kernel(p_ref, hp_ref, xs_in_ref, xs_ref, sem):
    del xs_in_ref
    ng = hp_ref.shape[0]

    def issue(g, carry):
        for u in range(SUBLANES):
            p = p_ref[0, 0, g * SUBLANES + u]
            pltpu.make_async_copy(hp_ref.at[g, pl.ds(u, 1)], _row_of(xs_ref, p), sem).start(priority=u % 2)
        return carry

    lax.fori_loop(0, ng, issue, 0)
    pltpu.make_async_copy(hp_ref, xs_ref.at[pl.ds(0, ng)], sem).wait()


def _dispatch(pos, hp, xs_buf):
    T = hp.shape[0]
    n_rows = xs_buf.shape[0]
    nb = min(DISPATCH_BLOCK, T)
    blk = pl.BlockSpec((1, 1, nb), lambda i: (i, 0, 0), memory_space=pltpu.SMEM)
    xs0 = xs_buf.reshape(n_rows // SUBLANES, SUBLANES, ROW_WORDS)
    xs = pl.pallas_call(
        _dispatch_kernel,
        grid=(T // nb,),
        in_specs=[blk, pl.BlockSpec((nb // SUBLANES, SUBLANES, ROW_WORDS), lambda i: (i, 0, 0)),
                  pl.BlockSpec(memory_space=pl.ANY)],
        out_specs=pl.BlockSpec(memory_space=pl.ANY),
        out_shape=jax.ShapeDtypeStruct(xs0.shape, U32),
        scratch_shapes=[pltpu.SemaphoreType.DMA(())],
        input_output_aliases={2: 0},
        compiler_params=_cparams(("arbitrary",)),
        name="moe_dispatch",
    )(pos.reshape(T // nb, 1, nb), hp.reshape(T // SUBLANES, SUBLANES, ROW_WORDS), xs0)
    return xs.reshape(n_rows, ROW_WORDS)


def _expert_kernel(ea_ref, eb_ref, nv_ref, xs_ref, wga_ref, wua_ref, wda_ref, wgb_ref, wub_ref, wdb_ref, ys_ref):
    del ea_ref, eb_ref

    @pl.when(pl.program_id(0) < nv_ref[0])
    def _():
        lo, hi = _unpack_halves(xs_ref[:, 0:HALF])
        lo = lo.astype(BF16)
        hi = hi.astype(BF16)
        wts = pltpu.bitcast(xs_ref[:, HALF:], F32)

        def hidden(wg_ref, wu_ref, w):
            g = (jnp.dot(lo, wg_ref[0:HALF, :], preferred_element_type=F32)
                 + jnp.dot(hi, wg_ref[HALF:, :], preferred_element_type=F32))
            u = (jnp.dot(lo, wu_ref[0:HALF, :], preferred_element_type=F32)
                 + jnp.dot(hi, wu_ref[HALF:, :], preferred_element_type=F32))
            return (g * jax.nn.sigmoid(g) * u * w).astype(BF16)

        y = (jnp.dot(hidden(wga_ref, wua_ref, wts[:, 0:1]), wda_ref[...], preferred_element_type=F32)
             + jnp.dot(hidden(wgb_ref, wub_ref, wts[:, 1:2]), wdb_ref[...], preferred_element_type=F32))
        ys_ref[...] = _pack_halves(y)


def _experts(tile_ea, tile_eb, n_valid, xs, w_gate, w_up, w_down, layer):
    n_rows = xs.shape[0]
    tm = EXPERT_TILE
    nt = n_rows // tm
    rows = lambda n, ea, eb, nv: (jnp.minimum(n, nv[0] - 1), 0)
    sel_a = lambda n, ea, eb, nv: (layer, ea[n], 0, 0)
    sel_b = lambda n, ea, eb, nv: (layer, eb[n], 0, 0)
    up = lambda sel: pl.BlockSpec((None, None, D_MODEL, D_EXPERT), sel)
    down = lambda sel: pl.BlockSpec((None, None, D_EXPERT, D_MODEL), sel)
    return pl.pallas_call(
        _expert_kernel,
        grid_spec=pltpu.PrefetchScalarGridSpec(
            num_scalar_prefetch=3,
            grid=(nt,),
            in_specs=[pl.BlockSpec((tm, ROW_WORDS), rows),
                      up(sel_a), up(sel_a), down(sel_a), up(sel_b), up(sel_b), down(sel_b)],
            out_specs=pl.BlockSpec((tm, HALF), rows),
        ),
        out_shape=jax.ShapeDtypeStruct((n_rows, HALF), U32),
        compiler_params=_cparams(("arbitrary",)),
        name="moe_experts",
    )(tile_ea, tile_eb, n_valid, xs, w_gate, w_up, w_down, w_gate, w_up, w_down)


def _combine_kernel(p_ref, pn_ref, ys_ref, x_ref, mod_ref, g_ref, b_ref, o_ref, buf, sems, *, alpha):
    D = D_MODEL
    tc = x_ref.shape[0]
    g = pl.program_id(0) * pl.num_programs(1) + pl.program_id(1)
    n_steps = pl.num_programs(0) * pl.num_programs(1)
    slot = g % 2

    def gather(pa_ref, s):
        def issue(g8, carry):
            for u in range(SUBLANES):
                p = pa_ref[0, 0, g8 * SUBLANES + u]
                pltpu.make_async_copy(_row_of(ys_ref, p), buf.at[s, g8, pl.ds(u, 1)], sems.at[s]).start(
                    priority=u % 2)
            return carry

        lax.fori_loop(0, tc // SUBLANES, issue, 0)

    @pl.when(g == 0)
    def _():
        gather(p_ref, 0)

    @pl.when(g + 1 < n_steps)
    def _():
        gather(pn_ref, 1 - slot)

    pltpu.make_async_copy(ys_ref.at[pl.ds(0, tc // SUBLANES)], buf.at[slot], sems.at[slot]).wait()

    lo, hi = _unpack_halves(buf[slot].reshape(tc, HALF))
    ffn = jnp.concatenate([lo, hi], axis=1)
    gt = mod_ref[:, 5 * D:6 * D]
    y = alpha * x_ref[...] + (1.0 + gt) * ffn
    o_ref[...] = _layer_norm_rows(y, g_ref[...], b_ref[...])


def _combine(pos, ys, x1, mod_l, ln_g, ln_b, alpha):
    B, S, D = x1.shape
    tc = min(COMBINE_TILE, S)
    nt = S // tc
    blk = pl.BlockSpec((1, 1, tc), lambda b, i: (b * nt + i, 0, 0), memory_space=pltpu.SMEM)
    nxt = pl.BlockSpec((1, 1, tc), lambda b, i: (jnp.minimum(b * nt + i + 1, B * nt - 1), 0, 0),
                       memory_space=pltpu.SMEM)
    row = lambda b, i: (b, i, 0)
    full2 = lambda a: pl.BlockSpec(a.shape, lambda b, i: (0,) * a.ndim)
    p = pos.reshape(B * nt, 1, tc)
    return pl.pallas_call(
        functools.partial(_combine_kernel, alpha=alpha),
        grid=(B, nt),
        in_specs=[
            blk, nxt,
            pl.BlockSpec(memory_space=pl.ANY),
            pl.BlockSpec((None, tc, D), row),
            pl.BlockSpec((None, 1, 6 * D), lambda b, i: (b, 0, 0)),
            full2(ln_g), full2(ln_b),
        ],
        out_specs=pl.BlockSpec((None, tc, D), row),
        out_shape=jax.ShapeDtypeStruct((B, S, D), F32),
        scratch_shapes=[pltpu.VMEM((2, tc // SUBLANES, SUBLANES, HALF), U32), pltpu.SemaphoreType.DMA((2,))],
        compiler_params=_cparams(("arbitrary", "arbitrary")),
        name="moe_combine_ln",
    )(p, p, ys.reshape(ys.shape[0] // SUBLANES, SUBLANES, HALF), x1, mod_l, ln_g, ln_b)


def _rope_tables(S):
    inv = 1.0 / (ROPE_THETA ** (jnp.arange(0, QK_DIM, 2, dtype=F32) / QK_DIM))
    ang = jnp.arange(S, dtype=F32)[:, None] * inv[None, :]
    cos, sin = jnp.cos(ang), jnp.sin(ang)
    half = QK_DIM // 2
    first = (jnp.arange(LANES) % QK_DIM) < half
    cs = jnp.tile(cos, (1, LANES // half))
    sn = jnp.tile(sin, (1, LANES // half))
    sa = jnp.where(first[None, :], -sn, 0.0)
    sb = jnp.where(first[None, :], 0.0, sn)
    return cs, sa, sb


def _router_matrix(w_group, b_group, w_router, b_router):
    D = w_group.shape[0]
    w = jnp.zeros((D, LANES), F32)
    w = w.at[:, 0:N_GROUPS].set(w_group)
    wr = jnp.transpose(w_router, (1, 0, 2)).reshape(D, N_EXPERTS)
    w = w.at[:, ROUTE_LANE0:ROUTE_LANE0 + N_EXPERTS].set(wr)
    b = jnp.zeros((1, LANES), F32)
    b = b.at[0, 0:N_GROUPS].set(b_group)
    b = b.at[0, ROUTE_LANE0:ROUTE_LANE0 + N_EXPERTS].set(b_router.reshape(N_EXPERTS))
    hi = w.astype(BF16)
    lo = (w - hi.astype(F32)).astype(BF16)
    return jnp.concatenate([hi, lo], axis=1), b


def _routing_tables(rt, cnt, n_tiles):
    B, _, S = rt.shape
    tm = EXPERT_TILE
    bucket = rt[:, 0, :].astype(I32)
    rank = rt[:, 1, :].astype(I32)
    counts = cnt[0:N_BUCKETS, 0].astype(I32)
    tiles = (counts + tm - 1) // tm
    tile_end = jnp.cumsum(tiles)
    tile_start = tile_end - tiles
    onehot = bucket[..., None] == jnp.arange(N_BUCKETS, dtype=I32)
    pos = (jnp.sum(jnp.where(onehot, tile_start * tm, 0), axis=-1) + rank).reshape(B * S)
    tile_ids = jnp.arange(n_tiles, dtype=I32)
    tile_bucket = jnp.minimum(jnp.sum(tile_ids[:, None] >= tile_end[None, :], axis=1), N_BUCKETS - 1)
    pair_a = jnp.asarray([a for a in range(EXP_PER_GROUP) for _ in range(a + 1, EXP_PER_GROUP)], I32)
    pair_b = jnp.asarray([b for a in range(EXP_PER_GROUP) for b in range(a + 1, EXP_PER_GROUP)], I32)
    group = tile_bucket // N_PAIRS
    pair = tile_bucket % N_PAIRS
    tile_ea = (group * EXP_PER_GROUP + pair_a[pair]).astype(I32)
    tile_eb = (group * EXP_PER_GROUP + pair_b[pair]).astype(I32)
    n_valid = tile_end[-1:].astype(I32)
    return pos, tile_ea, tile_eb, n_valid


def kernel(x, c, w_ada, b_ada, w_in, lambda_q1, lambda_k1, lambda_q2, lambda_k2, subln_g, sg_ln_g, sg_ln_b, w_spatial, b_spatial, w_out, ln1_g, ln1_b, w_group, b_group, w_router, b_router, w_gate, w_up, w_down, ln2_g, ln2_b):
    B, S, D = x.shape
    depth = w_in.shape[0]
    T = B * S
    alpha = (2.0 * depth) ** 0.25
    n_tiles = T // EXPERT_TILE + N_BUCKETS
    n_rows = n_tiles * EXPERT_TILE

    mod = _ada(c, w_ada, b_ada)
    tabs = _rope_tables(S)
    w_gate, w_up, w_down = w_gate.astype(BF16), w_up.astype(BF16), w_down.astype(BF16)
    xs = jnp.zeros((n_rows, ROW_WORDS), U32)
    for l in range(depth):
        mod_l = mod[l].reshape(B, 1, 6 * D)
        lam_init = 0.8 - 0.6 * math.exp(-0.3 * l)

        qT, k, vTb, zu, zvn = _inproj(x, mod_l, w_in[l].astype(BF16), tabs,
                                      sg_ln_g[l].reshape(1, SG_WIDTH), sg_ln_b[l].reshape(1, SG_WIDTH))
        lams = (lambda_q1[l].reshape(1, QK_DIM), lambda_k1[l].reshape(1, QK_DIM),
                lambda_q2[l].reshape(1, QK_DIM), lambda_k2[l].reshape(1, QK_DIM))
        att = _attention(qT, k, vTb, lams, subln_g[l].reshape(V_DIM, 1), lam_init)
        b_sp_full = jnp.broadcast_to(b_spatial[l][:, :, None], (N_SG, CHUNK, SG_DIM))
        x1 = _mix(x, mod_l, att, zu, zvn, w_spatial[l], b_sp_full, w_out[l].astype(BF16),
                  ln1_g[l].reshape(1, D), ln1_b[l].reshape(1, D), alpha)

        wr, br = _router_matrix(w_group[l], b_group[l], w_router[l], b_router[l])
        hp, rt, cnt = _route(x1, mod_l, wr, br)
        pos, tile_ea, tile_eb, n_valid = _routing_tables(rt, cnt, n_tiles)
        xs = _dispatch(pos, hp.reshape(T, ROW_WORDS), xs)
        ys = _experts(tile_ea, tile_eb, n_valid, xs, w_gate, w_up, w_down, l)
        x = _combine(pos, ys, x1, mod_l, ln2_g[l].reshape(1, D), ln2_b[l].reshape(1, D), alpha)
    return x
```

```python
import functools
import math

import jax
import jax.numpy as jnp
from jax import lax
from jax.experimental import pallas as pl
from jax.experimental.pallas import tpu as pltpu

F32 = jnp.float32
BF16 = jnp.bfloat16
U32 = jnp.uint32
I32 = jnp.int32

D_MODEL = 1024
N_HEADS = 4
QK_DIM = 64
V_DIM = 128
HEAD_COLS = 2 * QK_DIM
QK_COLS = N_HEADS * HEAD_COLS
DIFF_WIDTH = N_HEADS * V_DIM
N_SG = 4
SG_DIM = 128
SG_WIDTH = N_SG * SG_DIM
CHUNK = 128
N_GROUPS = 4
EXP_PER_GROUP = 8
N_EXPERTS = N_GROUPS * EXP_PER_GROUP
N_PAIRS = EXP_PER_GROUP * (EXP_PER_GROUP - 1) // 2
N_BUCKETS = N_GROUPS * N_PAIRS
D_EXPERT = 512
ROPE_THETA = 10000.0
LN_EPS = 1e-5
LANES = 128
HALF = D_MODEL // 2
ROW_WORDS = HALF + LANES

ROW_TILE = 512
EXPERT_TILE = 208
DISPATCH_BLOCK = 2048
COMBINE_TILE = 512
SUBLANES = 8
ROUTE_LANE0 = 8
VMEM_LIMIT = 48 * 1024 * 1024

DEN_MIN = 2.0 ** -40
DEN_MAX = 2.0 ** 40
LOG2E = 1.4426950408889634
Q_SCALE = (QK_DIM ** -0.5) * LOG2E


def _cparams(sem):
    return pltpu.CompilerParams(dimension_semantics=sem, vmem_limit_bytes=VMEM_LIMIT)


def _layer_norm_rows(y, g, b):
    mu = jnp.mean(y, axis=-1, keepdims=True)
    yc = y - mu
    var = jnp.mean(yc * yc, axis=-1, keepdims=True)
    return yc * lax.rsqrt(var + LN_EPS) * g + b


def _gelu(x):
    return 0.5 * x * (1.0 + lax.erf(x * (2.0 ** -0.5)))


def _pack_halves(y):
    lo = pltpu.bitcast(y[:, :HALF].astype(BF16).astype(F32), U32) >> 16
    hi = pltpu.bitcast(y[:, HALF:].astype(BF16).astype(F32), U32) & jnp.uint32(0xFFFF0000)
    return lo | hi


def _unpack_halves(p):
    lo = pltpu.bitcast(p << 16, F32)
    hi = pltpu.bitcast(p & jnp.uint32(0xFFFF0000), F32)
    return lo, hi


def _ada_kernel(c_ref, w_ref, b_ref, o_ref):
    c = c_ref[...]
    sc = c * jax.nn.sigmoid(c)
    o_ref[...] = jnp.dot(sc, w_ref[...], precision=lax.Precision.HIGHEST,
                         preferred_element_type=F32) + b_ref[...]


def _ada(c, w_ada, b_ada):
    L, D, N = w_ada.shape
    B = c.shape[0]
    tn = 1536
    return pl.pallas_call(
        _ada_kernel,
        grid=(L, N // tn),
        in_specs=[
            pl.BlockSpec((B, D), lambda l, j: (0, 0)),
            pl.BlockSpec((None, D, tn), lambda l, j: (l, 0, j)),
            pl.BlockSpec((None, 1, tn), lambda l, j: (l, 0, j)),
        ],
        out_specs=pl.BlockSpec((None, B, tn), lambda l, j: (l, 0, j)),
        out_shape=jax.ShapeDtypeStruct((L, B, N), F32),
        compiler_params=_cparams(("arbitrary", "arbitrary")),
        name="ada",
    )(c, w_ada, b_ada.reshape(L, 1, N))


def _inproj_kernel(x_ref, mod_ref, w_ref, cs_ref, sa_ref, sb_ref, lng_ref, lnb_ref,
                   qT_ref, k_ref, vT_ref, zu_ref, zvn_ref):
    D = D_MODEL
    x = x_ref[...]
    sh = mod_ref[:, 0:D]
    sc = mod_ref[:, D:2 * D]
    h = (x * (1.0 + sc) + sh).astype(BF16)
    cs = cs_ref[...]
    sa = sa_ref[...]
    sb = sb_ref[...]

    def rope(t, rows):
        return t * cs[rows] + pltpu.roll(t, 96, 1) * sa[rows] + pltpu.roll(t, 32, 1) * sb[rows]

    half = x.shape[0] // 2
    for hf in range(2):
        rows = slice(hf * half, (hf + 1) * half)
        hh = h[rows, :]
        q = jnp.dot(hh, w_ref[:, 0:QK_COLS], preferred_element_type=F32)
        for j in range(N_HEADS):
            sl = slice(j * LANES, (j + 1) * LANES)
            qT_ref[sl, rows] = (rope(q[:, sl], rows) * Q_SCALE).T.astype(BF16)
        k = jnp.dot(hh, w_ref[:, QK_COLS:2 * QK_COLS], preferred_element_type=F32)
        for j in range(N_HEADS):
            sl = slice(j * LANES, (j + 1) * LANES)
            k_ref[rows, sl] = rope(k[:, sl], rows).astype(BF16)
        c0 = 2 * QK_COLS
        v = jnp.dot(hh, w_ref[:, c0:c0 + DIFF_WIDTH], preferred_element_type=F32)
        vT_ref[:, rows] = v.T.astype(BF16)
        c0 += DIFF_WIDTH
        u = jnp.dot(hh, w_ref[:, c0:c0 + SG_WIDTH], preferred_element_type=F32)
        zu_ref[rows, :] = _gelu(u).astype(BF16)
        c0 += SG_WIDTH
        z = _gelu(jnp.dot(hh, w_ref[:, c0:c0 + SG_WIDTH], preferred_element_type=F32))
        for g in range(N_SG):
            sl = slice(g * SG_DIM, (g + 1) * SG_DIM)
            zvn_ref[rows, sl] = _layer_norm_rows(z[:, sl], lng_ref[:, sl], lnb_ref[:, sl]).astype(BF16)


def _inproj(x, mod_l, w_in_b, rope_tabs, ln_g, ln_b):
    B, S, D = x.shape
    tm = min(ROW_TILE, S)
    nt = S // tm
    cs, sa, sb = rope_tabs
    row = lambda b, i: (b, i, 0)
    tab = pl.BlockSpec((tm, LANES), lambda b, i: (i, 0))
    return pl.pallas_call(
        _inproj_kernel,
        grid=(B, nt),
        in_specs=[
            pl.BlockSpec((None, tm, D), row),
            pl.BlockSpec((None, 1, 6 * D), lambda b, i: (b, 0, 0)),
            pl.BlockSpec(w_in_b.shape, lambda b, i: (0, 0)),
            tab, tab, tab,
            pl.BlockSpec((1, SG_WIDTH), lambda b, i: (0, 0)),
            pl.BlockSpec((1, SG_WIDTH), lambda b, i: (0, 0)),
        ],
        out_specs=[
            pl.BlockSpec((None, QK_COLS, tm), lambda b, i: (b, 0, i)),
            pl.BlockSpec((None, tm, QK_COLS), row),
            pl.BlockSpec((None, None, DIFF_WIDTH, tm), lambda b, i: (b, i, 0, 0)),
            pl.BlockSpec((None, tm, SG_WIDTH), row),
            pl.BlockSpec((None, tm, SG_WIDTH), row),
        ],
        out_shape=[
            jax.ShapeDtypeStruct((B, QK_COLS, S), BF16),
            jax.ShapeDtypeStruct((B, S, QK_COLS), BF16),
            jax.ShapeDtypeStruct((B, nt, DIFF_WIDTH, tm), BF16),
            jax.ShapeDtypeStruct((B, S, SG_WIDTH), BF16),
            jax.ShapeDtypeStruct((B, S, SG_WIDTH), BF16),
        ],
        compiler_params=_cparams(("arbitrary", "arbitrary")),
        name="inproj",
    )(x, mod_l, w_in_b, cs, sa, sb, ln_g, ln_b)


def _attn_kernel(lq1_ref, lk1_ref, lq2_ref, lk2_ref, g_ref, qT_ref, k_ref, vT_ref, o_ref,
                 q_scr, s_scr, m_scr, l_scr, acc_scr, *, lam_init):
    tq = qT_ref.shape[1]
    tk = vT_ref.shape[2]
    i = pl.program_id(2)
    qT = qT_ref[...]
    rows = lax.broadcasted_iota(I32, qT.shape, 0)
    zero = jnp.zeros_like(qT)
    q_scr[0] = jnp.where(rows < QK_DIM, qT, zero)
    q_scr[1] = jnp.where(rows >= QK_DIM, qT, zero)

    def scores(j, slot, c0):
        kb = k_ref[pl.ds(pl.multiple_of(j * tk, tk), tk), :]
        for mp in range(2):
            s_scr[slot, mp, :, c0:] = jnp.dot(kb, q_scr[mp, :, c0:], preferred_element_type=F32)

    def block_scores(slot, mp, c0, c1, masked):
        s = s_scr[slot, mp, :, c0:c1]
        if masked:
            kpos = lax.broadcasted_iota(I32, s.shape, 0)
            qpos = lax.broadcasted_iota(I32, s.shape, 1)
            s = jnp.where(kpos <= qpos, s, -jnp.inf)
        return s

    def pv_unshifted(j, slot, c0, c1, masked):
        vb = vT_ref[j]
        for mp in range(2):
            p = jnp.exp2(block_scores(slot, mp, c0, c1, masked))
            l_scr[mp, :, c0:c1] += jnp.sum(p, axis=0, keepdims=True)
            acc_scr[mp, :, c0:c1] += jnp.dot(vb, p.astype(BF16), preferred_element_type=F32)

    def pv_online(j, slot, c0, c1, masked):
        vb = vT_ref[j]
        for mp in range(2):
            s = block_scores(slot, mp, c0, c1, masked)
            m_old = m_scr[mp, :, c0:c1]
            m_new = jnp.maximum(m_old, jnp.max(s, axis=0, keepdims=True))
            alpha = jnp.exp2(m_old - m_new)
            p = jnp.exp2(s - m_new)
            l_scr[mp, :, c0:c1] = alpha * l_scr[mp, :, c0:c1] + jnp.sum(p, axis=0, keepdims=True)
            acc_scr[mp, :, c0:c1] = (alpha * acc_scr[mp, :, c0:c1]
                                     + jnp.dot(vb, p.astype(BF16), preferred_element_type=F32))
            m_scr[mp, :, c0:c1] = m_new

    def sweep(block):
        acc_scr[...] = jnp.zeros(acc_scr.shape, F32)
        l_scr[...] = jnp.zeros(l_scr.shape, F32)
        scores(0, 0, 0)

        def pair(t):
            scores(t, 1, 0)
            block(t - 1, 0, 0, tq, False)
            scores(t + 1, 0, 0)
            block(t, 1, 0, tq, False)

        def body(u, carry):
            pair(4 * u + 1)
            pair(4 * u + 3)
            return carry

        lax.fori_loop(0, i // 2, body, 0)

        @pl.when(i % 2 == 1)
        def _():
            pair(2 * i - 1)

        scores(2 * i + 1, 1, tk)
        block(2 * i, 0, 0, tk, True)
        block(2 * i, 0, tk, tq, False)
        block(2 * i + 1, 1, tk, tq, True)

    def finalize():
        lam = (jnp.exp(jnp.sum(lq1_ref[...] * lk1_ref[...], axis=1, keepdims=True))
               - jnp.exp(jnp.sum(lq2_ref[...] * lk2_ref[...], axis=1, keepdims=True)) + lam_init)
        out = acc_scr[0] / l_scr[0] - lam * (acc_scr[1] / l_scr[1])
        ms = jnp.mean(out * out, axis=0, keepdims=True)
        y = out * lax.rsqrt(ms + LN_EPS) * g_ref[...] * (1.0 - lam_init)
        o_ref[...] = y.T.astype(BF16)

    sweep(pv_unshifted)
    finalize()
    den = jnp.concatenate([l_scr[0], l_scr[1]], axis=0)
    in_range = (den >= DEN_MIN) & (den <= DEN_MAX)
    n_bad = jnp.sum(jnp.where(in_range, 0.0, 1.0))

    @pl.when(n_bad > 0.0)
    def _():
        m_scr[...] = jnp.full(m_scr.shape, -jnp.inf, F32)
        sweep(pv_online)
        finalize()


def _attention(qT, k, vTb, lams, g_col, lam_init):
    B, _, S = qT.shape
    nkv, tk = vTb.shape[1], vTb.shape[3]
    tq = 2 * tk
    small = pl.BlockSpec((1, QK_DIM), lambda b, h, i: (0, 0))
    return pl.pallas_call(
        functools.partial(_attn_kernel, lam_init=lam_init),
        grid=(B, N_HEADS, S // tq),
        in_specs=[
            small, small, small, small,
            pl.BlockSpec((V_DIM, 1), lambda b, h, i: (0, 0)),
            pl.BlockSpec((None, HEAD_COLS, tq), lambda b, h, i: (b, h, i)),
            pl.BlockSpec((None, S, HEAD_COLS), lambda b, h, i: (b, 0, h)),
            pl.BlockSpec((None, nkv, V_DIM, tk), lambda b, h, i: (b, 0, h, 0)),
        ],
        out_specs=pl.BlockSpec((None, tq, V_DIM), lambda b, h, i: (b, i, h)),
        out_shape=jax.ShapeDtypeStruct((B, S, DIFF_WIDTH), BF16),
        scratch_shapes=[
            pltpu.VMEM((2, HEAD_COLS, tq), BF16),
            pltpu.VMEM((2, 2, tk, tq), F32),
            pltpu.VMEM((2, 1, tq), F32),
            pltpu.VMEM((2, 1, tq), F32),
            pltpu.VMEM((2, V_DIM, tq), F32),
        ],
        compiler_params=_cparams(("arbitrary", "arbitrary", "arbitrary")),
        name="diff_attn",
    )(*lams, g_col, qT, k, vTb)


def _mix_kernel(x_ref, mod_ref, att_ref, zu_ref, zvn_ref, wsp_ref, bsp_ref, wo_ref, g_ref, b_ref,
                o_ref, sg_scr, *, alpha):
    D = D_MODEL
    nc = zu_ref.shape[0]
    r = lax.broadcasted_iota(I32, (CHUNK, CHUNK), 0)
    c = lax.broadcasted_iota(I32, (CHUNK, CHUNK), 1)
    causal = r >= c
    for g in range(N_SG):
        sl = slice(g * SG_DIM, (g + 1) * SG_DIM)
        w = jnp.where(causal, wsp_ref[g], 0.0).astype(BF16)
        z = jnp.concatenate([zvn_ref[n, :, sl] for n in range(nc)], axis=1)
        mixed = jnp.dot(w, z, preferred_element_type=F32)
        for n in range(nc):
            gate = mixed[:, n * SG_DIM:(n + 1) * SG_DIM] + bsp_ref[g]
            sg_scr[n * CHUNK:(n + 1) * CHUNK, sl] = (zu_ref[n, :, sl].astype(F32) * gate).astype(BF16)
    gt1 = 1.0 + mod_ref[:, 2 * D:3 * D]
    half = x_ref.shape[0] // 2
    for hf in range(2):
        rows = slice(hf * half, (hf + 1) * half)
        mix = (jnp.dot(att_ref[rows, :], wo_ref[0:DIFF_WIDTH, :], preferred_element_type=F32)
               + jnp.dot(sg_scr[rows, :], wo_ref[DIFF_WIDTH:, :], preferred_element_type=F32))
        y = alpha * x_ref[rows, :] + gt1 * mix
        o_ref[rows, :] = _layer_norm_rows(y, g_ref[...], b_ref[...])


def _mix(x, mod_l, att, zu, zvn, w_sp, b_sp_full, w_out_b, ln_g, ln_b, alpha):
    B, S, D = x.shape
    tm = min(ROW_TILE, S)
    nc = tm // CHUNK
    row = lambda b, i: (b, i, 0)
    zu4 = zu.reshape(B, S // CHUNK, CHUNK, SG_WIDTH)
    zvn4 = zvn.reshape(B, S // CHUNK, CHUNK, SG_WIDTH)
    chunked = pl.BlockSpec((None, nc, CHUNK, SG_WIDTH), lambda b, i: (b, i, 0, 0))
    full2 = lambda a: pl.BlockSpec(a.shape, lambda b, i: (0,) * a.ndim)
    return pl.pallas_call(
        functools.partial(_mix_kernel, alpha=alpha),
        grid=(B, S // tm),
        in_specs=[
            pl.BlockSpec((None, tm, D), row),
            pl.BlockSpec((None, 1, 6 * D), lambda b, i: (b, 0, 0)),
            pl.BlockSpec((None, tm, DIFF_WIDTH), row),
            chunked, chunked,
            full2(w_sp), full2(b_sp_full), full2(w_out_b), full2(ln_g), full2(ln_b),
        ],
        out_specs=pl.BlockSpec((None, tm, D), row),
        out_shape=jax.ShapeDtypeStruct((B, S, D), F32),
        scratch_shapes=[pltpu.VMEM((tm, SG_WIDTH), BF16)],
        compiler_params=_cparams(("arbitrary", "arbitrary")),
        name="sgate_outproj_ln",
    )(x, mod_l, att, zu4, zvn4, w_sp, b_sp_full, w_out_b, ln_g, ln_b)


def _route_kernel(x_ref, mod_ref, wr_ref, br_ref, hp_ref, rt_ref, cnt_ref, run_scr):
    D = D_MODEL
    tm = x_ref.shape[0]

    @pl.when((pl.program_id(0) == 0) & (pl.program_id(1) == 0))
    def _():
        run_scr[...] = jnp.zeros(run_scr.shape, F32)

    sh = mod_ref[:, 3 * D:4 * D]
    sc = mod_ref[:, 4 * D:5 * D]
    h = x_ref[...] * (1.0 + sc) + sh
    h_hi = h.astype(BF16)
    h_lo = (h - h_hi.astype(F32)).astype(BF16)
    both = jnp.dot(h_hi, wr_ref[...], preferred_element_type=F32)
    logit = (both[:, 0:LANES] + both[:, LANES:]
             + jnp.dot(h_lo, wr_ref[:, 0:LANES], preferred_element_type=F32)) + br_ref[...]
    lt = logit.T[0:ROUTE_LANE0 + N_EXPERTS, :]
    row = lax.broadcasted_iota(I32, lt.shape, 0).astype(F32)
    neg = -jnp.inf
    big = float(LANES)

    def first_argmax(v):
        mx = jnp.max(v, axis=0, keepdims=True)
        idx = jnp.min(jnp.where(v == mx, row, big), axis=0, keepdims=True)
        return mx, idx

    in_grp = row < N_GROUPS
    gmax, gidx = first_argmax(jnp.where(in_grp, lt, neg))
    g_p = 1.0 / jnp.sum(jnp.where(in_grp, jnp.exp(lt - gmax), 0.0), axis=0, keepdims=True)
    lo_row = ROUTE_LANE0 + EXP_PER_GROUP * gidx
    sel = jnp.where((row >= lo_row) & (row < lo_row + EXP_PER_GROUP), lt, neg)
    v1, i1 = first_argmax(sel)
    v2, i2 = first_argmax(jnp.where(row == i1, neg, sel))
    t = jnp.exp(v2 - v1)
    w1 = g_p / (1.0 + t)
    w2 = g_p * t / (1.0 + t)

    first_lower = i1 < i2
    e_a = jnp.minimum(i1, i2) - lo_row
    e_b = jnp.maximum(i1, i2) - lo_row
    w_a = jnp.where(first_lower, w1, w2)
    w_b = jnp.where(first_lower, w2, w1)
    bucket = gidx * N_PAIRS + (e_a * (2 * EXP_PER_GROUP - 1 - e_a) * 0.5 + (e_b - e_a - 1.0))

    brow = lax.broadcasted_iota(I32, (LANES, tm), 0).astype(F32)
    hp_ref[:, 0:HALF] = _pack_halves(h)
    wslab = jnp.where(brow == 0.0, w_a, jnp.where(brow == 1.0, w_b, 0.0))
    hp_ref[:, HALF:] = pltpu.bitcast(wslab.T, U32)

    hot = brow == bucket
    onehot = jnp.where(hot, 1.0, 0.0)
    r = lax.broadcasted_iota(I32, (tm, tm), 0)
    c = lax.broadcasted_iota(I32, (tm, tm), 1)
    earlier = jnp.where(r < c, 1.0, 0.0).astype(BF16)
    before = jnp.dot(onehot.astype(BF16), earlier, preferred_element_type=F32) + run_scr[...]
    rank = jnp.sum(jnp.where(hot, before, 0.0), axis=0, keepdims=True)
    run_scr[...] = run_scr[...] + jnp.sum(onehot, axis=1, keepdims=True)
    cnt_ref[...] = run_scr[...]

    r8 = lax.broadcasted_iota(I32, (8, tm), 0)
    rt_ref[...] = jnp.where(r8 == 0, bucket, jnp.where(r8 == 1, rank, 0.0))


def _route(x1, mod_l, wr, br):
    B, S, D = x1.shape
    tm = min(ROW_TILE, S)
    row = lambda b, i: (b, i, 0)
    full2 = lambda a: pl.BlockSpec(a.shape, lambda b, i: (0,) * a.ndim)
    return pl.pallas_call(
        _route_kernel,
        grid=(B, S // tm),
        in_specs=[
            pl.BlockSpec((None, tm, D), row),
            pl.BlockSpec((None, 1, 6 * D), lambda b, i: (b, 0, 0)),
            full2(wr), full2(br),
        ],
        out_specs=[
            pl.BlockSpec((None, tm, ROW_WORDS), row),
            pl.BlockSpec((None, 8, tm), lambda b, i: (b, 0, i)),
            pl.BlockSpec((LANES, 1), lambda b, i: (0, 0)),
        ],
        out_shape=[
            jax.ShapeDtypeStruct((B, S, ROW_WORDS), U32),
            jax.ShapeDtypeStruct((B, 8, S), F32),
            jax.ShapeDtypeStruct((LANES, 1), F32),
        ],
        scratch_shapes=[pltpu.VMEM((LANES, 1), F32)],
        compiler_params=_cparams(("arbitrary", "arbitrary")),
        name="moe_route",
    )(x1, mod_l, wr, br)


def _row_of(ref, p):
    return ref.at[p >> 3, pl.ds(p & (SUBLANES - 1), 1)]


def _dispatch_kernel(p_ref, hp_ref, xs_in_ref, xs_ref, sem):
    del xs_in_ref
    ng = hp_ref.shape[0]

    def issue(g, carry):
        for u in range(SUBLANES):
            p = p_ref[0, 0, g * SUBLANES + u]
            pltpu.make_async_copy(hp_ref.at[g, pl.ds(u, 1)], _row_of(xs_ref, p), sem).start(priority=u % 2)
        return carry

    lax.fori_loop(0, ng, issue, 0)
    pltpu.make_async_copy(hp_ref, xs_ref.at[pl.ds(0, ng)], sem).wait()


def _dispatch(pos, hp, xs_buf):
    T = hp.shape[0]
    n_rows = xs_buf.shape[0]
    nb = min(DISPATCH_BLOCK, T)
    blk = pl.BlockSpec((1, 1, nb), lambda i: (i, 0, 0), memory_space=pltpu.SMEM)
    xs0 = xs_buf.reshape(n_rows // SUBLANES, SUBLANES, ROW_WORDS)
    xs = pl.pallas_call(
        _dispatch_kernel,
        grid=(T // nb,),
        in_specs=[blk, pl.BlockSpec((nb // SUBLANES, SUBLANES, ROW_WORDS), lambda i: (i, 0, 0)),
                  pl.BlockSpec(memory_space=pl.ANY)],
        out_specs=pl.BlockSpec(memory_space=pl.ANY),
        out_shape=jax.ShapeDtypeStruct(xs0.shape, U32),
        scratch_shapes=[pltpu.SemaphoreType.DMA(())],
        input_output_aliases={2: 0},
        compiler_params=_cparams(("arbitrary",)),
        name="moe_dispatch",
    )(pos.reshape(T // nb, 1, nb), hp.reshape(T // SUBLANES, SUBLANES, ROW_WORDS), xs0)
    return xs.reshape(n_rows, ROW_WORDS)


def _expert_kernel(ea_ref, eb_ref, nv_ref, xs_ref, wga_ref, wua_ref, wda_ref, wgb_ref, wub_ref, wdb_ref, ys_ref):
    del ea_ref, eb_ref

    @pl.when(pl.program_id(0) < nv_ref[0])
    def _():
        lo, hi = _unpack_halves(xs_ref[:, 0:HALF])
        lo = lo.astype(BF16)
        hi = hi.astype(BF16)
        wts = pltpu.bitcast(xs_ref[:, HALF:], F32)

        def hidden(wg_ref, wu_ref, w):
            g = (jnp.dot(lo, wg_ref[0:HALF, :], preferred_element_type=F32)
                 + jnp.dot(hi, wg_ref[HALF:, :], preferred_element_type=F32))
            u = (jnp.dot(lo, wu_ref[0:HALF, :], preferred_element_type=F32)
                 + jnp.dot(hi, wu_ref[HALF:, :], preferred_element_type=F32))
            return (g * jax.nn.sigmoid(g) * u * w).astype(BF16)

        y = (jnp.dot(hidden(wga_ref, wua_ref, wts[:, 0:1]), wda_ref[...], preferred_element_type=F32)
             + jnp.dot(hidden(wgb_ref, wub_ref, wts[:, 1:2]), wdb_ref[...], preferred_element_type=F32))
        ys_ref[...] = _pack_halves(y)


def _experts(tile_ea, tile_eb, n_valid, xs, w_gate, w_up, w_down, layer):
    n_rows = xs.shape[0]
    tm = EXPERT_TILE
    nt = n_rows // tm
    rows = lambda n, ea, eb, nv: (jnp.minimum(n, nv[0] - 1), 0)
    sel_a = lambda n, ea, eb, nv: (layer, ea[n], 0, 0)
    sel_b = lambda n, ea, eb, nv: (layer, eb[n], 0, 0)
    up = lambda sel: pl.BlockSpec((None, None, D_MODEL, D_EXPERT), sel)
    down = lambda sel: pl.BlockSpec((None, None, D_EXPERT, D_MODEL), sel)
    return pl.pallas_call(
        _expert_kernel,
        grid_spec=pltpu.PrefetchScalarGridSpec(
            num_scalar_prefetch=3,
            grid=(nt,),
            in_specs=[pl.BlockSpec((tm, ROW_WORDS), rows),
                      up(sel_a), up(sel_a), down(sel_a), up(sel_b), up(sel_b), down(sel_b)],
            out_specs=pl.BlockSpec((tm, HALF), rows),
        ),
        out_shape=jax.ShapeDtypeStruct((n_rows, HALF), U32),
        compiler_params=_cparams(("arbitrary",)),
        name="moe_experts",
    )(tile_ea, tile_eb, n_valid, xs, w_gate, w_up, w_down, w_gate, w_up, w_down)


def _combine_kernel(p_ref, pn_ref, ys_ref, x_ref, mod_ref, g_ref, b_ref, o_ref, buf, sems, *, alpha):
    D = D_MODEL
    tc = x_ref.shape[0]
    g = pl.program_id(0) * pl.num_programs(1) + pl.program_id(1)
    n_steps = pl.num_programs(0) * pl.num_programs(1)
    slot = g % 2

    def gather(pa_ref, s):
        def issue(g8, carry):
            for u in range(SUBLANES):
                p = pa_ref[0, 0, g8 * SUBLANES + u]
                pltpu.make_async_copy(_row_of(ys_ref, p), buf.at[s, g8, pl.ds(u, 1)], sems.at[s]).start(
                    priority=u % 2)
            return carry

        lax.fori_loop(0, tc // SUBLANES, issue, 0)

    @pl.when(g == 0)
    def _():
        gather(p_ref, 0)

    @pl.when(g + 1 < n_steps)
    def _():
        gather(pn_ref, 1 - slot)

    pltpu.make_async_copy(ys_ref.at[pl.ds(0, tc // SUBLANES)], buf.at[slot], sems.at[slot]).wait()

    lo, hi = _unpack_halves(buf[slot].reshape(tc, HALF))
    ffn = jnp.concatenate([lo, hi], axis=1)
    gt = mod_ref[:, 5 * D:6 * D]
    y = alpha * x_ref[...] + (1.0 + gt) * ffn
    o_ref[...] = _layer_norm_rows(y, g_ref[...], b_ref[...])


def _combine(pos, ys, x1, mod_l, ln_g, ln_b, alpha):
    B, S, D = x1.shape
    tc = min(COMBINE_TILE, S)
    nt = S // tc
    blk = pl.BlockSpec((1, 1, tc), lambda b, i: (b * nt + i, 0, 0), memory_space=pltpu.SMEM)
    nxt = pl.BlockSpec((1, 1, tc), lambda b, i: (jnp.minimum(b * nt + i + 1, B * nt - 1), 0, 0),
                       memory_space=pltpu.SMEM)
    row = lambda b, i: (b, i, 0)
    full2 = lambda a: pl.BlockSpec(a.shape, lambda b, i: (0,) * a.ndim)
    p = pos.reshape(B * nt, 1, tc)
    return pl.pallas_call(
        functools.partial(_combine_kernel, alpha=alpha),
        grid=(B, nt),
        in_specs=[
            blk, nxt,
            pl.BlockSpec(memory_space=pl.ANY),
            pl.BlockSpec((None, tc, D), row),
            pl.BlockSpec((None, 1, 6 * D), lambda b, i: (b, 0, 0)),
            full2(ln_g), full2(ln_b),
        ],
        out_specs=pl.BlockSpec((None, tc, D), row),
        out_shape=jax.ShapeDtypeStruct((B, S, D), F32),
        scratch_shapes=[pltpu.VMEM((2, tc // SUBLANES, SUBLANES, HALF), U32), pltpu.SemaphoreType.DMA((2,))],
        compiler_params=_cparams(("arbitrary", "arbitrary")),
        name="moe_combine_ln",
    )(p, p, ys.reshape(ys.shape[0] // SUBLANES, SUBLANES, HALF), x1, mod_l, ln_g, ln_b)


def _rope_tables(S):
    inv = 1.0 / (ROPE_THETA ** (jnp.arange(0, QK_DIM, 2, dtype=F32) / QK_DIM))
    ang = jnp.arange(S, dtype=F32)[:, None] * inv[None, :]
    cos, sin = jnp.cos(ang), jnp.sin(ang)
    half = QK_DIM // 2
    first = (jnp.arange(LANES) % QK_DIM) < half
    cs = jnp.tile(cos, (1, LANES // half))
    sn = jnp.tile(sin, (1, LANES // half))
    sa = jnp.where(first[None, :], -sn, 0.0)
    sb = jnp.where(first[None, :], 0.0, sn)
    return cs, sa, sb


def _router_matrix(w_group, b_group, w_router, b_router):
    D = w_group.shape[0]
    w = jnp.zeros((D, LANES), F32)
    w = w.at[:, 0:N_GROUPS].set(w_group)
    wr = jnp.transpose(w_router, (1, 0, 2)).reshape(D, N_EXPERTS)
    w = w.at[:, ROUTE_LANE0:ROUTE_LANE0 + N_EXPERTS].set(wr)
    b = jnp.zeros((1, LANES), F32)
    b = b.at[0, 0:N_GROUPS].set(b_group)
    b = b.at[0, ROUTE_LANE0:ROUTE_LANE0 + N_EXPERTS].set(b_router.reshape(N_EXPERTS))
    hi = w.astype(BF16)
    lo = (w - hi.astype(F32)).astype(BF16)
    return jnp.concatenate([hi, lo], axis=1), b


def _routing_tables(rt, cnt, n_tiles):
    B, _, S = rt.shape
    tm = EXPERT_TILE
    bucket = rt[:, 0, :].astype(I32)
    rank = rt[:, 1, :].astype(I32)
    counts = cnt[0:N_BUCKETS, 0].astype(I32)
    tiles = (counts + tm - 1) // tm
    tile_end = jnp.cumsum(tiles)
    tile_start = tile_end - tiles
    onehot = bucket[..., None] == jnp.arange(N_BUCKETS, dtype=I32)
    pos = (jnp.sum(jnp.where(onehot, tile_start * tm, 0), axis=-1) + rank).reshape(B * S)
    tile_ids = jnp.arange(n_tiles, dtype=I32)
    tile_bucket = jnp.minimum(jnp.sum(tile_ids[:, None] >= tile_end[None, :], axis=1), N_BUCKETS - 1)
    pair_a = jnp.asarray([a for a in range(EXP_PER_GROUP) for _ in range(a + 1, EXP_PER_GROUP)], I32)
    pair_b = jnp.asarray([b for a in range(EXP_PER_GROUP) for b in range(a + 1, EXP_PER_GROUP)], I32)
    group = tile_bucket // N_PAIRS
    pair = tile_bucket % N_PAIRS
    tile_ea = (group * EXP_PER_GROUP + pair_a[pair]).astype(I32)
    tile_eb = (group * EXP_PER_GROUP + pair_b[pair]).astype(I32)
    n_valid = tile_end[-1:].astype(I32)
    return pos, tile_ea, tile_eb, n_valid


def kernel(x, c, w_ada, b_ada, w_in, lambda_q1, lambda_k1, lambda_q2, lambda_k2, subln_g, sg_ln_g, sg_ln_b, w_spatial, b_spatial, w_out, ln1_g, ln1_b, w_group, b_group, w_router, b_router, w_gate, w_up, w_down, ln2_g, ln2_b):
    B, S, D = x.shape
    depth = w_in.shape[0]
    T = B * S
    alpha = (2.0 * depth) ** 0.25
    n_tiles = T // EXPERT_TILE + N_BUCKETS
    n_rows = n_tiles * EXPERT_TILE

    mod = _ada(c, w_ada, b_ada)
    tabs = _rope_tables(S)
    w_gate, w_up, w_down = w_gate.astype(BF16), w_up.astype(BF16), w_down.astype(BF16)
    xs = jnp.zeros((n_rows, ROW_WORDS), U32)
    for l in range(depth):
        mod_l = mod[l].reshape(B, 1, 6 * D)
        lam_init = 0.8 - 0.6 * math.exp(-0.3 * l)

        qT, k, vTb, zu, zvn = _inproj(x, mod_l, w_in[l].astype(BF16), tabs,
                                      sg_ln_g[l].reshape(1, SG_WIDTH), sg_ln_b[l].reshape(1, SG_WIDTH))
        lams = (lambda_q1[l].reshape(1, QK_DIM), lambda_k1[l].reshape(1, QK_DIM),
                lambda_q2[l].reshape(1, QK_DIM), lambda_k2[l].reshape(1, QK_DIM))
        att = _attention(qT, k, vTb, lams, subln_g[l].reshape(V_DIM, 1), lam_init)
        b_sp_full = jnp.broadcast_to(b_spatial[l][:, :, None], (N_SG, CHUNK, SG_DIM))
        x1 = _mix(x, mod_l, att, zu, zvn, w_spatial[l], b_sp_full, w_out[l].astype(BF16),
                  ln1_g[l].reshape(1, D), ln1_b[l].reshape(1, D), alpha)

        wr, br = _router_matrix(w_group[l], b_group[l], w_router[l], b_router[l])
        hp, rt, cnt = _route(x1, mod_l, wr, br)
        pos, tile_ea, tile_eb, n_valid = _routing_tables(rt, cnt, n_tiles)
        xs = _dispatch(pos, hp.reshape(T, ROW_WORDS), xs)
        ys = _experts(tile_ea, tile_eb, n_valid, xs, w_gate, w_up, w_down, l)
        x = _combine(pos, ys, x1, mod_l, ln2_g[l].reshape(1, D), ln2_b[l].reshape(1, D), alpha)
    return x
```

```python
import functools
import math

import jax
import jax.numpy as jnp
from jax import lax
from jax.experimental import pallas as pl
from jax.experimental.pallas import tpu as pltpu

F32 = jnp.float32
BF16 = jnp.bfloat16
U32 = jnp.uint32
I32 = jnp.int32

D_MODEL = 1024
N_HEADS = 4
QK_DIM = 64
V_DIM = 128
HEAD_COLS = 2 * QK_DIM
QK_COLS = N_HEADS * HEAD_COLS
DIFF_WIDTH = N_HEADS * V_DIM
N_SG = 4
SG_DIM = 128
SG_WIDTH = N_SG * SG_DIM
CHUNK = 128
N_GROUPS = 4
EXP_PER_GROUP = 8
N_EXPERTS = N_GROUPS * EXP_PER_GROUP
N_PAIRS = EXP_PER_GROUP * (EXP_PER_GROUP - 1) // 2
N_BUCKETS = N_GROUPS * N_PAIRS
D_EXPERT = 512
ROPE_THETA = 10000.0
LN_EPS = 1e-5
LANES = 128
HALF = D_MODEL // 2
ROW_WORDS = HALF + LANES

ROW_TILE = 512
EXPERT_TILE = 256
DISPATCH_BLOCK = 4096
COMBINE_TILE = 1024
SUBLANES = 8
ROUTE_LANE0 = 8
VMEM_LIMIT = 48 * 1024 * 1024

DEN_MIN = 2.0 ** -40
DEN_MAX = 2.0 ** 40
LOG2E = 1.4426950408889634
Q_SCALE = (QK_DIM ** -0.5) * LOG2E


def _cparams(sem):
    return pltpu.CompilerParams(dimension_semantics=sem, vmem_limit_bytes=VMEM_LIMIT)


def _layer_norm_rows(y, g, b):
    mu = jnp.mean(y, axis=-1, keepdims=True)
    yc = y - mu
    var = jnp.mean(yc * yc, axis=-1, keepdims=True)
    return yc * lax.rsqrt(var + LN_EPS) * g + b


def _gelu(x):
    return 0.5 * x * (1.0 + lax.erf(x * (2.0 ** -0.5)))


def _pack_halves(y):
    lo = pltpu.bitcast(y[:, :HALF].astype(BF16).astype(F32), U32) >> 16
    hi = pltpu.bitcast(y[:, HALF:].astype(BF16).astype(F32), U32) & jnp.uint32(0xFFFF0000)
    return lo | hi


def _unpack_halves(p):
    lo = pltpu.bitcast(p << 16, F32)
    hi = pltpu.bitcast(p & jnp.uint32(0xFFFF0000), F32)
    return lo, hi


def _ada_kernel(c_ref, w_ref, b_ref, o_ref):
    c = c_ref[...]
    sc = c * jax.nn.sigmoid(c)
    o_ref[...] = jnp.dot(sc, w_ref[...], precision=lax.Precision.HIGHEST,
                         preferred_element_type=F32) + b_ref[...]


def _ada(c, w_ada, b_ada):
    L, D, N = w_ada.shape
    B = c.shape[0]
    tn = 1536
    return pl.pallas_call(
        _ada_kernel,
        grid=(L, N // tn),
        in_specs=[
            pl.BlockSpec((B, D), lambda l, j: (0, 0)),
            pl.BlockSpec((None, D, tn), lambda l, j: (l, 0, j)),
            pl.BlockSpec((None, 1, tn), lambda l, j: (l, 0, j)),
        ],
        out_specs=pl.BlockSpec((None, B, tn), lambda l, j: (l, 0, j)),
        out_shape=jax.ShapeDtypeStruct((L, B, N), F32),
        compiler_params=_cparams(("arbitrary", "arbitrary")),
        name="ada",
    )(c, w_ada, b_ada.reshape(L, 1, N))


def _inproj_kernel(x_ref, mod_ref, w_ref, cs_ref, sa_ref, sb_ref, lng_ref, lnb_ref,
                   qT_ref, k_ref, vT_ref, zu_ref, zvn_ref):
    D = D_MODEL
    x = x_ref[...]
    sh = mod_ref[:, 0:D]
    sc = mod_ref[:, D:2 * D]
    h = (x * (1.0 + sc) + sh).astype(BF16)
    cs = cs_ref[...]
    sa = sa_ref[...]
    sb = sb_ref[...]

    def rope(t, rows):
        return t * cs[rows] + pltpu.roll(t, 96, 1) * sa[rows] + pltpu.roll(t, 32, 1) * sb[rows]

    half = x.shape[0] // 2
    for hf in range(2):
        rows = slice(hf * half, (hf + 1) * half)
        hh = h[rows, :]
        q = jnp.dot(hh, w_ref[:, 0:QK_COLS], preferred_element_type=F32)
        for j in range(N_HEADS):
            sl = slice(j * LANES, (j + 1) * LANES)
            qT_ref[sl, rows] = (rope(q[:, sl], rows) * Q_SCALE).T.astype(BF16)
        k = jnp.dot(hh, w_ref[:, QK_COLS:2 * QK_COLS], preferred_element_type=F32)
        for j in range(N_HEADS):
            sl = slice(j * LANES, (j + 1) * LANES)
            k_ref[rows, sl] = rope(k[:, sl], rows).astype(BF16)
        c0 = 2 * QK_COLS
        v = jnp.dot(hh, w_ref[:, c0:c0 + DIFF_WIDTH], preferred_element_type=F32)
        vT_ref[:, rows] = v.T.astype(BF16)
        c0 += DIFF_WIDTH
        u = jnp.dot(hh, w_ref[:, c0:c0 + SG_WIDTH], preferred_element_type=F32)
        zu_ref[rows, :] = _gelu(u).astype(BF16)
        c0 += SG_WIDTH
        z = _gelu(jnp.dot(hh, w_ref[:, c0:c0 + SG_WIDTH], preferred_element_type=F32))
        for g in range(N_SG):
            sl = slice(g * SG_DIM, (g + 1) * SG_DIM)
            zvn_ref[rows, sl] = _layer_norm_rows(z[:, sl], lng_ref[:, sl], lnb_ref[:, sl]).astype(BF16)


def _inproj(x, mod_l, w_in_b, rope_tabs, ln_g, ln_b):
    B, S, D = x.shape
    tm = min(ROW_TILE, S)
    nt = S // tm
    cs, sa, sb = rope_tabs
    row = lambda b, i: (b, i, 0)
    tab = pl.BlockSpec((tm, LANES), lambda b, i: (i, 0))
    return pl.pallas_call(
        _inproj_kernel,
        grid=(B, nt),
        in_specs=[
            pl.BlockSpec((None, tm, D), row),
            pl.BlockSpec((None, 1, 6 * D), lambda b, i: (b, 0, 0)),
            pl.BlockSpec(w_in_b.shape, lambda b, i: (0, 0)),
            tab, tab, tab,
            pl.BlockSpec((1, SG_WIDTH), lambda b, i: (0, 0)),
            pl.BlockSpec((1, SG_WIDTH), lambda b, i: (0, 0)),
        ],
        out_specs=[
            pl.BlockSpec((None, QK_COLS, tm), lambda b, i: (b, 0, i)),
            pl.BlockSpec((None, tm, QK_COLS), row),
            pl.BlockSpec((None, None, DIFF_WIDTH, tm), lambda b, i: (b, i, 0, 0)),
            pl.BlockSpec((None, tm, SG_WIDTH), row),
            pl.BlockSpec((None, tm, SG_WIDTH), row),
        ],
        out_shape=[
            jax.ShapeDtypeStruct((B, QK_COLS, S), BF16),
            jax.ShapeDtypeStruct((B, S, QK_COLS), BF16),
            jax.ShapeDtypeStruct((B, nt, DIFF_WIDTH, tm), BF16),
            jax.ShapeDtypeStruct((B, S, SG_WIDTH), BF16),
            jax.ShapeDtypeStruct((B, S, SG_WIDTH), BF16),
        ],
        compiler_params=_cparams(("arbitrary", "arbitrary")),
        name="inproj",
    )(x, mod_l, w_in_b, cs, sa, sb, ln_g, ln_b)


def _attn_kernel(lq1_ref, lk1_ref, lq2_ref, lk2_ref, g_ref, qT_ref, k_ref, vT_ref, o_ref,
                 q_scr, s_scr, m_scr, l_scr, acc_scr, *, lam_init):
    tq = qT_ref.shape[1]
    tk = vT_ref.shape[2]
    i = pl.program_id(2)
    qT = qT_ref[...]
    rows = lax.broadcasted_iota(I32, qT.shape, 0)
    zero = jnp.zeros_like(qT)
    q_scr[0] = jnp.where(rows < QK_DIM, qT, zero)
    q_scr[1] = jnp.where(rows >= QK_DIM, qT, zero)

    def scores(j, slot, c0):
        kb = k_ref[pl.ds(pl.multiple_of(j * tk, tk), tk), :]
        for mp in range(2):
            s_scr[slot, mp, :, c0:] = jnp.dot(kb, q_scr[mp, :, c0:], preferred_element_type=F32)

    def block_scores(slot, mp, c0, c1, masked):
        s = s_scr[slot, mp, :, c0:c1]
        if masked:
            kpos = lax.broadcasted_iota(I32, s.shape, 0)
            qpos = lax.broadcasted_iota(I32, s.shape, 1)
            s = jnp.where(kpos <= qpos, s, -jnp.inf)
        return s

    def pv_unshifted(j, slot, c0, c1, masked):
        vb = vT_ref[j]
        for mp in range(2):
            p = jnp.exp2(block_scores(slot, mp, c0, c1, masked))
            l_scr[mp, :, c0:c1] += jnp.sum(p, axis=0, keepdims=True)
            acc_scr[mp, :, c0:c1] += jnp.dot(vb, p.astype(BF16), preferred_element_type=F32)

    def pv_online(j, slot, c0, c1, masked):
        vb = vT_ref[j]
        for mp in range(2):
            s = block_scores(slot, mp, c0, c1, masked)
            m_old = m_scr[mp, :, c0:c1]
            m_new = jnp.maximum(m_old, jnp.max(s, axis=0, keepdims=True))
            alpha = jnp.exp2(m_old - m_new)
            p = jnp.exp2(s - m_new)
            l_scr[mp, :, c0:c1] = alpha * l_scr[mp, :, c0:c1] + jnp.sum(p, axis=0, keepdims=True)
            acc_scr[mp, :, c0:c1] = (alpha * acc_scr[mp, :, c0:c1]
                                     + jnp.dot(vb, p.astype(BF16), preferred_element_type=F32))
            m_scr[mp, :, c0:c1] = m_new

    def sweep(block):
        acc_scr[...] = jnp.zeros(acc_scr.shape, F32)
        l_scr[...] = jnp.zeros(l_scr.shape, F32)
        scores(0, 0, 0)

        def pair(t):
            scores(t, 1, 0)
            block(t - 1, 0, 0, tq, False)
            scores(t + 1, 0, 0)
            block(t, 1, 0, tq, False)

        def body(u, carry):
            pair(4 * u + 1)
            pair(4 * u + 3)
            return carry

        lax.fori_loop(0, i // 2, body, 0)

        @pl.when(i % 2 == 1)
        def _():
            pair(2 * i - 1)

        scores(2 * i + 1, 1, tk)
        block(2 * i, 0, 0, tk, True)
        block(2 * i, 0, tk, tq, False)
        block(2 * i + 1, 1, tk, tq, True)

    def finalize():
        lam = (jnp.exp(jnp.sum(lq1_ref[...] * lk1_ref[...], axis=1, keepdims=True))
               - jnp.exp(jnp.sum(lq2_ref[...] * lk2_ref[...], axis=1, keepdims=True)) + lam_init)
        out = acc_scr[0] / l_scr[0] - lam * (acc_scr[1] / l_scr[1])
        ms = jnp.mean(out * out, axis=0, keepdims=True)
        y = out * lax.rsqrt(ms + LN_EPS) * g_ref[...] * (1.0 - lam_init)
        o_ref[...] = y.T.astype(BF16)

    sweep(pv_unshifted)
    finalize()
    den = jnp.concatenate([l_scr[0], l_scr[1]], axis=0)
    in_range = (den >= DEN_MIN) & (den <= DEN_MAX)
    n_bad = jnp.sum(jnp.where(in_range, 0.0, 1.0))

    @pl.when(n_bad > 0.0)
    def _():
        m_scr[...] = jnp.full(m_scr.shape, -jnp.inf, F32)
        sweep(pv_online)
        finalize()


def _attention(qT, k, vTb, lams, g_col, lam_init):
    B, _, S = qT.shape
    nkv, tk = vTb.shape[1], vTb.shape[3]
    tq = 2 * tk
    small = pl.BlockSpec((1, QK_DIM), lambda b, h, i: (0, 0))
    return pl.pallas_call(
        functools.partial(_attn_kernel, lam_init=lam_init),
        grid=(B, N_HEADS, S // tq),
        in_specs=[
            small, small, small, small,
            pl.BlockSpec((V_DIM, 1), lambda b, h, i: (0, 0)),
            pl.BlockSpec((None, HEAD_COLS, tq), lambda b, h, i: (b, h, i)),
            pl.BlockSpec((None, S, HEAD_COLS), lambda b, h, i: (b, 0, h)),
            pl.BlockSpec((None, nkv, V_DIM, tk), lambda b, h, i: (b, 0, h, 0)),
        ],
        out_specs=pl.BlockSpec((None, tq, V_DIM), lambda b, h, i: (b, i, h)),
        out_shape=jax.ShapeDtypeStruct((B, S, DIFF_WIDTH), BF16),
        scratch_shapes=[
            pltpu.VMEM((2, HEAD_COLS, tq), BF16),
            pltpu.VMEM((2, 2, tk, tq), F32),
            pltpu.VMEM((2, 1, tq), F32),
            pltpu.VMEM((2, 1, tq), F32),
            pltpu.VMEM((2, V_DIM, tq), F32),
        ],
        compiler_params=_cparams(("arbitrary", "arbitrary", "arbitrary")),
        name="diff_attn",
    )(*lams, g_col, qT, k, vTb)


def _mix_kernel(x_ref, mod_ref, att_ref, zu_ref, zvn_ref, wsp_ref, bsp_ref, wo_ref, g_ref, b_ref,
                o_ref, sg_scr, *, alpha):
    D = D_MODEL
    nc = zu_ref.shape[0]
    r = lax.broadcasted_iota(I32, (CHUNK, CHUNK), 0)
    c = lax.broadcasted_iota(I32, (CHUNK, CHUNK), 1)
    causal = r >= c
    for g in range(N_SG):
        sl = slice(g * SG_DIM, (g + 1) * SG_DIM)
        w = jnp.where(causal, wsp_ref[g], 0.0).astype(BF16)
        z = jnp.concatenate([zvn_ref[n, :, sl] for n in range(nc)], axis=1)
        mixed = jnp.dot(w, z, preferred_element_type=F32)
        for n in range(nc):
            gate = mixed[:, n * SG_DIM:(n + 1) * SG_DIM] + bsp_ref[g]
            sg_scr[n * CHUNK:(n + 1) * CHUNK, sl] = (zu_ref[n, :, sl].astype(F32) * gate).astype(BF16)
    gt1 = 1.0 + mod_ref[:, 2 * D:3 * D]
    half = x_ref.shape[0] // 2
    for hf in range(2):
        rows = slice(hf * half, (hf + 1) * half)
        mix = (jnp.dot(att_ref[rows, :], wo_ref[0:DIFF_WIDTH, :], preferred_element_type=F32)
               + jnp.dot(sg_scr[rows, :], wo_ref[DIFF_WIDTH:, :], preferred_element_type=F32))
        y = alpha * x_ref[rows, :] + gt1 * mix
        o_ref[rows, :] = _layer_norm_rows(y, g_ref[...], b_ref[...])


def _mix(x, mod_l, att, zu, zvn, w_sp, b_sp_full, w_out_b, ln_g, ln_b, alpha):
    B, S, D = x.shape
    tm = min(ROW_TILE, S)
    nc = tm // CHUNK
    row = lambda b, i: (b, i, 0)
    zu4 = zu.reshape(B, S // CHUNK, CHUNK, SG_WIDTH)
    zvn4 = zvn.reshape(B, S // CHUNK, CHUNK, SG_WIDTH)
    chunked = pl.BlockSpec((None, nc, CHUNK, SG_WIDTH), lambda b, i: (b, i, 0, 0))
    full2 = lambda a: pl.BlockSpec(a.shape, lambda b, i: (0,) * a.ndim)
    return pl.pallas_call(
        functools.partial(_mix_kernel, alpha=alpha),
        grid=(B, S // tm),
        in_specs=[
            pl.BlockSpec((None, tm, D), row),
            pl.BlockSpec((None, 1, 6 * D), lambda b, i: (b, 0, 0)),
            pl.BlockSpec((None, tm, DIFF_WIDTH), row),
            chunked, chunked,
            full2(w_sp), full2(b_sp_full), full2(w_out_b), full2(ln_g), full2(ln_b),
        ],
        out_specs=pl.BlockSpec((None, tm, D), row),
        out_shape=jax.ShapeDtypeStruct((B, S, D), F32),
        scratch_shapes=[pltpu.VMEM((tm, SG_WIDTH), BF16)],
        compiler_params=_cparams(("arbitrary", "arbitrary")),
        name="sgate_outproj_ln",
    )(x, mod_l, att, zu4, zvn4, w_sp, b_sp_full, w_out_b, ln_g, ln_b)


def _route_kernel(x_ref, mod_ref, wr_ref, br_ref, hp_ref, rt_ref, cnt_ref, run_scr):
    D = D_MODEL
    tm = x_ref.shape[0]

    @pl.when((pl.program_id(0) == 0) & (pl.program_id(1) == 0))
    def _():
        run_scr[...] = jnp.zeros(run_scr.shape, F32)

    sh = mod_ref[:, 3 * D:4 * D]
    sc = mod_ref[:, 4 * D:5 * D]
    h = x_ref[...] * (1.0 + sc) + sh
    h_hi = h.astype(BF16)
    h_lo = (h - h_hi.astype(F32)).astype(BF16)
    both = jnp.dot(h_hi, wr_ref[...], preferred_element_type=F32)
    logit = (both[:, 0:LANES] + both[:, LANES:]
             + jnp.dot(h_lo, wr_ref[:, 0:LANES], preferred_element_type=F32)) + br_ref[...]
    lt = logit.T[0:ROUTE_LANE0 + N_EXPERTS, :]
    row = lax.broadcasted_iota(I32, lt.shape, 0).astype(F32)
    neg = -jnp.inf
    big = float(LANES)

    def first_argmax(v):
        mx = jnp.max(v, axis=0, keepdims=True)
        idx = jnp.min(jnp.where(v == mx, row, big), axis=0, keepdims=True)
        return mx, idx

    in_grp = row < N_GROUPS
    gmax, gidx = first_argmax(jnp.where(in_grp, lt, neg))
    g_p = 1.0 / jnp.sum(jnp.where(in_grp, jnp.exp(lt - gmax), 0.0), axis=0, keepdims=True)
    lo_row = ROUTE_LANE0 + EXP_PER_GROUP * gidx
    sel = jnp.where((row >= lo_row) & (row < lo_row + EXP_PER_GROUP), lt, neg)
    v1, i1 = first_argmax(sel)
    v2, i2 = first_argmax(jnp.where(row == i1, neg, sel))
    t = jnp.exp(v2 - v1)
    w1 = g_p / (1.0 + t)
    w2 = g_p * t / (1.0 + t)

    first_lower = i1 < i2
    e_a = jnp.minimum(i1, i2) - lo_row
    e_b = jnp.maximum(i1, i2) - lo_row
    w_a = jnp.where(first_lower, w1, w2)
    w_b = jnp.where(first_lower, w2, w1)
    bucket = gidx * N_PAIRS + (e_a * (2 * EXP_PER_GROUP - 1 - e_a) * 0.5 + (e_b - e_a - 1.0))

    brow = lax.broadcasted_iota(I32, (LANES, tm), 0).astype(F32)
    hp_ref[:, 0:HALF] = _pack_halves(h)
    wslab = jnp.where(brow == 0.0, w_a, jnp.where(brow == 1.0, w_b, 0.0))
    hp_ref[:, HALF:] = pltpu.bitcast(wslab.T, U32)

    hot = brow == bucket
    onehot = jnp.where(hot, 1.0, 0.0)
    r = lax.broadcasted_iota(I32, (tm, tm), 0)
    c = lax.broadcasted_iota(I32, (tm, tm), 1)
    earlier = jnp.where(r < c, 1.0, 0.0).astype(BF16)
    before = jnp.dot(onehot.astype(BF16), earlier, preferred_element_type=F32) + run_scr[...]
    rank = jnp.sum(jnp.where(hot, before, 0.0), axis=0, keepdims=True)
    run_scr[...] = run_scr[...] + jnp.sum(onehot, axis=1, keepdims=True)
    cnt_ref[...] = run_scr[...]

    r8 = lax.broadcasted_iota(I32, (8, tm), 0)
    rt_ref[...] = jnp.where(r8 == 0, bucket, jnp.where(r8 == 1, rank, 0.0))


def _route(x1, mod_l, wr, br):
    B, S, D = x1.shape
    tm = min(ROW_TILE, S)
    row = lambda b, i: (b, i, 0)
    full2 = lambda a: pl.BlockSpec(a.shape, lambda b, i: (0,) * a.ndim)
    return pl.pallas_call(
        _route_kernel,
        grid=(B, S // tm),
        in_specs=[
            pl.BlockSpec((None, tm, D), row),
            pl.BlockSpec((None, 1, 6 * D), lambda b, i: (b, 0, 0)),
            full2(wr), full2(br),
        ],
        out_specs=[
            pl.BlockSpec((None, tm, ROW_WORDS), row),
            pl.BlockSpec((None, 8, tm), lambda b, i: (b, 0, i)),
            pl.BlockSpec((LANES, 1), lambda b, i: (0, 0)),
        ],
        out_shape=[
            jax.ShapeDtypeStruct((B, S, ROW_WORDS), U32),
            jax.ShapeDtypeStruct((B, 8, S), F32),
            jax.ShapeDtypeStruct((LANES, 1), F32),
        ],
        scratch_shapes=[pltpu.VMEM((LANES, 1), F32)],
        compiler_params=_cparams(("arbitrary", "arbitrary")),
        name="moe_route",
    )(x1, mod_l, wr, br)


def _row_of(ref, p):
    return ref.at[p >> 3, pl.ds(p & (SUBLANES - 1), 1)]


def _dispatch_kernel(p_ref, hp_ref, xs_in_ref, xs_ref, sem):
    del xs_in_ref
    ng = hp_ref.shape[0]

    def issue(g, carry):
        for u in range(SUBLANES):
            p = p_ref[0, 0, g * SUBLANES + u]
            pltpu.make_async_copy(hp_ref.at[g, pl.ds(u, 1)], _row_of(xs_ref, p), sem).start(priority=u % 2)
        return carry

    lax.fori_loop(0, ng, issue, 0)
    pltpu.make_async_copy(hp_ref, xs_ref.at[pl.ds(0, ng)], sem).wait()


def _dispatch(pos, hp, xs_buf):
    T = hp.shape[0]
    n_rows = xs_buf.shape[0]
    nb = min(DISPATCH_BLOCK, T)
    blk = pl.BlockSpec((1, 1, nb), lambda i: (i, 0, 0), memory_space=pltpu.SMEM)
    xs0 = xs_buf.reshape(n_rows // SUBLANES, SUBLANES, ROW_WORDS)
    xs = pl.pallas_call(
        _dispatch_kernel,
        grid=(T // nb,),
        in_specs=[blk, pl.BlockSpec((nb // SUBLANES, SUBLANES, ROW_WORDS), lambda i: (i, 0, 0)),
                  pl.BlockSpec(memory_space=pl.ANY)],
        out_specs=pl.BlockSpec(memory_space=pl.ANY),
        out_shape=jax.ShapeDtypeStruct(xs0.shape, U32),
        scratch_shapes=[pltpu.SemaphoreType.DMA(())],
        input_output_aliases={2: 0},
        compiler_params=_cparams(("arbitrary",)),
        name="moe_dispatch",
    )(pos.reshape(T // nb, 1, nb), hp.reshape(T // SUBLANES, SUBLANES, ROW_WORDS), xs0)
    return xs.reshape(n_rows, ROW_WORDS)


def _expert_kernel(ea_ref, eb_ref, nv_ref, xs_ref, wga_ref, wua_ref, wda_ref, wgb_ref, wub_ref, wdb_ref, ys_ref):
    del ea_ref, eb_ref

    @pl.when(pl.program_id(0) < nv_ref[0])
    def _():
        lo, hi = _unpack_halves(xs_ref[:, 0:HALF])
        lo = lo.astype(BF16)
        hi = hi.astype(BF16)
        wts = pltpu.bitcast(xs_ref[:, HALF:], F32)

        def hidden(wg_ref, wu_ref, w):
            g = (jnp.dot(lo, wg_ref[0:HALF, :], preferred_element_type=F32)
                 + jnp.dot(hi, wg_ref[HALF:, :], preferred_element_type=F32))
            u = (jnp.dot(lo, wu_ref[0:HALF, :], preferred_element_type=F32)
                 + jnp.dot(hi, wu_ref[HALF:, :], preferred_element_type=F32))
            return (g * jax.nn.sigmoid(g) * u * w).astype(BF16)

        y = (jnp.dot(hidden(wga_ref, wua_ref, wts[:, 0:1]), wda_ref[...], preferred_element_type=F32)
             + jnp.dot(hidden(wgb_ref, wub_ref, wts[:, 1:2]), wdb_ref[...], preferred_element_type=F32))
        ys_ref[...] = _pack_halves(y)


def _experts(tile_ea, tile_eb, n_valid, xs, w_gate, w_up, w_down, layer):
    n_rows = xs.shape[0]
    tm = EXPERT_TILE
    nt = n_rows // tm
    rows = lambda n, ea, eb, nv: (jnp.minimum(n, nv[0] - 1), 0)
    sel_a = lambda n, ea, eb, nv: (layer, ea[n], 0, 0)
    sel_b = lambda n, ea, eb, nv: (layer, eb[n], 0, 0)
    up = lambda sel: pl.BlockSpec((None, None, D_MODEL, D_EXPERT), sel)
    down = lambda sel: pl.BlockSpec((None, None, D_EXPERT, D_MODEL), sel)
    return pl.pallas_call(
        _expert_kernel,
        grid_spec=pltpu.PrefetchScalarGridSpec(
            num_scalar_prefetch=3,
            grid=(nt,),
            in_specs=[pl.BlockSpec((tm, ROW_WORDS), rows),
                      up(sel_a), up(sel_a), down(sel_a), up(sel_b), up(sel_b), down(sel_b)],
            out_specs=pl.BlockSpec((tm, HALF), rows),
        ),
        out_shape=jax.ShapeDtypeStruct((n_rows, HALF), U32),
        compiler_params=_cparams(("arbitrary",)),
        name="moe_experts",
    )(tile_ea, tile_eb, n_valid, xs, w_gate, w_up, w_down, w_gate, w_up, w_down)


def _combine_kernel(p_ref, pn_ref, ys_ref, x_ref, mod_ref, g_ref, b_ref, o_ref, buf, sems, *, alpha):
    D = D_MODEL
    tc = x_ref.shape[0]
    g = pl.program_id(0) * pl.num_programs(1) + pl.program_id(1)
    n_steps = pl.num_programs(0) * pl.num_programs(1)
    slot = g % 2

    def gather(pa_ref, s):
        def issue(g8, carry):
            for u in range(SUBLANES):
                p = pa_ref[0, 0, g8 * SUBLANES + u]
                pltpu.make_async_copy(_row_of(ys_ref, p), buf.at[s, g8, pl.ds(u, 1)], sems.at[s]).start(
                    priority=u % 2)
            return carry

        lax.fori_loop(0, tc // SUBLANES, issue, 0)

    @pl.when(g == 0)
    def _():
        gather(p_ref, 0)

    @pl.when(g + 1 < n_steps)
    def _():
        gather(pn_ref, 1 - slot)

    pltpu.make_async_copy(ys_ref.at[pl.ds(0, tc // SUBLANES)], buf.at[slot], sems.at[slot]).wait()

    lo, hi = _unpack_halves(buf[slot].reshape(tc, HALF))
    ffn = jnp.concatenate([lo, hi], axis=1)
    gt = mod_ref[:, 5 * D:6 * D]
    y = alpha * x_ref[...] + (1.0 + gt) * ffn
    o_ref[...] = _layer_norm_rows(y, g_ref[...], b_ref[...])


def _combine(pos, ys, x1, mod_l, ln_g, ln_b, alpha):
    B, S, D = x1.shape
    tc = min(COMBINE_TILE, S)
    nt = S // tc
    blk = pl.BlockSpec((1, 1, tc), lambda b, i: (b * nt + i, 0, 0), memory_space=pltpu.SMEM)
    nxt = pl.BlockSpec((1, 1, tc), lambda b, i: (jnp.minimum(b * nt + i + 1, B * nt - 1), 0, 0),
                       memory_space=pltpu.SMEM)
    row = lambda b, i: (b, i, 0)
    full2 = lambda a: pl.BlockSpec(a.shape, lambda b, i: (0,) * a.ndim)
    p = pos.reshape(B * nt, 1, tc)
    return pl.pallas_call(
        functools.partial(_combine_kernel, alpha=alpha),
        grid=(B, nt),
        in_specs=[
            blk, nxt,
            pl.BlockSpec(memory_space=pl.ANY),
            pl.BlockSpec((None, tc, D), row),
            pl.BlockSpec((None, 1, 6 * D), lambda b, i: (b, 0, 0)),
            full2(ln_g), full2(ln_b),
        ],
        out_specs=pl.BlockSpec((None, tc, D), row),
        out_shape=jax.ShapeDtypeStruct((B, S, D), F32),
        scratch_shapes=[pltpu.VMEM((2, tc // SUBLANES, SUBLANES, HALF), U32), pltpu.SemaphoreType.DMA((2,))],
        compiler_params=_cparams(("arbitrary", "arbitrary")),
        name="moe_combine_ln",
    )(p, p, ys.reshape(ys.shape[0] // SUBLANES, SUBLANES, HALF), x1, mod_l, ln_g, ln_b)


def _rope_tables(S):
    inv = 1.0 / (ROPE_THETA ** (jnp.arange(0, QK_DIM, 2, dtype=F32) / QK_DIM))
    ang = jnp.arange(S, dtype=F32)[:, None] * inv[None, :]
    cos, sin = jnp.cos(ang), jnp.sin(ang)
    half = QK_DIM // 2
    first = (jnp.arange(LANES) % QK_DIM) < half
    cs = jnp.tile(cos, (1, LANES // half))
    sn = jnp.tile(sin, (1, LANES // half))
    sa = jnp.where(first[None, :], -sn, 0.0)
    sb = jnp.where(first[None, :], 0.0, sn)
    return cs, sa, sb


def _router_matrix(w_group, b_group, w_router, b_router):
    D = w_group.shape[0]
    w = jnp.zeros((D, LANES), F32)
    w = w.at[:, 0:N_GROUPS].set(w_group)
    wr = jnp.transpose(w_router, (1, 0, 2)).reshape(D, N_EXPERTS)
    w = w.at[:, ROUTE_LANE0:ROUTE_LANE0 + N_EXPERTS].set(wr)
    b = jnp.zeros((1, LANES), F32)
    b = b.at[0, 0:N_GROUPS].set(b_group)
    b = b.at[0, ROUTE_LANE0:ROUTE_LANE0 + N_EXPERTS].set(b_router.reshape(N_EXPERTS))
    hi = w.astype(BF16)
    lo = (w - hi.astype(F32)).astype(BF16)
    return jnp.concatenate([hi, lo], axis=1), b


def _routing_tables(rt, cnt, n_tiles):
    B, _, S = rt.shape
    tm = EXPERT_TILE
    bucket = rt[:, 0, :].astype(I32)
    rank = rt[:, 1, :].astype(I32)
    counts = cnt[0:N_BUCKETS, 0].astype(I32)
    tiles = (counts + tm - 1) // tm
    tile_end = jnp.cumsum(tiles)
    tile_start = tile_end - tiles
    onehot = bucket[..., None] == jnp.arange(N_BUCKETS, dtype=I32)
    pos = (jnp.sum(jnp.where(onehot, tile_start * tm, 0), axis=-1) + rank).reshape(B * S)
    tile_ids = jnp.arange(n_tiles, dtype=I32)
    tile_bucket = jnp.minimum(jnp.sum(tile_ids[:, None] >= tile_end[None, :], axis=1), N_BUCKETS - 1)
    pair_a = jnp.asarray([a for a in range(EXP_PER_GROUP) for _ in range(a + 1, EXP_PER_GROUP)], I32)
    pair_b = jnp.asarray([b for a in range(EXP_PER_GROUP) for b in range(a + 1, EXP_PER_GROUP)], I32)
    group = tile_bucket // N_PAIRS
    pair = tile_bucket % N_PAIRS
    tile_ea = (group * EXP_PER_GROUP + pair_a[pair]).astype(I32)
    tile_eb = (group * EXP_PER_GROUP + pair_b[pair]).astype(I32)
    n_valid = tile_end[-1:].astype(I32)
    return pos, tile_ea, tile_eb, n_valid


def kernel(x, c, w_ada, b_ada, w_in, lambda_q1, lambda_k1, lambda_q2, lambda_k2, subln_g, sg_ln_g, sg_ln_b, w_spatial, b_spatial, w_out, ln1_g, ln1_b, w_group, b_group, w_router, b_router, w_gate, w_up, w_down, ln2_g, ln2_b):
    B, S, D = x.shape
    depth = w_in.shape[0]
    T = B * S
    alpha = (2.0 * depth) ** 0.25
    n_tiles = T // EXPERT_TILE + N_BUCKETS
    n_rows = n_tiles * EXPERT_TILE

    mod = _ada(c, w_ada, b_ada)
    tabs = _rope_tables(S)
    w_gate, w_up, w_down = w_gate.astype(BF16), w_up.astype(BF16), w_down.astype(BF16)
    xs = jnp.zeros((n_rows, ROW_WORDS), U32)
    for l in range(depth):
        mod_l = mod[l].reshape(B, 1, 6 * D)
        lam_init = 0.8 - 0.6 * math.exp(-0.3 * l)

        qT, k, vTb, zu, zvn = _inproj(x, mod_l, w_in[l].astype(BF16), tabs,
                                      sg_ln_g[l].reshape(1, SG_WIDTH), sg_ln_b[l].reshape(1, SG_WIDTH))
        lams = (lambda_q1[l].reshape(1, QK_DIM), lambda_k1[l].reshape(1, QK_DIM),
                lambda_q2[l].reshape(1, QK_DIM), lambda_k2[l].reshape(1, QK_DIM))
        att = _attention(qT, k, vTb, lams, subln_g[l].reshape(V_DIM, 1), lam_init)
        b_sp_full = jnp.broadcast_to(b_spatial[l][:, :, None], (N_SG, CHUNK, SG_DIM))
        x1 = _mix(x, mod_l, att, zu, zvn, w_spatial[l], b_sp_full, w_out[l].astype(BF16),
                  ln1_g[l].reshape(1, D), ln1_b[l].reshape(1, D), alpha)

        wr, br = _router_matrix(w_group[l], b_group[l], w_router[l], b_router[l])
        hp, rt, cnt = _route(x1, mod_l, wr, br)
        pos, tile_ea, tile_eb, n_valid = _routing_tables(rt, cnt, n_tiles)
        xs = _dispatch(pos, hp.reshape(T, ROW_WORDS), xs)
        ys = _experts(tile_ea, tile_eb, n_valid, xs, w_gate, w_up, w_down, l)
        x = _combine(pos, ys, x1, mod_l, ln2_g[l].reshape(1, D), ln2_b[l].reshape(1, D), alpha)
    return x
```

```python
import functools
import math

import jax
import jax.numpy as jnp
from jax import lax
from jax.experimental import pallas as pl
from jax.experimental.pallas import tpu as pltpu

F32 = jnp.float32
BF16 = jnp.bfloat16
U32 = jnp.uint32
I32 = jnp.int32

D_MODEL = 1024
N_HEADS = 4
QK_DIM = 64
V_DIM = 128
HEAD_COLS = 2 * QK_DIM
QK_COLS = N_HEADS * HEAD_COLS
DIFF_WIDTH = N_HEADS * V_DIM
N_SG = 4
SG_DIM = 128
SG_WIDTH = N_SG * SG_DIM
CHUNK = 128
N_GROUPS = 4
EXP_PER_GROUP = 8
N_EXPERTS = N_GROUPS * EXP_PER_GROUP
N_PAIRS = EXP_PER_GROUP * (EXP_PER_GROUP - 1) // 2
N_BUCKETS = N_GROUPS * N_PAIRS
D_EXPERT = 512
ROPE_THETA = 10000.0
LN_EPS = 1e-5
LANES = 128
HALF = D_MODEL // 2
ROW_WORDS = HALF + LANES

ROW_TILE = 512
EXPERT_TILE = 256
DISPATCH_BLOCK = 4096
COMBINE_TILE = 1024
SUBLANES = 8
ROUTE_LANE0 = 8
VMEM_LIMIT = 48 * 1024 * 1024

DEN_MIN = 2.0 ** -40
DEN_MAX = 2.0 ** 40
LOG2E = 1.4426950408889634
Q_SCALE = (QK_DIM ** -0.5) * LOG2E


def _cparams(sem):
    return pltpu.CompilerParams(dimension_semantics=sem, vmem_limit_bytes=VMEM_LIMIT)


def _layer_norm_rows(y, g, b):
    mu = jnp.mean(y, axis=-1, keepdims=True)
    yc = y - mu
    var = jnp.mean(yc * yc, axis=-1, keepdims=True)
    return yc * lax.rsqrt(var + LN_EPS) * g + b


def _gelu(x):
    return 0.5 * x * (1.0 + lax.erf(x * (2.0 ** -0.5)))


def _pack_halves(y):
    lo = pltpu.bitcast(y[:, :HALF].astype(BF16).astype(F32), U32) >> 16
    hi = pltpu.bitcast(y[:, HALF:].astype(BF16).astype(F32), U32) & jnp.uint32(0xFFFF0000)
    return lo | hi


def _unpack_halves(p):
    lo = pltpu.bitcast(p << 16, F32)
    hi = pltpu.bitcast(p & jnp.uint32(0xFFFF0000), F32)
    return lo, hi


def _ada_kernel(c_ref, w_ref, b_ref, o_ref):
    c = c_ref[...]
    sc = c * jax.nn.sigmoid(c)
    o_ref[...] = jnp.dot(sc, w_ref[...], precision=lax.Precision.HIGHEST,
                         preferred_element_type=F32) + b_ref[...]


def _ada(c, w_ada, b_ada):
    L, D, N = w_ada.shape
    B = c.shape[0]
    tn = 1536
    return pl.pallas_call(
        _ada_kernel,
        grid=(L, N // tn),
        in_specs=[
            pl.BlockSpec((B, D), lambda l, j: (0, 0)),
            pl.BlockSpec((None, D, tn), lambda l, j: (l, 0, j)),
            pl.BlockSpec((None, 1, tn), lambda l, j: (l, 0, j)),
        ],
        out_specs=pl.BlockSpec((None, B, tn), lambda l, j: (l, 0, j)),
        out_shape=jax.ShapeDtypeStruct((L, B, N), F32),
        compiler_params=_cparams(("arbitrary", "arbitrary")),
        name="ada",
    )(c, w_ada, b_ada.reshape(L, 1, N))


def _inproj_kernel(x_ref, mod_ref, w_ref, cs_ref, sa_ref, sb_ref, lng_ref, lnb_ref,
                   qT_ref, k_ref, vT_ref, zu_ref, zvn_ref):
    D = D_MODEL
    x = x_ref[...]
    sh = mod_ref[:, 0:D]
    sc = mod_ref[:, D:2 * D]
    h = (x * (1.0 + sc) + sh).astype(BF16)
    cs = cs_ref[...]
    sa = sa_ref[...]
    sb = sb_ref[...]

    def rope(t, rows):
        return t * cs[rows] + pltpu.roll(t, 96, 1) * sa[rows] + pltpu.roll(t, 32, 1) * sb[rows]

    half = x.shape[0] // 2
    for hf in range(2):
        rows = slice(hf * half, (hf + 1) * half)
        hh = h[rows, :]
        q = jnp.dot(hh, w_ref[:, 0:QK_COLS], preferred_element_type=F32)
        for j in range(N_HEADS):
            sl = slice(j * LANES, (j + 1) * LANES)
            qT_ref[sl, rows] = (rope(q[:, sl], rows) * Q_SCALE).T.astype(BF16)
        k = jnp.dot(hh, w_ref[:, QK_COLS:2 * QK_COLS], preferred_element_type=F32)
        for j in range(N_HEADS):
            sl = slice(j * LANES, (j + 1) * LANES)
            k_ref[rows, sl] = rope(k[:, sl], rows).astype(BF16)
        c0 = 2 * QK_COLS
        v = jnp.dot(hh, w_ref[:, c0:c0 + DIFF_WIDTH], preferred_element_type=F32)
        vT_ref[:, rows] = v.T.astype(BF16)
        c0 += DIFF_WIDTH
        u = jnp.dot(hh, w_ref[:, c0:c0 + SG_WIDTH], preferred_element_type=F32)
        zu_ref[rows, :] = _gelu(u).astype(BF16)
        c0 += SG_WIDTH
        z = _gelu(jnp.dot(hh, w_ref[:, c0:c0 + SG_WIDTH], preferred_element_type=F32))
        for g in range(N_SG):
            sl = slice(g * SG_DIM, (g + 1) * SG_DIM)
            zvn_ref[rows, sl] = _layer_norm_rows(z[:, sl], lng_ref[:, sl], lnb_ref[:, sl]).astype(BF16)


def _inproj(x, mod_l, w_in_b, rope_tabs, ln_g, ln_b):
    B, S, D = x.shape
    tm = min(ROW_TILE, S)
    nt = S // tm
    cs, sa, sb = rope_tabs
    row = lambda b, i: (b, i, 0)
    tab = pl.BlockSpec((tm, LANES), lambda b, i: (i, 0))
    return pl.pallas_call(
        _inproj_kernel,
        grid=(B, nt),
        in_specs=[
            pl.BlockSpec((None, tm, D), row),
            pl.BlockSpec((None, 1, 6 * D), lambda b, i: (b, 0, 0)),
            pl.BlockSpec(w_in_b.shape, lambda b, i: (0, 0)),
            tab, tab, tab,
            pl.BlockSpec((1, SG_WIDTH), lambda b, i: (0, 0)),
            pl.BlockSpec((1, SG_WIDTH), lambda b, i: (0, 0)),
        ],
        out_specs=[
            pl.BlockSpec((None, QK_COLS, tm), lambda b, i: (b, 0, i)),
            pl.BlockSpec((None, tm, QK_COLS), row),
            pl.BlockSpec((None, None, DIFF_WIDTH, tm), lambda b, i: (b, i, 0, 0)),
            pl.BlockSpec((None, tm, SG_WIDTH), row),
            pl.BlockSpec((None, tm, SG_WIDTH), row),
        ],
        out_shape=[
            jax.ShapeDtypeStruct((B, QK_COLS, S), BF16),
            jax.ShapeDtypeStruct((B, S, QK_COLS), BF16),
            jax.ShapeDtypeStruct((B, nt, DIFF_WIDTH, tm), BF16),
            jax.ShapeDtypeStruct((B, S, SG_WIDTH), BF16),
            jax.ShapeDtypeStruct((B, S, SG_WIDTH), BF16),
        ],
        compiler_params=_cparams(("arbitrary", "arbitrary")),
        name="inproj",
    )(x, mod_l, w_in_b, cs, sa, sb, ln_g, ln_b)


def _attn_kernel(lq1_ref, lk1_ref, lq2_ref, lk2_ref, g_ref, qT_ref, k_ref, vT_ref, o_ref,
                 q_scr, s_scr, m_scr, l_scr, acc_scr, *, lam_init):
    tq = qT_ref.shape[1]
    tk = vT_ref.shape[2]
    i = pl.program_id(2)
    qT = qT_ref[...]
    rows = lax.broadcasted_iota(I32, qT.shape, 0)
    zero = jnp.zeros_like(qT)
    q_scr[0] = jnp.where(rows < QK_DIM, qT, zero)
    q_scr[1] = jnp.where(rows >= QK_DIM, qT, zero)

    def scores(j, slot, c0):
        kb = k_ref[pl.ds(pl.multiple_of(j * tk, tk), tk), :]
        for mp in range(2):
            s_scr[slot, mp, :, c0:] = jnp.dot(kb, q_scr[mp, :, c0:], preferred_element_type=F32)

    def block_scores(slot, mp, c0, c1, masked):
        s = s_scr[slot, mp, :, c0:c1]
        if masked:
            kpos = lax.broadcasted_iota(I32, s.shape, 0)
            qpos = lax.broadcasted_iota(I32, s.shape, 1)
            s = jnp.where(kpos <= qpos, s, -jnp.inf)
        return s

    def pv_unshifted(j, slot, c0, c1, masked):
        vb = vT_ref[j]
        for mp in range(2):
            p = jnp.exp2(block_scores(slot, mp, c0, c1, masked))
            l_scr[mp, :, c0:c1] += jnp.sum(p, axis=0, keepdims=True)
            acc_scr[mp, :, c0:c1] += jnp.dot(vb, p.astype(BF16), preferred_element_type=F32)

    def pv_online(j, slot, c0, c1, masked):
        vb = vT_ref[j]
        for mp in range(2):
            s = block_scores(slot, mp, c0, c1, masked)
            m_old = m_scr[mp, :, c0:c1]
            m_new = jnp.maximum(m_old, jnp.max(s, axis=0, keepdims=True))
            alpha = jnp.exp2(m_old - m_new)
            p = jnp.exp2(s - m_new)
            l_scr[mp, :, c0:c1] = alpha * l_scr[mp, :, c0:c1] + jnp.sum(p, axis=0, keepdims=True)
            acc_scr[mp, :, c0:c1] = (alpha * acc_scr[mp, :, c0:c1]
                                     + jnp.dot(vb, p.astype(BF16), preferred_element_type=F32))
            m_scr[mp, :, c0:c1] = m_new

    def sweep(block, finish):
        acc_scr[...] = jnp.zeros(acc_scr.shape, F32)
        l_scr[...] = jnp.zeros(l_scr.shape, F32)
        scores(0, 0, 0)

        def pair(t):
            scores(t, 1, 0)
            block(t - 1, 0, 0, tq, False)
            scores(t + 1, 0, 0)
            block(t, 1, 0, tq, False)

        def body(u, carry):
            pair(4 * u + 1)
            pair(4 * u + 3)
            return carry

        lax.fori_loop(0, i // 2, body, 0)

        def tail():
            scores(2 * i + 1, 1, tk)
            block(2 * i, 0, 0, tk, True)
            block(2 * i, 0, tk, tq, False)
            block(2 * i + 1, 1, tk, tq, True)
            finish()

        @pl.when(i % 2 == 1)
        def _():
            pair(2 * i - 1)
            tail()

        @pl.when(i % 2 == 0)
        def _():
            tail()

    def finalize():
        lam = (jnp.exp(jnp.sum(lq1_ref[...] * lk1_ref[...], axis=1, keepdims=True))
               - jnp.exp(jnp.sum(lq2_ref[...] * lk2_ref[...], axis=1, keepdims=True)) + lam_init)
        out = acc_scr[0] / l_scr[0] - lam * (acc_scr[1] / l_scr[1])
        ms = jnp.mean(out * out, axis=0, keepdims=True)
        y = out * lax.rsqrt(ms + LN_EPS) * g_ref[...] * (1.0 - lam_init)
        o_ref[...] = y.T.astype(BF16)

    sweep(pv_unshifted, finalize)
    den = jnp.concatenate([l_scr[0], l_scr[1]], axis=0)
    in_range = (den >= DEN_MIN) & (den <= DEN_MAX)
    n_bad = jnp.sum(jnp.where(in_range, 0.0, 1.0))

    @pl.when(n_bad > 0.0)
    def _():
        m_scr[...] = jnp.full(m_scr.shape, -jnp.inf, F32)
        sweep(pv_online, finalize)


def _attention(qT, k, vTb, lams, g_col, lam_init):
    B, _, S = qT.shape
    nkv, tk = vTb.shape[1], vTb.shape[3]
    tq = 2 * tk
    small = pl.BlockSpec((1, QK_DIM), lambda b, h, i: (0, 0))
    return pl.pallas_call(
        functools.partial(_attn_kernel, lam_init=lam_init),
        grid=(B, N_HEADS, S // tq),
        in_specs=[
            small, small, small, small,
            pl.BlockSpec((V_DIM, 1), lambda b, h, i: (0, 0)),
            pl.BlockSpec((None, HEAD_COLS, tq), lambda b, h, i: (b, h, i)),
            pl.BlockSpec((None, S, HEAD_COLS), lambda b, h, i: (b, 0, h)),
            pl.BlockSpec((None, nkv, V_DIM, tk), lambda b, h, i: (b, 0, h, 0)),
        ],
        out_specs=pl.BlockSpec((None, tq, V_DIM), lambda b, h, i: (b, i, h)),
        out_shape=jax.ShapeDtypeStruct((B, S, DIFF_WIDTH), BF16),
        scratch_shapes=[
            pltpu.VMEM((2, HEAD_COLS, tq), BF16),
            pltpu.VMEM((2, 2, tk, tq), F32),
            pltpu.VMEM((2, 1, tq), F32),
            pltpu.VMEM((2, 1, tq), F32),
            pltpu.VMEM((2, V_DIM, tq), F32),
        ],
        compiler_params=_cparams(("arbitrary", "arbitrary", "arbitrary")),
        name="diff_attn",
    )(*lams, g_col, qT, k, vTb)


def _mix_kernel(x_ref, mod_ref, att_ref, zu_ref, zvn_ref, wsp_ref, bsp_ref, wo_ref, g_ref, b_ref,
                o_ref, sg_scr, *, alpha):
    D = D_MODEL
    nc = zu_ref.shape[0]
    r = lax.broadcasted_iota(I32, (CHUNK, CHUNK), 0)
    c = lax.broadcasted_iota(I32, (CHUNK, CHUNK), 1)
    causal = r >= c
    for g in range(N_SG):
        sl = slice(g * SG_DIM, (g + 1) * SG_DIM)
        w = jnp.where(causal, wsp_ref[g], 0.0).astype(BF16)
        z = jnp.concatenate([zvn_ref[n, :, sl] for n in range(nc)], axis=1)
        mixed = jnp.dot(w, z, preferred_element_type=F32)
        for n in range(nc):
            gate = mixed[:, n * SG_DIM:(n + 1) * SG_DIM] + bsp_ref[g]
            sg_scr[n * CHUNK:(n + 1) * CHUNK, sl] = (zu_ref[n, :, sl].astype(F32) * gate).astype(BF16)
    gt1 = 1.0 + mod_ref[:, 2 * D:3 * D]
    half = x_ref.shape[0] // 2
    for hf in range(2):
        rows = slice(hf * half, (hf + 1) * half)
        mix = (jnp.dot(att_ref[rows, :], wo_ref[0:DIFF_WIDTH, :], preferred_element_type=F32)
               + jnp.dot(sg_scr[rows, :], wo_ref[DIFF_WIDTH:, :], preferred_element_type=F32))
        y = alpha * x_ref[rows, :] + gt1 * mix
        o_ref[rows, :] = _layer_norm_rows(y, g_ref[...], b_ref[...])


def _mix(x, mod_l, att, zu, zvn, w_sp, b_sp_full, w_out_b, ln_g, ln_b, alpha):
    B, S, D = x.shape
    tm = min(ROW_TILE, S)
    nc = tm // CHUNK
    row = lambda b, i: (b, i, 0)
    zu4 = zu.reshape(B, S // CHUNK, CHUNK, SG_WIDTH)
    zvn4 = zvn.reshape(B, S // CHUNK, CHUNK, SG_WIDTH)
    chunked = pl.BlockSpec((None, nc, CHUNK, SG_WIDTH), lambda b, i: (b, i, 0, 0))
    full2 = lambda a: pl.BlockSpec(a.shape, lambda b, i: (0,) * a.ndim)
    return pl.pallas_call(
        functools.partial(_mix_kernel, alpha=alpha),
        grid=(B, S // tm),
        in_specs=[
            pl.BlockSpec((None, tm, D), row),
            pl.BlockSpec((None, 1, 6 * D), lambda b, i: (b, 0, 0)),
            pl.BlockSpec((None, tm, DIFF_WIDTH), row),
            chunked, chunked,
            full2(w_sp), full2(b_sp_full), full2(w_out_b), full2(ln_g), full2(ln_b),
        ],
        out_specs=pl.BlockSpec((None, tm, D), row),
        out_shape=jax.ShapeDtypeStruct((B, S, D), F32),
        scratch_shapes=[pltpu.VMEM((tm, SG_WIDTH), BF16)],
        compiler_params=_cparams(("arbitrary", "arbitrary")),
        name="sgate_outproj_ln",
    )(x, mod_l, att, zu4, zvn4, w_sp, b_sp_full, w_out_b, ln_g, ln_b)


def _route_kernel(x_ref, mod_ref, wr_ref, br_ref, hp_ref, rt_ref, cnt_ref, run_scr):
    D = D_MODEL
    tm = x_ref.shape[0]

    @pl.when((pl.program_id(0) == 0) & (pl.program_id(1) == 0))
    def _():
        run_scr[...] = jnp.zeros(run_scr.shape, F32)

    sh = mod_ref[:, 3 * D:4 * D]
    sc = mod_ref[:, 4 * D:5 * D]
    h = x_ref[...] * (1.0 + sc) + sh
    h_hi = h.astype(BF16)
    h_lo = (h - h_hi.astype(F32)).astype(BF16)
    both = jnp.dot(h_hi, wr_ref[...], preferred_element_type=F32)
    logit = (both[:, 0:LANES] + both[:, LANES:]
             + jnp.dot(h_lo, wr_ref[:, 0:LANES], preferred_element_type=F32)) + br_ref[...]
    lt = logit.T[0:ROUTE_LANE0 + N_EXPERTS, :]
    row = lax.broadcasted_iota(I32, lt.shape, 0).astype(F32)
    neg = -jnp.inf
    big = float(LANES)

    def first_argmax(v):
        mx = jnp.max(v, axis=0, keepdims=True)
        idx = jnp.min(jnp.where(v == mx, row, big), axis=0, keepdims=True)
        return mx, idx

    in_grp = row < N_GROUPS
    gmax, gidx = first_argmax(jnp.where(in_grp, lt, neg))
    g_p = 1.0 / jnp.sum(jnp.where(in_grp, jnp.exp(lt - gmax), 0.0), axis=0, keepdims=True)
    lo_row = ROUTE_LANE0 + EXP_PER_GROUP * gidx
    sel = jnp.where((row >= lo_row) & (row < lo_row + EXP_PER_GROUP), lt, neg)
    v1, i1 = first_argmax(sel)
    v2, i2 = first_argmax(jnp.where(row == i1, neg, sel))
    t = jnp.exp(v2 - v1)
    w1 = g_p / (1.0 + t)
    w2 = g_p * t / (1.0 + t)

    first_lower = i1 < i2
    e_a = jnp.minimum(i1, i2) - lo_row
    e_b = jnp.maximum(i1, i2) - lo_row
    w_a = jnp.where(first_lower, w1, w2)
    w_b = jnp.where(first_lower, w2, w1)
    bucket = gidx * N_PAIRS + (e_a * (2 * EXP_PER_GROUP - 1 - e_a) * 0.5 + (e_b - e_a - 1.0))

    brow = lax.broadcasted_iota(I32, (LANES, tm), 0).astype(F32)
    hp_ref[:, 0:HALF] = _pack_halves(h)
    wslab = jnp.where(brow == 0.0, w_a, jnp.where(brow == 1.0, w_b, 0.0))
    hp_ref[:, HALF:] = pltpu.bitcast(wslab.T, U32)

    hot = brow == bucket
    onehot = jnp.where(hot, 1.0, 0.0)
    r = lax.broadcasted_iota(I32, (tm, tm), 0)
    c = lax.broadcasted_iota(I32, (tm, tm), 1)
    earlier = jnp.where(r < c, 1.0, 0.0).astype(BF16)
    before = jnp.dot(onehot.astype(BF16), earlier, preferred_element_type=F32) + run_scr[...]
    rank = jnp.sum(jnp.where(hot, before, 0.0), axis=0, keepdims=True)
    run_scr[...] = run_scr[...] + jnp.sum(onehot, axis=1, keepdims=True)
    cnt_ref[...] = run_scr[...]

    r8 = lax.broadcasted_iota(I32, (8, tm), 0)
    rt_ref[...] = jnp.where(r8 == 0, bucket, jnp.where(r8 == 1, rank, 0.0))


def _route(x1, mod_l, wr, br):
    B, S, D = x1.shape
    tm = min(ROW_TILE, S)
    row = lambda b, i: (b, i, 0)
    full2 = lambda a: pl.BlockSpec(a.shape, lambda b, i: (0,) * a.ndim)
    return pl.pallas_call(
        _route_kernel,
        grid=(B, S // tm),
        in_specs=[
            pl.BlockSpec((None, tm, D), row),
            pl.BlockSpec((None, 1, 6 * D), lambda b, i: (b, 0, 0)),
            full2(wr), full2(br),
        ],
        out_specs=[
            pl.BlockSpec((None, tm, ROW_WORDS), row),
            pl.BlockSpec((None, 8, tm), lambda b, i: (b, 0, i)),
            pl.BlockSpec((LANES, 1), lambda b, i: (0, 0)),
        ],
        out_shape=[
            jax.ShapeDtypeStruct((B, S, ROW_WORDS), U32),
            jax.ShapeDtypeStruct((B, 8, S), F32),
            jax.ShapeDtypeStruct((LANES, 1), F32),
        ],
        scratch_shapes=[pltpu.VMEM((LANES, 1), F32)],
        compiler_params=_cparams(("arbitrary", "arbitrary")),
        name="moe_route",
    )(x1, mod_l, wr, br)


def _row_of(ref, p):
    return ref.at[p >> 3, pl.ds(p & (SUBLANES - 1), 1)]


def _dispatch_kernel(p_ref, hp_ref, xs_in_ref, xs_ref, sem):
    del xs_in_ref
    ng = hp_ref.shape[0]

    def issue(g, carry):
        for u in range(SUBLANES):
            p = p_ref[0, 0, g * SUBLANES + u]
            pltpu.make_async_copy(hp_ref.at[g, pl.ds(u, 1)], _row_of(xs_ref, p), sem).start(priority=u % 2)
        return carry

    lax.fori_loop(0, ng, issue, 0)
    pltpu.make_async_copy(hp_ref, xs_ref.at[pl.ds(0, ng)], sem).wait()


def _dispatch(pos, hp, xs_buf):
    T = hp.shape[0]
    n_rows = xs_buf.shape[0]
    nb = min(DISPATCH_BLOCK, T)
    blk = pl.BlockSpec((1, 1, nb), lambda i: (i, 0, 0), memory_space=pltpu.SMEM)
    xs0 = xs_buf.reshape(n_rows // SUBLANES, SUBLANES, ROW_WORDS)
    xs = pl.pallas_call(
        _dispatch_kernel,
        grid=(T // nb,),
        in_specs=[blk, pl.BlockSpec((nb // SUBLANES, SUBLANES, ROW_WORDS), lambda i: (i, 0, 0)),
                  pl.BlockSpec(memory_space=pl.ANY)],
        out_specs=pl.BlockSpec(memory_space=pl.ANY),
        out_shape=jax.ShapeDtypeStruct(xs0.shape, U32),
        scratch_shapes=[pltpu.SemaphoreType.DMA(())],
        input_output_aliases={2: 0},
        compiler_params=_cparams(("arbitrary",)),
        name="moe_dispatch",
    )(pos.reshape(T // nb, 1, nb), hp.reshape(T // SUBLANES, SUBLANES, ROW_WORDS), xs0)
    return xs.reshape(n_rows, ROW_WORDS)


def _expert_kernel(ea_ref, eb_ref, nv_ref, xs_ref, wga_ref, wua_ref, wda_ref, wgb_ref, wub_ref, wdb_ref, ys_ref):
    del ea_ref, eb_ref

    @pl.when(pl.program_id(0) < nv_ref[0])
    def _():
        lo, hi = _unpack_halves(xs_ref[:, 0:HALF])
        lo = lo.astype(BF16)
        hi = hi.astype(BF16)
        wts = pltpu.bitcast(xs_ref[:, HALF:], F32)

        def hidden(wg_ref, wu_ref, w):
            g = (jnp.dot(lo, wg_ref[0:HALF, :], preferred_element_type=F32)
                 + jnp.dot(hi, wg_ref[HALF:, :], preferred_element_type=F32))
            u = (jnp.dot(lo, wu_ref[0:HALF, :], preferred_element_type=F32)
                 + jnp.dot(hi, wu_ref[HALF:, :], preferred_element_type=F32))
            return (g * jax.nn.sigmoid(g) * u * w).astype(BF16)

        y = (jnp.dot(hidden(wga_ref, wua_ref, wts[:, 0:1]), wda_ref[...], preferred_element_type=F32)
             + jnp.dot(hidden(wgb_ref, wub_ref, wts[:, 1:2]), wdb_ref[...], preferred_element_type=F32))
        ys_ref[...] = _pack_halves(y)


def _experts(tile_ea, tile_eb, n_valid, xs, w_gate, w_up, w_down, layer):
    n_rows = xs.shape[0]
    tm = EXPERT_TILE
    nt = n_rows // tm
    rows = lambda n, ea, eb, nv: (jnp.minimum(n, nv[0] - 1), 0)
    sel_a = lambda n, ea, eb, nv: (layer, ea[n], 0, 0)
    sel_b = lambda n, ea, eb, nv: (layer, eb[n], 0, 0)
    up = lambda sel: pl.BlockSpec((None, None, D_MODEL, D_EXPERT), sel)
    down = lambda sel: pl.BlockSpec((None, None, D_EXPERT, D_MODEL), sel)
    return pl.pallas_call(
        _expert_kernel,
        grid_spec=pltpu.PrefetchScalarGridSpec(
            num_scalar_prefetch=3,
            grid=(nt,),
            in_specs=[pl.BlockSpec((tm, ROW_WORDS), rows),
                      up(sel_a), up(sel_a), down(sel_a), up(sel_b), up(sel_b), down(sel_b)],
            out_specs=pl.BlockSpec((tm, HALF), rows),
        ),
        out_shape=jax.ShapeDtypeStruct((n_rows, HALF), U32),
        compiler_params=_cparams(("arbitrary",)),
        name="moe_experts",
    )(tile_ea, tile_eb, n_valid, xs, w_gate, w_up, w_down, w_gate, w_up, w_down)


def _combine_kernel(p_ref, pn_ref, ys_ref, x_ref, mod_ref, g_ref, b_ref, o_ref, buf, sems, *, alpha):
    D = D_MODEL
    tc = x_ref.shape[0]
    g = pl.program_id(0) * pl.num_programs(1) + pl.program_id(1)
    n_steps = pl.num_programs(0) * pl.num_programs(1)
    slot = g % 2

    def gather(pa_ref, s):
        def issue(g8, carry):
            for u in range(SUBLANES):
                p = pa_ref[0, 0, g8 * SUBLANES + u]
                pltpu.make_async_copy(_row_of(ys_ref, p), buf.at[s, g8, pl.ds(u, 1)], sems.at[s]).start(
                    priority=u % 2)
            return carry

        lax.fori_loop(0, tc // SUBLANES, issue, 0)

    @pl.when(g == 0)
    def _():
        gather(p_ref, 0)

    @pl.when(g + 1 < n_steps)
    def _():
        gather(pn_ref, 1 - slot)

    pltpu.make_async_copy(ys_ref.at[pl.ds(0, tc // SUBLANES)], buf.at[slot], sems.at[slot]).wait()

    lo, hi = _unpack_halves(buf[slot].reshape(tc, HALF))
    ffn = jnp.concatenate([lo, hi], axis=1)
    gt = mod_ref[:, 5 * D:6 * D]
    y = alpha * x_ref[...] + (1.0 + gt) * ffn
    o_ref[...] = _layer_norm_rows(y, g_ref[...], b_ref[...])


def _combine(pos, ys, x1, mod_l, ln_g, ln_b, alpha):
    B, S, D = x1.shape
    tc = min(COMBINE_TILE, S)
    nt = S // tc
    blk = pl.BlockSpec((1, 1, tc), lambda b, i: (b * nt + i, 0, 0), memory_space=pltpu.SMEM)
    nxt = pl.BlockSpec((1, 1, tc), lambda b, i: (jnp.minimum(b * nt + i + 1, B * nt - 1), 0, 0),
                       memory_space=pltpu.SMEM)
    row = lambda b, i: (b, i, 0)
    full2 = lambda a: pl.BlockSpec(a.shape, lambda b, i: (0,) * a.ndim)
    p = pos.reshape(B * nt, 1, tc)
    return pl.pallas_call(
        functools.partial(_combine_kernel, alpha=alpha),
        grid=(B, nt),
        in_specs=[
            blk, nxt,
            pl.BlockSpec(memory_space=pl.ANY),
            pl.BlockSpec((None, tc, D), row),
            pl.BlockSpec((None, 1, 6 * D), lambda b, i: (b, 0, 0)),
            full2(ln_g), full2(ln_b),
        ],
        out_specs=pl.BlockSpec((None, tc, D), row),
        out_shape=jax.ShapeDtypeStruct((B, S, D), F32),
        scratch_shapes=[pltpu.VMEM((2, tc // SUBLANES, SUBLANES, HALF), U32), pltpu.SemaphoreType.DMA((2,))],
        compiler_params=_cparams(("arbitrary", "arbitrary")),
        name="moe_combine_ln",
    )(p, p, ys.reshape(ys.shape[0] // SUBLANES, SUBLANES, HALF), x1, mod_l, ln_g, ln_b)


def _rope_tables(S):
    inv = 1.0 / (ROPE_THETA ** (jnp.arange(0, QK_DIM, 2, dtype=F32) / QK_DIM))
    ang = jnp.arange(S, dtype=F32)[:, None] * inv[None, :]
    cos, sin = jnp.cos(ang), jnp.sin(ang)
    half = QK_DIM // 2
    first = (jnp.arange(LANES) % QK_DIM) < half
    cs = jnp.tile(cos, (1, LANES // half))
    sn = jnp.tile(sin, (1, LANES // half))
    sa = jnp.where(first[None, :], -sn, 0.0)
    sb = jnp.where(first[None, :], 0.0, sn)
    return cs, sa, sb


def _router_matrix(w_group, b_group, w_router, b_router):
    D = w_group.shape[0]
    w = jnp.zeros((D, LANES), F32)
    w = w.at[:, 0:N_GROUPS].set(w_group)
    wr = jnp.transpose(w_router, (1, 0, 2)).reshape(D, N_EXPERTS)
    w = w.at[:, ROUTE_LANE0:ROUTE_LANE0 + N_EXPERTS].set(wr)
    b = jnp.zeros((1, LANES), F32)
    b = b.at[0, 0:N_GROUPS].set(b_group)
    b = b.at[0, ROUTE_LANE0:ROUTE_LANE0 + N_EXPERTS].set(b_router.reshape(N_EXPERTS))
    hi = w.astype(BF16)
    lo = (w - hi.astype(F32)).astype(BF16)
    return jnp.concatenate([hi, lo], axis=1), b


def _routing_tables(rt, cnt, n_tiles):
    B, _, S = rt.shape
    tm = EXPERT_TILE
    bucket = rt[:, 0, :].astype(I32)
    rank = rt[:, 1, :].astype(I32)
    counts = cnt[0:N_BUCKETS, 0].astype(I32)
    tiles = (counts + tm - 1) // tm
    tile_end = jnp.cumsum(tiles)
    tile_start = tile_end - tiles
    onehot = bucket[..., None] == jnp.arange(N_BUCKETS, dtype=I32)
    pos = (jnp.sum(jnp.where(onehot, tile_start * tm, 0), axis=-1) + rank).reshape(B * S)
    tile_ids = jnp.arange(n_tiles, dtype=I32)
    tile_bucket = jnp.minimum(jnp.sum(tile_ids[:, None] >= tile_end[None, :], axis=1), N_BUCKETS - 1)
    pair_a = jnp.asarray([a for a in range(EXP_PER_GROUP) for _ in range(a + 1, EXP_PER_GROUP)], I32)
    pair_b = jnp.asarray([b for a in range(EXP_PER_GROUP) for b in range(a + 1, EXP_PER_GROUP)], I32)
    group = tile_bucket // N_PAIRS
    pair = tile_bucket % N_PAIRS
    tile_ea = (group * EXP_PER_GROUP + pair_a[pair]).astype(I32)
    tile_eb = (group * EXP_PER_GROUP + pair_b[pair]).astype(I32)
    n_valid = tile_end[-1:].astype(I32)
    return pos, tile_ea, tile_eb, n_valid


def kernel(x, c, w_ada, b_ada, w_in, lambda_q1, lambda_k1, lambda_q2, lambda_k2, subln_g, sg_ln_g, sg_ln_b, w_spatial, b_spatial, w_out, ln1_g, ln1_b, w_group, b_group, w_router, b_router, w_gate, w_up, w_down, ln2_g, ln2_b):
    B, S, D = x.shape
    depth = w_in.shape[0]
    T = B * S
    alpha = (2.0 * depth) ** 0.25
    n_tiles = T // EXPERT_TILE + N_BUCKETS
    n_rows = n_tiles * EXPERT_TILE

    mod = _ada(c, w_ada, b_ada)
    tabs = _rope_tables(S)
    w_gate, w_up, w_down = w_gate.astype(BF16), w_up.astype(BF16), w_down.astype(BF16)
    xs = jnp.zeros((n_rows, ROW_WORDS), U32)
    for l in range(depth):
        mod_l = mod[l].reshape(B, 1, 6 * D)
        lam_init = 0.8 - 0.6 * math.exp(-0.3 * l)

        qT, k, vTb, zu, zvn = _inproj(x, mod_l, w_in[l].astype(BF16), tabs,
                                      sg_ln_g[l].reshape(1, SG_WIDTH), sg_ln_b[l].reshape(1, SG_WIDTH))
        lams = (lambda_q1[l].reshape(1, QK_DIM), lambda_k1[l].reshape(1, QK_DIM),
                lambda_q2[l].reshape(1, QK_DIM), lambda_k2[l].reshape(1, QK_DIM))
        att = _attention(qT, k, vTb, lams, subln_g[l].reshape(V_DIM, 1), lam_init)
        b_sp_full = jnp.broadcast_to(b_spatial[l][:, :, None], (N_SG, CHUNK, SG_DIM))
        x1 = _mix(x, mod_l, att, zu, zvn, w_spatial[l], b_sp_full, w_out[l].astype(BF16),
                  ln1_g[l].reshape(1, D), ln1_b[l].reshape(1, D), alpha)

        wr, br = _router_matrix(w_group[l], b_group[l], w_router[l], b_router[l])
        hp, rt, cnt = _route(x1, mod_l, wr, br)
        pos, tile_ea, tile_eb, n_valid = _routing_tables(rt, cnt, n_tiles)
        xs = _dispatch(pos, hp.reshape(T, ROW_WORDS), xs)
        ys = _experts(tile_ea, tile_eb, n_valid, xs, w_gate, w_up, w_down, l)
        x = _combine(pos, ys, x1, mod_l, ln2_g[l].reshape(1, D), ln2_b[l].reshape(1, D), alpha)
    return x
```

```python
import functools
import math

import jax
import jax.numpy as jnp
from jax import lax
from jax.experimental import pallas as pl
from jax.experimental.pallas import tpu as pltpu

F32 = jnp.float32
BF16 = jnp.bfloat16
U32 = jnp.uint32
I32 = jnp.int32

D_MODEL = 1024
N_HEADS = 4
QK_DIM = 64
V_DIM = 128
HEAD_COLS = 2 * QK_DIM
QK_COLS = N_HEADS * HEAD_COLS
DIFF_WIDTH = N_HEADS * V_DIM
N_SG = 4
SG_DIM = 128
SG_WIDTH = N_SG * SG_DIM
CHUNK = 128
N_GROUPS = 4
EXP_PER_GROUP = 8
N_EXPERTS = N_GROUPS * EXP_PER_GROUP
N_PAIRS = EXP_PER_GROUP * (EXP_PER_GROUP - 1) // 2
N_BUCKETS = N_GROUPS * N_PAIRS
D_EXPERT = 512
ROPE_THETA = 10000.0
LN_EPS = 1e-5
LANES = 128
HALF = D_MODEL // 2
ROW_WORDS = HALF + LANES

ROW_TILE = 512
EXPERT_TILE = 256
DISPATCH_BLOCK = 4096
COMBINE_TILE = 1024
SUBLANES = 8
ROUTE_LANE0 = 8
VMEM_LIMIT = 48 * 1024 * 1024

DEN_MIN = 2.0 ** -40
DEN_MAX = 2.0 ** 40
LOG2E = 1.4426950408889634
Q_SCALE = (QK_DIM ** -0.5) * LOG2E


def _cparams(sem):
    return pltpu.CompilerParams(dimension_semantics=sem, vmem_limit_bytes=VMEM_LIMIT)


def _layer_norm_rows(y, g, b):
    mu = jnp.mean(y, axis=-1, keepdims=True)
    yc = y - mu
    var = jnp.mean(yc * yc, axis=-1, keepdims=True)
    return yc * lax.rsqrt(var + LN_EPS) * g + b


def _gelu(x):
    return 0.5 * x * (1.0 + lax.erf(x * (2.0 ** -0.5)))


def _pack_halves(y):
    lo = pltpu.bitcast(y[:, :HALF].astype(BF16).astype(F32), U32) >> 16
    hi = pltpu.bitcast(y[:, HALF:].astype(BF16).astype(F32), U32) & jnp.uint32(0xFFFF0000)
    return lo | hi


def _unpack_halves(p):
    lo = pltpu.bitcast(p << 16, F32)
    hi = pltpu.bitcast(p & jnp.uint32(0xFFFF0000), F32)
    return lo, hi


def _ada_kernel(c_ref, w_ref, b_ref, o_ref):
    c = c_ref[...]
    sc = c * jax.nn.sigmoid(c)
    o_ref[...] = jnp.dot(sc, w_ref[...], precision=lax.Precision.HIGHEST,
                         preferred_element_type=F32) + b_ref[...]


def _ada(c, w_ada, b_ada):
    L, D, N = w_ada.shape
    B = c.shape[0]
    tn = 1536
    return pl.pallas_call(
        _ada_kernel,
        grid=(L, N // tn),
        in_specs=[
            pl.BlockSpec((B, D), lambda l, j: (0, 0)),
            pl.BlockSpec((None, D, tn), lambda l, j: (l, 0, j)),
            pl.BlockSpec((None, 1, tn), lambda l, j: (l, 0, j)),
        ],
        out_specs=pl.BlockSpec((None, B, tn), lambda l, j: (l, 0, j)),
        out_shape=jax.ShapeDtypeStruct((L, B, N), F32),
        compiler_params=_cparams(("arbitrary", "arbitrary")),
        name="ada",
    )(c, w_ada, b_ada.reshape(L, 1, N))


def _inproj_kernel(x_ref, mod_ref, w_ref, cs_ref, sa_ref, sb_ref, lng_ref, lnb_ref,
                   qT_ref, k_ref, vT_ref, zu_ref, zvn_ref):
    D = D_MODEL
    x = x_ref[...]
    sh = mod_ref[:, 0:D]
    sc = mod_ref[:, D:2 * D]
    h = (x * (1.0 + sc) + sh).astype(BF16)
    cs = cs_ref[...]
    sa = sa_ref[...]
    sb = sb_ref[...]

    def rope(t, rows):
        return t * cs[rows] + pltpu.roll(t, 96, 1) * sa[rows] + pltpu.roll(t, 32, 1) * sb[rows]

    half = x.shape[0] // 2
    for hf in range(2):
        rows = slice(hf * half, (hf + 1) * half)
        hh = h[rows, :]
        q = jnp.dot(hh, w_ref[:, 0:QK_COLS], preferred_element_type=F32)
        for j in range(N_HEADS):
            sl = slice(j * LANES, (j + 1) * LANES)
            qT_ref[sl, rows] = (rope(q[:, sl], rows) * Q_SCALE).T.astype(BF16)
        k = jnp.dot(hh, w_ref[:, QK_COLS:2 * QK_COLS], preferred_element_type=F32)
        for j in range(N_HEADS):
            sl = slice(j * LANES, (j + 1) * LANES)
            k_ref[rows, sl] = rope(k[:, sl], rows).astype(BF16)
        c0 = 2 * QK_COLS
        v = jnp.dot(hh, w_ref[:, c0:c0 + DIFF_WIDTH], preferred_element_type=F32)
        vT_ref[:, rows] = v.T.astype(BF16)
        c0 += DIFF_WIDTH
        u = jnp.dot(hh, w_ref[:, c0:c0 + SG_WIDTH], preferred_element_type=F32)
        zu_ref[rows, :] = _gelu(u).astype(BF16)
        c0 += SG_WIDTH
        z = _gelu(jnp.dot(hh, w_ref[:, c0:c0 + SG_WIDTH], preferred_element_type=F32))
        for g in range(N_SG):
            sl = slice(g * SG_DIM, (g + 1) * SG_DIM)
            zvn_ref[rows, sl] = _layer_norm_rows(z[:, sl], lng_ref[:, sl], lnb_ref[:, sl]).astype(BF16)


def _inproj(x, mod_l, w_in_b, rope_tabs, ln_g, ln_b):
    B, S, D = x.shape
    tm = min(ROW_TILE, S)
    nt = S // tm
    cs, sa, sb = rope_tabs
    row = lambda b, i: (b, i, 0)
    tab = pl.BlockSpec((tm, LANES), lambda b, i: (i, 0))
    return pl.pallas_call(
        _inproj_kernel,
        grid=(B, nt),
        in_specs=[
            pl.BlockSpec((None, tm, D), row),
            pl.BlockSpec((None, 1, 6 * D), lambda b, i: (b, 0, 0)),
            pl.BlockSpec(w_in_b.shape, lambda b, i: (0, 0)),
            tab, tab, tab,
            pl.BlockSpec((1, SG_WIDTH), lambda b, i: (0, 0)),
            pl.BlockSpec((1, SG_WIDTH), lambda b, i: (0, 0)),
        ],
        out_specs=[
            pl.BlockSpec((None, QK_COLS, tm), lambda b, i: (b, 0, i)),
            pl.BlockSpec((None, tm, QK_COLS), row),
            pl.BlockSpec((None, None, DIFF_WIDTH, tm), lambda b, i: (b, i, 0, 0)),
            pl.BlockSpec((None, tm, SG_WIDTH), row),
            pl.BlockSpec((None, tm, SG_WIDTH), row),
        ],
        out_shape=[
            jax.ShapeDtypeStruct((B, QK_COLS, S), BF16),
            jax.ShapeDtypeStruct((B, S, QK_COLS), BF16),
            jax.ShapeDtypeStruct((B, nt, DIFF_WIDTH, tm), BF16),
            jax.ShapeDtypeStruct((B, S, SG_WIDTH), BF16),
            jax.ShapeDtypeStruct((B, S, SG_WIDTH), BF16),
        ],
        compiler_params=_cparams(("arbitrary", "arbitrary")),
        name="inproj",
    )(x, mod_l, w_in_b, cs, sa, sb, ln_g, ln_b)


def _attn_kernel(lq1_ref, lk1_ref, lq2_ref, lk2_ref, g_ref, qT_ref, k_ref, vT_ref, o_ref,
                 q_scr, s_scr, m_scr, l_scr, acc_scr, *, lam_init):
    tq = qT_ref.shape[1]
    tk = vT_ref.shape[2]
    i = pl.program_id(2)
    qT = qT_ref[...]
    rows = lax.broadcasted_iota(I32, qT.shape, 0)
    zero = jnp.zeros_like(qT)
    q_scr[0] = jnp.where(rows < QK_DIM, qT, zero)
    q_scr[1] = jnp.where(rows >= QK_DIM, qT, zero)

    def scores(j, slot, c0):
        kb = k_ref[pl.ds(pl.multiple_of(j * tk, tk), tk), :]
        for mp in range(2):
            s_scr[slot, mp, :, c0:] = jnp.dot(kb, q_scr[mp, :, c0:], preferred_element_type=F32)

    def block_scores(slot, mp, c0, c1, masked):
        s = s_scr[slot, mp, :, c0:c1]
        if masked:
            kpos = lax.broadcasted_iota(I32, s.shape, 0)
            qpos = lax.broadcasted_iota(I32, s.shape, 1)
            s = jnp.where(kpos <= qpos, s, -jnp.inf)
        return s

    def pv_unshifted(j, slot, c0, c1, masked):
        vb = vT_ref[j]
        for mp in range(2):
            p = jnp.exp2(block_scores(slot, mp, c0, c1, masked))
            l_scr[mp, :, c0:c1] += jnp.sum(p, axis=0, keepdims=True)
            acc_scr[mp, :, c0:c1] += jnp.dot(vb, p.astype(BF16), preferred_element_type=F32)

    def pv_online(j, slot, c0, c1, masked):
        vb = vT_ref[j]
        for mp in range(2):
            s = block_scores(slot, mp, c0, c1, masked)
            m_old = m_scr[mp, :, c0:c1]
            m_new = jnp.maximum(m_old, jnp.max(s, axis=0, keepdims=True))
            alpha = jnp.exp2(m_old - m_new)
            p = jnp.exp2(s - m_new)
            l_scr[mp, :, c0:c1] = alpha * l_scr[mp, :, c0:c1] + jnp.sum(p, axis=0, keepdims=True)
            acc_scr[mp, :, c0:c1] = (alpha * acc_scr[mp, :, c0:c1]
                                     + jnp.dot(vb, p.astype(BF16), preferred_element_type=F32))
            m_scr[mp, :, c0:c1] = m_new

    def sweep(block, finish):
        acc_scr[...] = jnp.zeros(acc_scr.shape, F32)
        l_scr[...] = jnp.zeros(l_scr.shape, F32)
        scores(0, 0, 0)

        def pair(t):
            scores(t, 1, 0)
            block(t - 1, 0, 0, tq, False)
            scores(t + 1, 0, 0)
            block(t, 1, 0, tq, False)

        def body(u, carry):
            pair(4 * u + 1)
            pair(4 * u + 3)
            return carry

        lax.fori_loop(0, i // 2, body, 0)

        def tail():
            scores(2 * i + 1, 1, tk)
            block(2 * i, 0, 0, tk, True)
            block(2 * i, 0, tk, tq, False)
            block(2 * i + 1, 1, tk, tq, True)
            finish()

        @pl.when(i % 2 == 1)
        def _():
            pair(2 * i - 1)
            tail()

        @pl.when(i % 2 == 0)
        def _():
            tail()

    def finalize():
        lam = (jnp.exp(jnp.sum(lq1_ref[...] * lk1_ref[...], axis=1, keepdims=True))
               - jnp.exp(jnp.sum(lq2_ref[...] * lk2_ref[...], axis=1, keepdims=True)) + lam_init)
        out = acc_scr[0] / l_scr[0] - lam * (acc_scr[1] / l_scr[1])
        ms = jnp.mean(out * out, axis=0, keepdims=True)
        y = out * lax.rsqrt(ms + LN_EPS) * g_ref[...] * (1.0 - lam_init)
        o_ref[...] = y.T.astype(BF16)

    sweep(pv_unshifted, finalize)
    den = jnp.concatenate([l_scr[0], l_scr[1]], axis=0)
    in_range = (den >= DEN_MIN) & (den <= DEN_MAX)
    n_bad = jnp.sum(jnp.where(in_range, 0.0, 1.0))

    @pl.when(n_bad > 0.0)
    def _():
        m_scr[...] = jnp.full(m_scr.shape, -jnp.inf, F32)
        sweep(pv_online, finalize)


def _attention(qT, k, vTb, lams, g_col, lam_init):
    B, _, S = qT.shape
    nkv, tk = vTb.shape[1], vTb.shape[3]
    tq = 2 * tk
    small = pl.BlockSpec((1, QK_DIM), lambda b, h, i: (0, 0))
    return pl.pallas_call(
        functools.partial(_attn_kernel, lam_init=lam_init),
        grid=(B, N_HEADS, S // tq),
        in_specs=[
            small, small, small, small,
            pl.BlockSpec((V_DIM, 1), lambda b, h, i: (0, 0)),
            pl.BlockSpec((None, HEAD_COLS, tq), lambda b, h, i: (b, h, i)),
            pl.BlockSpec((None, S, HEAD_COLS), lambda b, h, i: (b, 0, h)),
            pl.BlockSpec((None, nkv, V_DIM, tk), lambda b, h, i: (b, 0, h, 0)),
        ],
        out_specs=pl.BlockSpec((None, tq, V_DIM), lambda b, h, i: (b, i, h)),
        out_shape=jax.ShapeDtypeStruct((B, S, DIFF_WIDTH), BF16),
        scratch_shapes=[
            pltpu.VMEM((2, HEAD_COLS, tq), BF16),
            pltpu.VMEM((2, 2, tk, tq), F32),
            pltpu.VMEM((2, 1, tq), F32),
            pltpu.VMEM((2, 1, tq), F32),
            pltpu.VMEM((2, V_DIM, tq), F32),
        ],
        compiler_params=_cparams(("arbitrary", "arbitrary", "arbitrary")),
        name="diff_attn",
    )(*lams, g_col, qT, k, vTb)


def _mix_kernel(x_ref, mod_ref, att_ref, zu_ref, zvn_ref, wsp_ref, bsp_ref, wo_ref, g_ref, b_ref, wr_ref, br_ref,
                o_ref, hp_ref, rt_ref, cnt_ref, sg_scr, run_scr, *, alpha):
    D = D_MODEL
    nc = zu_ref.shape[0]
    r = lax.broadcasted_iota(I32, (CHUNK, CHUNK), 0)
    c = lax.broadcasted_iota(I32, (CHUNK, CHUNK), 1)
    causal = r >= c
    for g in range(N_SG):
        sl = slice(g * SG_DIM, (g + 1) * SG_DIM)
        w = jnp.where(causal, wsp_ref[g], 0.0).astype(BF16)
        z = jnp.concatenate([zvn_ref[n, :, sl] for n in range(nc)], axis=1)
        mixed = jnp.dot(w, z, preferred_element_type=F32)
        for n in range(nc):
            gate = mixed[:, n * SG_DIM:(n + 1) * SG_DIM] + bsp_ref[g]
            sg_scr[n * CHUNK:(n + 1) * CHUNK, sl] = (zu_ref[n, :, sl].astype(F32) * gate).astype(BF16)
    gt1 = 1.0 + mod_ref[:, 2 * D:3 * D]
    half = x_ref.shape[0] // 2
    for hf in range(2):
        rows = slice(hf * half, (hf + 1) * half)
        mix = (jnp.dot(att_ref[rows, :], wo_ref[0:DIFF_WIDTH, :], preferred_element_type=F32)
               + jnp.dot(sg_scr[rows, :], wo_ref[DIFF_WIDTH:, :], preferred_element_type=F32))
        y = alpha * x_ref[rows, :] + gt1 * mix
        o_ref[rows, :] = _layer_norm_rows(y, g_ref[...], b_ref[...])
    _route_rows(o_ref[...], mod_ref, wr_ref, br_ref, hp_ref, rt_ref, cnt_ref, run_scr)


def _mix(x, mod_l, att, zu, zvn, w_sp, b_sp_full, w_out_b, ln_g, ln_b, wr, br, alpha):
    B, S, D = x.shape
    tm = min(ROW_TILE, S)
    nc = tm // CHUNK
    row = lambda b, i: (b, i, 0)
    zu4 = zu.reshape(B, S // CHUNK, CHUNK, SG_WIDTH)
    zvn4 = zvn.reshape(B, S // CHUNK, CHUNK, SG_WIDTH)
    chunked = pl.BlockSpec((None, nc, CHUNK, SG_WIDTH), lambda b, i: (b, i, 0, 0))
    full2 = lambda a: pl.BlockSpec(a.shape, lambda b, i: (0,) * a.ndim)
    return pl.pallas_call(
        functools.partial(_mix_kernel, alpha=alpha),
        grid=(B, S // tm),
        in_specs=[
            pl.BlockSpec((None, tm, D), row),
            pl.BlockSpec((None, 1, 6 * D), lambda b, i: (b, 0, 0)),
            pl.BlockSpec((None, tm, DIFF_WIDTH), row),
            chunked, chunked,
            full2(w_sp), full2(b_sp_full), full2(w_out_b), full2(ln_g), full2(ln_b), full2(wr), full2(br),
        ],
        out_specs=[
            pl.BlockSpec((None, tm, D), row),
            pl.BlockSpec((None, tm, ROW_WORDS), row),
            pl.BlockSpec((None, 8, tm), lambda b, i: (b, 0, i)),
            pl.BlockSpec((LANES, 1), lambda b, i: (0, 0)),
        ],
        out_shape=[
            jax.ShapeDtypeStruct((B, S, D), F32),
            jax.ShapeDtypeStruct((B, S, ROW_WORDS), U32),
            jax.ShapeDtypeStruct((B, 8, S), F32),
            jax.ShapeDtypeStruct((LANES, 1), F32),
        ],
        scratch_shapes=[pltpu.VMEM((tm, SG_WIDTH), BF16), pltpu.VMEM((LANES, 1), F32)],
        compiler_params=_cparams(("arbitrary", "arbitrary")),
        name="sgate_outproj_ln_route",
    )(x, mod_l, att, zu4, zvn4, w_sp, b_sp_full, w_out_b, ln_g, ln_b, wr, br)


def _route_rows(x1, mod_ref, wr_ref, br_ref, hp_ref, rt_ref, cnt_ref, run_scr):
    D = D_MODEL
    tm = x1.shape[0]

    @pl.when((pl.program_id(0) == 0) & (pl.program_id(1) == 0))
    def _():
        run_scr[...] = jnp.zeros(run_scr.shape, F32)

    sh = mod_ref[:, 3 * D:4 * D]
    sc = mod_ref[:, 4 * D:5 * D]
    h = x1 * (1.0 + sc) + sh
    h_hi = h.astype(BF16)
    h_lo = (h - h_hi.astype(F32)).astype(BF16)
    both = jnp.dot(h_hi, wr_ref[...], preferred_element_type=F32)
    logit = (both[:, 0:LANES] + both[:, LANES:]
             + jnp.dot(h_lo, wr_ref[:, 0:LANES], preferred_element_type=F32)) + br_ref[...]
    lt = logit.T[0:ROUTE_LANE0 + N_EXPERTS, :]
    row = lax.broadcasted_iota(I32, lt.shape, 0).astype(F32)
    neg = -jnp.inf
    big = float(LANES)

    def first_argmax(v):
        mx = jnp.max(v, axis=0, keepdims=True)
        idx = jnp.min(jnp.where(v == mx, row, big), axis=0, keepdims=True)
        return mx, idx

    in_grp = row < N_GROUPS
    gmax, gidx = first_argmax(jnp.where(in_grp, lt, neg))
    g_p = 1.0 / jnp.sum(jnp.where(in_grp, jnp.exp(lt - gmax), 0.0), axis=0, keepdims=True)
    lo_row = ROUTE_LANE0 + EXP_PER_GROUP * gidx
    sel = jnp.where((row >= lo_row) & (row < lo_row + EXP_PER_GROUP), lt, neg)
    v1, i1 = first_argmax(sel)
    v2, i2 = first_argmax(jnp.where(row == i1, neg, sel))
    t = jnp.exp(v2 - v1)
    w1 = g_p / (1.0 + t)
    w2 = g_p * t / (1.0 + t)

    first_lower = i1 < i2
    e_a = jnp.minimum(i1, i2) - lo_row
    e_b = jnp.maximum(i1, i2) - lo_row
    w_a = jnp.where(first_lower, w1, w2)
    w_b = jnp.where(first_lower, w2, w1)
    bucket = gidx * N_PAIRS + (e_a * (2 * EXP_PER_GROUP - 1 - e_a) * 0.5 + (e_b - e_a - 1.0))

    brow = lax.broadcasted_iota(I32, (LANES, tm), 0).astype(F32)
    hp_ref[:, 0:HALF] = _pack_halves(h)
    wslab = jnp.where(brow == 0.0, w_a, jnp.where(brow == 1.0, w_b, 0.0))
    hp_ref[:, HALF:] = pltpu.bitcast(wslab.T, U32)

    hot = brow == bucket
    onehot = jnp.where(hot, 1.0, 0.0)
    r = lax.broadcasted_iota(I32, (tm, tm), 0)
    c = lax.broadcasted_iota(I32, (tm, tm), 1)
    earlier = jnp.where(r < c, 1.0, 0.0).astype(BF16)
    before = jnp.dot(onehot.astype(BF16), earlier, preferred_element_type=F32) + run_scr[...]
    rank = jnp.sum(jnp.where(hot, before, 0.0), axis=0, keepdims=True)
    run_scr[...] = run_scr[...] + jnp.sum(onehot, axis=1, keepdims=True)
    cnt_ref[...] = run_scr[...]

    r8 = lax.broadcasted_iota(I32, (8, tm), 0)
    rt_ref[...] = jnp.where(r8 == 0, bucket, jnp.where(r8 == 1, rank, 0.0))


def _row_of(ref, p):
    return ref.at[p >> 3, pl.ds(p & (SUBLANES - 1), 1)]


def _dispatch_kernel(p_ref, hp_ref, xs_in_ref, xs_ref, sem):
    del xs_in_ref
    ng = hp_ref.shape[0]

    def issue(g, carry):
        for u in range(SUBLANES):
            p = p_ref[0, 0, g * SUBLANES + u]
            pltpu.make_async_copy(hp_ref.at[g, pl.ds(u, 1)], _row_of(xs_ref, p), sem).start(priority=u % 2)
        return carry

    lax.fori_loop(0, ng, issue, 0)
    pltpu.make_async_copy(hp_ref, xs_ref.at[pl.ds(0, ng)], sem).wait()


def _dispatch(pos, hp, xs_buf):
    T = hp.shape[0]
    n_rows = xs_buf.shape[0]
    nb = min(DISPATCH_BLOCK, T)
    blk = pl.BlockSpec((1, 1, nb), lambda i: (i, 0, 0), memory_space=pltpu.SMEM)
    xs0 = xs_buf.reshape(n_rows // SUBLANES, SUBLANES, ROW_WORDS)
    xs = pl.pallas_call(
        _dispatch_kernel,
        grid=(T // nb,),
        in_specs=[blk, pl.BlockSpec((nb // SUBLANES, SUBLANES, ROW_WORDS), lambda i: (i, 0, 0)),
                  pl.BlockSpec(memory_space=pl.ANY)],
        out_specs=pl.BlockSpec(memory_space=pl.ANY),
        out_shape=jax.ShapeDtypeStruct(xs0.shape, U32),
        scratch_shapes=[pltpu.SemaphoreType.DMA(())],
        input_output_aliases={2: 0},
        compiler_params=_cparams(("arbitrary",)),
        name="moe_dispatch",
    )(pos.reshape(T // nb, 1, nb), hp.reshape(T // SUBLANES, SUBLANES, ROW_WORDS), xs0)
    return xs.reshape(n_rows, ROW_WORDS)


def _expert_kernel(ea_ref, eb_ref, nv_ref, xs_ref, wga_ref, wua_ref, wda_ref, wgb_ref, wub_ref, wdb_ref, ys_ref):
    del ea_ref, eb_ref

    @pl.when(pl.program_id(0) < nv_ref[0])
    def _():
        lo, hi = _unpack_halves(xs_ref[:, 0:HALF])
        lo = lo.astype(BF16)
        hi = hi.astype(BF16)
        wts = pltpu.bitcast(xs_ref[:, HALF:], F32)

        def hidden(wg_ref, wu_ref, w):
            g = (jnp.dot(lo, wg_ref[0:HALF, :], preferred_element_type=F32)
                 + jnp.dot(hi, wg_ref[HALF:, :], preferred_element_type=F32))
            u = (jnp.dot(lo, wu_ref[0:HALF, :], preferred_element_type=F32)
                 + jnp.dot(hi, wu_ref[HALF:, :], preferred_element_type=F32))
            return (g * jax.nn.sigmoid(g) * u * w).astype(BF16)

        y = (jnp.dot(hidden(wga_ref, wua_ref, wts[:, 0:1]), wda_ref[...], preferred_element_type=F32)
             + jnp.dot(hidden(wgb_ref, wub_ref, wts[:, 1:2]), wdb_ref[...], preferred_element_type=F32))
        ys_ref[...] = _pack_halves(y)


def _experts(tile_ea, tile_eb, n_valid, xs, w_gate, w_up, w_down, layer):
    n_rows = xs.shape[0]
    tm = EXPERT_TILE
    nt = n_rows // tm
    rows = lambda n, ea, eb, nv: (jnp.minimum(n, nv[0] - 1), 0)
    sel_a = lambda n, ea, eb, nv: (layer, ea[n], 0, 0)
    sel_b = lambda n, ea, eb, nv: (layer, eb[n], 0, 0)
    up = lambda sel: pl.BlockSpec((None, None, D_MODEL, D_EXPERT), sel)
    down = lambda sel: pl.BlockSpec((None, None, D_EXPERT, D_MODEL), sel)
    return pl.pallas_call(
        _expert_kernel,
        grid_spec=pltpu.PrefetchScalarGridSpec(
            num_scalar_prefetch=3,
            grid=(nt,),
            in_specs=[pl.BlockSpec((tm, ROW_WORDS), rows),
                      up(sel_a), up(sel_a), down(sel_a), up(sel_b), up(sel_b), down(sel_b)],
            out_specs=pl.BlockSpec((tm, HALF), rows),
        ),
        out_shape=jax.ShapeDtypeStruct((n_rows, HALF), U32),
        compiler_params=_cparams(("arbitrary",)),
        name="moe_experts",
    )(tile_ea, tile_eb, n_valid, xs, w_gate, w_up, w_down, w_gate, w_up, w_down)


def _combine_kernel(p_ref, pn_ref, ys_ref, x_ref, mod_ref, g_ref, b_ref, o_ref, buf, sems, *, alpha):
    D = D_MODEL
    tc = x_ref.shape[0]
    g = pl.program_id(0) * pl.num_programs(1) + pl.program_id(1)
    n_steps = pl.num_programs(0) * pl.num_programs(1)
    slot = g % 2

    def gather(pa_ref, s):
        def issue(g8, carry):
            for u in range(SUBLANES):
                p = pa_ref[0, 0, g8 * SUBLANES + u]
                pltpu.make_async_copy(_row_of(ys_ref, p), buf.at[s, g8, pl.ds(u, 1)], sems.at[s]).start(
                    priority=u % 2)
            return carry

        lax.fori_loop(0, tc // SUBLANES, issue, 0)

    @pl.when(g == 0)
    def _():
        gather(p_ref, 0)

    @pl.when(g + 1 < n_steps)
    def _():
        gather(pn_ref, 1 - slot)

    pltpu.make_async_copy(ys_ref.at[pl.ds(0, tc // SUBLANES)], buf.at[slot], sems.at[slot]).wait()

    lo, hi = _unpack_halves(buf[slot].reshape(tc, HALF))
    ffn = jnp.concatenate([lo, hi], axis=1)
    gt = mod_ref[:, 5 * D:6 * D]
    y = alpha * x_ref[...] + (1.0 + gt) * ffn
    o_ref[...] = _layer_norm_rows(y, g_ref[...], b_ref[...])


def _combine(pos, ys, x1, mod_l, ln_g, ln_b, alpha):
    B, S, D = x1.shape
    tc = min(COMBINE_TILE, S)
    nt = S // tc
    blk = pl.BlockSpec((1, 1, tc), lambda b, i: (b * nt + i, 0, 0), memory_space=pltpu.SMEM)
    nxt = pl.BlockSpec((1, 1, tc), lambda b, i: (jnp.minimum(b * nt + i + 1, B * nt - 1), 0, 0),
                       memory_space=pltpu.SMEM)
    row = lambda b, i: (b, i, 0)
    full2 = lambda a: pl.BlockSpec(a.shape, lambda b, i: (0,) * a.ndim)
    p = pos.reshape(B * nt, 1, tc)
    return pl.pallas_call(
        functools.partial(_combine_kernel, alpha=alpha),
        grid=(B, nt),
        in_specs=[
            blk, nxt,
            pl.BlockSpec(memory_space=pl.ANY),
            pl.BlockSpec((None, tc, D), row),
            pl.BlockSpec((None, 1, 6 * D), lambda b, i: (b, 0, 0)),
            full2(ln_g), full2(ln_b),
        ],
        out_specs=pl.BlockSpec((None, tc, D), row),
        out_shape=jax.ShapeDtypeStruct((B, S, D), F32),
        scratch_shapes=[pltpu.VMEM((2, tc // SUBLANES, SUBLANES, HALF), U32), pltpu.SemaphoreType.DMA((2,))],
        compiler_params=_cparams(("arbitrary", "arbitrary")),
        name="moe_combine_ln",
    )(p, p, ys.reshape(ys.shape[0] // SUBLANES, SUBLANES, HALF), x1, mod_l, ln_g, ln_b)


def _rope_tables(S):
    inv = 1.0 / (ROPE_THETA ** (jnp.arange(0, QK_DIM, 2, dtype=F32) / QK_DIM))
    ang = jnp.arange(S, dtype=F32)[:, None] * inv[None, :]
    cos, sin = jnp.cos(ang), jnp.sin(ang)
    half = QK_DIM // 2
    first = (jnp.arange(LANES) % QK_DIM) < half
    cs = jnp.tile(cos, (1, LANES // half))
    sn = jnp.tile(sin, (1, LANES // half))
    sa = jnp.where(first[None, :], -sn, 0.0)
    sb = jnp.where(first[None, :], 0.0, sn)
    return cs, sa, sb


def _router_matrix(w_group, b_group, w_router, b_router):
    D = w_group.shape[0]
    w = jnp.zeros((D, LANES), F32)
    w = w.at[:, 0:N_GROUPS].set(w_group)
    wr = jnp.transpose(w_router, (1, 0, 2)).reshape(D, N_EXPERTS)
    w = w.at[:, ROUTE_LANE0:ROUTE_LANE0 + N_EXPERTS].set(wr)
    b = jnp.zeros((1, LANES), F32)
    b = b.at[0, 0:N_GROUPS].set(b_group)
    b = b.at[0, ROUTE_LANE0:ROUTE_LANE0 + N_EXPERTS].set(b_router.reshape(N_EXPERTS))
    hi = w.astype(BF16)
    lo = (w - hi.astype(F32)).astype(BF16)
    return jnp.concatenate([hi, lo], axis=1), b


def _routing_tables(rt, cnt, n_tiles):
    B, _, S = rt.shape
    tm = EXPERT_TILE
    bucket = rt[:, 0, :].astype(I32)
    rank = rt[:, 1, :].astype(I32)
    counts = cnt[0:N_BUCKETS, 0].astype(I32)
    tiles = (counts + tm - 1) // tm
    tile_end = jnp.cumsum(tiles)
    tile_start = tile_end - tiles
    onehot = bucket[..., None] == jnp.arange(N_BUCKETS, dtype=I32)
    pos = (jnp.sum(jnp.where(onehot, tile_start * tm, 0), axis=-1) + rank).reshape(B * S)
    tile_ids = jnp.arange(n_tiles, dtype=I32)
    tile_bucket = jnp.minimum(jnp.sum(tile_ids[:, None] >= tile_end[None, :], axis=1), N_BUCKETS - 1)
    pair_a = jnp.asarray([a for a in range(EXP_PER_GROUP) for _ in range(a + 1, EXP_PER_GROUP)], I32)
    pair_b = jnp.asarray([b for a in range(EXP_PER_GROUP) for b in range(a + 1, EXP_PER_GROUP)], I32)
    group = tile_bucket // N_PAIRS
    pair = tile_bucket % N_PAIRS
    tile_ea = (group * EXP_PER_GROUP + pair_a[pair]).astype(I32)
    tile_eb = (group * EXP_PER_GROUP + pair_b[pair]).astype(I32)
    n_valid = tile_end[-1:].astype(I32)
    return pos, tile_ea, tile_eb, n_valid


def kernel(x, c, w_ada, b_ada, w_in, lambda_q1, lambda_k1, lambda_q2, lambda_k2, subln_g, sg_ln_g, sg_ln_b, w_spatial, b_spatial, w_out, ln1_g, ln1_b, w_group, b_group, w_router, b_router, w_gate, w_up, w_down, ln2_g, ln2_b):
    B, S, D = x.shape
    depth = w_in.shape[0]
    T = B * S
    alpha = (2.0 * depth) ** 0.25
    n_tiles = T // EXPERT_TILE + N_BUCKETS
    n_rows = n_tiles * EXPERT_TILE

    mod = _ada(c, w_ada, b_ada)
    tabs = _rope_tables(S)
    w_gate, w_up, w_down = w_gate.astype(BF16), w_up.astype(BF16), w_down.astype(BF16)
    xs = jnp.zeros((n_rows, ROW_WORDS), U32)
    for l in range(depth):
        mod_l = mod[l].reshape(B, 1, 6 * D)
        lam_init = 0.8 - 0.6 * math.exp(-0.3 * l)

        qT, k, vTb, zu, zvn = _inproj(x, mod_l, w_in[l].astype(BF16), tabs,
                                      sg_ln_g[l].reshape(1, SG_WIDTH), sg_ln_b[l].reshape(1, SG_WIDTH))
        lams = (lambda_q1[l].reshape(1, QK_DIM), lambda_k1[l].reshape(1, QK_DIM),
                lambda_q2[l].reshape(1, QK_DIM), lambda_k2[l].reshape(1, QK_DIM))
        att = _attention(qT, k, vTb, lams, subln_g[l].reshape(V_DIM, 1), lam_init)
        b_sp_full = jnp.broadcast_to(b_spatial[l][:, :, None], (N_SG, CHUNK, SG_DIM))
        wr, br = _router_matrix(w_group[l], b_group[l], w_router[l], b_router[l])
        x1, hp, rt, cnt = _mix(x, mod_l, att, zu, zvn, w_spatial[l], b_sp_full, w_out[l].astype(BF16),
                               ln1_g[l].reshape(1, D), ln1_b[l].reshape(1, D), wr, br, alpha)
        pos, tile_ea, tile_eb, n_valid = _routing_tables(rt, cnt, n_tiles)
        xs = _dispatch(pos, hp.reshape(T, ROW_WORDS), xs)
        ys = _experts(tile_ea, tile_eb, n_valid, xs, w_gate, w_up, w_down, l)
        x = _combine(pos, ys, x1, mod_l, ln2_g[l].reshape(1, D), ln2_b[l].reshape(1, D), alpha)
    return x
```
